```python
import math
import jax, jax.numpy as jnp
from jax import lax
import numpy as np

D_MODEL = 1024
BATCH = 2
SEQ = 16384
DEPTH = 2

D_MIX = D_MODEL
GROUP_W = D_MIX // 4
HEAD_DIM = 64
SC_WIDTH = GROUP_W
SC_KERNEL = 3
CF_WIDTH = GROUP_W
CF_KERNEL = 31
POOL_WIDTH = GROUP_W
POOL_WINDOWS = (2, 4, 8, 16)
POOL_GROUPS = len(POOL_WINDOWS)
POOL_GW = POOL_WIDTH // POOL_GROUPS
NSA_HEADS = GROUP_W // HEAD_DIM
NSA_KV_DIM = HEAD_DIM
CMP_BLOCK = 32
CMP_STRIDE = 16
CMP_HIDDEN = 256
SLC_BLOCK = 64
SLC_TOPN = 16
WIN = 512
Q_BLOCK = 128
ROPE_THETA = 10000.0
N_EXPERTS = 16
N_GROUPS = 4
EXPERTS_PER_GROUP = N_EXPERTS // N_GROUPS
TOPK_GROUP = 1
TOP_K = 2
D_EXPERT = 512
MOE_BLOCK = 256
DN_ALPHA = (2 * DEPTH) ** 0.25
DN_BETA = (8 * DEPTH) ** -0.25
LN_EPS = 1e-5
NEG = -1e30
FORCE = 1e4

IN_SIZES = (SC_WIDTH, SC_WIDTH, SC_WIDTH,
            CF_WIDTH, CF_WIDTH,
            POOL_WIDTH,
            NSA_HEADS * HEAD_DIM,
            NSA_KV_DIM, NSA_KV_DIM,
            NSA_KV_DIM, NSA_KV_DIM,
            NSA_KV_DIM, NSA_KV_DIM,
            NSA_HEADS * 3)
IN_WIDTH = sum(IN_SIZES)

kernel_name = "hymba_style_conv_pool_nsa_moe_deepnorm"


def layer_norm(x, g, b):
    xf = x.astype(jnp.float32)
    mu = xf.mean(-1, keepdims=True)
    var = jnp.square(xf - mu).mean(-1, keepdims=True)
    y = (xf - mu) * lax.rsqrt(var + LN_EPS)
    return (y * g.astype(jnp.float32) + b.astype(jnp.float32)).astype(x.dtype)


def causal_depthwise_conv(x, w):
    k = w.shape[0]
    return lax.conv_general_dilated(
        x, w[:, None, :].astype(x.dtype), window_strides=(1,), padding=[(k - 1, 0)],
        dimension_numbers=('NWC', 'WIO', 'NWC'), feature_group_count=x.shape[-1])


def rope_tables(n):
    inv = 1.0 / (ROPE_THETA ** (jnp.arange(0, HEAD_DIM, 2, dtype=jnp.float32) / HEAD_DIM))
    ang = jnp.arange(n, dtype=jnp.float32)[:, None] * inv[None, :]
    ang = jnp.concatenate([ang, ang], -1)
    return jnp.cos(ang), jnp.sin(ang)


def apply_rope(x, cos, sin):
    x1, x2 = jnp.split(x, 2, axis=-1)
    rot = jnp.concatenate([-x2, x1], -1)
    return (x.astype(jnp.float32) * cos + rot.astype(jnp.float32) * sin).astype(x.dtype)


def short_conv_mixer(b_gate, c_gate, h, w):
    return b_gate * causal_depthwise_conv(c_gate * h, w)


def conformer_conv_mixer(a, gate, w, bias, g, beta):
    u = a * jax.nn.sigmoid(gate)
    u = causal_depthwise_conv(u, w) + bias
    u = layer_norm(u, g, beta)
    return jax.nn.silu(u)


def multiscale_pool_mixer(p, w, scale):
    b_, s_, _ = p.shape
    pg = p.reshape(b_, s_, POOL_GROUPS, POOL_GW).astype(jnp.float32)
    cs = jnp.cumsum(pg, axis=1)
    cs = jnp.concatenate([jnp.zeros_like(cs[:, :1]), cs], axis=1)
    t1 = jnp.arange(1, s_ + 1)[:, None]
    lo = jnp.maximum(t1 - jnp.array(POOL_WINDOWS)[None, :], 0)
    cnt = (t1 - lo).astype(jnp.float32)
    lo_sum = cs[:, lo, jnp.arange(POOL_GROUPS)[None, :]]
    mean = (cs[:, 1:] - lo_sum) / cnt[None, :, :, None]
    d = (mean - pg).astype(p.dtype)
    y = jnp.einsum('bsgc,gcd->bsgd', d, w)
    return y.reshape(b_, s_, POOL_WIDTH) * scale


def compress_blocks(kv, pe, w1, w2):
    b_, s_, dh = kv.shape
    n_cmp = (s_ - CMP_BLOCK) // CMP_STRIDE + 1
    idx = jnp.arange(n_cmp)[:, None] * CMP_STRIDE + jnp.arange(CMP_BLOCK)[None, :]
    blocks = kv[:, idx] + pe
    hid = jax.nn.silu(blocks.reshape(b_, n_cmp, CMP_BLOCK * dh) @ w1)
    return hid @ w2


def nsa_mixer(q, kc, vc, ks, vs, kw, vw, gates, pe_k, pe_v, wk1, wk2, wv1, wv2, cos, sin):
    b_, s_, nh, dh = q.shape
    dt = q.dtype
    scale = dh ** -0.5
    k_cmp = compress_blocks(kc, pe_k, wk1, wk2)
    v_cmp = compress_blocks(vc, pe_v, wv1, wv2)
    n_cmp = k_cmp.shape[1]
    n_slc = s_ // SLC_BLOCK
    top_n = min(SLC_TOPN, n_slc)
    cmp_start = jnp.arange(n_cmp) * CMP_STRIDE
    cmp_end = cmp_start + CMP_BLOCK - 1
    slc_start = jnp.arange(n_slc) * SLC_BLOCK
    overlap = ((cmp_start[:, None] < slc_start[None, :] + SLC_BLOCK) &
               (cmp_start[:, None] + CMP_BLOCK > slc_start[None, :])).astype(jnp.float32)
    q_rot = apply_rope(q, cos[:, None, :], sin[:, None, :])
    ks_blk = apply_rope(ks, cos, sin).reshape(b_, n_slc, SLC_BLOCK, dh)
    vs_blk = vs.reshape(b_, n_slc, SLC_BLOCK, dh)
    kw_pad = jnp.pad(apply_rope(kw, cos, sin), ((0, 0), (WIN, 0), (0, 0)))
    vw_pad = jnp.pad(vw, ((0, 0), (WIN, 0), (0, 0)))
    b_idx = jnp.arange(b_)[:, None, None]
    blk = jnp.arange(n_slc)

    def one_query_block(qb):
        s0 = qb * Q_BLOCK
        tpos = s0 + jnp.arange(Q_BLOCK)
        qn = lax.dynamic_slice_in_dim(q, s0, Q_BLOCK, axis=1)
        qr = lax.dynamic_slice_in_dim(q_rot, s0, Q_BLOCK, axis=1)
        g = lax.dynamic_slice_in_dim(gates, s0, Q_BLOCK, axis=1)
        sc = jnp.einsum('bqhd,bnd->bqhn', qn, k_cmp).astype(jnp.float32) * scale
        cmask = cmp_end[None, :] <= tpos[:, None]
        p_cmp = jax.nn.softmax(jnp.where(cmask[None, :, None, :], sc, NEG), axis=-1)
        p_cmp = jnp.where(cmask.any(-1)[None, :, None, None], p_cmp, 0.0)
        o_cmp = jnp.einsum('bqhn,bnd->bqhd', p_cmp.astype(dt), v_cmp)
        imp = jnp.einsum('bqhn,nj->bqj', p_cmp, overlap)
        cur = tpos // SLC_BLOCK
        forced = (blk[None, :] == 0) | (blk[None, :] == cur[:, None]) | (blk[None, :] == cur[:, None] - 1)
        imp = jnp.where(forced[None], FORCE, imp)
        imp = jnp.where((slc_start[None, :] <= tpos[:, None])[None], imp, NEG)
        _, sel = lax.top_k(imp, top_n)
        k_sel = ks_blk[b_idx, sel]
        v_sel = vs_blk[b_idx, sel]
        ss = jnp.einsum('bqhd,bqnld->bqhnl', qr, k_sel).astype(jnp.float32) * scale
        kpos = sel[..., None] * SLC_BLOCK + jnp.arange(SLC_BLOCK)
        smask = kpos <= tpos[None, :, None, None]
        ss = jnp.where(smask[:, :, None], ss, NEG)
        ps = jax.nn.softmax(ss.reshape(b_, Q_BLOCK, nh, top_n * SLC_BLOCK), axis=-1).reshape(ss.shape)
        o_slc = jnp.einsum('bqhnl,bqnld->bqhd', ps.astype(dt), v_sel)
        k_win = lax.dynamic_slice_in_dim(kw_pad, s0, Q_BLOCK + WIN, axis=1)
        v_win = lax.dynamic_slice_in_dim(vw_pad, s0, Q_BLOCK + WIN, axis=1)
        sw = jnp.einsum('bqhd,bkd->bqhk', qr, k_win).astype(jnp.float32) * scale
        wpos = s0 - WIN + jnp.arange(Q_BLOCK + WIN)
        wmask = (wpos[None, :] <= tpos[:, None]) & (wpos[None, :] > tpos[:, None] - WIN) & (wpos[None, :] >= 0)
        pw = jax.nn.softmax(jnp.where(wmask[None, :, None, :], sw, NEG), axis=-1)
        o_win = jnp.einsum('bqhk,bkd->bqhd', pw.astype(dt), v_win)
        o = g[..., 0:1] * o_cmp + g[..., 1:2] * o_slc + g[..., 2:3] * o_win
        return o.astype(dt)

    out = lax.map(one_query_block, jnp.arange(s_ // Q_BLOCK))
    return out.transpose(1, 0, 2, 3, 4).reshape(b_, s_, nh * dh)


def route(xf, router_w, router_bias):
    n = xf.shape[0]
    aff = jax.nn.sigmoid((xf @ router_w).astype(jnp.float32))
    biased = aff + router_bias.astype(jnp.float32)
    gscore = lax.top_k(biased.reshape(n, N_GROUPS, EXPERTS_PER_GROUP), 2)[0].sum(-1)
    _, gsel = lax.top_k(gscore, TOPK_GROUP)
    gmask = (jnp.arange(N_GROUPS)[None, None, :] == gsel[:, :, None]).any(1)
    emask = jnp.repeat(gmask, EXPERTS_PER_GROUP, axis=1)
    _, eidx = lax.top_k(jnp.where(emask, biased, NEG), TOP_K)
    w = jnp.take_along_axis(aff, eidx, axis=-1)
    return eidx, w / w.sum(-1, keepdims=True)


def moe_ffn(x, router_w, router_bias, w_gate, w_up, w_down):
    b_, s_, d = x.shape
    n = b_ * s_
    xf = x.reshape(n, d)
    eidx, gw = route(xf, router_w, router_bias)
    a = n * TOP_K
    e_flat = eidx.reshape(a)
    tok = jnp.repeat(jnp.arange(n, dtype=jnp.int32), TOP_K)
    order = jnp.argsort(e_flat)
    e_s = e_flat[order]
    tok_s = tok[order]
    g_s = gw.reshape(a)[order]
    counts = jnp.bincount(e_flat, length=N_EXPERTS)
    padded = (counts + MOE_BLOCK - 1) // MOE_BLOCK * MOE_BLOCK
    starts = jnp.cumsum(counts) - counts
    pends = jnp.cumsum(padded)
    pstarts = pends - padded
    dest = pstarts[e_s] + jnp.arange(a) - starts[e_s]
    n_blocks = -(-(a + N_EXPERTS * (MOE_BLOCK - 1)) // MOE_BLOCK)
    p_len = n_blocks * MOE_BLOCK
    buf_tok = jnp.full((p_len,), n, jnp.int32).at[dest].set(tok_s)
    buf_gate = jnp.zeros((p_len,), x.dtype).at[dest].set(g_s.astype(x.dtype))
    blk_e = jnp.minimum(jnp.searchsorted(pends, jnp.arange(n_blocks) * MOE_BLOCK, side='right'), N_EXPERTS - 1)
    x_pad = jnp.concatenate([xf, jnp.zeros((1, d), x.dtype)], axis=0)
    xin = x_pad[buf_tok].reshape(n_blocks, MOE_BLOCK, d)

    def expert_block(args):
        xb, e = args
        h = jax.nn.silu(xb @ w_gate[e]) * (xb @ w_up[e])
        return h @ w_down[e]

    y = lax.map(expert_block, (xin, blk_e)).reshape(p_len, d) * buf_gate[:, None]
    out = jnp.zeros((n + 1, d), x.dtype).at[buf_tok].add(y)[:n]
    return out.reshape(b_, s_, d)


def hybrid_layer(x, w_in, conv_a_w, conv_b_w, conv_b_b, cf_ln_g, cf_ln_b, pool_w, pool_scale,
                 cmp_pe_k, cmp_pe_v, cmp_k_w1, cmp_k_w2, cmp_v_w1, cmp_v_w2, w_out,
                 ln1_g, ln1_b, ln2_g, ln2_b, router_w, router_bias, moe_w_gate, moe_w_up, moe_w_down,
                 cos, sin):
    b_, s_, _ = x.shape
    proj = x @ w_in
    split_at = [int(v) for v in np.cumsum(IN_SIZES)[:-1]]
    (a_b, a_c, a_h, b_val, b_gate, c_p, q, kc, vc, ks, vs, kw, vw, g) = jnp.split(proj, split_at, axis=-1)
    y_a = short_conv_mixer(a_b, a_c, a_h, conv_a_w)
    y_b = conformer_conv_mixer(b_val, b_gate, conv_b_w, conv_b_b, cf_ln_g, cf_ln_b)
    y_c = multiscale_pool_mixer(c_p, pool_w, pool_scale)
    gates = jax.nn.sigmoid(g.astype(jnp.float32)).astype(x.dtype).reshape(b_, s_, NSA_HEADS, 3)
    y_d = nsa_mixer(q.reshape(b_, s_, NSA_HEADS, HEAD_DIM), kc, vc, ks, vs, kw, vw, gates,
                    cmp_pe_k, cmp_pe_v, cmp_k_w1, cmp_k_w2, cmp_v_w1, cmp_v_w2, cos, sin)
    mix = jnp.concatenate([y_a, y_b, y_c, y_d], axis=-1) @ w_out
    x = layer_norm(DN_ALPHA * x + mix, ln1_g, ln1_b)
    x = layer_norm(DN_ALPHA * x + moe_ffn(x, router_w, router_bias, moe_w_gate, moe_w_up, moe_w_down), ln2_g, ln2_b)
    return x


def setup_inputs(seed: int = 0) -> dict:
    key = jax.random.key(seed)
    ks = jax.random.split(key, 26)
    f32 = jnp.float32
    nrm = lambda k, shape, s: jax.random.normal(k, shape, f32) * s
    return {
        "x": nrm(ks[0], (BATCH, SEQ, D_MODEL), 1.0),
        "w_in": nrm(ks[1], (DEPTH, D_MODEL, IN_WIDTH), D_MODEL ** -0.5),
        "conv_a_w": nrm(ks[2], (DEPTH, SC_KERNEL, SC_WIDTH), SC_KERNEL ** -0.5),
        "conv_b_w": nrm(ks[3], (DEPTH, CF_KERNEL, CF_WIDTH), CF_KERNEL ** -0.5),
        "conv_b_b": nrm(ks[4], (DEPTH, CF_WIDTH), 0.01),
        "cf_ln_g": 1.0 + nrm(ks[5], (DEPTH, CF_WIDTH), 0.02),
        "cf_ln_b": nrm(ks[6], (DEPTH, CF_WIDTH), 0.01),
        "pool_w": nrm(ks[7], (DEPTH, POOL_GROUPS, POOL_GW, POOL_GW), POOL_GW ** -0.5),
        "pool_scale": 1.0 + nrm(ks[8], (DEPTH, POOL_WIDTH), 0.05),
        "cmp_pe_k": nrm(ks[9], (DEPTH, CMP_BLOCK, HEAD_DIM), 0.1),
        "cmp_pe_v": nrm(ks[10], (DEPTH, CMP_BLOCK, HEAD_DIM), 0.1),
        "cmp_k_w1": nrm(ks[11], (DEPTH, CMP_BLOCK * HEAD_DIM, CMP_HIDDEN), (CMP_BLOCK * HEAD_DIM) ** -0.5),
        "cmp_k_w2": nrm(ks[12], (DEPTH, CMP_HIDDEN, HEAD_DIM), CMP_HIDDEN ** -0.5),
        "cmp_v_w1": nrm(ks[13], (DEPTH, CMP_BLOCK * HEAD_DIM, CMP_HIDDEN), (CMP_BLOCK * HEAD_DIM) ** -0.5),
        "cmp_v_w2": nrm(ks[14], (DEPTH, CMP_HIDDEN, HEAD_DIM), CMP_HIDDEN ** -0.5),
        "w_out": nrm(ks[15], (DEPTH, D_MIX, D_MODEL), D_MIX ** -0.5 * DN_BETA),
        "ln1_g": 1.0 + nrm(ks[16], (DEPTH, D_MODEL), 0.02),
        "ln1_b": nrm(ks[17], (DEPTH, D_MODEL), 0.01),
        "ln2_g": 1.0 + nrm(ks[18], (DEPTH, D_MODEL), 0.02),
        "ln2_b": nrm(ks[19], (DEPTH, D_MODEL), 0.01),
        "router_w": nrm(ks[20], (D_MODEL, N_EXPERTS), D_MODEL ** -0.5),
        "router_bias": nrm(ks[21], (N_EXPERTS,), 0.01),
        "moe_w_gate": nrm(ks[22], (DEPTH, N_EXPERTS, D_MODEL, D_EXPERT), D_MODEL ** -0.5),
        "moe_w_up": nrm(ks[23], (DEPTH, N_EXPERTS, D_MODEL, D_EXPERT), D_MODEL ** -0.5),
        "moe_w_down": nrm(ks[24], (DEPTH, N_EXPERTS, D_EXPERT, D_MODEL), D_EXPERT ** -0.5 * DN_BETA),
    }


def reference(x, w_in, conv_a_w, conv_b_w, conv_b_b, cf_ln_g, cf_ln_b, pool_w, pool_scale,
              cmp_pe_k, cmp_pe_v, cmp_k_w1, cmp_k_w2, cmp_v_w1, cmp_v_w2, w_out,
              ln1_g, ln1_b, ln2_g, ln2_b, router_w, router_bias, moe_w_gate, moe_w_up, moe_w_down):
    cos, sin = rope_tables(x.shape[1])
    for l in range(DEPTH):
        x = hybrid_layer(x, w_in[l], conv_a_w[l], conv_b_w[l], conv_b_b[l], cf_ln_g[l], cf_ln_b[l],
                         pool_w[l], pool_scale[l], cmp_pe_k[l], cmp_pe_v[l], cmp_k_w1[l], cmp_k_w2[l],
                         cmp_v_w1[l], cmp_v_w2[l], w_out[l], ln1_g[l], ln1_b[l], ln2_g[l], ln2_b[l],
                         router_w, router_bias, moe_w_gate[l], moe_w_up[l], moe_w_down[l], cos, sin)
    return x
```

```python
import functools

import jax
import jax.numpy as jnp
import numpy as np
from jax import lax
from jax.experimental import pallas as pl
from jax.experimental.pallas import tpu as pltpu

F32 = jnp.float32
BF16 = jnp.bfloat16

D_MODEL = 1024
DEPTH = 2
GROUP_W = 256
HEAD_DIM = 64
SC_KERNEL = 3
CF_KERNEL = 31
POOL_WINDOWS = (2, 4, 8, 16)
POOL_GW = 64
NSA_HEADS = 4
CMP_BLOCK = 32
CMP_STRIDE = 16
CMP_HIDDEN = 256
SLC_BLOCK = 64
SLC_TOPN = 16
WIN = 512
N_EXPERTS = 16
N_GROUPS = 4
EXPERTS_PER_GROUP = 4
D_EXPERT = 512
DN_ALPHA = (2 * DEPTH) ** 0.25
LN_EPS = 1e-5
NEG = -1e30
FORCE = 1e4
ROPE_THETA = 10000.0
QK_SCALE = HEAD_DIM ** -0.5

LANES = 128
OFF_A, OFF_B, OFF_C, OFF_Q, OFF_KCVC, OFF_KK, OFF_VV, OFF_G, IN_PAD = 0, 768, 1280, 1536, 1792, 1920, 2048, 2176, 2304

HALO_A = 8
HALO_B = 32
HALO_C = 16

TS_IN = 512
TQ = 128
KT = 512
TS_OUT = 512
TM_MOE = 512
VMEM_LIMIT = 56 * 1024 * 1024


def _sigmoid(x):
    return jax.nn.sigmoid(x)


def _layer_norm(h, g, b):
    mu = jnp.mean(h, axis=-1, keepdims=True)
    d = h - mu
    var = jnp.mean(d * d, axis=-1, keepdims=True)
    return d * lax.rsqrt(var + LN_EPS) * g + b


def _dot(a, b):
    return jnp.dot(a, b, preferred_element_type=F32)


def _inproj_kernel(x_ref, w_ref, cos_ref, sin_ref, cva_ref, cvb_ref, cvbb_ref, lng_ref, lnb_ref,
                   poolw_ref, pools_ref,
                   yabc_ref, qt_ref, qrt_ref, kcvc_ref, kk_ref, vvt_ref, gt_ref,
                   ch_ext, u_ext, p_ext):
    j = pl.program_id(1)
    ts = x_ref.shape[1]
    xb = x_ref[0].astype(BF16)

    def proj(lo, hi):
        return _dot(xb, w_ref[:, lo:hi])

    def carry_halo(ext, halo):
        @pl.when(j == 0)
        def _():
            ext[0:halo, :] = jnp.zeros((halo, ext.shape[1]), F32)

        @pl.when(j > 0)
        def _():
            ext[0:halo, :] = ext[ts:ts + halo, :]

    pa = proj(OFF_A, OFF_B)
    a_b, ch = pa[:, 0:256], pa[:, 256:512] * pa[:, 512:768]
    carry_halo(ch_ext, HALO_A)
    ch_ext[HALO_A:HALO_A + ts, :] = ch
    conv = cva_ref[2:3, :] * ch
    for k in range(SC_KERNEL - 1):
        off = HALO_A - (SC_KERNEL - 1) + k
        conv = conv + cva_ref[k:k + 1, :] * ch_ext[off:off + ts, :]
    yabc_ref[0, :, 0:256] = (a_b * conv).astype(BF16)

    pb = proj(OFF_B, OFF_C)
    u = pb[:, 0:256] * _sigmoid(pb[:, 256:512])
    carry_halo(u_ext, HALO_B)
    u_ext[HALO_B:HALO_B + ts, :] = u
    rows = 64
    for c in range(ts // rows):
        acc = jnp.zeros((rows, 256), F32) + cvbb_ref[...]
        for k in range(CF_KERNEL):
            off = HALO_B - (CF_KERNEL - 1) + k + c * rows
            acc = acc + cvb_ref[k:k + 1, :] * u_ext[off:off + rows, :]
        v = _layer_norm(acc, lng_ref[...], lnb_ref[...])
        yabc_ref[0, c * rows:(c + 1) * rows, 256:512] = (v * _sigmoid(v)).astype(BF16)

    pc = proj(OFF_C, OFF_Q)
    carry_halo(p_ext, HALO_C)
    p_ext[HALO_C:HALO_C + ts, :] = pc
    pe = p_ext[...]
    s2 = pe + pltpu.roll(pe, 1, 0)
    s4 = s2 + pltpu.roll(s2, 2, 0)
    s8 = s4 + pltpu.roll(s4, 4, 0)
    s16 = s8 + pltpu.roll(s8, 8, 0)
    lane = lax.broadcasted_iota(jnp.int32, (ts, 256), 1)
    grp = jnp.right_shift(lane, 6)
    wsum = jnp.where(grp == 0, s2[HALO_C:], jnp.where(grp == 1, s4[HALO_C:], jnp.where(grp == 2, s8[HALO_C:], s16[HALO_C:])))
    win = jnp.where(grp == 0, 2, jnp.where(grp == 1, 4, jnp.where(grp == 2, 8, 16)))
    t1 = j * ts + lax.broadcasted_iota(jnp.int32, (ts, 256), 0) + 1
    cnt = jnp.minimum(win, t1).astype(F32)
    dlt = wsum / cnt - pc
    yabc_ref[0, :, 512:768] = (_dot(dlt.astype(BF16), poolw_ref[...]) * pools_ref[...]).astype(BF16)

    cos4, sin4 = cos_ref[...], sin_ref[...]
    q = proj(OFF_Q, OFF_KCVC)
    lane_q = jnp.bitwise_and(lax.broadcasted_iota(jnp.int32, (ts, 256), 1), HEAD_DIM - 1)
    q_sw = jnp.where(lane_q < HEAD_DIM // 2, pltpu.roll(q, 256 - HEAD_DIM // 2, 1), pltpu.roll(q, HEAD_DIM // 2, 1))
    qr = q * cos4 + q_sw * sin4
    qt_ref[0] = (q * QK_SCALE).T.astype(BF16)
    qrt_ref[0] = (qr * QK_SCALE).T.astype(BF16)

    kcvc_ref[0] = proj(OFF_KCVC, OFF_KK)

    k2 = proj(OFF_KK, OFF_VV)
    lane_k = jnp.bitwise_and(lax.broadcasted_iota(jnp.int32, (ts, LANES), 1), HEAD_DIM - 1)
    k_sw = jnp.where(lane_k < HEAD_DIM // 2, pltpu.roll(k2, LANES - HEAD_DIM // 2, 1), pltpu.roll(k2, HEAD_DIM // 2, 1))
    kk_ref[0] = (k2 * cos4[:, 0:LANES] + k_sw * sin4[:, 0:LANES]).astype(BF16)

    v2t = proj(OFF_VV, OFF_G).T
    for c in range(ts // LANES):
        vvt_ref[0, c] = v2t[:, c * LANES:(c + 1) * LANES].astype(BF16)

    gt_ref[0] = _sigmoid(proj(OFF_G, IN_PAD)).T[0:16, :]


def _inproj(x, w_in_p, cos4, sin4, cva, cvb, cvbb, lng, lnb, poolw, pools):
    b, s, d = x.shape
    ts = TS_IN
    grid = (b, s // ts)
    const = lambda shape: pl.BlockSpec(shape, lambda bi, ji: (0,) * len(shape))
    out_shape = (
        jax.ShapeDtypeStruct((b, s, 768), BF16),
        jax.ShapeDtypeStruct((b, 256, s), BF16),
        jax.ShapeDtypeStruct((b, 256, s), BF16),
        jax.ShapeDtypeStruct((b, s, LANES), F32),
        jax.ShapeDtypeStruct((b, s, LANES), BF16),
        jax.ShapeDtypeStruct((b, s // LANES, LANES, LANES), BF16),
        jax.ShapeDtypeStruct((b, 16, s), F32),
    )
    return pl.pallas_call(
        _inproj_kernel,
        grid=grid,
        in_specs=[
            pl.BlockSpec((1, ts, d), lambda bi, ji: (bi, ji, 0)),
            const((d, IN_PAD)),
            pl.BlockSpec((ts, 256), lambda bi, ji: (ji, 0)),
            pl.BlockSpec((ts, 256), lambda bi, ji: (ji, 0)),
            const((SC_KERNEL, 256)), const((CF_KERNEL, 256)), const((1, 256)), const((1, 256)), const((1, 256)),
            const((256, 256)), const((1, 256)),
        ],
        out_specs=(
            pl.BlockSpec((1, ts, 768), lambda bi, ji: (bi, ji, 0)),
            pl.BlockSpec((1, 256, ts), lambda bi, ji: (bi, 0, ji)),
            pl.BlockSpec((1, 256, ts), lambda bi, ji: (bi, 0, ji)),
            pl.BlockSpec((1, ts, LANES), lambda bi, ji: (bi, ji, 0)),
            pl.BlockSpec((1, ts, LANES), lambda bi, ji: (bi, ji, 0)),
            pl.BlockSpec((1, ts // LANES, LANES, LANES), lambda bi, ji: (bi, ji, 0, 0)),
            pl.BlockSpec((1, 16, ts), lambda bi, ji: (bi, 0, ji)),
        ),
        out_shape=out_shape,
        scratch_shapes=[
            pltpu.VMEM((ts + HALO_A, 256), F32),
            pltpu.VMEM((ts + HALO_B, 256), F32),
            pltpu.VMEM((ts + HALO_C, 256), F32),
        ],
        compiler_params=pltpu.CompilerParams(dimension_semantics=("arbitrary", "arbitrary"),
                                             vmem_limit_bytes=VMEM_LIMIT),
        name="inproj_mixers",
    )(x, w_in_p, cos4, sin4, cva, cvb, cvbb, lng, lnb, poolw, pools)


def _compress_kernel(kc_ref, vc_ref, pek_ref, pev_ref, wk1_ref, wk2_ref, wv1_ref, wv2_ref, kcmp_ref, vcmpt_ref):
    def comp(x_ref, pe_ref, w1_ref, w2_ref):
        x = x_ref[0]
        m = x.shape[0]
        half = x.shape[1]
        first = _dot((x + pe_ref[0:1, :]).astype(BF16), w1_ref[0:half, :])
        second = _dot((x + pe_ref[1:2, :]).astype(BF16), w1_ref[half:2 * half, :])
        hid = first + pltpu.roll(second, m - 1, 0)
        return _dot((hid * _sigmoid(hid)).astype(BF16), w2_ref[...])

    kcmp_ref[0] = comp(kc_ref, pek_ref, wk1_ref, wk2_ref).astype(BF16)
    vcmpt_ref[0] = comp(vc_ref, pev_ref, wv1_ref, wv2_ref).T[0:HEAD_DIM, :].astype(BF16)


def _compress(kc16, vc16, pek, pev, wk1, wk2, wv1, wv2):
    b, m, w = kc16.shape
    const = lambda shape: pl.BlockSpec(shape, lambda bi: (0,) * len(shape))
    return pl.pallas_call(
        _compress_kernel,
        grid=(b,),
        in_specs=[
            pl.BlockSpec((1, m, w), lambda bi: (bi, 0, 0)),
            pl.BlockSpec((1, m, w), lambda bi: (bi, 0, 0)),
            const((2, w)), const((2, w)),
            const((2 * w, CMP_HIDDEN)), const((CMP_HIDDEN, LANES)),
            const((2 * w, CMP_HIDDEN)), const((CMP_HIDDEN, LANES)),
        ],
        out_specs=(
            pl.BlockSpec((1, m, LANES), lambda bi: (bi, 0, 0)),
            pl.BlockSpec((1, HEAD_DIM, m), lambda bi: (bi, 0, 0)),
        ),
        out_shape=(jax.ShapeDtypeStruct((b, m, LANES), BF16), jax.ShapeDtypeStruct((b, HEAD_DIM, m), BF16)),
        compiler_params=pltpu.CompilerParams(dimension_semantics=("arbitrary",), vmem_limit_bytes=VMEM_LIMIT),
        name="compress_kv",
    )(kc16, vc16, pek, pev, wk1, wk2, wv1, wv2)


def _nsa_kernel(qt_ref, qrt_ref, gt_ref, kk_ref, vvt_ref, kcmp_ref, vcmpt_ref, ovt_ref, yd_ref, sel_scr):
    qb = pl.program_id(1)
    tq = qt_ref.shape[2]
    nh = NSA_HEADS
    wq = nh * tq
    s0 = qb * tq
    n_slc = sel_scr.shape[0]
    n_cmp_pad = kcmp_ref.shape[1]

    def stack_heads(ref):
        x = ref[0]
        return jnp.concatenate([x[h * HEAD_DIM:(h + 1) * HEAD_DIM, :] for h in range(nh)], axis=1)

    zeros = jnp.zeros((HEAD_DIM, wq), BF16)
    q_top, qr_top = stack_heads(qt_ref), stack_heads(qrt_ref)
    q_lo = jnp.concatenate([q_top, zeros], axis=0)
    qr_lo = jnp.concatenate([qr_top, zeros], axis=0)
    qr_hi = jnp.concatenate([zeros, qr_top], axis=0)
    t_row = s0 + lax.broadcasted_iota(jnp.int32, (1, tq), 1)
    lanes4 = lambda x: jnp.concatenate([x] * nh, axis=1)
    t_all = lanes4(t_row)

    sc = _dot(kcmp_ref[0], q_lo)
    cmp_end = lax.broadcasted_iota(jnp.int32, (n_cmp_pad, wq), 0) * CMP_STRIDE + (CMP_BLOCK - 1)
    sc = jnp.where(cmp_end <= t_all, sc, NEG)
    mx = jnp.max(sc, axis=0, keepdims=True)
    ex = jnp.exp(sc - mx)
    den = jnp.sum(ex, axis=0, keepdims=True)
    p_cmp = jnp.where(t_all >= CMP_BLOCK - 1, ex * (1.0 / den), 0.0)
    o_cmp = _dot(vcmpt_ref[0], p_cmp.astype(BF16))
    p_heads = p_cmp[:, 0:tq]
    for h in range(1, nh):
        p_heads = p_heads + p_cmp[:, h * tq:(h + 1) * tq]
    imp = _dot(ovt_ref[...], p_heads.astype(BF16))

    blk_i = lax.broadcasted_iota(jnp.int32, (n_slc, tq), 0)
    blk = blk_i.astype(F32)
    cur = jnp.right_shift(t_row, 6)
    forced = (blk_i == 0) | (blk_i == cur) | (blk_i == cur - 1)
    valid = blk_i * SLC_BLOCK <= t_row
    imp = jnp.where(valid, jnp.where(forced, FORCE, imp), NEG)
    sel = jnp.zeros((n_slc, tq), F32)
    for _ in range(min(SLC_TOPN, n_slc)):
        top = jnp.max(imp, axis=0, keepdims=True)
        first = jnp.min(jnp.where(imp == top, blk, float(n_slc)), axis=0, keepdims=True)
        pick = blk == first
        sel = jnp.where(pick, 1.0, sel)
        imp = jnp.where(pick, -jnp.inf, imp)
    sel_scr[...] = jnp.where(valid, sel, 0.0)

    blocks_per_tile = KT // SLC_BLOCK
    sub = KT // LANES

    def slc_body(kt, carry):
        m_i, l_i, acc = carry
        k0 = pl.multiple_of(kt * KT, KT)
        s = _dot(kk_ref[0, pl.ds(k0, KT), :], qr_lo)
        sel8 = sel_scr[pl.ds(pl.multiple_of(kt * blocks_per_tile, blocks_per_tile), blocks_per_tile), :]
        selx = jnp.concatenate([jnp.broadcast_to(sel8[r:r + 1, :], (SLC_BLOCK, tq)) for r in range(blocks_per_tile)], axis=0)
        key = k0 + lax.broadcasted_iota(jnp.int32, (KT, wq), 0)
        mask = (lanes4(selx) > 0.0) & (key <= t_all)
        s = jnp.where(mask, s, NEG)
        m_new = jnp.maximum(m_i, jnp.max(s, axis=0, keepdims=True))
        alpha = jnp.exp(m_i - m_new)
        p = jnp.where(mask, jnp.exp(s - m_new), 0.0)
        l_new = alpha * l_i + jnp.sum(p, axis=0, keepdims=True)
        pb = p.astype(BF16)
        vt = vvt_ref[0, pl.ds(pl.multiple_of(kt * sub, sub), sub)]
        pv = _dot(vt[0, 0:HEAD_DIM, :], pb[0:LANES, :])
        for i in range(1, sub):
            pv = pv + _dot(vt[i, 0:HEAD_DIM, :], pb[i * LANES:(i + 1) * LANES, :])
        return m_new, l_new, alpha * acc + pv

    init = (jnp.full((1, wq), NEG, F32), jnp.zeros((1, wq), F32), jnp.zeros((HEAD_DIM, wq), F32))
    n_tiles = (s0 + tq + KT - 1) // KT
    _, l_s, acc_s = lax.fori_loop(0, n_tiles, slc_body, init)

    def win_body(w, carry):
        m_i, l_i, acc = carry
        k0 = pl.multiple_of(w * LANES, LANES)
        s = _dot(kk_ref[0, pl.ds(k0, LANES), :], qr_hi)
        key = k0 + lax.broadcasted_iota(jnp.int32, (LANES, wq), 0)
        mask = (key <= t_all) & (key > t_all - WIN)
        s = jnp.where(mask, s, NEG)
        m_new = jnp.maximum(m_i, jnp.max(s, axis=0, keepdims=True))
        alpha = jnp.exp(m_i - m_new)
        p = jnp.where(mask, jnp.exp(s - m_new), 0.0)
        l_new = alpha * l_i + jnp.sum(p, axis=0, keepdims=True)
        pv = _dot(vvt_ref[0, w][HEAD_DIM:2 * HEAD_DIM, :], p.astype(BF16))
        return m_new, l_new, alpha * acc + pv

    w_hi = (s0 + tq) // LANES
    w_lo = jnp.maximum(s0 - WIN, 0) // LANES
    _, l_w, acc_w = lax.fori_loop(w_lo, w_hi, win_body, init)

    g = gt_ref[0]
    gate = lambda br: jnp.concatenate([g[h * 3 + br:h * 3 + br + 1, :] for h in range(nh)], axis=1)
    o = gate(0) * o_cmp + gate(1) * (acc_s * (1.0 / l_s)) + gate(2) * (acc_w * (1.0 / l_w))
    o_rows = jnp.concatenate([o[:, h * tq:(h + 1) * tq] for h in range(nh)], axis=0)
    yd_ref[0] = o_rows.T.astype(BF16)


def _nsa(qt, qrt, gt, kk, vvt, kcmp, vcmpt, ovt):
    b, _, s = qt.shape
    n_slc = s // SLC_BLOCK
    m = kcmp.shape[1]
    return pl.pallas_call(
        _nsa_kernel,
        grid=(b, s // TQ),
        in_specs=[
            pl.BlockSpec((1, 256, TQ), lambda bi, qi: (bi, 0, qi)),
            pl.BlockSpec((1, 256, TQ), lambda bi, qi: (bi, 0, qi)),
            pl.BlockSpec((1, 16, TQ), lambda bi, qi: (bi, 0, qi)),
            pl.BlockSpec((1, s, LANES), lambda bi, qi: (bi, 0, 0)),
            pl.BlockSpec((1, s // LANES, LANES, LANES), lambda bi, qi: (bi, 0, 0, 0)),
            pl.BlockSpec((1, m, LANES), lambda bi, qi: (bi, 0, 0)),
            pl.BlockSpec((1, HEAD_DIM, m), lambda bi, qi: (bi, 0, 0)),
            pl.BlockSpec((n_slc, m), lambda bi, qi: (0, 0)),
        ],
        out_specs=pl.BlockSpec((1, TQ, 256), lambda bi, qi: (bi, qi, 0)),
        out_shape=jax.ShapeDtypeStruct((b, s, 256), BF16),
        scratch_shapes=[pltpu.VMEM((n_slc, TQ), F32)],
        compiler_params=pltpu.CompilerParams(dimension_semantics=("arbitrary", "arbitrary"),
                                             vmem_limit_bytes=VMEM_LIMIT),
        name="sparse_attention",
    )(qt, qrt, gt, kk, vvt, kcmp, vcmpt, ovt)


def _outproj_kernel(x_ref, yabc_ref, yd_ref, wo_ref, g_ref, b_ref, rw_ref, rb_ref, x1_ref, gates_ref):
    mix = _dot(yabc_ref[...], wo_ref[0:768, :]) + _dot(yd_ref[...], wo_ref[768:1024, :])
    x1 = _layer_norm(DN_ALPHA * x_ref[...] + mix, g_ref[...], b_ref[...])
    x1_ref[...] = x1

    ts = x1.shape[0]
    aff = _sigmoid(_dot(x1.astype(BF16), rw_ref[...])).T[0:N_EXPERTS, :]
    biased = aff + rb_ref[...]
    row = lambda a, i: a[i:i + 1, :]

    gscores = []
    for gi in range(N_GROUPS):
        a, b, c, d = (row(biased, gi * EXPERTS_PER_GROUP + i) for i in range(EXPERTS_PER_GROUP))
        hi1, lo1, hi2, lo2 = jnp.maximum(a, b), jnp.minimum(a, b), jnp.maximum(c, d), jnp.minimum(c, d)
        gscores.append(jnp.maximum(hi1, hi2) + jnp.maximum(jnp.minimum(hi1, hi2), jnp.maximum(lo1, lo2)))
    best, gsel = gscores[0], jnp.zeros((1, ts), jnp.int32)
    for gi in range(1, N_GROUPS):
        better = gscores[gi] > best
        gsel = jnp.where(better, gi, gsel)
        best = jnp.where(better, gscores[gi], best)

    eid_i = lax.broadcasted_iota(jnp.int32, (N_EXPERTS, ts), 0)
    eid = eid_i.astype(F32)
    masked = jnp.where(jnp.right_shift(eid_i, 2) == gsel, biased, NEG)
    picks = []
    for _ in range(2):
        top = jnp.max(masked, axis=0, keepdims=True)
        first = jnp.min(jnp.where(masked == top, eid, float(N_EXPERTS)), axis=0, keepdims=True)
        pick = eid == first
        picks.append(pick)
        masked = jnp.where(pick, -jnp.inf, masked)
    chosen = picks[0] | picks[1]
    w_sel = jnp.where(chosen, aff, 0.0)
    gates_t = w_sel / jnp.sum(w_sel, axis=0, keepdims=True)
    gates_ref[...] = jnp.concatenate([gates_t, jnp.zeros((LANES - N_EXPERTS, ts), F32)], axis=0).T


def _outproj(x2d, yabc, yd, wo, g, b, rw, rb):
    n, d = x2d.shape
    ts = TS_OUT
    const = lambda shape: pl.BlockSpec(shape, lambda i: (0,) * len(shape))
    return pl.pallas_call(
        _outproj_kernel,
        grid=(n // ts,),
        in_specs=[
            pl.BlockSpec((ts, d), lambda i: (i, 0)),
            pl.BlockSpec((ts, 768), lambda i: (i, 0)),
            pl.BlockSpec((ts, 256), lambda i: (i, 0)),
            const((d, d)), const((1, d)), const((1, d)), const((d, LANES)), const((N_EXPERTS, 1)),
        ],
        out_specs=(pl.BlockSpec((ts, d), lambda i: (i, 0)), pl.BlockSpec((ts, LANES), lambda i: (i, 0))),
        out_shape=(jax.ShapeDtypeStruct((n, d), F32), jax.ShapeDtypeStruct((n, LANES), F32)),
        compiler_params=pltpu.CompilerParams(dimension_semantics=("arbitrary",), vmem_limit_bytes=VMEM_LIMIT),
        name="outproj_ln_route",
    )(x2d, yabc, yd, wo, g, b, rw, rb)


def _moe_kernel(x1_ref, gates_ref, wg_ref, wu_ref, wd_ref, g_ref, b_ref, out_ref, xb_scr, acc_scr):
    e = pl.program_id(1)

    @pl.when(e == 0)
    def _():
        xb_scr[...] = x1_ref[...].astype(BF16)
        acc_scr[...] = jnp.zeros(acc_scr.shape, F32)

    xb = xb_scr[...]
    hg = _dot(xb, wg_ref[0])
    h = hg * _sigmoid(hg) * _dot(xb, wu_ref[0])
    y = _dot(h.astype(BF16), wd_ref[0])
    gates = gates_ref[...]
    lane = lax.broadcasted_iota(jnp.int32, gates.shape, 1)
    gcol = jnp.sum(jnp.where(lane == e, gates, 0.0), axis=1, keepdims=True)
    acc_scr[...] += y * gcol

    @pl.when(e == pl.num_programs(1) - 1)
    def _():
        out_ref[...] = _layer_norm(DN_ALPHA * x1_ref[...] + acc_scr[...], g_ref[...], b_ref[...])


def _moe(x1, gates, wg, wu, wd, g, b):
    n, d = x1.shape
    tm = TM_MOE
    ne, _, de = wg.shape
    const = lambda shape: pl.BlockSpec(shape, lambda i, e: (0,) * len(shape))
    return pl.pallas_call(
        _moe_kernel,
        grid=(n // tm, ne),
        in_specs=[
            pl.BlockSpec((tm, d), lambda i, e: (i, 0)),
            pl.BlockSpec((tm, LANES), lambda i, e: (i, 0)),
            pl.BlockSpec((1, d, de), lambda i, e: (e, 0, 0)),
            pl.BlockSpec((1, d, de), lambda i, e: (e, 0, 0)),
            pl.BlockSpec((1, de, d), lambda i, e: (e, 0, 0)),
            const((1, d)), const((1, d)),
        ],
        out_specs=pl.BlockSpec((tm, d), lambda i, e: (i, 0)),
        out_shape=jax.ShapeDtypeStruct((n, d), F32),
        scratch_shapes=[pltpu.VMEM((tm, d), BF16), pltpu.VMEM((tm, d), F32)],
        compiler_params=pltpu.CompilerParams(dimension_semantics=("arbitrary", "arbitrary"),
                                             vmem_limit_bytes=VMEM_LIMIT),
        name="moe_ln",
    )(x1, gates, wg, wu, wd, g, b)


def _permute_w_in(w):
    sizes = (256, 256, 256, 256, 256, 256, 256, 64, 64, 64, 64, 64, 64, 12)
    offs = np.concatenate([[0], np.cumsum(sizes)])
    part = lambda i: w[:, int(offs[i]):int(offs[i + 1])]
    a_b, a_c, a_h, b_val, b_gate, c_p, q, kc, vc, ks, vs, kw, vw, g = (part(i) for i in range(14))
    pad = jnp.zeros((w.shape[0], LANES - g.shape[1]), w.dtype)
    return jnp.concatenate([a_b, a_c, a_h, b_val, b_gate, c_p, q, kc, vc, ks, kw, vs, vw, g, pad], axis=1).astype(BF16)


def _rope_tables(s):
    inv = 1.0 / (ROPE_THETA ** (jnp.arange(0, HEAD_DIM, 2, dtype=F32) / HEAD_DIM))
    ang = jnp.arange(s, dtype=F32)[:, None] * inv[None, :]
    ang = jnp.concatenate([ang, ang], -1)
    sign = jnp.concatenate([-jnp.ones((HEAD_DIM // 2,), F32), jnp.ones((HEAD_DIM // 2,), F32)])
    return jnp.tile(jnp.cos(ang), (1, NSA_HEADS)), jnp.tile(jnp.sin(ang) * sign[None, :], (1, NSA_HEADS))


def _overlap_t(s, m):
    n_cmp = (s - CMP_BLOCK) // CMP_STRIDE + 1
    cmp_start = np.arange(m) * CMP_STRIDE
    slc_start = np.arange(s // SLC_BLOCK) * SLC_BLOCK
    ov = (cmp_start[None, :] < slc_start[:, None] + SLC_BLOCK) & (cmp_start[None, :] + CMP_BLOCK > slc_start[:, None])
    ov = ov & (np.arange(m)[None, :] < n_cmp)
    return jnp.asarray(ov, BF16)


def _block_diag(w):
    g, c, _ = w.shape
    out = jnp.zeros((g * c, g * c), w.dtype)
    for i in range(g):
        out = out.at[i * c:(i + 1) * c, i * c:(i + 1) * c].set(w[i])
    return out


def kernel(x, w_in, conv_a_w, conv_b_w, conv_b_b, cf_ln_g, cf_ln_b, pool_w, pool_scale, cmp_pe_k, cmp_pe_v, cmp_k_w1, cmp_k_w2, cmp_v_w1, cmp_v_w2, w_out, ln1_g, ln1_b, ln2_g, ln2_b, router_w, router_bias, moe_w_gate, moe_w_up, moe_w_down):
    b, s, d = x.shape
    n = b * s
    m = s // CMP_STRIDE
    tok_per_row = CMP_STRIDE
    cos4, sin4 = _rope_tables(s)
    ovt = _overlap_t(s, m)
    rw = jnp.concatenate([router_w, jnp.zeros((d, LANES - N_EXPERTS), router_w.dtype)], axis=1).astype(BF16)
    rb = router_bias.reshape(N_EXPERTS, 1).astype(F32)
    pad_w2 = lambda w: jnp.concatenate([w, jnp.zeros((CMP_HIDDEN, LANES - HEAD_DIM), w.dtype)], axis=1).astype(BF16)
    row = lambda v: v.reshape(1, -1)

    for l in range(DEPTH):
        yabc, qt, qrt, kcvc, kk, vvt, gt = _inproj(
            x, _permute_w_in(w_in[l]), cos4, sin4, conv_a_w[l], conv_b_w[l], row(conv_b_b[l]),
            row(cf_ln_g[l]), row(cf_ln_b[l]), _block_diag(pool_w[l]).astype(BF16), row(pool_scale[l]))
        kc16 = kcvc[:, :, 0:HEAD_DIM].reshape(b, m, tok_per_row * HEAD_DIM)
        vc16 = kcvc[:, :, HEAD_DIM:2 * HEAD_DIM].reshape(b, m, tok_per_row * HEAD_DIM)
        kcmp, vcmpt = _compress(
            kc16, vc16, cmp_pe_k[l].reshape(2, tok_per_row * HEAD_DIM), cmp_pe_v[l].reshape(2, tok_per_row * HEAD_DIM),
            cmp_k_w1[l].astype(BF16), pad_w2(cmp_k_w2[l]), cmp_v_w1[l].astype(BF16), pad_w2(cmp_v_w2[l]))
        yd = _nsa(qt, qrt, gt, kk, vvt, kcmp, vcmpt, ovt)
        x1, gates = _outproj(x.reshape(n, d), yabc.reshape(n, 768), yd.reshape(n, 256), w_out[l].astype(BF16),
                             row(ln1_g[l]), row(ln1_b[l]), rw, rb)
        x = _moe(x1, gates, moe_w_gate[l].astype(BF16), moe_w_up[l].astype(BF16), moe_w_down[l].astype(BF16),
                 row(ln2_g[l]), row(ln2_b[l])).reshape(b, s, d)
    return x
```

```python
import functools

import jax
import jax.numpy as jnp
import numpy as np
from jax import lax
from jax.experimental import pallas as pl
from jax.experimental.pallas import tpu as pltpu

F32 = jnp.float32
BF16 = jnp.bfloat16

D_MODEL = 1024
DEPTH = 2
GROUP_W = 256
HEAD_DIM = 64
SC_KERNEL = 3
CF_KERNEL = 31
POOL_WINDOWS = (2, 4, 8, 16)
POOL_GW = 64
NSA_HEADS = 4
CMP_BLOCK = 32
CMP_STRIDE = 16
CMP_HIDDEN = 256
SLC_BLOCK = 64
SLC_TOPN = 16
WIN = 512
N_EXPERTS = 16
N_GROUPS = 4
EXPERTS_PER_GROUP = 4
D_EXPERT = 512
DN_ALPHA = (2 * DEPTH) ** 0.25
LN_EPS = 1e-5
NEG = -1e30
FORCE = 1e4
ROPE_THETA = 10000.0
QK_SCALE = HEAD_DIM ** -0.5

LANES = 128
OFF_A, OFF_B, OFF_C, OFF_Q, OFF_KCVC, OFF_KK, OFF_VV, OFF_G, IN_PAD = 0, 768, 1280, 1536, 1792, 1920, 2048, 2176, 2304

HALO_A = 8
HALO_B = 32
HALO_C = 16

TS_IN = 512
TQ = 128
KT = 512
TS_OUT = 512
TM_MOE = 512
VMEM_LIMIT = 56 * 1024 * 1024


def _sigmoid(x):
    return jax.nn.sigmoid(x)


def _layer_norm(h, g, b):
    mu = jnp.mean(h, axis=-1, keepdims=True)
    d = h - mu
    var = jnp.mean(d * d, axis=-1, keepdims=True)
    return d * lax.rsqrt(var + LN_EPS) * g + b


def _dot(a, b):
    return jnp.dot(a, b, preferred_element_type=F32)


def _inproj_kernel(x_ref, w_ref, cos_ref, sin_ref, cva_ref, cvb_ref, cvbb_ref, lng_ref, lnb_ref,
                   poolw_ref, pools_ref,
                   yabc_ref, qt_ref, qrt_ref, kcvc_ref, kk_ref, vvt_ref, gt_ref,
                   ch_ext, u_ext, p_ext):
    j = pl.program_id(1)
    ts = x_ref.shape[1]
    xb = x_ref[0].astype(BF16)

    def proj(lo, hi):
        return _dot(xb, w_ref[:, lo:hi])

    def carry_halo(ext, halo):
        @pl.when(j == 0)
        def _():
            ext[0:halo, :] = jnp.zeros((halo, ext.shape[1]), F32)

        @pl.when(j > 0)
        def _():
            ext[0:halo, :] = ext[ts:ts + halo, :]

    pa = proj(OFF_A, OFF_B)
    a_b, ch = pa[:, 0:256], pa[:, 256:512] * pa[:, 512:768]
    carry_halo(ch_ext, HALO_A)
    ch_ext[HALO_A:HALO_A + ts, :] = ch
    conv = cva_ref[2:3, :] * ch
    for k in range(SC_KERNEL - 1):
        off = HALO_A - (SC_KERNEL - 1) + k
        conv = conv + cva_ref[k:k + 1, :] * ch_ext[off:off + ts, :]
    yabc_ref[0, :, 0:256] = (a_b * conv).astype(BF16)

    pb = proj(OFF_B, OFF_C)
    u = pb[:, 0:256] * _sigmoid(pb[:, 256:512])
    carry_halo(u_ext, HALO_B)
    u_ext[HALO_B:HALO_B + ts, :] = u
    rows = 64
    for c in range(ts // rows):
        acc = jnp.zeros((rows, 256), F32) + cvbb_ref[...]
        for k in range(CF_KERNEL):
            off = HALO_B - (CF_KERNEL - 1) + k + c * rows
            acc = acc + cvb_ref[k:k + 1, :] * u_ext[off:off + rows, :]
        v = _layer_norm(acc, lng_ref[...], lnb_ref[...])
        yabc_ref[0, c * rows:(c + 1) * rows, 256:512] = (v * _sigmoid(v)).astype(BF16)

    pc = proj(OFF_C, OFF_Q)
    carry_halo(p_ext, HALO_C)
    p_ext[HALO_C:HALO_C + ts, :] = pc
    pe = p_ext[...]
    s2 = pe + pltpu.roll(pe, 1, 0)
    s4 = s2 + pltpu.roll(s2, 2, 0)
    s8 = s4 + pltpu.roll(s4, 4, 0)
    s16 = s8 + pltpu.roll(s8, 8, 0)
    lane = lax.broadcasted_iota(jnp.int32, (ts, 256), 1)
    grp = jnp.right_shift(lane, 6)
    wsum = jnp.where(grp == 0, s2[HALO_C:], jnp.where(grp == 1, s4[HALO_C:], jnp.where(grp == 2, s8[HALO_C:], s16[HALO_C:])))
    win = jnp.where(grp == 0, 2, jnp.where(grp == 1, 4, jnp.where(grp == 2, 8, 16)))
    t1 = j * ts + lax.broadcasted_iota(jnp.int32, (ts, 256), 0) + 1
    cnt = jnp.minimum(win, t1).astype(F32)
    dlt = wsum / cnt - pc
    yabc_ref[0, :, 512:768] = (_dot(dlt.astype(BF16), poolw_ref[...]) * pools_ref[...]).astype(BF16)

    cos4, sin4 = cos_ref[...], sin_ref[...]
    q = proj(OFF_Q, OFF_KCVC)
    lane_q = jnp.bitwise_and(lax.broadcasted_iota(jnp.int32, (ts, 256), 1), HEAD_DIM - 1)
    q_sw = jnp.where(lane_q < HEAD_DIM // 2, pltpu.roll(q, 256 - HEAD_DIM // 2, 1), pltpu.roll(q, HEAD_DIM // 2, 1))
    qr = q * cos4 + q_sw * sin4
    qt_ref[0] = (q * QK_SCALE).T.astype(BF16)
    qrt_ref[0] = (qr * QK_SCALE).T.astype(BF16)

    kcvc_ref[0] = proj(OFF_KCVC, OFF_KK)

    k2 = proj(OFF_KK, OFF_VV)
    lane_k = jnp.bitwise_and(lax.broadcasted_iota(jnp.int32, (ts, LANES), 1), HEAD_DIM - 1)
    k_sw = jnp.where(lane_k < HEAD_DIM // 2, pltpu.roll(k2, LANES - HEAD_DIM // 2, 1), pltpu.roll(k2, HEAD_DIM // 2, 1))
    kk_ref[0] = (k2 * cos4[:, 0:LANES] + k_sw * sin4[:, 0:LANES]).astype(BF16)

    v2t = proj(OFF_VV, OFF_G).T
    for c in range(ts // LANES):
        vvt_ref[0, c] = v2t[:, c * LANES:(c + 1) * LANES].astype(BF16)

    gt_ref[0] = _sigmoid(proj(OFF_G, IN_PAD)).T[0:16, :]


def _inproj(x, w_in_p, cos4, sin4, cva, cvb, cvbb, lng, lnb, poolw, pools):
    b, s, d = x.shape
    ts = TS_IN
    grid = (b, s // ts)
    const = lambda shape: pl.BlockSpec(shape, lambda bi, ji: (0,) * len(shape))
    out_shape = (
        jax.ShapeDtypeStruct((b, s, 768), BF16),
        jax.ShapeDtypeStruct((b, 256, s), BF16),
        jax.ShapeDtypeStruct((b, 256, s), BF16),
        jax.ShapeDtypeStruct((b, s, LANES), F32),
        jax.ShapeDtypeStruct((b, s, LANES), BF16),
        jax.ShapeDtypeStruct((b, s // LANES, LANES, LANES), BF16),
        jax.ShapeDtypeStruct((b, 16, s), F32),
    )
    return pl.pallas_call(
        _inproj_kernel,
        grid=grid,
        in_specs=[
            pl.BlockSpec((1, ts, d), lambda bi, ji: (bi, ji, 0)),
            const((d, IN_PAD)),
            pl.BlockSpec((ts, 256), lambda bi, ji: (ji, 0)),
            pl.BlockSpec((ts, 256), lambda bi, ji: (ji, 0)),
            const((SC_KERNEL, 256)), const((CF_KERNEL, 256)), const((1, 256)), const((1, 256)), const((1, 256)),
            const((256, 256)), const((1, 256)),
        ],
        out_specs=(
            pl.BlockSpec((1, ts, 768), lambda bi, ji: (bi, ji, 0)),
            pl.BlockSpec((1, 256, ts), lambda bi, ji: (bi, 0, ji)),
            pl.BlockSpec((1, 256, ts), lambda bi, ji: (bi, 0, ji)),
            pl.BlockSpec((1, ts, LANES), lambda bi, ji: (bi, ji, 0)),
            pl.BlockSpec((1, ts, LANES), lambda bi, ji: (bi, ji, 0)),
            pl.BlockSpec((1, ts // LANES, LANES, LANES), lambda bi, ji: (bi, ji, 0, 0)),
            pl.BlockSpec((1, 16, ts), lambda bi, ji: (bi, 0, ji)),
        ),
        out_shape=out_shape,
        scratch_shapes=[
            pltpu.VMEM((ts + HALO_A, 256), F32),
            pltpu.VMEM((ts + HALO_B, 256), F32),
            pltpu.VMEM((ts + HALO_C, 256), F32),
        ],
        compiler_params=pltpu.CompilerParams(dimension_semantics=("arbitrary", "arbitrary"),
                                             vmem_limit_bytes=VMEM_LIMIT),
        name="inproj_mixers",
    )(x, w_in_p, cos4, sin4, cva, cvb, cvbb, lng, lnb, poolw, pools)


def _compress_kernel(kc_ref, vc_ref, pek_ref, pev_ref, wk1_ref, wk2_ref, wv1_ref, wv2_ref, kcmp_ref, vcmpt_ref):
    def comp(x_ref, pe_ref, w1_ref, w2_ref):
        x = x_ref[0]
        m = x.shape[0]
        half = x.shape[1]
        first = _dot((x + pe_ref[0:1, :]).astype(BF16), w1_ref[0:half, :])
        second = _dot((x + pe_ref[1:2, :]).astype(BF16), w1_ref[half:2 * half, :])
        hid = first + pltpu.roll(second, m - 1, 0)
        return _dot((hid * _sigmoid(hid)).astype(BF16), w2_ref[...])

    kcmp_ref[0] = comp(kc_ref, pek_ref, wk1_ref, wk2_ref).astype(BF16)
    vcmpt_ref[0] = comp(vc_ref, pev_ref, wv1_ref, wv2_ref).T[0:HEAD_DIM, :].astype(BF16)


def _compress(kc16, vc16, pek, pev, wk1, wk2, wv1, wv2):
    b, m, w = kc16.shape
    const = lambda shape: pl.BlockSpec(shape, lambda bi: (0,) * len(shape))
    return pl.pallas_call(
        _compress_kernel,
        grid=(b,),
        in_specs=[
            pl.BlockSpec((1, m, w), lambda bi: (bi, 0, 0)),
            pl.BlockSpec((1, m, w), lambda bi: (bi, 0, 0)),
            const((2, w)), const((2, w)),
            const((2 * w, CMP_HIDDEN)), const((CMP_HIDDEN, LANES)),
            const((2 * w, CMP_HIDDEN)), const((CMP_HIDDEN, LANES)),
        ],
        out_specs=(
            pl.BlockSpec((1, m, LANES), lambda bi: (bi, 0, 0)),
            pl.BlockSpec((1, HEAD_DIM, m), lambda bi: (bi, 0, 0)),
        ),
        out_shape=(jax.ShapeDtypeStruct((b, m, LANES), BF16), jax.ShapeDtypeStruct((b, HEAD_DIM, m), BF16)),
        compiler_params=pltpu.CompilerParams(dimension_semantics=("arbitrary",), vmem_limit_bytes=VMEM_LIMIT),
        name="compress_kv",
    )(kc16, vc16, pek, pev, wk1, wk2, wv1, wv2)


def _nsa_kernel(qt_ref, qrt_ref, gt_ref, kk_ref, vvt_ref, kcmp_ref, vcmpt_ref, ovt_ref, yd_ref, sel_scr, sa_scr, sb_scr):
    qb = pl.program_id(1)
    tq = qt_ref.shape[2]
    nh = NSA_HEADS
    wq = nh * tq
    s0 = qb * tq
    n_slc = sel_scr.shape[0]
    n_cmp_pad = kcmp_ref.shape[1]

    def stack_heads(ref):
        x = ref[0]
        return jnp.concatenate([x[h * HEAD_DIM:(h + 1) * HEAD_DIM, :] for h in range(nh)], axis=1)

    zeros = jnp.zeros((HEAD_DIM, wq), BF16)
    q_top, qr_top = stack_heads(qt_ref), stack_heads(qrt_ref)
    q_lo = jnp.concatenate([q_top, zeros], axis=0)
    qr_lo = jnp.concatenate([qr_top, zeros], axis=0)
    qr_hi = jnp.concatenate([zeros, qr_top], axis=0)
    t_row = s0 + lax.broadcasted_iota(jnp.int32, (1, tq), 1)
    lanes4 = lambda x: jnp.concatenate([x] * nh, axis=1)
    t_all = lanes4(t_row)

    sc = _dot(kcmp_ref[0], q_lo)
    cmp_end = lax.broadcasted_iota(jnp.int32, (n_cmp_pad, wq), 0) * CMP_STRIDE + (CMP_BLOCK - 1)
    sc = jnp.where(cmp_end <= t_all, sc, NEG)
    mx = jnp.max(sc, axis=0, keepdims=True)
    ex = jnp.exp(sc - mx)
    den = jnp.sum(ex, axis=0, keepdims=True)
    p_cmp = jnp.where(t_all >= CMP_BLOCK - 1, ex * (1.0 / den), 0.0)
    o_cmp = _dot(vcmpt_ref[0], p_cmp.astype(BF16))
    p_heads = p_cmp[:, 0:tq]
    for h in range(1, nh):
        p_heads = p_heads + p_cmp[:, h * tq:(h + 1) * tq]
    imp = _dot(ovt_ref[...], p_heads.astype(BF16))

    blk_i = lax.broadcasted_iota(jnp.int32, (n_slc, tq), 0)
    blk = blk_i.astype(F32)
    cur = jnp.right_shift(t_row, 6)
    forced = (blk_i == 0) | (blk_i == cur) | (blk_i == cur - 1)
    valid = blk_i * SLC_BLOCK <= t_row
    imp = jnp.where(valid, jnp.where(forced, FORCE, imp), NEG)
    sel = jnp.zeros((n_slc, tq), F32)
    for _ in range(min(SLC_TOPN, n_slc)):
        top = jnp.max(imp, axis=0, keepdims=True)
        first = jnp.min(jnp.where(imp == top, blk, float(n_slc)), axis=0, keepdims=True)
        pick = blk == first
        sel = jnp.where(pick, 1.0, sel)
        imp = jnp.where(pick, -jnp.inf, imp)
    sel_scr[...] = jnp.where(valid & (sel > 0.0), 0.0, NEG)

    blocks_per_tile = KT // SLC_BLOCK
    sub = KT // LANES

    def online_update(state, s, v_rows):
        m_i, l_i, acc = state
        m_new = jnp.maximum(m_i, jnp.max(s, axis=0, keepdims=True))
        alpha = jnp.exp(m_i - m_new)
        p = jnp.exp(s - m_new)
        l_new = alpha * l_i + jnp.sum(p, axis=0, keepdims=True)
        pb = p.astype(BF16)
        pv = _dot(v_rows[0], pb[0:LANES, :])
        for i in range(1, len(v_rows)):
            pv = pv + _dot(v_rows[i], pb[i * LANES:(i + 1) * LANES, :])
        return m_new, l_new, alpha * acc + pv

    last_tile = kk_ref.shape[1] // KT - 1

    def score_chunks(kt):
        k0 = pl.multiple_of(kt * KT, KT)
        bias8 = sel_scr[pl.ds(pl.multiple_of(kt * blocks_per_tile, blocks_per_tile), blocks_per_tile), :]
        for c in range(sub):
            rows = [jnp.broadcast_to(bias8[r:r + 1, :], (SLC_BLOCK, tq))
                    for r in range(c * LANES // SLC_BLOCK, (c + 1) * LANES // SLC_BLOCK)]
            key = k0 + c * LANES + lax.broadcasted_iota(jnp.int32, (LANES, tq), 0)
            bias = jnp.where(key <= t_row, jnp.concatenate(rows, axis=0), NEG)
            kc0 = pl.multiple_of(k0 + c * LANES, LANES)
            yield c, _dot(kk_ref[0, pl.ds(kc0, LANES), :], qr_lo) + lanes4(bias)

    def fold8(x, op):
        out = x[0:8, :]
        for r in range(1, x.shape[0] // 8):
            out = op(out, x[8 * r:8 * r + 8, :])
        return out

    def phase(kt_cur, kt_next, s_cur, s_next, state, mt_cur):
        m_i, l_i, acc = state
        m_new = jnp.maximum(m_i, jnp.max(mt_cur, axis=0, keepdims=True))
        alpha = jnp.exp(m_i - m_new)
        vt = vvt_ref[0, pl.ds(pl.multiple_of(kt_cur * sub, sub), sub)]
        mt_next = jnp.full((8, wq), NEG, F32)
        l_new = alpha * l_i
        pv = jnp.zeros((HEAD_DIM, wq), F32)
        for c, s_n in score_chunks(kt_next):
            rows = slice(c * LANES, (c + 1) * LANES)
            mt_next = jnp.maximum(mt_next, fold8(s_n, jnp.maximum))
            s_next[rows, :] = s_n
            p = jnp.exp(s_cur[rows, :] - m_new)
            l_new = l_new + fold8(p, jnp.add)
            pv = pv + _dot(vt[c, 0:HEAD_DIM, :], p.astype(BF16))
        return (m_new, l_new, alpha * acc + pv), mt_next

    def slc_pair(j, carry):
        state, mt_a = carry
        kt = 2 * j
        state, mt_b = phase(kt, kt + 1, sa_scr, sb_scr, state, mt_a)
        return phase(kt + 1, jnp.minimum(kt + 2, last_tile), sb_scr, sa_scr, state, mt_b)

    init = (jnp.full((1, wq), NEG, F32), jnp.zeros((1, wq), F32), jnp.zeros((HEAD_DIM, wq), F32))
    init_s = (init[0], jnp.zeros((8, wq), F32), init[2])
    mt0 = jnp.full((8, wq), NEG, F32)
    for c, s_n in score_chunks(0):
        mt0 = jnp.maximum(mt0, fold8(s_n, jnp.maximum))
        sa_scr[c * LANES:(c + 1) * LANES, :] = s_n
    n_tiles = (s0 + tq + KT - 1) // KT
    (_, l8_s, acc_s), _ = lax.fori_loop(0, (n_tiles + 1) // 2, slc_pair, (init_s, mt0))
    l_s = jnp.sum(l8_s, axis=0, keepdims=True)

    wkeys = WIN + tq
    k0 = pl.multiple_of(jnp.maximum(s0 - WIN, 0), LANES)
    key = k0 + lax.broadcasted_iota(jnp.int32, (wkeys, tq), 0)
    wbias = jnp.where((key <= t_row) & (key > t_row - WIN), 0.0, NEG)
    s_w = _dot(kk_ref[0, pl.ds(k0, wkeys), :], qr_hi) + lanes4(wbias)
    vt_w = vvt_ref[0, pl.ds(jnp.maximum(qb - WIN // tq, 0), wkeys // LANES)]
    _, l_w, acc_w = online_update(init, s_w, [vt_w[i, HEAD_DIM:2 * HEAD_DIM, :] for i in range(wkeys // LANES)])

    g = gt_ref[0]
    gate = lambda br: jnp.concatenate([g[h * 3 + br:h * 3 + br + 1, :] for h in range(nh)], axis=1)
    o = gate(0) * o_cmp + gate(1) * (acc_s * (1.0 / l_s)) + gate(2) * (acc_w * (1.0 / l_w))
    o_rows = jnp.concatenate([o[:, h * tq:(h + 1) * tq] for h in range(nh)], axis=0)
    yd_ref[0] = o_rows.T.astype(BF16)


def _nsa(qt, qrt, gt, kk, vvt, kcmp, vcmpt, ovt):
    b, _, s = qt.shape
    n_slc = s // SLC_BLOCK
    m = kcmp.shape[1]
    return pl.pallas_call(
        _nsa_kernel,
        grid=(b, s // TQ),
        in_specs=[
            pl.BlockSpec((1, 256, TQ), lambda bi, qi: (bi, 0, qi)),
            pl.BlockSpec((1, 256, TQ), lambda bi, qi: (bi, 0, qi)),
            pl.BlockSpec((1, 16, TQ), lambda bi, qi: (bi, 0, qi)),
            pl.BlockSpec((1, s, LANES), lambda bi, qi: (bi, 0, 0)),
            pl.BlockSpec((1, s // LANES, LANES, LANES), lambda bi, qi: (bi, 0, 0, 0)),
            pl.BlockSpec((1, m, LANES), lambda bi, qi: (bi, 0, 0)),
            pl.BlockSpec((1, HEAD_DIM, m), lambda bi, qi: (bi, 0, 0)),
            pl.BlockSpec((n_slc, m), lambda bi, qi: (0, 0)),
        ],
        out_specs=pl.BlockSpec((1, TQ, 256), lambda bi, qi: (bi, qi, 0)),
        out_shape=jax.ShapeDtypeStruct((b, s, 256), BF16),
        scratch_shapes=[pltpu.VMEM((n_slc, TQ), F32),
                        pltpu.VMEM((KT, NSA_HEADS * TQ), F32), pltpu.VMEM((KT, NSA_HEADS * TQ), F32)],
        compiler_params=pltpu.CompilerParams(dimension_semantics=("arbitrary", "arbitrary"),
                                             vmem_limit_bytes=VMEM_LIMIT),
        name="sparse_attention",
    )(qt, qrt, gt, kk, vvt, kcmp, vcmpt, ovt)


def _outproj_kernel(x_ref, yabc_ref, yd_ref, wo_ref, g_ref, b_ref, rw_ref, rb_ref, x1_ref, gates_ref):
    mix = _dot(yabc_ref[...], wo_ref[0:768, :]) + _dot(yd_ref[...], wo_ref[768:1024, :])
    x1 = _layer_norm(DN_ALPHA * x_ref[...] + mix, g_ref[...], b_ref[...])
    x1_ref[...] = x1

    ts = x1.shape[0]
    aff = _sigmoid(_dot(x1.astype(BF16), rw_ref[...])).T[0:N_EXPERTS, :]
    biased = aff + rb_ref[...]
    row = lambda a, i: a[i:i + 1, :]

    gscores = []
    for gi in range(N_GROUPS):
        a, b, c, d = (row(biased, gi * EXPERTS_PER_GROUP + i) for i in range(EXPERTS_PER_GROUP))
        hi1, lo1, hi2, lo2 = jnp.maximum(a, b), jnp.minimum(a, b), jnp.maximum(c, d), jnp.minimum(c, d)
        gscores.append(jnp.maximum(hi1, hi2) + jnp.maximum(jnp.minimum(hi1, hi2), jnp.maximum(lo1, lo2)))
    best, gsel = gscores[0], jnp.zeros((1, ts), jnp.int32)
    for gi in range(1, N_GROUPS):
        better = gscores[gi] > best
        gsel = jnp.where(better, gi, gsel)
        best = jnp.where(better, gscores[gi], best)

    eid_i = lax.broadcasted_iota(jnp.int32, (N_EXPERTS, ts), 0)
    eid = eid_i.astype(F32)
    masked = jnp.where(jnp.right_shift(eid_i, 2) == gsel, biased, NEG)
    picks = []
    for _ in range(2):
        top = jnp.max(masked, axis=0, keepdims=True)
        first = jnp.min(jnp.where(masked == top, eid, float(N_EXPERTS)), axis=0, keepdims=True)
        pick = eid == first
        picks.append(pick)
        masked = jnp.where(pick, -jnp.inf, masked)
    chosen = picks[0] | picks[1]
    w_sel = jnp.where(chosen, aff, 0.0)
    gates_t = w_sel / jnp.sum(w_sel, axis=0, keepdims=True)
    gates_ref[...] = jnp.concatenate([gates_t, jnp.zeros((LANES - N_EXPERTS, ts), F32)], axis=0).T


def _outproj(x2d, yabc, yd, wo, g, b, rw, rb):
    n, d = x2d.shape
    ts = TS_OUT
    const = lambda shape: pl.BlockSpec(shape, lambda i: (0,) * len(shape))
    return pl.pallas_call(
        _outproj_kernel,
        grid=(n // ts,),
        in_specs=[
            pl.BlockSpec((ts, d), lambda i: (i, 0)),
            pl.BlockSpec((ts, 768), lambda i: (i, 0)),
            pl.BlockSpec((ts, 256), lambda i: (i, 0)),
            const((d, d)), const((1, d)), const((1, d)), const((d, LANES)), const((N_EXPERTS, 1)),
        ],
        out_specs=(pl.BlockSpec((ts, d), lambda i: (i, 0)), pl.BlockSpec((ts, LANES), lambda i: (i, 0))),
        out_shape=(jax.ShapeDtypeStruct((n, d), F32), jax.ShapeDtypeStruct((n, LANES), F32)),
        compiler_params=pltpu.CompilerParams(dimension_semantics=("arbitrary",), vmem_limit_bytes=VMEM_LIMIT),
        name="outproj_ln_route",
    )(x2d, yabc, yd, wo, g, b, rw, rb)


def _moe_kernel(x1_ref, gates_ref, wg_ref, wu_ref, wd_ref, g_ref, b_ref, out_ref, xb_scr, acc_scr):
    e = pl.program_id(1)

    @pl.when(e == 0)
    def _():
        xb_scr[...] = x1_ref[...].astype(BF16)
        acc_scr[...] = jnp.zeros(acc_scr.shape, F32)

    xb = xb_scr[...]
    hg = _dot(xb, wg_ref[0])
    h = hg * _sigmoid(hg) * _dot(xb, wu_ref[0])
    y = _dot(h.astype(BF16), wd_ref[0])
    gates = gates_ref[...]
    lane = lax.broadcasted_iota(jnp.int32, gates.shape, 1)
    gcol = jnp.sum(jnp.where(lane == e, gates, 0.0), axis=1, keepdims=True)
    acc_scr[...] += y * gcol

    @pl.when(e == pl.num_programs(1) - 1)
    def _():
        out_ref[...] = _layer_norm(DN_ALPHA * x1_ref[...] + acc_scr[...], g_ref[...], b_ref[...])


def _moe(x1, gates, wg, wu, wd, g, b):
    n, d = x1.shape
    tm = TM_MOE
    ne, _, de = wg.shape
    const = lambda shape: pl.BlockSpec(shape, lambda i, e: (0,) * len(shape))
    return pl.pallas_call(
        _moe_kernel,
        grid=(n // tm, ne),
        in_specs=[
            pl.BlockSpec((tm, d), lambda i, e: (i, 0)),
            pl.BlockSpec((tm, LANES), lambda i, e: (i, 0)),
            pl.BlockSpec((1, d, de), lambda i, e: (e, 0, 0)),
            pl.BlockSpec((1, d, de), lambda i, e: (e, 0, 0)),
            pl.BlockSpec((1, de, d), lambda i, e: (e, 0, 0)),
            const((1, d)), const((1, d)),
        ],
        out_specs=pl.BlockSpec((tm, d), lambda i, e: (i, 0)),
        out_shape=jax.ShapeDtypeStruct((n, d), F32),
        scratch_shapes=[pltpu.VMEM((tm, d), BF16), pltpu.VMEM((tm, d), F32)],
        compiler_params=pltpu.CompilerParams(dimension_semantics=("arbitrary", "arbitrary"),
                                             vmem_limit_bytes=VMEM_LIMIT),
        name="moe_ln",
    )(x1, gates, wg, wu, wd, g, b)


def _permute_w_in(w):
    sizes = (256, 256, 256, 256, 256, 256, 256, 64, 64, 64, 64, 64, 64, 12)
    offs = np.concatenate([[0], np.cumsum(sizes)])
    part = lambda i: w[:, int(offs[i]):int(offs[i + 1])]
    a_b, a_c, a_h, b_val, b_gate, c_p, q, kc, vc, ks, vs, kw, vw, g = (part(i) for i in range(14))
    pad = jnp.zeros((w.shape[0], LANES - g.shape[1]), w.dtype)
    return jnp.concatenate([a_b, a_c, a_h, b_val, b_gate, c_p, q, kc, vc, ks, kw, vs, vw, g, pad], axis=1).astype(BF16)


def _rope_tables(s):
    inv = 1.0 / (ROPE_THETA ** (jnp.arange(0, HEAD_DIM, 2, dtype=F32) / HEAD_DIM))
    ang = jnp.arange(s, dtype=F32)[:, None] * inv[None, :]
    ang = jnp.concatenate([ang, ang], -1)
    sign = jnp.concatenate([-jnp.ones((HEAD_DIM // 2,), F32), jnp.ones((HEAD_DIM // 2,), F32)])
    return jnp.tile(jnp.cos(ang), (1, NSA_HEADS)), jnp.tile(jnp.sin(ang) * sign[None, :], (1, NSA_HEADS))


def _overlap_t(s, m):
    n_cmp = (s - CMP_BLOCK) // CMP_STRIDE + 1
    cmp_start = np.arange(m) * CMP_STRIDE
    slc_start = np.arange(s // SLC_BLOCK) * SLC_BLOCK
    ov = (cmp_start[None, :] < slc_start[:, None] + SLC_BLOCK) & (cmp_start[None, :] + CMP_BLOCK > slc_start[:, None])
    ov = ov & (np.arange(m)[None, :] < n_cmp)
    return jnp.asarray(ov, BF16)


def _block_diag(w):
    g, c, _ = w.shape
    out = jnp.zeros((g * c, g * c), w.dtype)
    for i in range(g):
        out = out.at[i * c:(i + 1) * c, i * c:(i + 1) * c].set(w[i])
    return out


def kernel(x, w_in, conv_a_w, conv_b_w, conv_b_b, cf_ln_g, cf_ln_b, pool_w, pool_scale, cmp_pe_k, cmp_pe_v, cmp_k_w1, cmp_k_w2, cmp_v_w1, cmp_v_w2, w_out, ln1_g, ln1_b, ln2_g, ln2_b, router_w, router_bias, moe_w_gate, moe_w_up, moe_w_down):
    b, s, d = x.shape
    n = b * s
    m = s // CMP_STRIDE
    tok_per_row = CMP_STRIDE
    cos4, sin4 = _rope_tables(s)
    ovt = _overlap_t(s, m)
    rw = jnp.concatenate([router_w, jnp.zeros((d, LANES - N_EXPERTS), router_w.dtype)], axis=1).astype(BF16)
    rb = router_bias.reshape(N_EXPERTS, 1).astype(F32)
    pad_w2 = lambda w: jnp.concatenate([w, jnp.zeros((CMP_HIDDEN, LANES - HEAD_DIM), w.dtype)], axis=1).astype(BF16)
    row = lambda v: v.reshape(1, -1)

    for l in range(DEPTH):
        yabc, qt, qrt, kcvc, kk, vvt, gt = _inproj(
            x, _permute_w_in(w_in[l]), cos4, sin4, conv_a_w[l], conv_b_w[l], row(conv_b_b[l]),
            row(cf_ln_g[l]), row(cf_ln_b[l]), _block_diag(pool_w[l]).astype(BF16), row(pool_scale[l]))
        kc16 = kcvc[:, :, 0:HEAD_DIM].reshape(b, m, tok_per_row * HEAD_DIM)
        vc16 = kcvc[:, :, HEAD_DIM:2 * HEAD_DIM].reshape(b, m, tok_per_row * HEAD_DIM)
        kcmp, vcmpt = _compress(
            kc16, vc16, cmp_pe_k[l].reshape(2, tok_per_row * HEAD_DIM), cmp_pe_v[l].reshape(2, tok_per_row * HEAD_DIM),
            cmp_k_w1[l].astype(BF16), pad_w2(cmp_k_w2[l]), cmp_v_w1[l].astype(BF16), pad_w2(cmp_v_w2[l]))
        yd = _nsa(qt, qrt, gt, kk, vvt, kcmp, vcmpt, ovt)
        x1, gates = _outproj(x.reshape(n, d), yabc.reshape(n, 768), yd.reshape(n, 256), w_out[l].astype(BF16),
                             row(ln1_g[l]), row(ln1_b[l]), rw, rb)
        x = _moe(x1, gates, moe_w_gate[l].astype(BF16), moe_w_up[l].astype(BF16), moe_w_down[l].astype(BF16),
                 row(ln2_g[l]), row(ln2_b[l])).reshape(b, s, d)
    return x
```

```python
import functools

import jax
import jax.numpy as jnp
import numpy as np
from jax import lax
from jax.experimental import pallas as pl
from jax.experimental.pallas import tpu as pltpu

F32 = jnp.float32
BF16 = jnp.bfloat16

D_MODEL = 1024
DEPTH = 2
GROUP_W = 256
HEAD_DIM = 64
SC_KERNEL = 3
CF_KERNEL = 31
POOL_WINDOWS = (2, 4, 8, 16)
POOL_GW = 64
NSA_HEADS = 4
CMP_BLOCK = 32
CMP_STRIDE = 16
CMP_HIDDEN = 256
SLC_BLOCK = 64
SLC_TOPN = 16
WIN = 512
N_EXPERTS = 16
N_GROUPS = 4
EXPERTS_PER_GROUP = 4
D_EXPERT = 512
DN_ALPHA = (2 * DEPTH) ** 0.25
LN_EPS = 1e-5
NEG = -1e30
FORCE = 1e4
ROPE_THETA = 10000.0
QK_SCALE = HEAD_DIM ** -0.5

LANES = 128
OFF_A, OFF_B, OFF_C, OFF_Q, OFF_KCVC, OFF_KK, OFF_VV, OFF_G, IN_PAD = 0, 768, 1280, 1536, 1792, 1920, 2048, 2176, 2304

HALO_A = 8
HALO_B = 32
HALO_C = 16

TS_IN = 512
TQ = 128
KT = 512
TS_OUT = 512
TM_MOE = 1024
ROW_ALIGN = 16
WIN_ROWS = 320
SORT_ROWS = TM_MOE + LANES
SORT_BUF = SORT_ROWS + 2 * LANES
GROUP_LANE = 16
assert SORT_ROWS >= TM_MOE + N_GROUPS * (ROW_ALIGN - 1)
assert SORT_BUF >= TM_MOE + N_GROUPS * (ROW_ALIGN - 1) + WIN_ROWS
VMEM_LIMIT = 56 * 1024 * 1024


def _sigmoid(x):
    return jax.nn.sigmoid(x)


def _layer_norm(h, g, b):
    mu = jnp.mean(h, axis=-1, keepdims=True)
    d = h - mu
    var = jnp.mean(d * d, axis=-1, keepdims=True)
    return d * lax.rsqrt(var + LN_EPS) * g + b


def _dot(a, b):
    return jnp.dot(a, b, preferred_element_type=F32)


def _inproj_kernel(x_ref, w_ref, cos_ref, sin_ref, cva_ref, cvb_ref, cvbb_ref, lng_ref, lnb_ref,
                   poolw_ref, pools_ref,
                   yabc_ref, qt_ref, qrt_ref, kcvc_ref, kk_ref, vvt_ref, gt_ref,
                   ch_ext, u_ext, p_ext):
    j = pl.program_id(1)
    ts = x_ref.shape[1]
    xb = x_ref[0].astype(BF16)

    def proj(lo, hi):
        return _dot(xb, w_ref[:, lo:hi])

    def carry_halo(ext, halo):
        @pl.when(j == 0)
        def _():
            ext[0:halo, :] = jnp.zeros((halo, ext.shape[1]), F32)

        @pl.when(j > 0)
        def _():
            ext[0:halo, :] = ext[ts:ts + halo, :]

    pa = proj(OFF_A, OFF_B)
    a_b, ch = pa[:, 0:256], pa[:, 256:512] * pa[:, 512:768]
    carry_halo(ch_ext, HALO_A)
    ch_ext[HALO_A:HALO_A + ts, :] = ch
    conv = cva_ref[2:3, :] * ch
    for k in range(SC_KERNEL - 1):
        off = HALO_A - (SC_KERNEL - 1) + k
        conv = conv + cva_ref[k:k + 1, :] * ch_ext[off:off + ts, :]
    yabc_ref[0, :, 0:256] = (a_b * conv).astype(BF16)

    pb = proj(OFF_B, OFF_C)
    u = pb[:, 0:256] * _sigmoid(pb[:, 256:512])
    carry_halo(u_ext, HALO_B)
    u_ext[HALO_B:HALO_B + ts, :] = u
    rows = 64
    for c in range(ts // rows):
        acc = jnp.zeros((rows, 256), F32) + cvbb_ref[...]
        for k in range(CF_KERNEL):
            off = HALO_B - (CF_KERNEL - 1) + k + c * rows
            acc = acc + cvb_ref[k:k + 1, :] * u_ext[off:off + rows, :]
        v = _layer_norm(acc, lng_ref[...], lnb_ref[...])
        yabc_ref[0, c * rows:(c + 1) * rows, 256:512] = (v * _sigmoid(v)).astype(BF16)

    pc = proj(OFF_C, OFF_Q)
    carry_halo(p_ext, HALO_C)
    p_ext[HALO_C:HALO_C + ts, :] = pc
    pe = p_ext[...]
    s2 = pe + pltpu.roll(pe, 1, 0)
    s4 = s2 + pltpu.roll(s2, 2, 0)
    s8 = s4 + pltpu.roll(s4, 4, 0)
    s16 = s8 + pltpu.roll(s8, 8, 0)
    lane = lax.broadcasted_iota(jnp.int32, (ts, 256), 1)
    grp = jnp.right_shift(lane, 6)
    wsum = jnp.where(grp == 0, s2[HALO_C:], jnp.where(grp == 1, s4[HALO_C:], jnp.where(grp == 2, s8[HALO_C:], s16[HALO_C:])))
    win = jnp.where(grp == 0, 2, jnp.where(grp == 1, 4, jnp.where(grp == 2, 8, 16)))
    t1 = j * ts + lax.broadcasted_iota(jnp.int32, (ts, 256), 0) + 1
    cnt = jnp.minimum(win, t1).astype(F32)
    dlt = wsum / cnt - pc
    yabc_ref[0, :, 512:768] = (_dot(dlt.astype(BF16), poolw_ref[...]) * pools_ref[...]).astype(BF16)

    cos4, sin4 = cos_ref[...], sin_ref[...]
    q = proj(OFF_Q, OFF_KCVC)
    lane_q = jnp.bitwise_and(lax.broadcasted_iota(jnp.int32, (ts, 256), 1), HEAD_DIM - 1)
    q_sw = jnp.where(lane_q < HEAD_DIM // 2, pltpu.roll(q, 256 - HEAD_DIM // 2, 1), pltpu.roll(q, HEAD_DIM // 2, 1))
    qr = q * cos4 + q_sw * sin4
    qt_ref[0] = (q * QK_SCALE).T.astype(BF16)
    qrt_ref[0] = (qr * QK_SCALE).T.astype(BF16)

    kcvc_ref[0] = proj(OFF_KCVC, OFF_KK)

    k2 = proj(OFF_KK, OFF_VV)
    lane_k = jnp.bitwise_and(lax.broadcasted_iota(jnp.int32, (ts, LANES), 1), HEAD_DIM - 1)
    k_sw = jnp.where(lane_k < HEAD_DIM // 2, pltpu.roll(k2, LANES - HEAD_DIM // 2, 1), pltpu.roll(k2, HEAD_DIM // 2, 1))
    kk_ref[0] = (k2 * cos4[:, 0:LANES] + k_sw * sin4[:, 0:LANES]).astype(BF16)

    v2t = proj(OFF_VV, OFF_G).T
    for c in range(ts // LANES):
        vvt_ref[0, c] = v2t[:, c * LANES:(c + 1) * LANES].astype(BF16)

    gt_ref[0] = _sigmoid(proj(OFF_G, IN_PAD)).T[0:16, :]


def _inproj(x, w_in_p, cos4, sin4, cva, cvb, cvbb, lng, lnb, poolw, pools):
    b, s, d = x.shape
    ts = TS_IN
    grid = (b, s // ts)
    const = lambda shape: pl.BlockSpec(shape, lambda bi, ji: (0,) * len(shape))
    out_shape = (
        jax.ShapeDtypeStruct((b, s, 768), BF16),
        jax.ShapeDtypeStruct((b, 256, s), BF16),
        jax.ShapeDtypeStruct((b, 256, s), BF16),
        jax.ShapeDtypeStruct((b, s, LANES), F32),
        jax.ShapeDtypeStruct((b, s, LANES), BF16),
        jax.ShapeDtypeStruct((b, s // LANES, LANES, LANES), BF16),
        jax.ShapeDtypeStruct((b, 16, s), F32),
    )
    return pl.pallas_call(
        _inproj_kernel,
        grid=grid,
        in_specs=[
            pl.BlockSpec((1, ts, d), lambda bi, ji: (bi, ji, 0)),
            const((d, IN_PAD)),
            pl.BlockSpec((ts, 256), lambda bi, ji: (ji, 0)),
            pl.BlockSpec((ts, 256), lambda bi, ji: (ji, 0)),
            const((SC_KERNEL, 256)), const((CF_KERNEL, 256)), const((1, 256)), const((1, 256)), const((1, 256)),
            const((256, 256)), const((1, 256)),
        ],
        out_specs=(
            pl.BlockSpec((1, ts, 768), lambda bi, ji: (bi, ji, 0)),
            pl.BlockSpec((1, 256, ts), lambda bi, ji: (bi, 0, ji)),
            pl.BlockSpec((1, 256, ts), lambda bi, ji: (bi, 0, ji)),
            pl.BlockSpec((1, ts, LANES), lambda bi, ji: (bi, ji, 0)),
            pl.BlockSpec((1, ts, LANES), lambda bi, ji: (bi, ji, 0)),
            pl.BlockSpec((1, ts // LANES, LANES, LANES), lambda bi, ji: (bi, ji, 0, 0)),
            pl.BlockSpec((1, 16, ts), lambda bi, ji: (bi, 0, ji)),
        ),
        out_shape=out_shape,
        scratch_shapes=[
            pltpu.VMEM((ts + HALO_A, 256), F32),
            pltpu.VMEM((ts + HALO_B, 256), F32),
            pltpu.VMEM((ts + HALO_C, 256), F32),
        ],
        compiler_params=pltpu.CompilerParams(dimension_semantics=("arbitrary", "arbitrary"),
                                             vmem_limit_bytes=VMEM_LIMIT),
        name="inproj_mixers",
    )(x, w_in_p, cos4, sin4, cva, cvb, cvbb, lng, lnb, poolw, pools)


def _compress_kernel(kc_ref, vc_ref, pek_ref, pev_ref, wk1_ref, wk2_ref, wv1_ref, wv2_ref, kcmp_ref, vcmpt_ref):
    def comp(x_ref, pe_ref, w1_ref, w2_ref):
        x = x_ref[0]
        m = x.shape[0]
        half = x.shape[1]
        first = _dot((x + pe_ref[0:1, :]).astype(BF16), w1_ref[0:half, :])
        second = _dot((x + pe_ref[1:2, :]).astype(BF16), w1_ref[half:2 * half, :])
        hid = first + pltpu.roll(second, m - 1, 0)
        return _dot((hid * _sigmoid(hid)).astype(BF16), w2_ref[...])

    kcmp_ref[0] = comp(kc_ref, pek_ref, wk1_ref, wk2_ref).astype(BF16)
    vcmpt_ref[0] = comp(vc_ref, pev_ref, wv1_ref, wv2_ref).T[0:HEAD_DIM, :].astype(BF16)


def _compress(kc16, vc16, pek, pev, wk1, wk2, wv1, wv2):
    b, m, w = kc16.shape
    const = lambda shape: pl.BlockSpec(shape, lambda bi: (0,) * len(shape))
    return pl.pallas_call(
        _compress_kernel,
        grid=(b,),
        in_specs=[
            pl.BlockSpec((1, m, w), lambda bi: (bi, 0, 0)),
            pl.BlockSpec((1, m, w), lambda bi: (bi, 0, 0)),
            const((2, w)), const((2, w)),
            const((2 * w, CMP_HIDDEN)), const((CMP_HIDDEN, LANES)),
            const((2 * w, CMP_HIDDEN)), const((CMP_HIDDEN, LANES)),
        ],
        out_specs=(
            pl.BlockSpec((1, m, LANES), lambda bi: (bi, 0, 0)),
            pl.BlockSpec((1, HEAD_DIM, m), lambda bi: (bi, 0, 0)),
        ),
        out_shape=(jax.ShapeDtypeStruct((b, m, LANES), BF16), jax.ShapeDtypeStruct((b, HEAD_DIM, m), BF16)),
        compiler_params=pltpu.CompilerParams(dimension_semantics=("arbitrary",), vmem_limit_bytes=VMEM_LIMIT),
        name="compress_kv",
    )(kc16, vc16, pek, pev, wk1, wk2, wv1, wv2)


def _nsa_kernel(qt_ref, qrt_ref, gt_ref, kk_ref, vvt_ref, kcmp_ref, vcmpt_ref, ovt_ref, yd_ref, sel_scr, sa_scr, sb_scr):
    qb = pl.program_id(1)
    tq = qt_ref.shape[2]
    nh = NSA_HEADS
    wq = nh * tq
    s0 = qb * tq
    n_slc = sel_scr.shape[0]
    n_cmp_pad = kcmp_ref.shape[1]

    def stack_heads(ref):
        x = ref[0]
        return jnp.concatenate([x[h * HEAD_DIM:(h + 1) * HEAD_DIM, :] for h in range(nh)], axis=1)

    zeros = jnp.zeros((HEAD_DIM, wq), BF16)
    q_top, qr_top = stack_heads(qt_ref), stack_heads(qrt_ref)
    q_lo = jnp.concatenate([q_top, zeros], axis=0)
    qr_lo = jnp.concatenate([qr_top, zeros], axis=0)
    qr_hi = jnp.concatenate([zeros, qr_top], axis=0)
    t_row = s0 + lax.broadcasted_iota(jnp.int32, (1, tq), 1)
    lanes4 = lambda x: jnp.concatenate([x] * nh, axis=1)
    t_all = lanes4(t_row)

    sc = _dot(kcmp_ref[0], q_lo)
    cmp_end = lax.broadcasted_iota(jnp.int32, (n_cmp_pad, wq), 0) * CMP_STRIDE + (CMP_BLOCK - 1)
    sc = jnp.where(cmp_end <= t_all, sc, NEG)
    mx = jnp.max(sc, axis=0, keepdims=True)
    ex = jnp.exp(sc - mx)
    den = jnp.sum(ex, axis=0, keepdims=True)
    p_cmp = jnp.where(t_all >= CMP_BLOCK - 1, ex * (1.0 / den), 0.0)
    o_cmp = _dot(vcmpt_ref[0], p_cmp.astype(BF16))
    p_heads = p_cmp[:, 0:tq]
    for h in range(1, nh):
        p_heads = p_heads + p_cmp[:, h * tq:(h + 1) * tq]
    imp = _dot(ovt_ref[...], p_heads.astype(BF16))

    blk_i = lax.broadcasted_iota(jnp.int32, (n_slc, tq), 0)
    blk = blk_i.astype(F32)
    cur = jnp.right_shift(t_row, 6)
    forced = (blk_i == 0) | (blk_i == cur) | (blk_i == cur - 1)
    valid = blk_i * SLC_BLOCK <= t_row
    imp = jnp.where(valid, jnp.where(forced, FORCE, imp), NEG)
    sel = jnp.zeros((n_slc, tq), F32)
    for _ in range(min(SLC_TOPN, n_slc)):
        top = jnp.max(imp, axis=0, keepdims=True)
        first = jnp.min(jnp.where(imp == top, blk, float(n_slc)), axis=0, keepdims=True)
        pick = blk == first
        sel = jnp.where(pick, 1.0, sel)
        imp = jnp.where(pick, -jnp.inf, imp)
    sel_scr[...] = jnp.where(valid & (sel > 0.0), 0.0, NEG)

    blocks_per_tile = KT // SLC_BLOCK
    sub = KT // LANES

    def online_update(state, s, v_rows):
        m_i, l_i, acc = state
        m_new = jnp.maximum(m_i, jnp.max(s, axis=0, keepdims=True))
        alpha = jnp.exp(m_i - m_new)
        p = jnp.exp(s - m_new)
        l_new = alpha * l_i + jnp.sum(p, axis=0, keepdims=True)
        pb = p.astype(BF16)
        pv = _dot(v_rows[0], pb[0:LANES, :])
        for i in range(1, len(v_rows)):
            pv = pv + _dot(v_rows[i], pb[i * LANES:(i + 1) * LANES, :])
        return m_new, l_new, alpha * acc + pv

    last_tile = kk_ref.shape[1] // KT - 1

    def score_chunks(kt):
        k0 = pl.multiple_of(kt * KT, KT)
        bias8 = sel_scr[pl.ds(pl.multiple_of(kt * blocks_per_tile, blocks_per_tile), blocks_per_tile), :]
        for c in range(sub):
            rows = [jnp.broadcast_to(bias8[r:r + 1, :], (SLC_BLOCK, tq))
                    for r in range(c * LANES // SLC_BLOCK, (c + 1) * LANES // SLC_BLOCK)]
            key = k0 + c * LANES + lax.broadcasted_iota(jnp.int32, (LANES, tq), 0)
            bias = jnp.where(key <= t_row, jnp.concatenate(rows, axis=0), NEG)
            kc0 = pl.multiple_of(k0 + c * LANES, LANES)
            yield c, _dot(kk_ref[0, pl.ds(kc0, LANES), :], qr_lo) + lanes4(bias)

    def fold8(x, op):
        out = x[0:8, :]
        for r in range(1, x.shape[0] // 8):
            out = op(out, x[8 * r:8 * r + 8, :])
        return out

    def phase(kt_cur, kt_next, s_cur, s_next, state, mt_cur):
        m_i, l_i, acc = state
        m_new = jnp.maximum(m_i, jnp.max(mt_cur, axis=0, keepdims=True))
        alpha = jnp.exp(m_i - m_new)
        vt = vvt_ref[0, pl.ds(pl.multiple_of(kt_cur * sub, sub), sub)]
        mt_next = jnp.full((8, wq), NEG, F32)
        l_new = alpha * l_i
        pv = jnp.zeros((HEAD_DIM, wq), F32)
        for c, s_n in score_chunks(kt_next):
            rows = slice(c * LANES, (c + 1) * LANES)
            mt_next = jnp.maximum(mt_next, fold8(s_n, jnp.maximum))
            s_next[rows, :] = s_n
            p = jnp.exp(s_cur[rows, :] - m_new)
            l_new = l_new + fold8(p, jnp.add)
            pv = pv + _dot(vt[c, 0:HEAD_DIM, :], p.astype(BF16))
        return (m_new, l_new, alpha * acc + pv), mt_next

    def slc_pair(j, carry):
        state, mt_a = carry
        kt = 2 * j
        state, mt_b = phase(kt, kt + 1, sa_scr, sb_scr, state, mt_a)
        return phase(kt + 1, jnp.minimum(kt + 2, last_tile), sb_scr, sa_scr, state, mt_b)

    init = (jnp.full((1, wq), NEG, F32), jnp.zeros((1, wq), F32), jnp.zeros((HEAD_DIM, wq), F32))
    init_s = (init[0], jnp.zeros((8, wq), F32), init[2])
    mt0 = jnp.full((8, wq), NEG, F32)
    for c, s_n in score_chunks(0):
        mt0 = jnp.maximum(mt0, fold8(s_n, jnp.maximum))
        sa_scr[c * LANES:(c + 1) * LANES, :] = s_n
    n_tiles = (s0 + tq + KT - 1) // KT
    (_, l8_s, acc_s), _ = lax.fori_loop(0, (n_tiles + 1) // 2, slc_pair, (init_s, mt0))
    l_s = jnp.sum(l8_s, axis=0, keepdims=True)

    wkeys = WIN + tq
    k0 = pl.multiple_of(jnp.maximum(s0 - WIN, 0), LANES)
    key = k0 + lax.broadcasted_iota(jnp.int32, (wkeys, tq), 0)
    wbias = jnp.where((key <= t_row) & (key > t_row - WIN), 0.0, NEG)
    s_w = _dot(kk_ref[0, pl.ds(k0, wkeys), :], qr_hi) + lanes4(wbias)
    vt_w = vvt_ref[0, pl.ds(jnp.maximum(qb - WIN // tq, 0), wkeys // LANES)]
    _, l_w, acc_w = online_update(init, s_w, [vt_w[i, HEAD_DIM:2 * HEAD_DIM, :] for i in range(wkeys // LANES)])

    g = gt_ref[0]
    gate = lambda br: jnp.concatenate([g[h * 3 + br:h * 3 + br + 1, :] for h in range(nh)], axis=1)
    o = gate(0) * o_cmp + gate(1) * (acc_s * (1.0 / l_s)) + gate(2) * (acc_w * (1.0 / l_w))
    o_rows = jnp.concatenate([o[:, h * tq:(h + 1) * tq] for h in range(nh)], axis=0)
    yd_ref[0] = o_rows.T.astype(BF16)


def _nsa(qt, qrt, gt, kk, vvt, kcmp, vcmpt, ovt):
    b, _, s = qt.shape
    n_slc = s // SLC_BLOCK
    m = kcmp.shape[1]
    return pl.pallas_call(
        _nsa_kernel,
        grid=(b, s // TQ),
        in_specs=[
            pl.BlockSpec((1, 256, TQ), lambda bi, qi: (bi, 0, qi)),
            pl.BlockSpec((1, 256, TQ), lambda bi, qi: (bi, 0, qi)),
            pl.BlockSpec((1, 16, TQ), lambda bi, qi: (bi, 0, qi)),
            pl.BlockSpec((1, s, LANES), lambda bi, qi: (bi, 0, 0)),
            pl.BlockSpec((1, s // LANES, LANES, LANES), lambda bi, qi: (bi, 0, 0, 0)),
            pl.BlockSpec((1, m, LANES), lambda bi, qi: (bi, 0, 0)),
            pl.BlockSpec((1, HEAD_DIM, m), lambda bi, qi: (bi, 0, 0)),
            pl.BlockSpec((n_slc, m), lambda bi, qi: (0, 0)),
        ],
        out_specs=pl.BlockSpec((1, TQ, 256), lambda bi, qi: (bi, qi, 0)),
        out_shape=jax.ShapeDtypeStruct((b, s, 256), BF16),
        scratch_shapes=[pltpu.VMEM((n_slc, TQ), F32),
                        pltpu.VMEM((KT, NSA_HEADS * TQ), F32), pltpu.VMEM((KT, NSA_HEADS * TQ), F32)],
        compiler_params=pltpu.CompilerParams(dimension_semantics=("arbitrary", "arbitrary"),
                                             vmem_limit_bytes=VMEM_LIMIT),
        name="sparse_attention",
    )(qt, qrt, gt, kk, vvt, kcmp, vcmpt, ovt)


def _outproj_kernel(x_ref, yabc_ref, yd_ref, wo_ref, g_ref, b_ref, rw_ref, rb_ref, x1_ref, gates_ref):
    mix = _dot(yabc_ref[...], wo_ref[0:768, :]) + _dot(yd_ref[...], wo_ref[768:1024, :])
    x1 = _layer_norm(DN_ALPHA * x_ref[...] + mix, g_ref[...], b_ref[...])
    x1_ref[...] = x1

    ts = x1.shape[0]
    aff = _sigmoid(_dot(x1.astype(BF16), rw_ref[...])).T[0:N_EXPERTS, :]
    biased = aff + rb_ref[...]
    row = lambda a, i: a[i:i + 1, :]

    gscores = []
    for gi in range(N_GROUPS):
        a, b, c, d = (row(biased, gi * EXPERTS_PER_GROUP + i) for i in range(EXPERTS_PER_GROUP))
        hi1, lo1, hi2, lo2 = jnp.maximum(a, b), jnp.minimum(a, b), jnp.maximum(c, d), jnp.minimum(c, d)
        gscores.append(jnp.maximum(hi1, hi2) + jnp.maximum(jnp.minimum(hi1, hi2), jnp.maximum(lo1, lo2)))
    best, gsel = gscores[0], jnp.zeros((1, ts), jnp.int32)
    for gi in range(1, N_GROUPS):
        better = gscores[gi] > best
        gsel = jnp.where(better, gi, gsel)
        best = jnp.where(better, gscores[gi], best)

    eid_i = lax.broadcasted_iota(jnp.int32, (N_EXPERTS, ts), 0)
    eid = eid_i.astype(F32)
    masked = jnp.where(jnp.right_shift(eid_i, 2) == gsel, biased, NEG)
    picks = []
    for _ in range(2):
        top = jnp.max(masked, axis=0, keepdims=True)
        first = jnp.min(jnp.where(masked == top, eid, float(N_EXPERTS)), axis=0, keepdims=True)
        pick = eid == first
        picks.append(pick)
        masked = jnp.where(pick, -jnp.inf, masked)
    chosen = picks[0] | picks[1]
    w_sel = jnp.where(chosen, aff, 0.0)
    gates_t = w_sel / jnp.sum(w_sel, axis=0, keepdims=True)
    grp8 = jnp.where(lax.broadcasted_iota(jnp.int32, (8, ts), 0) == 0, gsel.astype(F32), 0.0)
    gates_ref[...] = jnp.concatenate([gates_t, grp8, jnp.zeros((LANES - N_EXPERTS - 8, ts), F32)], axis=0).T


def _outproj(x2d, yabc, yd, wo, g, b, rw, rb):
    n, d = x2d.shape
    ts = TS_OUT
    const = lambda shape: pl.BlockSpec(shape, lambda i: (0,) * len(shape))
    return pl.pallas_call(
        _outproj_kernel,
        grid=(n // ts,),
        in_specs=[
            pl.BlockSpec((ts, d), lambda i: (i, 0)),
            pl.BlockSpec((ts, 768), lambda i: (i, 0)),
            pl.BlockSpec((ts, 256), lambda i: (i, 0)),
            const((d, d)), const((1, d)), const((1, d)), const((d, LANES)), const((N_EXPERTS, 1)),
        ],
        out_specs=(pl.BlockSpec((ts, d), lambda i: (i, 0)), pl.BlockSpec((ts, LANES), lambda i: (i, 0))),
        out_shape=(jax.ShapeDtypeStruct((n, d), F32), jax.ShapeDtypeStruct((n, LANES), F32)),
        compiler_params=pltpu.CompilerParams(dimension_semantics=("arbitrary",), vmem_limit_bytes=VMEM_LIMIT),
        name="outproj_ln_route",
    )(x2d, yabc, yd, wo, g, b, rw, rb)


def _split3(x):
    hi = x.astype(BF16)
    r1 = x - hi.astype(F32)
    mid = r1.astype(BF16)
    lo = (r1 - mid.astype(F32)).astype(BF16)
    return hi, mid, lo


def _dispatch_kernel(x1_ref, gates_ref, tri_ref, xs_ref, gs_ref, post_ref, cnt_ref):
    tm = x1_ref.shape[0]
    gates = gates_ref[...]
    gsel = gates.T[GROUP_LANE:GROUP_LANE + 1, :]
    onehot = jnp.where(lax.broadcasted_iota(jnp.int32, (8, tm), 0).astype(F32) == gsel, 1.0, 0.0)
    n = jnp.sum(onehot, axis=1, keepdims=True)
    n_al = jnp.floor((n + (ROW_ALIGN - 1)) * (1.0 / ROW_ALIGN)) * ROW_ALIGN
    starts = [jnp.zeros((1, 1), F32)]
    for gi in range(1, N_GROUPS):
        starts.append(starts[-1] + n_al[gi - 1:gi, :])
    start = jnp.concatenate(starts + [jnp.zeros((8 - N_GROUPS, 1), F32)], axis=0)
    before = _dot(onehot.astype(BF16), tri_ref[...])
    pos = jnp.sum(onehot * (start + before), axis=0, keepdims=True)
    perm = jnp.where(lax.broadcasted_iota(jnp.int32, (SORT_ROWS, tm), 0).astype(F32) == pos, 1.0, 0.0).astype(BF16)

    xs_ref[0, 0:SORT_ROWS, :] = _dot(perm, x1_ref[...].astype(BF16)).astype(BF16)
    xs_ref[0, SORT_ROWS:SORT_BUF, :] = jnp.zeros((SORT_BUF - SORT_ROWS, xs_ref.shape[2]), BF16)
    g_hi, g_mid, g_lo = _split3(gates)
    gs_ref[0, 0:SORT_ROWS, :] = _dot(perm, g_hi) + _dot(perm, g_mid) + _dot(perm, g_lo)
    gs_ref[0, SORT_ROWS:SORT_BUF, :] = jnp.zeros((SORT_BUF - SORT_ROWS, LANES), F32)
    pos8 = jnp.where(lax.broadcasted_iota(jnp.int32, (8, tm), 0) == 0, pos, 0.0)
    post_ref[...] = jnp.concatenate([pos8, jnp.zeros((LANES - 8, tm), F32)], axis=0).T
    cnt_ref[0] = jnp.broadcast_to(n, (8, LANES))


def _dispatch(x1, gates, tri):
    n, d = x1.shape
    tm = TM_MOE
    nt = n // tm
    return pl.pallas_call(
        _dispatch_kernel,
        grid=(nt,),
        in_specs=[
            pl.BlockSpec((tm, d), lambda i: (i, 0)),
            pl.BlockSpec((tm, LANES), lambda i: (i, 0)),
            pl.BlockSpec((tm, tm), lambda i: (0, 0)),
        ],
        out_specs=(
            pl.BlockSpec((1, SORT_BUF, d), lambda i: (i, 0, 0)),
            pl.BlockSpec((1, SORT_BUF, LANES), lambda i: (i, 0, 0)),
            pl.BlockSpec((tm, LANES), lambda i: (i, 0)),
            pl.BlockSpec((1, 8, LANES), lambda i: (i, 0, 0)),
        ),
        out_shape=(
            jax.ShapeDtypeStruct((nt, SORT_BUF, d), BF16),
            jax.ShapeDtypeStruct((nt, SORT_BUF, LANES), F32),
            jax.ShapeDtypeStruct((n, LANES), F32),
            jax.ShapeDtypeStruct((nt, 8, LANES), F32),
        ),
        compiler_params=pltpu.CompilerParams(dimension_semantics=("arbitrary",), vmem_limit_bytes=VMEM_LIMIT),
        name="moe_dispatch",
    )(x1, gates, tri)


def _moe_kernel(start_ref, trips_ref, x1_ref, xs_ref, gs_ref, post_ref, wg_ref, wu_ref, wd_ref, g_ref, b_ref,
                out_ref, acc_scr):
    i, e = pl.program_id(0), pl.program_id(1)

    @pl.when(e == 0)
    def _():
        acc_scr[...] = jnp.zeros(acc_scr.shape, F32)

    grp = lax.shift_right_logical(e, 2)
    row0 = start_ref[i * N_GROUPS + grp]

    def window(w, carry):
        rows = pl.ds(pl.multiple_of(row0 + w * WIN_ROWS, ROW_ALIGN), WIN_ROWS)
        xw = xs_ref[0, rows, :]
        hg = _dot(xw, wg_ref[0])
        h = hg * _sigmoid(hg) * _dot(xw, wu_ref[0])
        y = _dot(h.astype(BF16), wd_ref[0])
        gw = gs_ref[0, rows, :]
        lane = lax.broadcasted_iota(jnp.int32, gw.shape, 1)
        gcol = jnp.sum(jnp.where(lane == e, gw, 0.0), axis=1, keepdims=True)
        acc_scr[rows, :] += y * gcol
        return carry

    lax.fori_loop(0, trips_ref[i * N_GROUPS + grp], window, 0)

    @pl.when(e == pl.num_programs(1) - 1)
    def _():
        tm = x1_ref.shape[0]
        acc = acc_scr[0:SORT_ROWS, :]
        hi = acc.astype(BF16)
        lo = (acc - hi.astype(F32)).astype(BF16)
        pos = post_ref[...][:, 0:1]
        unperm = jnp.where(lax.broadcasted_iota(jnp.int32, (tm, SORT_ROWS), 1).astype(F32) == pos, 1.0, 0.0).astype(BF16)
        moe = _dot(unperm, hi) + _dot(unperm, lo)
        out_ref[...] = _layer_norm(DN_ALPHA * x1_ref[...] + moe, g_ref[...], b_ref[...])


def _moe(start, trips, x1, xs, gs, post, wg, wu, wd, g, b):
    n, d = x1.shape
    tm = TM_MOE
    ne, _, de = wg.shape
    const = lambda shape: pl.BlockSpec(shape, lambda i, e, *_: (0,) * len(shape))
    grid_spec = pltpu.PrefetchScalarGridSpec(
        num_scalar_prefetch=2,
        grid=(n // tm, ne),
        in_specs=[
            pl.BlockSpec((tm, d), lambda i, e, *_: (i, 0)),
            pl.BlockSpec((1, SORT_BUF, d), lambda i, e, *_: (i, 0, 0)),
            pl.BlockSpec((1, SORT_BUF, LANES), lambda i, e, *_: (i, 0, 0)),
            pl.BlockSpec((tm, LANES), lambda i, e, *_: (i, 0)),
            pl.BlockSpec((1, d, de), lambda i, e, *_: (e, 0, 0)),
            pl.BlockSpec((1, d, de), lambda i, e, *_: (e, 0, 0)),
            pl.BlockSpec((1, de, d), lambda i, e, *_: (e, 0, 0)),
            const((1, d)), const((1, d)),
        ],
        out_specs=pl.BlockSpec((tm, d), lambda i, e, *_: (i, 0)),
        scratch_shapes=[pltpu.VMEM((SORT_BUF, d), F32)],
    )
    return pl.pallas_call(
        _moe_kernel,
        grid_spec=grid_spec,
        out_shape=jax.ShapeDtypeStruct((n, d), F32),
        compiler_params=pltpu.CompilerParams(dimension_semantics=("arbitrary", "arbitrary"),
                                             vmem_limit_bytes=VMEM_LIMIT),
        name="moe_ln",
    )(start, trips, x1, xs, gs, post, wg, wu, wd, g, b)


def _group_windows(cnt):
    n = cnt[:, 0:N_GROUPS, 0].astype(jnp.int32)
    n_al = (n + (ROW_ALIGN - 1)) // ROW_ALIGN * ROW_ALIGN
    start = jnp.cumsum(n_al, axis=1) - n_al
    trips = (n + (WIN_ROWS - 1)) // WIN_ROWS
    return start.reshape(-1), trips.reshape(-1)


def _permute_w_in(w):
    sizes = (256, 256, 256, 256, 256, 256, 256, 64, 64, 64, 64, 64, 64, 12)
    offs = np.concatenate([[0], np.cumsum(sizes)])
    part = lambda i: w[:, int(offs[i]):int(offs[i + 1])]
    a_b, a_c, a_h, b_val, b_gate, c_p, q, kc, vc, ks, vs, kw, vw, g = (part(i) for i in range(14))
    pad = jnp.zeros((w.shape[0], LANES - g.shape[1]), w.dtype)
    return jnp.concatenate([a_b, a_c, a_h, b_val, b_gate, c_p, q, kc, vc, ks, kw, vs, vw, g, pad], axis=1).astype(BF16)


def _rope_tables(s):
    inv = 1.0 / (ROPE_THETA ** (jnp.arange(0, HEAD_DIM, 2, dtype=F32) / HEAD_DIM))
    ang = jnp.arange(s, dtype=F32)[:, None] * inv[None, :]
    ang = jnp.concatenate([ang, ang], -1)
    sign = jnp.concatenate([-jnp.ones((HEAD_DIM // 2,), F32), jnp.ones((HEAD_DIM // 2,), F32)])
    return jnp.tile(jnp.cos(ang), (1, NSA_HEADS)), jnp.tile(jnp.sin(ang) * sign[None, :], (1, NSA_HEADS))


def _overlap_t(s, m):
    n_cmp = (s - CMP_BLOCK) // CMP_STRIDE + 1
    cmp_start = np.arange(m) * CMP_STRIDE
    slc_start = np.arange(s // SLC_BLOCK) * SLC_BLOCK
    ov = (cmp_start[None, :] < slc_start[:, None] + SLC_BLOCK) & (cmp_start[None, :] + CMP_BLOCK > slc_start[:, None])
    ov = ov & (np.arange(m)[None, :] < n_cmp)
    return jnp.asarray(ov, BF16)


def _block_diag(w):
    g, c, _ = w.shape
    out = jnp.zeros((g * c, g * c), w.dtype)
    for i in range(g):
        out = out.at[i * c:(i + 1) * c, i * c:(i + 1) * c].set(w[i])
    return out


def kernel(x, w_in, conv_a_w, conv_b_w, conv_b_b, cf_ln_g, cf_ln_b, pool_w, pool_scale, cmp_pe_k, cmp_pe_v, cmp_k_w1, cmp_k_w2, cmp_v_w1, cmp_v_w2, w_out, ln1_g, ln1_b, ln2_g, ln2_b, router_w, router_bias, moe_w_gate, moe_w_up, moe_w_down):
    b, s, d = x.shape
    n = b * s
    m = s // CMP_STRIDE
    tok_per_row = CMP_STRIDE
    cos4, sin4 = _rope_tables(s)
    ovt = _overlap_t(s, m)
    rw = jnp.concatenate([router_w, jnp.zeros((d, LANES - N_EXPERTS), router_w.dtype)], axis=1).astype(BF16)
    rb = router_bias.reshape(N_EXPERTS, 1).astype(F32)
    pad_w2 = lambda w: jnp.concatenate([w, jnp.zeros((CMP_HIDDEN, LANES - HEAD_DIM), w.dtype)], axis=1).astype(BF16)
    row = lambda v: v.reshape(1, -1)
    tri = jnp.asarray(np.triu(np.ones((TM_MOE, TM_MOE), np.float32), k=1), BF16)

    for l in range(DEPTH):
        yabc, qt, qrt, kcvc, kk, vvt, gt = _inproj(
            x, _permute_w_in(w_in[l]), cos4, sin4, conv_a_w[l], conv_b_w[l], row(conv_b_b[l]),
            row(cf_ln_g[l]), row(cf_ln_b[l]), _block_diag(pool_w[l]).astype(BF16), row(pool_scale[l]))
        kc16 = kcvc[:, :, 0:HEAD_DIM].reshape(b, m, tok_per_row * HEAD_DIM)
        vc16 = kcvc[:, :, HEAD_DIM:2 * HEAD_DIM].reshape(b, m, tok_per_row * HEAD_DIM)
        kcmp, vcmpt = _compress(
            kc16, vc16, cmp_pe_k[l].reshape(2, tok_per_row * HEAD_DIM), cmp_pe_v[l].reshape(2, tok_per_row * HEAD_DIM),
            cmp_k_w1[l].astype(BF16), pad_w2(cmp_k_w2[l]), cmp_v_w1[l].astype(BF16), pad_w2(cmp_v_w2[l]))
        yd = _nsa(qt, qrt, gt, kk, vvt, kcmp, vcmpt, ovt)
        x1, gates = _outproj(x.reshape(n, d), yabc.reshape(n, 768), yd.reshape(n, 256), w_out[l].astype(BF16),
                             row(ln1_g[l]), row(ln1_b[l]), rw, rb)
        xs, gs, post, cnt = _dispatch(x1, gates, tri)
        start, trips = _group_windows(cnt)
        x = _moe(start, trips, x1, xs, gs, post, moe_w_gate[l].astype(BF16), moe_w_up[l].astype(BF16),
                 moe_w_down[l].astype(BF16), row(ln2_g[l]), row(ln2_b[l])).reshape(b, s, d)
    return x
```

```python
import functools

import jax
import jax.numpy as jnp
import numpy as np
from jax import lax
from jax.experimental import pallas as pl
from jax.experimental.pallas import tpu as pltpu

F32 = jnp.float32
BF16 = jnp.bfloat16

D_MODEL = 1024
DEPTH = 2
GROUP_W = 256
HEAD_DIM = 64
SC_KERNEL = 3
CF_KERNEL = 31
POOL_WINDOWS = (2, 4, 8, 16)
POOL_GW = 64
NSA_HEADS = 4
CMP_BLOCK = 32
CMP_STRIDE = 16
CMP_HIDDEN = 256
SLC_BLOCK = 64
SLC_TOPN = 16
WIN = 512
N_EXPERTS = 16
N_GROUPS = 4
EXPERTS_PER_GROUP = 4
D_EXPERT = 512
DN_ALPHA = (2 * DEPTH) ** 0.25
LN_EPS = 1e-5
NEG = -1e30
FORCE = 1e4
ROPE_THETA = 10000.0
QK_SCALE = HEAD_DIM ** -0.5
LOG2E = 1.4426950408889634

LANES = 128
OFF_A, OFF_B, OFF_C, OFF_Q, OFF_KCVC, OFF_KK, OFF_VV, OFF_G, IN_PAD = 0, 768, 1280, 1536, 1792, 1920, 2048, 2176, 2304

HALO_A = 8
HALO_B = 32
HALO_C = 16

TS_IN = 512
TQ = 128
KT = 512
TS_OUT = 512
TM_MOE = 1024
ROW_ALIGN = 16
WIN_ROWS = 320
SORT_ROWS = TM_MOE + LANES
SORT_BUF = SORT_ROWS + 2 * LANES
GROUP_LANE = 16
assert SORT_ROWS >= TM_MOE + N_GROUPS * (ROW_ALIGN - 1)
assert SORT_BUF >= TM_MOE + N_GROUPS * (ROW_ALIGN - 1) + WIN_ROWS
VMEM_LIMIT = 56 * 1024 * 1024


def _sigmoid(x):
    return jax.nn.sigmoid(x)


def _layer_norm(h, g, b):
    mu = jnp.mean(h, axis=-1, keepdims=True)
    d = h - mu
    var = jnp.mean(d * d, axis=-1, keepdims=True)
    return d * lax.rsqrt(var + LN_EPS) * g + b


def _dot(a, b):
    return jnp.dot(a, b, preferred_element_type=F32)


def _inproj_kernel(x_ref, w_ref, cos_ref, sin_ref, cva_ref, cvb_ref, cvbb_ref, lng_ref, lnb_ref,
                   poolw_ref, pools_ref,
                   yabc_ref, qt_ref, qrt_ref, kcvc_ref, kk_ref, ksx_ref, vvt_ref, gt_ref,
                   ch_ext, u_ext, p_ext):
    j = pl.program_id(1)
    ts = x_ref.shape[1]
    xb = x_ref[0].astype(BF16)

    def proj(lo, hi):
        return _dot(xb, w_ref[:, lo:hi])

    def carry_halo(ext, halo):
        @pl.when(j == 0)
        def _():
            ext[0:halo, :] = jnp.zeros((halo, ext.shape[1]), F32)

        @pl.when(j > 0)
        def _():
            ext[0:halo, :] = ext[ts:ts + halo, :]

    pa = proj(OFF_A, OFF_B)
    a_b, ch = pa[:, 0:256], pa[:, 256:512] * pa[:, 512:768]
    carry_halo(ch_ext, HALO_A)
    ch_ext[HALO_A:HALO_A + ts, :] = ch
    conv = cva_ref[2:3, :] * ch
    for k in range(SC_KERNEL - 1):
        off = HALO_A - (SC_KERNEL - 1) + k
        conv = conv + cva_ref[k:k + 1, :] * ch_ext[off:off + ts, :]
    yabc_ref[0, :, 0:256] = (a_b * conv).astype(BF16)

    pb = proj(OFF_B, OFF_C)
    u = pb[:, 0:256] * _sigmoid(pb[:, 256:512])
    carry_halo(u_ext, HALO_B)
    u_ext[HALO_B:HALO_B + ts, :] = u
    rows = 64
    for c in range(ts // rows):
        acc = jnp.zeros((rows, 256), F32) + cvbb_ref[...]
        for k in range(CF_KERNEL):
            off = HALO_B - (CF_KERNEL - 1) + k + c * rows
            acc = acc + cvb_ref[k:k + 1, :] * u_ext[off:off + rows, :]
        v = _layer_norm(acc, lng_ref[...], lnb_ref[...])
        yabc_ref[0, c * rows:(c + 1) * rows, 256:512] = (v * _sigmoid(v)).astype(BF16)

    pc = proj(OFF_C, OFF_Q)
    carry_halo(p_ext, HALO_C)
    p_ext[HALO_C:HALO_C + ts, :] = pc
    pe = p_ext[...]
    s2 = pe + pltpu.roll(pe, 1, 0)
    s4 = s2 + pltpu.roll(s2, 2, 0)
    s8 = s4 + pltpu.roll(s4, 4, 0)
    s16 = s8 + pltpu.roll(s8, 8, 0)
    lane = lax.broadcasted_iota(jnp.int32, (ts, 256), 1)
    grp = jnp.right_shift(lane, 6)
    wsum = jnp.where(grp == 0, s2[HALO_C:], jnp.where(grp == 1, s4[HALO_C:], jnp.where(grp == 2, s8[HALO_C:], s16[HALO_C:])))
    win = jnp.where(grp == 0, 2, jnp.where(grp == 1, 4, jnp.where(grp == 2, 8, 16)))
    t1 = j * ts + lax.broadcasted_iota(jnp.int32, (ts, 256), 0) + 1
    cnt = jnp.minimum(win, t1).astype(F32)
    dlt = wsum / cnt - pc
    yabc_ref[0, :, 512:768] = (_dot(dlt.astype(BF16), poolw_ref[...]) * pools_ref[...]).astype(BF16)

    cos4, sin4 = cos_ref[...], sin_ref[...]
    q = proj(OFF_Q, OFF_KCVC)
    lane_q = jnp.bitwise_and(lax.broadcasted_iota(jnp.int32, (ts, 256), 1), HEAD_DIM - 1)
    q_sw = jnp.where(lane_q < HEAD_DIM // 2, pltpu.roll(q, 256 - HEAD_DIM // 2, 1), pltpu.roll(q, HEAD_DIM // 2, 1))
    qr = q * cos4 + q_sw * sin4
    qt_ref[0] = (q * QK_SCALE).T.astype(BF16)
    qrt_ref[0] = (qr * (QK_SCALE * LOG2E)).T.astype(BF16)

    kcvc_ref[0] = proj(OFF_KCVC, OFF_KK)

    k2 = proj(OFF_KK, OFF_VV)
    lane_k = jnp.bitwise_and(lax.broadcasted_iota(jnp.int32, (ts, LANES), 1), HEAD_DIM - 1)
    k_sw = jnp.where(lane_k < HEAD_DIM // 2, pltpu.roll(k2, LANES - HEAD_DIM // 2, 1), pltpu.roll(k2, HEAD_DIM // 2, 1))
    k_rot = k2 * cos4[:, 0:LANES] + k_sw * sin4[:, 0:LANES]
    kk_ref[0] = k_rot.astype(BF16)
    lane_i = lax.broadcasted_iota(jnp.int32, (ts, LANES), 1)
    blk_in_tile = jnp.bitwise_and(jnp.right_shift(j * ts + lax.broadcasted_iota(jnp.int32, (ts, LANES), 0), 6),
                                  KT // SLC_BLOCK - 1)
    ksx_ref[0] = jnp.where(lane_i < HEAD_DIM, k_rot, jnp.where(lane_i - HEAD_DIM == blk_in_tile, 1.0, 0.0)).astype(BF16)

    v2t = proj(OFF_VV, OFF_G).T
    for c in range(ts // LANES):
        vvt_ref[0, c] = v2t[:, c * LANES:(c + 1) * LANES].astype(BF16)

    gt_ref[0] = _sigmoid(proj(OFF_G, IN_PAD)).T[0:16, :]


def _inproj(x, w_in_p, cos4, sin4, cva, cvb, cvbb, lng, lnb, poolw, pools):
    b, s, d = x.shape
    ts = TS_IN
    grid = (b, s // ts)
    const = lambda shape: pl.BlockSpec(shape, lambda bi, ji: (0,) * len(shape))
    out_shape = (
        jax.ShapeDtypeStruct((b, s, 768), BF16),
        jax.ShapeDtypeStruct((b, 256, s), BF16),
        jax.ShapeDtypeStruct((b, 256, s), BF16),
        jax.ShapeDtypeStruct((b, s, LANES), F32),
        jax.ShapeDtypeStruct((b, s, LANES), BF16),
        jax.ShapeDtypeStruct((b, s, LANES), BF16),
        jax.ShapeDtypeStruct((b, s // LANES, LANES, LANES), BF16),
        jax.ShapeDtypeStruct((b, 16, s), F32),
    )
    return pl.pallas_call(
        _inproj_kernel,
        grid=grid,
        in_specs=[
            pl.BlockSpec((1, ts, d), lambda bi, ji: (bi, ji, 0)),
            const((d, IN_PAD)),
            pl.BlockSpec((ts, 256), lambda bi, ji: (ji, 0)),
            pl.BlockSpec((ts, 256), lambda bi, ji: (ji, 0)),
            const((SC_KERNEL, 256)), const((CF_KERNEL, 256)), const((1, 256)), const((1, 256)), const((1, 256)),
            const((256, 256)), const((1, 256)),
        ],
        out_specs=(
            pl.BlockSpec((1, ts, 768), lambda bi, ji: (bi, ji, 0)),
            pl.BlockSpec((1, 256, ts), lambda bi, ji: (bi, 0, ji)),
            pl.BlockSpec((1, 256, ts), lambda bi, ji: (bi, 0, ji)),
            pl.BlockSpec((1, ts, LANES), lambda bi, ji: (bi, ji, 0)),
            pl.BlockSpec((1, ts, LANES), lambda bi, ji: (bi, ji, 0)),
            pl.BlockSpec((1, ts, LANES), lambda bi, ji: (bi, ji, 0)),
            pl.BlockSpec((1, ts // LANES, LANES, LANES), lambda bi, ji: (bi, ji, 0, 0)),
            pl.BlockSpec((1, 16, ts), lambda bi, ji: (bi, 0, ji)),
        ),
        out_shape=out_shape,
        scratch_shapes=[
            pltpu.VMEM((ts + HALO_A, 256), F32),
            pltpu.VMEM((ts + HALO_B, 256), F32),
            pltpu.VMEM((ts + HALO_C, 256), F32),
        ],
        compiler_params=pltpu.CompilerParams(dimension_semantics=("arbitrary", "arbitrary"),
                                             vmem_limit_bytes=VMEM_LIMIT),
        name="inproj_mixers",
    )(x, w_in_p, cos4, sin4, cva, cvb, cvbb, lng, lnb, poolw, pools)


def _compress_kernel(kc_ref, vc_ref, pek_ref, pev_ref, wk1_ref, wk2_ref, wv1_ref, wv2_ref, kcmp_ref, vcmpt_ref):
    def comp(x_ref, pe_ref, w1_ref, w2_ref):
        x = x_ref[0]
        m = x.shape[0]
        half = x.shape[1]
        first = _dot((x + pe_ref[0:1, :]).astype(BF16), w1_ref[0:half, :])
        second = _dot((x + pe_ref[1:2, :]).astype(BF16), w1_ref[half:2 * half, :])
        hid = first + pltpu.roll(second, m - 1, 0)
        return _dot((hid * _sigmoid(hid)).astype(BF16), w2_ref[...])

    kcmp_ref[0] = comp(kc_ref, pek_ref, wk1_ref, wk2_ref).astype(BF16)
    vcmpt_ref[0] = comp(vc_ref, pev_ref, wv1_ref, wv2_ref).T[0:HEAD_DIM, :].astype(BF16)


def _compress(kc16, vc16, pek, pev, wk1, wk2, wv1, wv2):
    b, m, w = kc16.shape
    const = lambda shape: pl.BlockSpec(shape, lambda bi: (0,) * len(shape))
    return pl.pallas_call(
        _compress_kernel,
        grid=(b,),
        in_specs=[
            pl.BlockSpec((1, m, w), lambda bi: (bi, 0, 0)),
            pl.BlockSpec((1, m, w), lambda bi: (bi, 0, 0)),
            const((2, w)), const((2, w)),
            const((2 * w, CMP_HIDDEN)), const((CMP_HIDDEN, LANES)),
            const((2 * w, CMP_HIDDEN)), const((CMP_HIDDEN, LANES)),
        ],
        out_specs=(
            pl.BlockSpec((1, m, LANES), lambda bi: (bi, 0, 0)),
            pl.BlockSpec((1, HEAD_DIM, m), lambda bi: (bi, 0, 0)),
        ),
        out_shape=(jax.ShapeDtypeStruct((b, m, LANES), BF16), jax.ShapeDtypeStruct((b, HEAD_DIM, m), BF16)),
        compiler_params=pltpu.CompilerParams(dimension_semantics=("arbitrary",), vmem_limit_bytes=VMEM_LIMIT),
        name="compress_kv",
    )(kc16, vc16, pek, pev, wk1, wk2, wv1, wv2)


def _nsa_kernel(qt_ref, qrt_ref, gt_ref, kk_ref, ksx_ref, vvt_ref, kcmp_ref, vcmpt_ref, ovt_ref, yd_ref, sel_scr, sa_scr, sb_scr, pa_scr, pb_scr):
    qb = pl.program_id(1)
    tq = qt_ref.shape[2]
    nh = NSA_HEADS
    wq = nh * tq
    s0 = qb * tq
    n_slc = sel_scr.shape[0]
    n_cmp_pad = kcmp_ref.shape[1]

    def stack_heads(ref):
        x = ref[0]
        return jnp.concatenate([x[h * HEAD_DIM:(h + 1) * HEAD_DIM, :] for h in range(nh)], axis=1)

    zeros = jnp.zeros((HEAD_DIM, wq), BF16)
    q_top, qr_top = stack_heads(qt_ref), stack_heads(qrt_ref)
    q_lo = jnp.concatenate([q_top, zeros], axis=0)
    qr_lo = jnp.concatenate([qr_top, zeros], axis=0)
    qr_hi = jnp.concatenate([zeros, qr_top], axis=0)
    t_row = s0 + lax.broadcasted_iota(jnp.int32, (1, tq), 1)
    lanes4 = lambda x: jnp.concatenate([x] * nh, axis=1)
    t_all = lanes4(t_row)

    sc = _dot(kcmp_ref[0], q_lo)
    cmp_end = lax.broadcasted_iota(jnp.int32, (n_cmp_pad, wq), 0) * CMP_STRIDE + (CMP_BLOCK - 1)
    sc = jnp.where(cmp_end <= t_all, sc, NEG)
    mx = jnp.max(sc, axis=0, keepdims=True)
    ex = jnp.exp(sc - mx)
    den = jnp.sum(ex, axis=0, keepdims=True)
    p_cmp = jnp.where(t_all >= CMP_BLOCK - 1, ex * (1.0 / den), 0.0)
    o_cmp = _dot(vcmpt_ref[0], p_cmp.astype(BF16))
    p_heads = p_cmp[:, 0:tq]
    for h in range(1, nh):
        p_heads = p_heads + p_cmp[:, h * tq:(h + 1) * tq]
    imp = _dot(ovt_ref[...], p_heads.astype(BF16))

    blk_i = lax.broadcasted_iota(jnp.int32, (n_slc, tq), 0)
    blk = blk_i.astype(F32)
    cur = jnp.right_shift(t_row, 6)
    forced = (blk_i == 0) | (blk_i == cur) | (blk_i == cur - 1)
    valid = blk_i * SLC_BLOCK <= t_row
    imp = jnp.where(valid, jnp.where(forced, FORCE, imp), NEG)
    sel = jnp.zeros((n_slc, tq), F32)
    for _ in range(min(SLC_TOPN, n_slc)):
        top = jnp.max(imp, axis=0, keepdims=True)
        first = jnp.min(jnp.where(imp == top, blk, float(n_slc)), axis=0, keepdims=True)
        pick = blk == first
        sel = jnp.where(pick, 1.0, sel)
        imp = jnp.where(pick, -jnp.inf, imp)
    own = (blk_i >= 2 * qb) & (blk_i < 2 * qb + tq // SLC_BLOCK)
    sel_scr[...] = jnp.where(valid & (sel > 0.0) & jnp.logical_not(own), 0.0, NEG)

    blocks_per_tile = KT // SLC_BLOCK
    sub = KT // LANES

    def online_update(state, s, v_rows):
        m_i, l_i, acc = state
        m_new = jnp.maximum(m_i, jnp.max(s, axis=0, keepdims=True))
        alpha = jnp.exp2(m_i - m_new)
        p = jnp.exp2(s - m_new)
        l_new = alpha * l_i + jnp.sum(p, axis=0, keepdims=True)
        pb = p.astype(BF16)
        pv = _dot(v_rows[0], pb[0:LANES, :])
        for i in range(1, len(v_rows)):
            pv = pv + _dot(v_rows[i], pb[i * LANES:(i + 1) * LANES, :])
        return m_new, l_new, alpha * acc + pv

    last_tile = kk_ref.shape[1] // KT - 1

    def score_chunks(kt):
        kd = jnp.minimum(kt, last_tile)
        bias8 = sel_scr[pl.ds(pl.multiple_of(kd * blocks_per_tile, blocks_per_tile), blocks_per_tile), :]
        bias8 = jnp.where(kt <= last_tile, bias8, NEG)
        bias_rows = jnp.concatenate([lanes4(bias8), jnp.zeros((16 - blocks_per_tile, wq), F32)], axis=0).astype(BF16)
        q_bias = jnp.concatenate([qr_top, bias_rows, jnp.zeros((HEAD_DIM - 16, wq), BF16)], axis=0)
        for c in range(sub):
            kc0 = pl.multiple_of(kd * KT + c * LANES, LANES)
            yield c, _dot(ksx_ref[0, pl.ds(kc0, LANES), :], q_bias)

    def fold8(x, op):
        out = x[0:8, :]
        for r in range(1, x.shape[0] // 8):
            out = op(out, x[8 * r:8 * r + 8, :])
        return out

    def pv_chunk(vt, c, p_ref):
        return _dot(vt[c, 0:HEAD_DIM, :], p_ref[c * LANES:(c + 1) * LANES, :])

    def v_tile(kt):
        return vvt_ref[0, pl.ds(pl.multiple_of(jnp.minimum(kt, last_tile) * sub, sub), sub)]

    def phase(kt, s_cur, s_next, p_cur, p_prev, state):
        m_i, l_i, acc, mt_cur, alpha_prev = state
        m_new = jnp.maximum(m_i, jnp.max(mt_cur, axis=0, keepdims=True))
        alpha = jnp.exp2(m_i - m_new)
        mt_next = jnp.full((8, wq), NEG, F32)
        l_new = alpha * l_i
        pv = jnp.zeros((HEAD_DIM, wq), F32)
        vt = v_tile(kt - 1) if p_prev is not None else None
        for c, s_n in score_chunks(kt + 1):
            rows = slice(c * LANES, (c + 1) * LANES)
            mt_next = jnp.maximum(mt_next, fold8(s_n, jnp.maximum))
            s_next[rows, :] = s_n
            p = jnp.exp2(s_cur[rows, :] - m_new)
            l_new = l_new + fold8(p, jnp.add)
            p_cur[rows, :] = p.astype(BF16)
            if p_prev is not None:
                pv = pv + pv_chunk(vt, c, p_prev)
        return m_new, l_new, alpha_prev * acc + pv, mt_next, alpha

    def slc_pair(j, state):
        state = phase(2 * j + 1, sb_scr, sa_scr, pb_scr, pa_scr, state)
        return phase(2 * j + 2, sa_scr, sb_scr, pa_scr, pb_scr, state)

    init = (jnp.full((1, wq), NEG, F32), jnp.zeros((1, wq), F32), jnp.zeros((HEAD_DIM, wq), F32))
    mt0 = jnp.full((8, wq), NEG, F32)
    for c, s_n in score_chunks(0):
        mt0 = jnp.maximum(mt0, fold8(s_n, jnp.maximum))
        sa_scr[c * LANES:(c + 1) * LANES, :] = s_n
    key_d = s0 + lax.broadcasted_iota(jnp.int32, (tq, tq), 0)
    s_d = _dot(kk_ref[0, pl.ds(pl.multiple_of(s0, tq), tq), :], qr_lo) + lanes4(jnp.where(key_d <= t_row, 0.0, NEG))
    m_d = jnp.max(s_d, axis=0, keepdims=True)
    p_d = jnp.exp2(s_d - m_d)
    acc_d = _dot(vvt_ref[0, qb][0:HEAD_DIM, :], p_d.astype(BF16))
    state = (m_d, fold8(p_d, jnp.add), acc_d, mt0, jnp.ones((1, wq), F32))
    state = phase(0, sa_scr, sb_scr, pa_scr, None, state)
    n_tiles = (s0 + tq + KT - 1) // KT
    pairs = n_tiles // 2
    _, l8_s, acc_s, _, alpha_last = lax.fori_loop(0, pairs, slc_pair, state)
    vt_last = v_tile(2 * pairs)
    pv = pv_chunk(vt_last, 0, pa_scr)
    for c in range(1, sub):
        pv = pv + pv_chunk(vt_last, c, pa_scr)
    acc_s = alpha_last * acc_s + pv
    l_s = jnp.sum(l8_s, axis=0, keepdims=True)

    wkeys = WIN + tq
    k0 = pl.multiple_of(jnp.maximum(s0 - WIN, 0), LANES)
    key = k0 + lax.broadcasted_iota(jnp.int32, (wkeys, tq), 0)
    wbias = jnp.where((key <= t_row) & (key > t_row - WIN), 0.0, NEG)
    s_w = _dot(kk_ref[0, pl.ds(k0, wkeys), :], qr_hi) + lanes4(wbias)
    vt_w = vvt_ref[0, pl.ds(jnp.maximum(qb - WIN // tq, 0), wkeys // LANES)]
    _, l_w, acc_w = online_update(init, s_w, [vt_w[i, HEAD_DIM:2 * HEAD_DIM, :] for i in range(wkeys // LANES)])

    g = gt_ref[0]
    gate = lambda br: jnp.concatenate([g[h * 3 + br:h * 3 + br + 1, :] for h in range(nh)], axis=1)
    o = gate(0) * o_cmp + gate(1) * (acc_s * (1.0 / l_s)) + gate(2) * (acc_w * (1.0 / l_w))
    o_rows = jnp.concatenate([o[:, h * tq:(h + 1) * tq] for h in range(nh)], axis=0)
    yd_ref[0] = o_rows.T.astype(BF16)


def _nsa(qt, qrt, gt, kk, ksx, vvt, kcmp, vcmpt, ovt):
    b, _, s = qt.shape
    n_slc = s // SLC_BLOCK
    m = kcmp.shape[1]
    return pl.pallas_call(
        _nsa_kernel,
        grid=(b, s // TQ),
        in_specs=[
            pl.BlockSpec((1, 256, TQ), lambda bi, qi: (bi, 0, qi)),
            pl.BlockSpec((1, 256, TQ), lambda bi, qi: (bi, 0, qi)),
            pl.BlockSpec((1, 16, TQ), lambda bi, qi: (bi, 0, qi)),
            pl.BlockSpec((1, s, LANES), lambda bi, qi: (bi, 0, 0)),
            pl.BlockSpec((1, s, LANES), lambda bi, qi: (bi, 0, 0)),
            pl.BlockSpec((1, s // LANES, LANES, LANES), lambda bi, qi: (bi, 0, 0, 0)),
            pl.BlockSpec((1, m, LANES), lambda bi, qi: (bi, 0, 0)),
            pl.BlockSpec((1, HEAD_DIM, m), lambda bi, qi: (bi, 0, 0)),
            pl.BlockSpec((n_slc, m), lambda bi, qi: (0, 0)),
        ],
        out_specs=pl.BlockSpec((1, TQ, 256), lambda bi, qi: (bi, qi, 0)),
        out_shape=jax.ShapeDtypeStruct((b, s, 256), BF16),
        scratch_shapes=[pltpu.VMEM((n_slc, TQ), F32),
                        pltpu.VMEM((KT, NSA_HEADS * TQ), F32), pltpu.VMEM((KT, NSA_HEADS * TQ), F32),
                        pltpu.VMEM((KT, NSA_HEADS * TQ), BF16), pltpu.VMEM((KT, NSA_HEADS * TQ), BF16)],
        compiler_params=pltpu.CompilerParams(dimension_semantics=("arbitrary", "arbitrary"),
                                             vmem_limit_bytes=VMEM_LIMIT),
        name="sparse_attention",
    )(qt, qrt, gt, kk, ksx, vvt, kcmp, vcmpt, ovt)


def _outproj_kernel(x_ref, yabc_ref, yd_ref, wo_ref, g_ref, b_ref, rw_ref, rb_ref, x1_ref, gates_ref):
    mix = _dot(yabc_ref[...], wo_ref[0:768, :]) + _dot(yd_ref[...], wo_ref[768:1024, :])
    x1 = _layer_norm(DN_ALPHA * x_ref[...] + mix, g_ref[...], b_ref[...])
    x1_ref[...] = x1

    ts = x1.shape[0]
    aff = _sigmoid(_dot(x1.astype(BF16), rw_ref[...])).T[0:N_EXPERTS, :]
    biased = aff + rb_ref[...]
    row = lambda a, i: a[i:i + 1, :]

    gscores = []
    for gi in range(N_GROUPS):
        a, b, c, d = (row(biased, gi * EXPERTS_PER_GROUP + i) for i in range(EXPERTS_PER_GROUP))
        hi1, lo1, hi2, lo2 = jnp.maximum(a, b), jnp.minimum(a, b), jnp.maximum(c, d), jnp.minimum(c, d)
        gscores.append(jnp.maximum(hi1, hi2) + jnp.maximum(jnp.minimum(hi1, hi2), jnp.maximum(lo1, lo2)))
    best, gsel = gscores[0], jnp.zeros((1, ts), jnp.int32)
    for gi in range(1, N_GROUPS):
        better = gscores[gi] > best
        gsel = jnp.where(better, gi, gsel)
        best = jnp.where(better, gscores[gi], best)

    eid_i = lax.broadcasted_iota(jnp.int32, (N_EXPERTS, ts), 0)
    eid = eid_i.astype(F32)
    masked = jnp.where(jnp.right_shift(eid_i, 2) == gsel, biased, NEG)
    picks = []
    for _ in range(2):
        top = jnp.max(masked, axis=0, keepdims=True)
        first = jnp.min(jnp.where(masked == top, eid, float(N_EXPERTS)), axis=0, keepdims=True)
        pick = eid == first
        picks.append(pick)
        masked = jnp.where(pick, -jnp.inf, masked)
    chosen = picks[0] | picks[1]
    w_sel = jnp.where(chosen, aff, 0.0)
    gates_t = w_sel / jnp.sum(w_sel, axis=0, keepdims=True)
    grp8 = jnp.where(lax.broadcasted_iota(jnp.int32, (8, ts), 0) == 0, gsel.astype(F32), 0.0)
    gates_ref[...] = jnp.concatenate([gates_t, grp8, jnp.zeros((LANES - N_EXPERTS - 8, ts), F32)], axis=0).T


def _outproj(x2d, yabc, yd, wo, g, b, rw, rb):
    n, d = x2d.shape
    ts = TS_OUT
    const = lambda shape: pl.BlockSpec(shape, lambda i: (0,) * len(shape))
    return pl.pallas_call(
        _outproj_kernel,
        grid=(n // ts,),
        in_specs=[
            pl.BlockSpec((ts, d), lambda i: (i, 0)),
            pl.BlockSpec((ts, 768), lambda i: (i, 0)),
            pl.BlockSpec((ts, 256), lambda i: (i, 0)),
            const((d, d)), const((1, d)), const((1, d)), const((d, LANES)), const((N_EXPERTS, 1)),
        ],
        out_specs=(pl.BlockSpec((ts, d), lambda i: (i, 0)), pl.BlockSpec((ts, LANES), lambda i: (i, 0))),
        out_shape=(jax.ShapeDtypeStruct((n, d), F32), jax.ShapeDtypeStruct((n, LANES), F32)),
        compiler_params=pltpu.CompilerParams(dimension_semantics=("arbitrary",), vmem_limit_bytes=VMEM_LIMIT),
        name="outproj_ln_route",
    )(x2d, yabc, yd, wo, g, b, rw, rb)


def _split3(x):
    hi = x.astype(BF16)
    r1 = x - hi.astype(F32)
    mid = r1.astype(BF16)
    lo = (r1 - mid.astype(F32)).astype(BF16)
    return hi, mid, lo


def _dispatch_kernel(x1_ref, gates_ref, tri_ref, xs_ref, gs_ref, post_ref, cnt_ref):
    tm = x1_ref.shape[0]
    gates = gates_ref[...]
    gsel = gates.T[GROUP_LANE:GROUP_LANE + 1, :]
    onehot = jnp.where(lax.broadcasted_iota(jnp.int32, (8, tm), 0).astype(F32) == gsel, 1.0, 0.0)
    n = jnp.sum(onehot, axis=1, keepdims=True)
    n_al = jnp.floor((n + (ROW_ALIGN - 1)) * (1.0 / ROW_ALIGN)) * ROW_ALIGN
    starts = [jnp.zeros((1, 1), F32)]
    for gi in range(1, N_GROUPS):
        starts.append(starts[-1] + n_al[gi - 1:gi, :])
    start = jnp.concatenate(starts + [jnp.zeros((8 - N_GROUPS, 1), F32)], axis=0)
    before = _dot(onehot.astype(BF16), tri_ref[...])
    pos = jnp.sum(onehot * (start + before), axis=0, keepdims=True)
    perm = jnp.where(lax.broadcasted_iota(jnp.int32, (SORT_ROWS, tm), 0).astype(F32) == pos, 1.0, 0.0).astype(BF16)

    xs_ref[0, 0:SORT_ROWS, :] = _dot(perm, x1_ref[...].astype(BF16)).astype(BF16)
    xs_ref[0, SORT_ROWS:SORT_BUF, :] = jnp.zeros((SORT_BUF - SORT_ROWS, xs_ref.shape[2]), BF16)
    g_hi, g_mid, g_lo = _split3(gates)
    gs_ref[0, 0:SORT_ROWS, :] = _dot(perm, g_hi) + _dot(perm, g_mid) + _dot(perm, g_lo)
    gs_ref[0, SORT_ROWS:SORT_BUF, :] = jnp.zeros((SORT_BUF - SORT_ROWS, LANES), F32)
    pos8 = jnp.where(lax.broadcasted_iota(jnp.int32, (8, tm), 0) == 0, pos, 0.0)
    post_ref[...] = jnp.concatenate([pos8, jnp.zeros((LANES - 8, tm), F32)], axis=0).T
    cnt_ref[0] = jnp.broadcast_to(n, (8, LANES))


def _dispatch(x1, gates, tri):
    n, d = x1.shape
    tm = TM_MOE
    nt = n // tm
    return pl.pallas_call(
        _dispatch_kernel,
        grid=(nt,),
        in_specs=[
            pl.BlockSpec((tm, d), lambda i: (i, 0)),
            pl.BlockSpec((tm, LANES), lambda i: (i, 0)),
            pl.BlockSpec((tm, tm), lambda i: (0, 0)),
        ],
        out_specs=(
            pl.BlockSpec((1, SORT_BUF, d), lambda i: (i, 0, 0)),
            pl.BlockSpec((1, SORT_BUF, LANES), lambda i: (i, 0, 0)),
            pl.BlockSpec((tm, LANES), lambda i: (i, 0)),
            pl.BlockSpec((1, 8, LANES), lambda i: (i, 0, 0)),
        ),
        out_shape=(
            jax.ShapeDtypeStruct((nt, SORT_BUF, d), BF16),
            jax.ShapeDtypeStruct((nt, SORT_BUF, LANES), F32),
            jax.ShapeDtypeStruct((n, LANES), F32),
            jax.ShapeDtypeStruct((nt, 8, LANES), F32),
        ),
        compiler_params=pltpu.CompilerParams(dimension_semantics=("arbitrary",), vmem_limit_bytes=VMEM_LIMIT),
        name="moe_dispatch",
    )(x1, gates, tri)


def _moe_kernel(start_ref, trips_ref, x1_ref, xs_ref, gs_ref, post_ref, wg_ref, wu_ref, wd_ref, g_ref, b_ref,
                out_ref, acc_scr):
    i, e = pl.program_id(0), pl.program_id(1)

    @pl.when(e == 0)
    def _():
        acc_scr[...] = jnp.zeros(acc_scr.shape, F32)

    grp = lax.shift_right_logical(e, 2)
    row0 = start_ref[i * N_GROUPS + grp]

    def window(w, carry):
        rows = pl.ds(pl.multiple_of(row0 + w * WIN_ROWS, ROW_ALIGN), WIN_ROWS)
        xw = xs_ref[0, rows, :]
        hg = _dot(xw, wg_ref[0])
        h = hg * _sigmoid(hg) * _dot(xw, wu_ref[0])
        y = _dot(h.astype(BF16), wd_ref[0])
        gw = gs_ref[0, rows, :]
        lane = lax.broadcasted_iota(jnp.int32, gw.shape, 1)
        gcol = jnp.sum(jnp.where(lane == e, gw, 0.0), axis=1, keepdims=True)
        acc_scr[rows, :] += y * gcol
        return carry

    lax.fori_loop(0, trips_ref[i * N_GROUPS + grp], window, 0)

    @pl.when(e == pl.num_programs(1) - 1)
    def _():
        tm = x1_ref.shape[0]
        acc = acc_scr[0:SORT_ROWS, :]
        hi = acc.astype(BF16)
        lo = (acc - hi.astype(F32)).astype(BF16)
        pos = post_ref[...][:, 0:1]
        unperm = jnp.where(lax.broadcasted_iota(jnp.int32, (tm, SORT_ROWS), 1).astype(F32) == pos, 1.0, 0.0).astype(BF16)
        moe = _dot(unperm, hi) + _dot(unperm, lo)
        out_ref[...] = _layer_norm(DN_ALPHA * x1_ref[...] + moe, g_ref[...], b_ref[...])


def _moe(start, trips, x1, xs, gs, post, wg, wu, wd, g, b):
    n, d = x1.shape
    tm = TM_MOE
    ne, _, de = wg.shape
    const = lambda shape: pl.BlockSpec(shape, lambda i, e, *_: (0,) * len(shape))
    grid_spec = pltpu.PrefetchScalarGridSpec(
        num_scalar_prefetch=2,
        grid=(n // tm, ne),
        in_specs=[
            pl.BlockSpec((tm, d), lambda i, e, *_: (i, 0)),
            pl.BlockSpec((1, SORT_BUF, d), lambda i, e, *_: (i, 0, 0)),
            pl.BlockSpec((1, SORT_BUF, LANES), lambda i, e, *_: (i, 0, 0)),
            pl.BlockSpec((tm, LANES), lambda i, e, *_: (i, 0)),
            pl.BlockSpec((1, d, de), lambda i, e, *_: (e, 0, 0)),
            pl.BlockSpec((1, d, de), lambda i, e, *_: (e, 0, 0)),
            pl.BlockSpec((1, de, d), lambda i, e, *_: (e, 0, 0)),
            const((1, d)), const((1, d)),
        ],
        out_specs=pl.BlockSpec((tm, d), lambda i, e, *_: (i, 0)),
        scratch_shapes=[pltpu.VMEM((SORT_BUF, d), F32)],
    )
    return pl.pallas_call(
        _moe_kernel,
        grid_spec=grid_spec,
        out_shape=jax.ShapeDtypeStruct((n, d), F32),
        compiler_params=pltpu.CompilerParams(dimension_semantics=("arbitrary", "arbitrary"),
                                             vmem_limit_bytes=VMEM_LIMIT),
        name="moe_ln",
    )(start, trips, x1, xs, gs, post, wg, wu, wd, g, b)


def _group_windows(cnt):
    n = cnt[:, 0:N_GROUPS, 0].astype(jnp.int32)
    n_al = (n + (ROW_ALIGN - 1)) // ROW_ALIGN * ROW_ALIGN
    start = jnp.cumsum(n_al, axis=1) - n_al
    trips = (n + (WIN_ROWS - 1)) // WIN_ROWS
    return start.reshape(-1), trips.reshape(-1)


def _permute_w_in(w):
    sizes = (256, 256, 256, 256, 256, 256, 256, 64, 64, 64, 64, 64, 64, 12)
    offs = np.concatenate([[0], np.cumsum(sizes)])
    part = lambda i: w[:, int(offs[i]):int(offs[i + 1])]
    a_b, a_c, a_h, b_val, b_gate, c_p, q, kc, vc, ks, vs, kw, vw, g = (part(i) for i in range(14))
    pad = jnp.zeros((w.shape[0], LANES - g.shape[1]), w.dtype)
    return jnp.concatenate([a_b, a_c, a_h, b_val, b_gate, c_p, q, kc, vc, ks, kw, vs, vw, g, pad], axis=1).astype(BF16)


def _rope_tables(s):
    inv = 1.0 / (ROPE_THETA ** (jnp.arange(0, HEAD_DIM, 2, dtype=F32) / HEAD_DIM))
    ang = jnp.arange(s, dtype=F32)[:, None] * inv[None, :]
    ang = jnp.concatenate([ang, ang], -1)
    sign = jnp.concatenate([-jnp.ones((HEAD_DIM // 2,), F32), jnp.ones((HEAD_DIM // 2,), F32)])
    return jnp.tile(jnp.cos(ang), (1, NSA_HEADS)), jnp.tile(jnp.sin(ang) * sign[None, :], (1, NSA_HEADS))


def _overlap_t(s, m):
    n_cmp = (s - CMP_BLOCK) // CMP_STRIDE + 1
    cmp_start = np.arange(m) * CMP_STRIDE
    slc_start = np.arange(s // SLC_BLOCK) * SLC_BLOCK
    ov = (cmp_start[None, :] < slc_start[:, None] + SLC_BLOCK) & (cmp_start[None, :] + CMP_BLOCK > slc_start[:, None])
    ov = ov & (np.arange(m)[None, :] < n_cmp)
    return jnp.asarray(ov, BF16)


def _block_diag(w):
    g, c, _ = w.shape
    out = jnp.zeros((g * c, g * c), w.dtype)
    for i in range(g):
        out = out.at[i * c:(i + 1) * c, i * c:(i + 1) * c].set(w[i])
    return out


def kernel(x, w_in, conv_a_w, conv_b_w, conv_b_b, cf_ln_g, cf_ln_b, pool_w, pool_scale, cmp_pe_k, cmp_pe_v, cmp_k_w1, cmp_k_w2, cmp_v_w1, cmp_v_w2, w_out, ln1_g, ln1_b, ln2_g, ln2_b, router_w, router_bias, moe_w_gate, moe_w_up, moe_w_down):
    b, s, d = x.shape
    n = b * s
    m = s // CMP_STRIDE
    tok_per_row = CMP_STRIDE
    cos4, sin4 = _rope_tables(s)
    ovt = _overlap_t(s, m)
    rw = jnp.concatenate([router_w, jnp.zeros((d, LANES - N_EXPERTS), router_w.dtype)], axis=1).astype(BF16)
    rb = router_bias.reshape(N_EXPERTS, 1).astype(F32)
    pad_w2 = lambda w: jnp.concatenate([w, jnp.zeros((CMP_HIDDEN, LANES - HEAD_DIM), w.dtype)], axis=1).astype(BF16)
    row = lambda v: v.reshape(1, -1)
    tri = jnp.asarray(np.triu(np.ones((TM_MOE, TM_MOE), np.float32), k=1), BF16)

    for l in range(DEPTH):
        yabc, qt, qrt, kcvc, kk, ksx, vvt, gt = _inproj(
            x, _permute_w_in(w_in[l]), cos4, sin4, conv_a_w[l], conv_b_w[l], row(conv_b_b[l]),
            row(cf_ln_g[l]), row(cf_ln_b[l]), _block_diag(pool_w[l]).astype(BF16), row(pool_scale[l]))
        kc16 = kcvc[:, :, 0:HEAD_DIM].reshape(b, m, tok_per_row * HEAD_DIM)
        vc16 = kcvc[:, :, HEAD_DIM:2 * HEAD_DIM].reshape(b, m, tok_per_row * HEAD_DIM)
        kcmp, vcmpt = _compress(
            kc16, vc16, cmp_pe_k[l].reshape(2, tok_per_row * HEAD_DIM), cmp_pe_v[l].reshape(2, tok_per_row * HEAD_DIM),
            cmp_k_w1[l].astype(BF16), pad_w2(cmp_k_w2[l]), cmp_v_w1[l].astype(BF16), pad_w2(cmp_v_w2[l]))
        yd = _nsa(qt, qrt, gt, kk, ksx, vvt, kcmp, vcmpt, ovt)
        x1, gates = _outproj(x.reshape(n, d), yabc.reshape(n, 768), yd.reshape(n, 256), w_out[l].astype(BF16),
                             row(ln1_g[l]), row(ln1_b[l]), rw, rb)
        xs, gs, post, cnt = _dispatch(x1, gates, tri)
        start, trips = _group_windows(cnt)
        x = _moe(start, trips, x1, xs, gs, post, moe_w_gate[l].astype(BF16), moe_w_up[l].astype(BF16),
                 moe_w_down[l].astype(BF16), row(ln2_g[l]), row(ln2_b[l])).reshape(b, s, d)
    return x
```

```python
import functools

import jax
import jax.numpy as jnp
import numpy as np
from jax import lax
from jax.experimental import pallas as pl
from jax.experimental.pallas import tpu as pltpu

F32 = jnp.float32
BF16 = jnp.bfloat16

D_MODEL = 1024
DEPTH = 2
GROUP_W = 256
HEAD_DIM = 64
SC_KERNEL = 3
CF_KERNEL = 31
POOL_WINDOWS = (2, 4, 8, 16)
POOL_GW = 64
NSA_HEADS = 4
CMP_BLOCK = 32
CMP_STRIDE = 16
CMP_HIDDEN = 256
SLC_BLOCK = 64
SLC_TOPN = 16
WIN = 512
N_EXPERTS = 16
N_GROUPS = 4
EXPERTS_PER_GROUP = 4
D_EXPERT = 512
DN_ALPHA = (2 * DEPTH) ** 0.25
LN_EPS = 1e-5
NEG = -1e30
FORCE = 1e4
ROPE_THETA = 10000.0
QK_SCALE = HEAD_DIM ** -0.5
LOG2E = 1.4426950408889634

LANES = 128
OFF_A, OFF_B, OFF_C, OFF_Q, OFF_KCVC, OFF_KK, OFF_VV, OFF_G, IN_PAD = 0, 768, 1280, 1536, 1792, 1920, 2048, 2176, 2304

HALO_A = 8
HALO_B = 32
HALO_C = 16

TS_IN = 512
TQ = 128
KT = 512
TS_OUT = 512
TM_MOE = 1024
ROW_ALIGN = 16
WIN_ROWS = 288
SORT_ROWS = TM_MOE + LANES
SORT_BUF = SORT_ROWS + 2 * LANES
GROUP_LANE = 16
assert SORT_ROWS >= TM_MOE + N_GROUPS * (ROW_ALIGN - 1)
assert SORT_BUF >= TM_MOE + N_GROUPS * (ROW_ALIGN - 1) + WIN_ROWS
VMEM_LIMIT = 56 * 1024 * 1024


def _sigmoid(x):
    return jax.nn.sigmoid(x)


def _layer_norm(h, g, b):
    mu = jnp.mean(h, axis=-1, keepdims=True)
    d = h - mu
    var = jnp.mean(d * d, axis=-1, keepdims=True)
    return d * lax.rsqrt(var + LN_EPS) * g + b


def _dot(a, b):
    return jnp.dot(a, b, preferred_element_type=F32)


def _inproj_kernel(x_ref, w_ref, cos_ref, sin_ref, cva_ref, cvb_ref, cvbb_ref, lng_ref, lnb_ref,
                   poolw_ref, pools_ref,
                   yabc_ref, qt_ref, qrt_ref, kcvc_ref, kk_ref, ksx_ref, vvt_ref, gt_ref,
                   ch_ext, u_ext, p_ext, u_sh):
    j = pl.program_id(1)
    ts = x_ref.shape[1]
    xb = x_ref[0].astype(BF16)

    def proj(lo, hi):
        return _dot(xb, w_ref[:, lo:hi])

    def carry_halo(ext, halo):
        @pl.when(j == 0)
        def _():
            ext[0:halo, :] = jnp.zeros((halo, ext.shape[1]), F32)

        @pl.when(j > 0)
        def _():
            ext[0:halo, :] = ext[ts:ts + halo, :]

    pa = proj(OFF_A, OFF_B)
    a_b, ch = pa[:, 0:256], pa[:, 256:512] * pa[:, 512:768]
    carry_halo(ch_ext, HALO_A)
    ch_ext[HALO_A:HALO_A + ts, :] = ch
    conv = cva_ref[2:3, :] * ch
    for k in range(SC_KERNEL - 1):
        off = HALO_A - (SC_KERNEL - 1) + k
        conv = conv + cva_ref[k:k + 1, :] * ch_ext[off:off + ts, :]
    yabc_ref[0, :, 0:256] = (a_b * conv).astype(BF16)

    pb = proj(OFF_B, OFF_C)
    u = pb[:, 0:256] * _sigmoid(pb[:, 256:512])
    carry_halo(u_ext, HALO_B)
    u_ext[HALO_B:HALO_B + ts, :] = u
    span = u_sh.shape[1]
    for r in range(1, 8):
        u_sh[r - 1, :, :] = u_ext[r:r + span, :]
    rows = 64
    for c in range(ts // rows):
        acc = jnp.zeros((rows, 256), F32) + cvbb_ref[...]
        for k in range(CF_KERNEL):
            off = HALO_B - (CF_KERNEL - 1) + k
            base = off // 8 * 8 + c * rows
            src = u_ext if off % 8 == 0 else u_sh.at[off % 8 - 1]
            acc = acc + cvb_ref[k:k + 1, :] * src[base:base + rows, :]
        v = _layer_norm(acc, lng_ref[...], lnb_ref[...])
        yabc_ref[0, c * rows:(c + 1) * rows, 256:512] = (v * _sigmoid(v)).astype(BF16)

    pc = proj(OFF_C, OFF_Q)
    carry_halo(p_ext, HALO_C)
    p_ext[HALO_C:HALO_C + ts, :] = pc
    pe = p_ext[...]
    s2 = pe + pltpu.roll(pe, 1, 0)
    s4 = s2 + pltpu.roll(s2, 2, 0)
    s8 = s4 + pltpu.roll(s4, 4, 0)
    s16 = s8 + pltpu.roll(s8, 8, 0)
    lane = lax.broadcasted_iota(jnp.int32, (ts, 256), 1)
    grp = jnp.right_shift(lane, 6)
    wsum = jnp.where(grp == 0, s2[HALO_C:], jnp.where(grp == 1, s4[HALO_C:], jnp.where(grp == 2, s8[HALO_C:], s16[HALO_C:])))
    win = jnp.where(grp == 0, 2, jnp.where(grp == 1, 4, jnp.where(grp == 2, 8, 16)))
    t1 = j * ts + lax.broadcasted_iota(jnp.int32, (ts, 256), 0) + 1
    cnt = jnp.minimum(win, t1).astype(F32)
    dlt = wsum / cnt - pc
    yabc_ref[0, :, 512:768] = (_dot(dlt.astype(BF16), poolw_ref[...]) * pools_ref[...]).astype(BF16)

    cos4, sin4 = cos_ref[...], sin_ref[...]
    q = proj(OFF_Q, OFF_KCVC)
    lane_q = jnp.bitwise_and(lax.broadcasted_iota(jnp.int32, (ts, 256), 1), HEAD_DIM - 1)
    q_sw = jnp.where(lane_q < HEAD_DIM // 2, pltpu.roll(q, 256 - HEAD_DIM // 2, 1), pltpu.roll(q, HEAD_DIM // 2, 1))
    qr = q * cos4 + q_sw * sin4
    qt_ref[0] = (q * QK_SCALE).T.astype(BF16)
    qrt_ref[0] = (qr * (QK_SCALE * LOG2E)).T.astype(BF16)

    kcvc_ref[0] = proj(OFF_KCVC, OFF_KK)

    k2 = proj(OFF_KK, OFF_VV)
    lane_k = jnp.bitwise_and(lax.broadcasted_iota(jnp.int32, (ts, LANES), 1), HEAD_DIM - 1)
    k_sw = jnp.where(lane_k < HEAD_DIM // 2, pltpu.roll(k2, LANES - HEAD_DIM // 2, 1), pltpu.roll(k2, HEAD_DIM // 2, 1))
    k_rot = k2 * cos4[:, 0:LANES] + k_sw * sin4[:, 0:LANES]
    kk_ref[0] = k_rot.astype(BF16)
    lane_i = lax.broadcasted_iota(jnp.int32, (ts, LANES), 1)
    blk_in_tile = jnp.bitwise_and(jnp.right_shift(j * ts + lax.broadcasted_iota(jnp.int32, (ts, LANES), 0), 6),
                                  KT // SLC_BLOCK - 1)
    ksx_ref[0] = jnp.where(lane_i < HEAD_DIM, k_rot, jnp.where(lane_i - HEAD_DIM == blk_in_tile, 1.0, 0.0)).astype(BF16)

    v2t = proj(OFF_VV, OFF_G).T
    for c in range(ts // LANES):
        vvt_ref[0, c] = v2t[:, c * LANES:(c + 1) * LANES].astype(BF16)

    gt_ref[0] = _sigmoid(proj(OFF_G, IN_PAD)).T[0:16, :]


def _inproj(x, w_in_p, cos4, sin4, cva, cvb, cvbb, lng, lnb, poolw, pools):
    b, s, d = x.shape
    ts = TS_IN
    grid = (b, s // ts)
    const = lambda shape: pl.BlockSpec(shape, lambda bi, ji: (0,) * len(shape))
    out_shape = (
        jax.ShapeDtypeStruct((b, s, 768), BF16),
        jax.ShapeDtypeStruct((b, 256, s), BF16),
        jax.ShapeDtypeStruct((b, 256, s), BF16),
        jax.ShapeDtypeStruct((b, s, LANES), F32),
        jax.ShapeDtypeStruct((b, s, LANES), BF16),
        jax.ShapeDtypeStruct((b, s, LANES), BF16),
        jax.ShapeDtypeStruct((b, s // LANES, LANES, LANES), BF16),
        jax.ShapeDtypeStruct((b, 16, s), F32),
    )
    return pl.pallas_call(
        _inproj_kernel,
        grid=grid,
        in_specs=[
            pl.BlockSpec((1, ts, d), lambda bi, ji: (bi, ji, 0)),
            const((d, IN_PAD)),
            pl.BlockSpec((ts, 256), lambda bi, ji: (ji, 0)),
            pl.BlockSpec((ts, 256), lambda bi, ji: (ji, 0)),
            const((SC_KERNEL, 256)), const((CF_KERNEL, 256)), const((1, 256)), const((1, 256)), const((1, 256)),
            const((256, 256)), const((1, 256)),
        ],
        out_specs=(
            pl.BlockSpec((1, ts, 768), lambda bi, ji: (bi, ji, 0)),
            pl.BlockSpec((1, 256, ts), lambda bi, ji: (bi, 0, ji)),
            pl.BlockSpec((1, 256, ts), lambda bi, ji: (bi, 0, ji)),
            pl.BlockSpec((1, ts, LANES), lambda bi, ji: (bi, ji, 0)),
            pl.BlockSpec((1, ts, LANES), lambda bi, ji: (bi, ji, 0)),
            pl.BlockSpec((1, ts, LANES), lambda bi, ji: (bi, ji, 0)),
            pl.BlockSpec((1, ts // LANES, LANES, LANES), lambda bi, ji: (bi, ji, 0, 0)),
            pl.BlockSpec((1, 16, ts), lambda bi, ji: (bi, 0, ji)),
        ),
        out_shape=out_shape,
        scratch_shapes=[
            pltpu.VMEM((ts + HALO_A, 256), F32),
            pltpu.VMEM((ts + HALO_B, 256), F32),
            pltpu.VMEM((ts + HALO_C, 256), F32),
            pltpu.VMEM((7, ts + HALO_B - 8, 256), F32),
        ],
        compiler_params=pltpu.CompilerParams(dimension_semantics=("arbitrary", "arbitrary"),
                                             vmem_limit_bytes=VMEM_LIMIT),
        name="inproj_mixers",
    )(x, w_in_p, cos4, sin4, cva, cvb, cvbb, lng, lnb, poolw, pools)


def _compress_kernel(kc_ref, vc_ref, pek_ref, pev_ref, wk1_ref, wk2_ref, wv1_ref, wv2_ref, kcmp_ref, vcmpt_ref):
    def comp(x_ref, pe_ref, w1_ref, w2_ref):
        x = x_ref[0]
        m = x.shape[0]
        half = x.shape[1]
        first = _dot((x + pe_ref[0:1, :]).astype(BF16), w1_ref[0:half, :])
        second = _dot((x + pe_ref[1:2, :]).astype(BF16), w1_ref[half:2 * half, :])
        hid = first + pltpu.roll(second, m - 1, 0)
        return _dot((hid * _sigmoid(hid)).astype(BF16), w2_ref[...])

    kcmp_ref[0] = comp(kc_ref, pek_ref, wk1_ref, wk2_ref).astype(BF16)
    vcmpt_ref[0] = comp(vc_ref, pev_ref, wv1_ref, wv2_ref).T[0:HEAD_DIM, :].astype(BF16)


def _compress(kc16, vc16, pek, pev, wk1, wk2, wv1, wv2):
    b, m, w = kc16.shape
    const = lambda shape: pl.BlockSpec(shape, lambda bi: (0,) * len(shape))
    return pl.pallas_call(
        _compress_kernel,
        grid=(b,),
        in_specs=[
            pl.BlockSpec((1, m, w), lambda bi: (bi, 0, 0)),
            pl.BlockSpec((1, m, w), lambda bi: (bi, 0, 0)),
            const((2, w)), const((2, w)),
            const((2 * w, CMP_HIDDEN)), const((CMP_HIDDEN, LANES)),
            const((2 * w, CMP_HIDDEN)), const((CMP_HIDDEN, LANES)),
        ],
        out_specs=(
            pl.BlockSpec((1, m, LANES), lambda bi: (bi, 0, 0)),
            pl.BlockSpec((1, HEAD_DIM, m), lambda bi: (bi, 0, 0)),
        ),
        out_shape=(jax.ShapeDtypeStruct((b, m, LANES), BF16), jax.ShapeDtypeStruct((b, HEAD_DIM, m), BF16)),
        compiler_params=pltpu.CompilerParams(dimension_semantics=("arbitrary",), vmem_limit_bytes=VMEM_LIMIT),
        name="compress_kv",
    )(kc16, vc16, pek, pev, wk1, wk2, wv1, wv2)


def _nsa_kernel(qt_ref, qrt_ref, gt_ref, kk_ref, ksx_ref, vvt_ref, kcmp_ref, vcmpt_ref, ovt_ref, yd_ref, sel_scr, sa_scr, sb_scr, pa_scr, pb_scr):
    qb = pl.program_id(1)
    tq = qt_ref.shape[2]
    nh = NSA_HEADS
    wq = nh * tq
    s0 = qb * tq
    n_slc = sel_scr.shape[0]
    n_cmp_pad = kcmp_ref.shape[1]

    def stack_heads(ref):
        x = ref[0]
        return jnp.concatenate([x[h * HEAD_DIM:(h + 1) * HEAD_DIM, :] for h in range(nh)], axis=1)

    zeros = jnp.zeros((HEAD_DIM, wq), BF16)
    q_top, qr_top = stack_heads(qt_ref), stack_heads(qrt_ref)
    q_lo = jnp.concatenate([q_top, zeros], axis=0)
    qr_lo = jnp.concatenate([qr_top, zeros], axis=0)
    qr_hi = jnp.concatenate([zeros, qr_top], axis=0)
    t_row = s0 + lax.broadcasted_iota(jnp.int32, (1, tq), 1)
    lanes4 = lambda x: jnp.concatenate([x] * nh, axis=1)
    t_all = lanes4(t_row)

    def compress_and_select(nc):
        nb = nc // (SLC_BLOCK // CMP_STRIDE)
        sc = _dot(kcmp_ref[0, 0:nc, :], q_lo)
        cmp_end = lax.broadcasted_iota(jnp.int32, (nc, wq), 0) * CMP_STRIDE + (CMP_BLOCK - 1)
        sc = jnp.where(cmp_end <= t_all, sc, NEG)
        mx = jnp.max(sc, axis=0, keepdims=True)
        ex = jnp.exp(sc - mx)
        den = jnp.sum(ex, axis=0, keepdims=True)
        p_cmp = jnp.where(t_all >= CMP_BLOCK - 1, ex * (1.0 / den), 0.0)
        o_cmp = _dot(vcmpt_ref[0, :, 0:nc], p_cmp.astype(BF16))
        p_heads = p_cmp[:, 0:tq]
        for h in range(1, nh):
            p_heads = p_heads + p_cmp[:, h * tq:(h + 1) * tq]
        imp = _dot(ovt_ref[0:nb, 0:nc], p_heads.astype(BF16))

        blk_i = lax.broadcasted_iota(jnp.int32, (nb, tq), 0)
        blk = blk_i.astype(F32)
        cur = jnp.right_shift(t_row, 6)
        forced = (blk_i == 0) | (blk_i == cur) | (blk_i == cur - 1)
        valid = blk_i * SLC_BLOCK <= t_row
        imp = jnp.where(valid, jnp.where(forced, FORCE, imp), NEG)
        sel = jnp.zeros((nb, tq), F32)
        for _ in range(min(SLC_TOPN, nb)):
            top = jnp.max(imp, axis=0, keepdims=True)
            first = jnp.min(jnp.where(imp == top, blk, float(nb)), axis=0, keepdims=True)
            pick = blk == first
            sel = jnp.where(pick, 1.0, sel)
            imp = jnp.where(pick, -jnp.inf, imp)
        own = (blk_i >= 2 * qb) & (blk_i < 2 * qb + tq // SLC_BLOCK)
        sel_scr[0:nb, :] = jnp.where(valid & (sel > 0.0) & jnp.logical_not(own), 0.0, NEG)
        if nb < n_slc:
            sel_scr[nb:n_slc, :] = jnp.full((n_slc - nb, tq), NEG, F32)
        return o_cmp

    size_classes = 4
    class_rows = n_cmp_pad // size_classes
    visible = (s0 + tq) // CMP_STRIDE
    o_cmp = lax.switch((visible - 1) // class_rows,
                       [functools.partial(compress_and_select, (i + 1) * class_rows) for i in range(size_classes)])

    blocks_per_tile = KT // SLC_BLOCK
    sub = KT // LANES

    def online_update(state, s, v_rows):
        m_i, l_i, acc = state
        m_new = jnp.maximum(m_i, jnp.max(s, axis=0, keepdims=True))
        alpha = jnp.exp2(m_i - m_new)
        p = jnp.exp2(s - m_new)
        l_new = alpha * l_i + jnp.sum(p, axis=0, keepdims=True)
        pb = p.astype(BF16)
        pv = _dot(v_rows[0], pb[0:LANES, :])
        for i in range(1, len(v_rows)):
            pv = pv + _dot(v_rows[i], pb[i * LANES:(i + 1) * LANES, :])
        return m_new, l_new, alpha * acc + pv

    last_tile = kk_ref.shape[1] // KT - 1

    def score_chunks(kt):
        kd = jnp.minimum(kt, last_tile)
        bias8 = sel_scr[pl.ds(pl.multiple_of(kd * blocks_per_tile, blocks_per_tile), blocks_per_tile), :]
        bias8 = jnp.where(kt <= last_tile, bias8, NEG)
        bias_rows = jnp.concatenate([lanes4(bias8), jnp.zeros((16 - blocks_per_tile, wq), F32)], axis=0).astype(BF16)
        q_bias = jnp.concatenate([qr_top, bias_rows, jnp.zeros((HEAD_DIM - 16, wq), BF16)], axis=0)
        for c in range(sub):
            kc0 = pl.multiple_of(kd * KT + c * LANES, LANES)
            yield c, _dot(ksx_ref[0, pl.ds(kc0, LANES), :], q_bias)

    def fold8(x, op):
        out = x[0:8, :]
        for r in range(1, x.shape[0] // 8):
            out = op(out, x[8 * r:8 * r + 8, :])
        return out

    def pv_chunk(vt, c, p_ref):
        return _dot(vt[c, 0:HEAD_DIM, :], p_ref[c * LANES:(c + 1) * LANES, :])

    def v_tile(kt):
        return vvt_ref[0, pl.ds(pl.multiple_of(jnp.minimum(kt, last_tile) * sub, sub), sub)]

    def phase(kt, s_cur, s_next, p_cur, p_prev, state):
        m_i, l_i, acc, mt_cur, alpha_prev = state
        m_new = jnp.maximum(m_i, jnp.max(mt_cur, axis=0, keepdims=True))
        alpha = jnp.exp2(m_i - m_new)
        mt_next = jnp.full((8, wq), NEG, F32)
        l_new = alpha * l_i
        pv = jnp.zeros((HEAD_DIM, wq), F32)
        vt = v_tile(kt - 1) if p_prev is not None else None
        for c, s_n in score_chunks(kt + 1):
            rows = slice(c * LANES, (c + 1) * LANES)
            mt_next = jnp.maximum(mt_next, fold8(s_n, jnp.maximum))
            s_next[rows, :] = s_n
            p = jnp.exp2(s_cur[rows, :] - m_new)
            l_new = l_new + fold8(p, jnp.add)
            p_cur[rows, :] = p.astype(BF16)
            if p_prev is not None:
                pv = pv + pv_chunk(vt, c, p_prev)
        return m_new, l_new, alpha_prev * acc + pv, mt_next, alpha

    def slc_pair(j, state):
        state = phase(2 * j + 1, sb_scr, sa_scr, pb_scr, pa_scr, state)
        return phase(2 * j + 2, sa_scr, sb_scr, pa_scr, pb_scr, state)

    init = (jnp.full((1, wq), NEG, F32), jnp.zeros((1, wq), F32), jnp.zeros((HEAD_DIM, wq), F32))
    mt0 = jnp.full((8, wq), NEG, F32)
    for c, s_n in score_chunks(0):
        mt0 = jnp.maximum(mt0, fold8(s_n, jnp.maximum))
        sa_scr[c * LANES:(c + 1) * LANES, :] = s_n
    key_d = s0 + lax.broadcasted_iota(jnp.int32, (tq, tq), 0)
    s_d = _dot(kk_ref[0, pl.ds(pl.multiple_of(s0, tq), tq), :], qr_lo) + lanes4(jnp.where(key_d <= t_row, 0.0, NEG))
    m_d = jnp.max(s_d, axis=0, keepdims=True)
    p_d = jnp.exp2(s_d - m_d)
    acc_d = _dot(vvt_ref[0, qb][0:HEAD_DIM, :], p_d.astype(BF16))
    state = (m_d, fold8(p_d, jnp.add), acc_d, mt0, jnp.ones((1, wq), F32))
    state = phase(0, sa_scr, sb_scr, pa_scr, None, state)
    n_tiles = (s0 + tq + KT - 1) // KT
    pairs = n_tiles // 2
    _, l8_s, acc_s, _, alpha_last = lax.fori_loop(0, pairs, slc_pair, state)
    vt_last = v_tile(2 * pairs)
    pv = pv_chunk(vt_last, 0, pa_scr)
    for c in range(1, sub):
        pv = pv + pv_chunk(vt_last, c, pa_scr)
    acc_s = alpha_last * acc_s + pv
    l_s = jnp.sum(l8_s, axis=0, keepdims=True)

    wkeys = WIN + tq
    k0 = pl.multiple_of(jnp.maximum(s0 - WIN, 0), LANES)
    key = k0 + lax.broadcasted_iota(jnp.int32, (wkeys, tq), 0)
    wbias = jnp.where((key <= t_row) & (key > t_row - WIN), 0.0, NEG)
    s_w = _dot(kk_ref[0, pl.ds(k0, wkeys), :], qr_hi) + lanes4(wbias)
    vt_w = vvt_ref[0, pl.ds(jnp.maximum(qb - WIN // tq, 0), wkeys // LANES)]
    _, l_w, acc_w = online_update(init, s_w, [vt_w[i, HEAD_DIM:2 * HEAD_DIM, :] for i in range(wkeys // LANES)])

    g = gt_ref[0]
    gate = lambda br: jnp.concatenate([g[h * 3 + br:h * 3 + br + 1, :] for h in range(nh)], axis=1)
    o = gate(0) * o_cmp + gate(1) * (acc_s * (1.0 / l_s)) + gate(2) * (acc_w * (1.0 / l_w))
    o_rows = jnp.concatenate([o[:, h * tq:(h + 1) * tq] for h in range(nh)], axis=0)
    yd_ref[0] = o_rows.T.astype(BF16)


def _nsa(qt, qrt, gt, kk, ksx, vvt, kcmp, vcmpt, ovt):
    b, _, s = qt.shape
    n_slc = s // SLC_BLOCK
    m = kcmp.shape[1]
    return pl.pallas_call(
        _nsa_kernel,
        grid=(b, s // TQ),
        in_specs=[
            pl.BlockSpec((1, 256, TQ), lambda bi, qi: (bi, 0, qi)),
            pl.BlockSpec((1, 256, TQ), lambda bi, qi: (bi, 0, qi)),
            pl.BlockSpec((1, 16, TQ), lambda bi, qi: (bi, 0, qi)),
            pl.BlockSpec((1, s, LANES), lambda bi, qi: (bi, 0, 0)),
            pl.BlockSpec((1, s, LANES), lambda bi, qi: (bi, 0, 0)),
            pl.BlockSpec((1, s // LANES, LANES, LANES), lambda bi, qi: (bi, 0, 0, 0)),
            pl.BlockSpec((1, m, LANES), lambda bi, qi: (bi, 0, 0)),
            pl.BlockSpec((1, HEAD_DIM, m), lambda bi, qi: (bi, 0, 0)),
            pl.BlockSpec((n_slc, m), lambda bi, qi: (0, 0)),
        ],
        out_specs=pl.BlockSpec((1, TQ, 256), lambda bi, qi: (bi, qi, 0)),
        out_shape=jax.ShapeDtypeStruct((b, s, 256), BF16),
        scratch_shapes=[pltpu.VMEM((n_slc, TQ), F32),
                        pltpu.VMEM((KT, NSA_HEADS * TQ), F32), pltpu.VMEM((KT, NSA_HEADS * TQ), F32),
                        pltpu.VMEM((KT, NSA_HEADS * TQ), BF16), pltpu.VMEM((KT, NSA_HEADS * TQ), BF16)],
        compiler_params=pltpu.CompilerParams(dimension_semantics=("arbitrary", "arbitrary"),
                                             vmem_limit_bytes=VMEM_LIMIT),
        name="sparse_attention",
    )(qt, qrt, gt, kk, ksx, vvt, kcmp, vcmpt, ovt)


def _outproj_kernel(x_ref, yabc_ref, yd_ref, wo_ref, g_ref, b_ref, rw_ref, rb_ref, x1_ref, gates_ref):
    mix = _dot(yabc_ref[...], wo_ref[0:768, :]) + _dot(yd_ref[...], wo_ref[768:1024, :])
    x1 = _layer_norm(DN_ALPHA * x_ref[...] + mix, g_ref[...], b_ref[...])
    x1_ref[...] = x1

    ts = x1.shape[0]
    aff = _sigmoid(_dot(x1.astype(BF16), rw_ref[...])).T[0:N_EXPERTS, :]
    biased = aff + rb_ref[...]
    row = lambda a, i: a[i:i + 1, :]

    gscores = []
    for gi in range(N_GROUPS):
        a, b, c, d = (row(biased, gi * EXPERTS_PER_GROUP + i) for i in range(EXPERTS_PER_GROUP))
        hi1, lo1, hi2, lo2 = jnp.maximum(a, b), jnp.minimum(a, b), jnp.maximum(c, d), jnp.minimum(c, d)
        gscores.append(jnp.maximum(hi1, hi2) + jnp.maximum(jnp.minimum(hi1, hi2), jnp.maximum(lo1, lo2)))
    best, gsel = gscores[0], jnp.zeros((1, ts), jnp.int32)
    for gi in range(1, N_GROUPS):
        better = gscores[gi] > best
        gsel = jnp.where(better, gi, gsel)
        best = jnp.where(better, gscores[gi], best)

    eid_i = lax.broadcasted_iota(jnp.int32, (N_EXPERTS, ts), 0)
    eid = eid_i.astype(F32)
    masked = jnp.where(jnp.right_shift(eid_i, 2) == gsel, biased, NEG)
    picks = []
    for _ in range(2):
        top = jnp.max(masked, axis=0, keepdims=True)
        first = jnp.min(jnp.where(masked == top, eid, float(N_EXPERTS)), axis=0, keepdims=True)
        pick = eid == first
        picks.append(pick)
        masked = jnp.where(pick, -jnp.inf, masked)
    chosen = picks[0] | picks[1]
    w_sel = jnp.where(chosen, aff, 0.0)
    gates_t = w_sel / jnp.sum(w_sel, axis=0, keepdims=True)
    grp8 = jnp.where(lax.broadcasted_iota(jnp.int32, (8, ts), 0) == 0, gsel.astype(F32), 0.0)
    gates_ref[...] = jnp.concatenate([gates_t, grp8, jnp.zeros((LANES - N_EXPERTS - 8, ts), F32)], axis=0).T


def _outproj(x2d, yabc, yd, wo, g, b, rw, rb):
    n, d = x2d.shape
    ts = TS_OUT
    const = lambda shape: pl.BlockSpec(shape, lambda i: (0,) * len(shape))
    return pl.pallas_call(
        _outproj_kernel,
        grid=(n // ts,),
        in_specs=[
            pl.BlockSpec((ts, d), lambda i: (i, 0)),
            pl.BlockSpec((ts, 768), lambda i: (i, 0)),
            pl.BlockSpec((ts, 256), lambda i: (i, 0)),
            const((d, d)), const((1, d)), const((1, d)), const((d, LANES)), const((N_EXPERTS, 1)),
        ],
        out_specs=(pl.BlockSpec((ts, d), lambda i: (i, 0)), pl.BlockSpec((ts, LANES), lambda i: (i, 0))),
        out_shape=(jax.ShapeDtypeStruct((n, d), F32), jax.ShapeDtypeStruct((n, LANES), F32)),
        compiler_params=pltpu.CompilerParams(dimension_semantics=("arbitrary",), vmem_limit_bytes=VMEM_LIMIT),
        name="outproj_ln_route",
    )(x2d, yabc, yd, wo, g, b, rw, rb)


def _split3(x):
    hi = x.astype(BF16)
    r1 = x - hi.astype(F32)
    mid = r1.astype(BF16)
    lo = (r1 - mid.astype(F32)).astype(BF16)
    return hi, mid, lo


def _dispatch_kernel(x1_ref, gates_ref, tri_ref, xs_ref, gs_ref, post_ref, cnt_ref):
    tm = x1_ref.shape[0]
    gates = gates_ref[...]
    gsel = gates.T[GROUP_LANE:GROUP_LANE + 1, :]
    onehot = jnp.where(lax.broadcasted_iota(jnp.int32, (8, tm), 0).astype(F32) == gsel, 1.0, 0.0)
    n = jnp.sum(onehot, axis=1, keepdims=True)
    n_al = jnp.floor((n + (ROW_ALIGN - 1)) * (1.0 / ROW_ALIGN)) * ROW_ALIGN
    starts = [jnp.zeros((1, 1), F32)]
    for gi in range(1, N_GROUPS):
        starts.append(starts[-1] + n_al[gi - 1:gi, :])
    start = jnp.concatenate(starts + [jnp.zeros((8 - N_GROUPS, 1), F32)], axis=0)
    before = _dot(onehot.astype(BF16), tri_ref[...])
    pos = jnp.sum(onehot * (start + before), axis=0, keepdims=True)
    perm = jnp.where(lax.broadcasted_iota(jnp.int32, (SORT_ROWS, tm), 0).astype(F32) == pos, 1.0, 0.0).astype(BF16)

    xs_ref[0, 0:SORT_ROWS, :] = _dot(perm, x1_ref[...].astype(BF16)).astype(BF16)
    xs_ref[0, SORT_ROWS:SORT_BUF, :] = jnp.zeros((SORT_BUF - SORT_ROWS, xs_ref.shape[2]), BF16)
    g_hi, g_mid, g_lo = _split3(gates)
    gs_ref[0, 0:SORT_ROWS, :] = _dot(perm, g_hi) + _dot(perm, g_mid) + _dot(perm, g_lo)
    gs_ref[0, SORT_ROWS:SORT_BUF, :] = jnp.zeros((SORT_BUF - SORT_ROWS, LANES), F32)
    pos8 = jnp.where(lax.broadcasted_iota(jnp.int32, (8, tm), 0) == 0, pos, 0.0)
    post_ref[...] = jnp.concatenate([pos8, jnp.zeros((LANES - 8, tm), F32)], axis=0).T
    cnt_ref[0] = jnp.broadcast_to(n, (8, LANES))


def _dispatch(x1, gates, tri):
    n, d = x1.shape
    tm = TM_MOE
    nt = n // tm
    return pl.pallas_call(
        _dispatch_kernel,
        grid=(nt,),
        in_specs=[
            pl.BlockSpec((tm, d), lambda i: (i, 0)),
            pl.BlockSpec((tm, LANES), lambda i: (i, 0)),
            pl.BlockSpec((tm, tm), lambda i: (0, 0)),
        ],
        out_specs=(
            pl.BlockSpec((1, SORT_BUF, d), lambda i: (i, 0, 0)),
            pl.BlockSpec((1, SORT_BUF, LANES), lambda i: (i, 0, 0)),
            pl.BlockSpec((tm, LANES), lambda i: (i, 0)),
            pl.BlockSpec((1, 8, LANES), lambda i: (i, 0, 0)),
        ),
        out_shape=(
            jax.ShapeDtypeStruct((nt, SORT_BUF, d), BF16),
            jax.ShapeDtypeStruct((nt, SORT_BUF, LANES), F32),
            jax.ShapeDtypeStruct((n, LANES), F32),
            jax.ShapeDtypeStruct((nt, 8, LANES), F32),
        ),
        compiler_params=pltpu.CompilerParams(dimension_semantics=("arbitrary",), vmem_limit_bytes=VMEM_LIMIT),
        name="moe_dispatch",
    )(x1, gates, tri)


def _moe_kernel(start_ref, trips_ref, x1_ref, xs_ref, gs_ref, post_ref, wg_ref, wu_ref, wd_ref, g_ref, b_ref,
                out_ref, acc_scr):
    i, e = pl.program_id(0), pl.program_id(1)

    @pl.when(e == 0)
    def _():
        acc_scr[...] = jnp.zeros(acc_scr.shape, F32)

    grp = lax.shift_right_logical(e, 2)
    row0 = start_ref[i * N_GROUPS + grp]

    def window(w, carry):
        rows = pl.ds(pl.multiple_of(row0 + w * WIN_ROWS, ROW_ALIGN), WIN_ROWS)
        xw = xs_ref[0, rows, :]
        hg = _dot(xw, wg_ref[0])
        h = hg * _sigmoid(hg) * _dot(xw, wu_ref[0])
        y = _dot(h.astype(BF16), wd_ref[0])
        gw = gs_ref[0, rows, :]
        lane = lax.broadcasted_iota(jnp.int32, gw.shape, 1)
        gcol = jnp.sum(jnp.where(lane == e, gw, 0.0), axis=1, keepdims=True)
        acc_scr[rows, :] += y * gcol
        return carry

    lax.fori_loop(0, trips_ref[i * N_GROUPS + grp], window, 0)

    @pl.when(e == pl.num_programs(1) - 1)
    def _():
        tm = x1_ref.shape[0]
        acc = acc_scr[0:SORT_ROWS, :]
        hi = acc.astype(BF16)
        lo = (acc - hi.astype(F32)).astype(BF16)
        pos = post_ref[...][:, 0:1]
        unperm = jnp.where(lax.broadcasted_iota(jnp.int32, (tm, SORT_ROWS), 1).astype(F32) == pos, 1.0, 0.0).astype(BF16)
        moe = _dot(unperm, hi) + _dot(unperm, lo)
        out_ref[...] = _layer_norm(DN_ALPHA * x1_ref[...] + moe, g_ref[...], b_ref[...])


def _moe(start, trips, x1, xs, gs, post, wg, wu, wd, g, b):
    n, d = x1.shape
    tm = TM_MOE
    ne, _, de = wg.shape
    const = lambda shape: pl.BlockSpec(shape, lambda i, e, *_: (0,) * len(shape))
    grid_spec = pltpu.PrefetchScalarGridSpec(
        num_scalar_prefetch=2,
        grid=(n // tm, ne),
        in_specs=[
            pl.BlockSpec((tm, d), lambda i, e, *_: (i, 0)),
            pl.BlockSpec((1, SORT_BUF, d), lambda i, e, *_: (i, 0, 0)),
            pl.BlockSpec((1, SORT_BUF, LANES), lambda i, e, *_: (i, 0, 0)),
            pl.BlockSpec((tm, LANES), lambda i, e, *_: (i, 0)),
            pl.BlockSpec((1, d, de), lambda i, e, *_: (e, 0, 0)),
            pl.BlockSpec((1, d, de), lambda i, e, *_: (e, 0, 0)),
            pl.BlockSpec((1, de, d), lambda i, e, *_: (e, 0, 0)),
            const((1, d)), const((1, d)),
        ],
        out_specs=pl.BlockSpec((tm, d), lambda i, e, *_: (i, 0)),
        scratch_shapes=[pltpu.VMEM((SORT_BUF, d), F32)],
    )
    return pl.pallas_call(
        _moe_kernel,
        grid_spec=grid_spec,
        out_shape=jax.ShapeDtypeStruct((n, d), F32),
        compiler_params=pltpu.CompilerParams(dimension_semantics=("arbitrary", "arbitrary"),
                                             vmem_limit_bytes=VMEM_LIMIT),
        name="moe_ln",
    )(start, trips, x1, xs, gs, post, wg, wu, wd, g, b)


def _group_windows(cnt):
    n = cnt[:, 0:N_GROUPS, 0].astype(jnp.int32)
    n_al = (n + (ROW_ALIGN - 1)) // ROW_ALIGN * ROW_ALIGN
    start = jnp.cumsum(n_al, axis=1) - n_al
    trips = (n + (WIN_ROWS - 1)) // WIN_ROWS
    return start.reshape(-1), trips.reshape(-1)


def _permute_w_in(w):
    sizes = (256, 256, 256, 256, 256, 256, 256, 64, 64, 64, 64, 64, 64, 12)
    offs = np.concatenate([[0], np.cumsum(sizes)])
    part = lambda i: w[:, int(offs[i]):int(offs[i + 1])]
    a_b, a_c, a_h, b_val, b_gate, c_p, q, kc, vc, ks, vs, kw, vw, g = (part(i) for i in range(14))
    pad = jnp.zeros((w.shape[0], LANES - g.shape[1]), w.dtype)
    return jnp.concatenate([a_b, a_c, a_h, b_val, b_gate, c_p, q, kc, vc, ks, kw, vs, vw, g, pad], axis=1).astype(BF16)


def _rope_tables(s):
    inv = 1.0 / (ROPE_THETA ** (jnp.arange(0, HEAD_DIM, 2, dtype=F32) / HEAD_DIM))
    ang = jnp.arange(s, dtype=F32)[:, None] * inv[None, :]
    ang = jnp.concatenate([ang, ang], -1)
    sign = jnp.concatenate([-jnp.ones((HEAD_DIM // 2,), F32), jnp.ones((HEAD_DIM // 2,), F32)])
    return jnp.tile(jnp.cos(ang), (1, NSA_HEADS)), jnp.tile(jnp.sin(ang) * sign[None, :], (1, NSA_HEADS))


def _overlap_t(s, m):
    n_cmp = (s - CMP_BLOCK) // CMP_STRIDE + 1
    cmp_start = np.arange(m) * CMP_STRIDE
    slc_start = np.arange(s // SLC_BLOCK) * SLC_BLOCK
    ov = (cmp_start[None, :] < slc_start[:, None] + SLC_BLOCK) & (cmp_start[None, :] + CMP_BLOCK > slc_start[:, None])
    ov = ov & (np.arange(m)[None, :] < n_cmp)
    return jnp.asarray(ov, BF16)


def _block_diag(w):
    g, c, _ = w.shape
    out = jnp.zeros((g * c, g * c), w.dtype)
    for i in range(g):
        out = out.at[i * c:(i + 1) * c, i * c:(i + 1) * c].set(w[i])
    return out


def kernel(x, w_in, conv_a_w, conv_b_w, conv_b_b, cf_ln_g, cf_ln_b, pool_w, pool_scale, cmp_pe_k, cmp_pe_v, cmp_k_w1, cmp_k_w2, cmp_v_w1, cmp_v_w2, w_out, ln1_g, ln1_b, ln2_g, ln2_b, router_w, router_bias, moe_w_gate, moe_w_up, moe_w_down):
    b, s, d = x.shape
    n = b * s
    m = s // CMP_STRIDE
    tok_per_row = CMP_STRIDE
    cos4, sin4 = _rope_tables(s)
    ovt = _overlap_t(s, m)
    rw = jnp.concatenate([router_w, jnp.zeros((d, LANES - N_EXPERTS), router_w.dtype)], axis=1).astype(BF16)
    rb = router_bias.reshape(N_EXPERTS, 1).astype(F32)
    pad_w2 = lambda w: jnp.concatenate([w, jnp.zeros((CMP_HIDDEN, LANES - HEAD_DIM), w.dtype)], axis=1).astype(BF16)
    row = lambda v: v.reshape(1, -1)
    tri = jnp.asarray(np.triu(np.ones((TM_MOE, TM_MOE), np.float32), k=1), BF16)

    for l in range(DEPTH):
        yabc, qt, qrt, kcvc, kk, ksx, vvt, gt = _inproj(
            x, _permute_w_in(w_in[l]), cos4, sin4, conv_a_w[l], conv_b_w[l], row(conv_b_b[l]),
            row(cf_ln_g[l]), row(cf_ln_b[l]), _block_diag(pool_w[l]).astype(BF16), row(pool_scale[l]))
        kc16 = kcvc[:, :, 0:HEAD_DIM].reshape(b, m, tok_per_row * HEAD_DIM)
        vc16 = kcvc[:, :, HEAD_DIM:2 * HEAD_DIM].reshape(b, m, tok_per_row * HEAD_DIM)
        kcmp, vcmpt = _compress(
            kc16, vc16, cmp_pe_k[l].reshape(2, tok_per_row * HEAD_DIM), cmp_pe_v[l].reshape(2, tok_per_row * HEAD_DIM),
            cmp_k_w1[l].astype(BF16), pad_w2(cmp_k_w2[l]), cmp_v_w1[l].astype(BF16), pad_w2(cmp_v_w2[l]))
        yd = _nsa(qt, qrt, gt, kk, ksx, vvt, kcmp, vcmpt, ovt)
        x1, gates = _outproj(x.reshape(n, d), yabc.reshape(n, 768), yd.reshape(n, 256), w_out[l].astype(BF16),
                             row(ln1_g[l]), row(ln1_b[l]), rw, rb)
        xs, gs, post, cnt = _dispatch(x1, gates, tri)
        start, trips = _group_windows(cnt)
        x = _moe(start, trips, x1, xs, gs, post, moe_w_gate[l].astype(BF16), moe_w_up[l].astype(BF16),
                 moe_w_down[l].astype(BF16), row(ln2_g[l]), row(ln2_b[l])).reshape(b, s, d)
    return x
```

```python
import functools

import jax
import jax.numpy as jnp
import numpy as np
from jax import lax
from jax.experimental import pallas as pl
from jax.experimental.pallas import tpu as pltpu

F32 = jnp.float32
BF16 = jnp.bfloat16

D_MODEL = 1024
DEPTH = 2
GROUP_W = 256
HEAD_DIM = 64
SC_KERNEL = 3
CF_KERNEL = 31
POOL_WINDOWS = (2, 4, 8, 16)
POOL_GW = 64
NSA_HEADS = 4
CMP_BLOCK = 32
CMP_STRIDE = 16
CMP_HIDDEN = 256
SLC_BLOCK = 64
SLC_TOPN = 16
WIN = 512
N_EXPERTS = 16
N_GROUPS = 4
EXPERTS_PER_GROUP = 4
D_EXPERT = 512
DN_ALPHA = (2 * DEPTH) ** 0.25
LN_EPS = 1e-5
NEG = -1e30
FORCE = 1e4
ROPE_THETA = 10000.0
QK_SCALE = HEAD_DIM ** -0.5
LOG2E = 1.4426950408889634

LANES = 128
OFF_A, OFF_B, OFF_C, OFF_Q, OFF_KCVC, OFF_KK, OFF_VV, OFF_G, IN_PAD = 0, 768, 1280, 1536, 1792, 1920, 2048, 2176, 2304

HALO_A = 8
HALO_B = 32
HALO_C = 16

TS_IN = 512
TQ = 128
KT = 512
TS_OUT = 512
TM_MOE = 1024
ROW_ALIGN = 16
WIN_ROWS = 288
SORT_ROWS = TM_MOE + LANES
SORT_BUF = SORT_ROWS + 2 * LANES
GROUP_LANE = 16
TILES_PER_STEP = 2
assert SORT_ROWS >= TM_MOE + N_GROUPS * (ROW_ALIGN - 1)
assert SORT_BUF >= TM_MOE + N_GROUPS * (ROW_ALIGN - 1) + WIN_ROWS
VMEM_LIMIT = 56 * 1024 * 1024


def _sigmoid(x):
    return jax.nn.sigmoid(x)


def _layer_norm(h, g, b):
    mu = jnp.mean(h, axis=-1, keepdims=True)
    d = h - mu
    var = jnp.mean(d * d, axis=-1, keepdims=True)
    return d * lax.rsqrt(var + LN_EPS) * g + b


def _dot(a, b):
    return jnp.dot(a, b, preferred_element_type=F32)


def _inproj_kernel(x_ref, w_ref, cos_ref, sin_ref, cva_ref, cvb_ref, cvbb_ref, lng_ref, lnb_ref,
                   poolw_ref, pools_ref,
                   yabc_ref, qt_ref, qrt_ref, kcvc_ref, kk_ref, ksx_ref, vvt_ref, gt_ref,
                   ch_ext, u_ext, p_ext, u_sh):
    j = pl.program_id(1)
    ts = x_ref.shape[1]
    xb = x_ref[0].astype(BF16)

    def proj(lo, hi):
        return _dot(xb, w_ref[:, lo:hi])

    def carry_halo(ext, halo):
        @pl.when(j == 0)
        def _():
            ext[0:halo, :] = jnp.zeros((halo, ext.shape[1]), F32)

        @pl.when(j > 0)
        def _():
            ext[0:halo, :] = ext[ts:ts + halo, :]

    pa = proj(OFF_A, OFF_B)
    a_b, ch = pa[:, 0:256], pa[:, 256:512] * pa[:, 512:768]
    carry_halo(ch_ext, HALO_A)
    ch_ext[HALO_A:HALO_A + ts, :] = ch
    conv = cva_ref[2:3, :] * ch
    for k in range(SC_KERNEL - 1):
        off = HALO_A - (SC_KERNEL - 1) + k
        conv = conv + cva_ref[k:k + 1, :] * ch_ext[off:off + ts, :]
    yabc_ref[0, :, 0:256] = (a_b * conv).astype(BF16)

    pb = proj(OFF_B, OFF_C)
    u = pb[:, 0:256] * _sigmoid(pb[:, 256:512])
    carry_halo(u_ext, HALO_B)
    u_ext[HALO_B:HALO_B + ts, :] = u
    span = u_sh.shape[1]
    for r in range(1, 8):
        u_sh[r - 1, :, :] = u_ext[r:r + span, :]
    rows = 64
    for c in range(ts // rows):
        acc = jnp.zeros((rows, 256), F32) + cvbb_ref[...]
        for k in range(CF_KERNEL):
            off = HALO_B - (CF_KERNEL - 1) + k
            base = off // 8 * 8 + c * rows
            src = u_ext if off % 8 == 0 else u_sh.at[off % 8 - 1]
            acc = acc + cvb_ref[k:k + 1, :] * src[base:base + rows, :]
        v = _layer_norm(acc, lng_ref[...], lnb_ref[...])
        yabc_ref[0, c * rows:(c + 1) * rows, 256:512] = (v * _sigmoid(v)).astype(BF16)

    pc = proj(OFF_C, OFF_Q)
    carry_halo(p_ext, HALO_C)
    p_ext[HALO_C:HALO_C + ts, :] = pc
    pe = p_ext[...]
    s2 = pe + pltpu.roll(pe, 1, 0)
    s4 = s2 + pltpu.roll(s2, 2, 0)
    s8 = s4 + pltpu.roll(s4, 4, 0)
    s16 = s8 + pltpu.roll(s8, 8, 0)
    lane = lax.broadcasted_iota(jnp.int32, (ts, 256), 1)
    grp = jnp.right_shift(lane, 6)
    wsum = jnp.where(grp == 0, s2[HALO_C:], jnp.where(grp == 1, s4[HALO_C:], jnp.where(grp == 2, s8[HALO_C:], s16[HALO_C:])))
    win = jnp.where(grp == 0, 2, jnp.where(grp == 1, 4, jnp.where(grp == 2, 8, 16)))
    t1 = j * ts + lax.broadcasted_iota(jnp.int32, (ts, 256), 0) + 1
    cnt = jnp.minimum(win, t1).astype(F32)
    dlt = wsum / cnt - pc
    yabc_ref[0, :, 512:768] = (_dot(dlt.astype(BF16), poolw_ref[...]) * pools_ref[...]).astype(BF16)

    cos4, sin4 = cos_ref[...], sin_ref[...]
    q = proj(OFF_Q, OFF_KCVC)
    lane_q = jnp.bitwise_and(lax.broadcasted_iota(jnp.int32, (ts, 256), 1), HEAD_DIM - 1)
    q_sw = jnp.where(lane_q < HEAD_DIM // 2, pltpu.roll(q, 256 - HEAD_DIM // 2, 1), pltpu.roll(q, HEAD_DIM // 2, 1))
    qr = q * cos4 + q_sw * sin4
    qt_ref[0] = (q * QK_SCALE).T.astype(BF16)
    qrt_ref[0] = (qr * (QK_SCALE * LOG2E)).T.astype(BF16)

    kcvc_ref[0] = proj(OFF_KCVC, OFF_KK)

    k2 = proj(OFF_KK, OFF_VV)
    lane_k = jnp.bitwise_and(lax.broadcasted_iota(jnp.int32, (ts, LANES), 1), HEAD_DIM - 1)
    k_sw = jnp.where(lane_k < HEAD_DIM // 2, pltpu.roll(k2, LANES - HEAD_DIM // 2, 1), pltpu.roll(k2, HEAD_DIM // 2, 1))
    k_rot = k2 * cos4[:, 0:LANES] + k_sw * sin4[:, 0:LANES]
    kk_ref[0] = k_rot.astype(BF16)
    lane_i = lax.broadcasted_iota(jnp.int32, (ts, LANES), 1)
    blk_in_tile = jnp.bitwise_and(jnp.right_shift(j * ts + lax.broadcasted_iota(jnp.int32, (ts, LANES), 0), 6),
                                  KT // SLC_BLOCK - 1)
    ksx_ref[0] = jnp.where(lane_i < HEAD_DIM, k_rot, jnp.where(lane_i - HEAD_DIM == blk_in_tile, 1.0, 0.0)).astype(BF16)

    v2t = proj(OFF_VV, OFF_G).T
    for c in range(ts // LANES):
        vvt_ref[0, c] = v2t[:, c * LANES:(c + 1) * LANES].astype(BF16)

    gt_ref[0] = _sigmoid(proj(OFF_G, IN_PAD)).T[0:16, :]


def _inproj(x, w_in_p, cos4, sin4, cva, cvb, cvbb, lng, lnb, poolw, pools):
    b, s, d = x.shape
    ts = TS_IN
    grid = (b, s // ts)
    const = lambda shape: pl.BlockSpec(shape, lambda bi, ji: (0,) * len(shape))
    out_shape = (
        jax.ShapeDtypeStruct((b, s, 768), BF16),
        jax.ShapeDtypeStruct((b, 256, s), BF16),
        jax.ShapeDtypeStruct((b, 256, s), BF16),
        jax.ShapeDtypeStruct((b, s, LANES), F32),
        jax.ShapeDtypeStruct((b, s, LANES), BF16),
        jax.ShapeDtypeStruct((b, s, LANES), BF16),
        jax.ShapeDtypeStruct((b, s // LANES, LANES, LANES), BF16),
        jax.ShapeDtypeStruct((b, 16, s), F32),
    )
    return pl.pallas_call(
        _inproj_kernel,
        grid=grid,
        in_specs=[
            pl.BlockSpec((1, ts, d), lambda bi, ji: (bi, ji, 0)),
            const((d, IN_PAD)),
            pl.BlockSpec((ts, 256), lambda bi, ji: (ji, 0)),
            pl.BlockSpec((ts, 256), lambda bi, ji: (ji, 0)),
            const((SC_KERNEL, 256)), const((CF_KERNEL, 256)), const((1, 256)), const((1, 256)), const((1, 256)),
            const((256, 256)), const((1, 256)),
        ],
        out_specs=(
            pl.BlockSpec((1, ts, 768), lambda bi, ji: (bi, ji, 0)),
            pl.BlockSpec((1, 256, ts), lambda bi, ji: (bi, 0, ji)),
            pl.BlockSpec((1, 256, ts), lambda bi, ji: (bi, 0, ji)),
            pl.BlockSpec((1, ts, LANES), lambda bi, ji: (bi, ji, 0)),
            pl.BlockSpec((1, ts, LANES), lambda bi, ji: (bi, ji, 0)),
            pl.BlockSpec((1, ts, LANES), lambda bi, ji: (bi, ji, 0)),
            pl.BlockSpec((1, ts // LANES, LANES, LANES), lambda bi, ji: (bi, ji, 0, 0)),
            pl.BlockSpec((1, 16, ts), lambda bi, ji: (bi, 0, ji)),
        ),
        out_shape=out_shape,
        scratch_shapes=[
            pltpu.VMEM((ts + HALO_A, 256), F32),
            pltpu.VMEM((ts + HALO_B, 256), F32),
            pltpu.VMEM((ts + HALO_C, 256), F32),
            pltpu.VMEM((7, ts + HALO_B - 8, 256), F32),
        ],
        compiler_params=pltpu.CompilerParams(dimension_semantics=("arbitrary", "arbitrary"),
                                             vmem_limit_bytes=VMEM_LIMIT),
        name="inproj_mixers",
    )(x, w_in_p, cos4, sin4, cva, cvb, cvbb, lng, lnb, poolw, pools)


def _compress_kernel(kc_ref, vc_ref, pek_ref, pev_ref, wk1_ref, wk2_ref, wv1_ref, wv2_ref, kcmp_ref, vcmpt_ref):
    def comp(x_ref, pe_ref, w1_ref, w2_ref):
        x = x_ref[0]
        m = x.shape[0]
        half = x.shape[1]
        first = _dot((x + pe_ref[0:1, :]).astype(BF16), w1_ref[0:half, :])
        second = _dot((x + pe_ref[1:2, :]).astype(BF16), w1_ref[half:2 * half, :])
        hid = first + pltpu.roll(second, m - 1, 0)
        return _dot((hid * _sigmoid(hid)).astype(BF16), w2_ref[...])

    kcmp_ref[0] = comp(kc_ref, pek_ref, wk1_ref, wk2_ref).astype(BF16)
    vcmpt_ref[0] = comp(vc_ref, pev_ref, wv1_ref, wv2_ref).T[0:HEAD_DIM, :].astype(BF16)


def _compress(kc16, vc16, pek, pev, wk1, wk2, wv1, wv2):
    b, m, w = kc16.shape
    const = lambda shape: pl.BlockSpec(shape, lambda bi: (0,) * len(shape))
    return pl.pallas_call(
        _compress_kernel,
        grid=(b,),
        in_specs=[
            pl.BlockSpec((1, m, w), lambda bi: (bi, 0, 0)),
            pl.BlockSpec((1, m, w), lambda bi: (bi, 0, 0)),
            const((2, w)), const((2, w)),
            const((2 * w, CMP_HIDDEN)), const((CMP_HIDDEN, LANES)),
            const((2 * w, CMP_HIDDEN)), const((CMP_HIDDEN, LANES)),
        ],
        out_specs=(
            pl.BlockSpec((1, m, LANES), lambda bi: (bi, 0, 0)),
            pl.BlockSpec((1, HEAD_DIM, m), lambda bi: (bi, 0, 0)),
        ),
        out_shape=(jax.ShapeDtypeStruct((b, m, LANES), BF16), jax.ShapeDtypeStruct((b, HEAD_DIM, m), BF16)),
        compiler_params=pltpu.CompilerParams(dimension_semantics=("arbitrary",), vmem_limit_bytes=VMEM_LIMIT),
        name="compress_kv",
    )(kc16, vc16, pek, pev, wk1, wk2, wv1, wv2)


def _nsa_kernel(qt_ref, qrt_ref, gt_ref, kk_ref, ksx_ref, vvt_ref, kcmp_ref, vcmpt_ref, ovt_ref, yd_ref, sel_scr, sa_scr, sb_scr, pa_scr, pb_scr):
    qb = pl.program_id(1)
    tq = qt_ref.shape[2]
    nh = NSA_HEADS
    wq = nh * tq
    s0 = qb * tq
    n_slc = sel_scr.shape[0]
    n_cmp_pad = kcmp_ref.shape[1]

    def stack_heads(ref):
        x = ref[0]
        return jnp.concatenate([x[h * HEAD_DIM:(h + 1) * HEAD_DIM, :] for h in range(nh)], axis=1)

    zeros = jnp.zeros((HEAD_DIM, wq), BF16)
    q_top, qr_top = stack_heads(qt_ref), stack_heads(qrt_ref)
    q_lo = jnp.concatenate([q_top, zeros], axis=0)
    qr_lo = jnp.concatenate([qr_top, zeros], axis=0)
    qr_hi = jnp.concatenate([zeros, qr_top], axis=0)
    t_row = s0 + lax.broadcasted_iota(jnp.int32, (1, tq), 1)
    lanes4 = lambda x: jnp.concatenate([x] * nh, axis=1)
    t_all = lanes4(t_row)

    def compress_and_select(nc):
        nb = nc // (SLC_BLOCK // CMP_STRIDE)
        sc = _dot(kcmp_ref[0, 0:nc, :], q_lo)
        cmp_end = lax.broadcasted_iota(jnp.int32, (nc, wq), 0) * CMP_STRIDE + (CMP_BLOCK - 1)
        sc = jnp.where(cmp_end <= t_all, sc, NEG)
        mx = jnp.max(sc, axis=0, keepdims=True)
        ex = jnp.exp(sc - mx)
        den = jnp.sum(ex, axis=0, keepdims=True)
        p_cmp = jnp.where(t_all >= CMP_BLOCK - 1, ex * (1.0 / den), 0.0)
        o_cmp = _dot(vcmpt_ref[0, :, 0:nc], p_cmp.astype(BF16))
        p_heads = p_cmp[:, 0:tq]
        for h in range(1, nh):
            p_heads = p_heads + p_cmp[:, h * tq:(h + 1) * tq]
        imp = _dot(ovt_ref[0:nb, 0:nc], p_heads.astype(BF16))

        blk_i = lax.broadcasted_iota(jnp.int32, (nb, tq), 0)
        blk = blk_i.astype(F32)
        cur = jnp.right_shift(t_row, 6)
        forced = (blk_i == 0) | (blk_i == cur) | (blk_i == cur - 1)
        valid = blk_i * SLC_BLOCK <= t_row
        imp = jnp.where(valid, jnp.where(forced, FORCE, imp), NEG)
        sel = jnp.zeros((nb, tq), F32)
        for _ in range(min(SLC_TOPN, nb)):
            top = jnp.max(imp, axis=0, keepdims=True)
            first = jnp.min(jnp.where(imp == top, blk, float(nb)), axis=0, keepdims=True)
            pick = blk == first
            sel = jnp.where(pick, 1.0, sel)
            imp = jnp.where(pick, -jnp.inf, imp)
        own = (blk_i >= 2 * qb) & (blk_i < 2 * qb + tq // SLC_BLOCK)
        sel_scr[0:nb, :] = jnp.where(valid & (sel > 0.0) & jnp.logical_not(own), 0.0, NEG)
        if nb < n_slc:
            sel_scr[nb:n_slc, :] = jnp.full((n_slc - nb, tq), NEG, F32)
        return o_cmp

    size_classes = 4
    class_rows = n_cmp_pad // size_classes
    visible = (s0 + tq) // CMP_STRIDE
    o_cmp = lax.switch((visible - 1) // class_rows,
                       [functools.partial(compress_and_select, (i + 1) * class_rows) for i in range(size_classes)])

    blocks_per_tile = KT // SLC_BLOCK
    sub = KT // LANES

    def online_update(state, s, v_rows):
        m_i, l_i, acc = state
        m_new = jnp.maximum(m_i, jnp.max(s, axis=0, keepdims=True))
        alpha = jnp.exp2(m_i - m_new)
        p = jnp.exp2(s - m_new)
        l_new = alpha * l_i + jnp.sum(p, axis=0, keepdims=True)
        pb = p.astype(BF16)
        pv = _dot(v_rows[0], pb[0:LANES, :])
        for i in range(1, len(v_rows)):
            pv = pv + _dot(v_rows[i], pb[i * LANES:(i + 1) * LANES, :])
        return m_new, l_new, alpha * acc + pv

    last_tile = kk_ref.shape[1] // KT - 1

    def score_chunks(kt):
        kd = jnp.minimum(kt, last_tile)
        bias8 = sel_scr[pl.ds(pl.multiple_of(kd * blocks_per_tile, blocks_per_tile), blocks_per_tile), :]
        bias8 = jnp.where(kt <= last_tile, bias8, NEG)
        bias_rows = jnp.concatenate([lanes4(bias8), jnp.zeros((16 - blocks_per_tile, wq), F32)], axis=0).astype(BF16)
        q_bias = jnp.concatenate([qr_top, bias_rows, jnp.zeros((HEAD_DIM - 16, wq), BF16)], axis=0)
        for c in range(sub):
            kc0 = pl.multiple_of(kd * KT + c * LANES, LANES)
            yield c, _dot(ksx_ref[0, pl.ds(kc0, LANES), :], q_bias)

    def fold8(x, op):
        out = x[0:8, :]
        for r in range(1, x.shape[0] // 8):
            out = op(out, x[8 * r:8 * r + 8, :])
        return out

    def pv_chunk(vt, c, p_ref):
        return _dot(vt[c, 0:HEAD_DIM, :], p_ref[c * LANES:(c + 1) * LANES, :])

    def v_tile(kt):
        return vvt_ref[0, pl.ds(pl.multiple_of(jnp.minimum(kt, last_tile) * sub, sub), sub)]

    def phase(kt, s_cur, s_next, p_cur, p_prev, state):
        m_i, l_i, acc, mt_cur, alpha_prev = state
        m_new = jnp.maximum(m_i, jnp.max(mt_cur, axis=0, keepdims=True))
        alpha = jnp.exp2(m_i - m_new)
        mt_next = jnp.full((8, wq), NEG, F32)
        l_new = alpha * l_i
        pv = jnp.zeros((HEAD_DIM, wq), F32)
        vt = v_tile(kt - 1) if p_prev is not None else None
        for c, s_n in score_chunks(kt + 1):
            rows = slice(c * LANES, (c + 1) * LANES)
            mt_next = jnp.maximum(mt_next, fold8(s_n, jnp.maximum))
            s_next[rows, :] = s_n
            p = jnp.exp2(s_cur[rows, :] - m_new)
            l_new = l_new + fold8(p, jnp.add)
            p_cur[rows, :] = p.astype(BF16)
            if p_prev is not None:
                pv = pv + pv_chunk(vt, c, p_prev)
        return m_new, l_new, alpha_prev * acc + pv, mt_next, alpha

    def slc_pair(j, state):
        state = phase(2 * j + 1, sb_scr, sa_scr, pb_scr, pa_scr, state)
        return phase(2 * j + 2, sa_scr, sb_scr, pa_scr, pb_scr, state)

    init = (jnp.full((1, wq), NEG, F32), jnp.zeros((1, wq), F32), jnp.zeros((HEAD_DIM, wq), F32))
    mt0 = jnp.full((8, wq), NEG, F32)
    for c, s_n in score_chunks(0):
        mt0 = jnp.maximum(mt0, fold8(s_n, jnp.maximum))
        sa_scr[c * LANES:(c + 1) * LANES, :] = s_n
    key_d = s0 + lax.broadcasted_iota(jnp.int32, (tq, tq), 0)
    s_d = _dot(kk_ref[0, pl.ds(pl.multiple_of(s0, tq), tq), :], qr_lo) + lanes4(jnp.where(key_d <= t_row, 0.0, NEG))
    m_d = jnp.max(s_d, axis=0, keepdims=True)
    p_d = jnp.exp2(s_d - m_d)
    acc_d = _dot(vvt_ref[0, qb][0:HEAD_DIM, :], p_d.astype(BF16))
    state = (m_d, fold8(p_d, jnp.add), acc_d, mt0, jnp.ones((1, wq), F32))
    state = phase(0, sa_scr, sb_scr, pa_scr, None, state)
    n_tiles = (s0 + tq + KT - 1) // KT
    pairs = n_tiles // 2
    _, l8_s, acc_s, _, alpha_last = lax.fori_loop(0, pairs, slc_pair, state)
    vt_last = v_tile(2 * pairs)
    pv = pv_chunk(vt_last, 0, pa_scr)
    for c in range(1, sub):
        pv = pv + pv_chunk(vt_last, c, pa_scr)
    acc_s = alpha_last * acc_s + pv
    l_s = jnp.sum(l8_s, axis=0, keepdims=True)

    wkeys = WIN + tq
    k0 = pl.multiple_of(jnp.maximum(s0 - WIN, 0), LANES)
    key = k0 + lax.broadcasted_iota(jnp.int32, (wkeys, tq), 0)
    wbias = jnp.where((key <= t_row) & (key > t_row - WIN), 0.0, NEG)
    s_w = _dot(kk_ref[0, pl.ds(k0, wkeys), :], qr_hi) + lanes4(wbias)
    vt_w = vvt_ref[0, pl.ds(jnp.maximum(qb - WIN // tq, 0), wkeys // LANES)]
    _, l_w, acc_w = online_update(init, s_w, [vt_w[i, HEAD_DIM:2 * HEAD_DIM, :] for i in range(wkeys // LANES)])

    g = gt_ref[0]
    gate = lambda br: jnp.concatenate([g[h * 3 + br:h * 3 + br + 1, :] for h in range(nh)], axis=1)
    o = gate(0) * o_cmp + gate(1) * (acc_s * (1.0 / l_s)) + gate(2) * (acc_w * (1.0 / l_w))
    o_rows = jnp.concatenate([o[:, h * tq:(h + 1) * tq] for h in range(nh)], axis=0)
    yd_ref[0] = o_rows.T.astype(BF16)


def _nsa(qt, qrt, gt, kk, ksx, vvt, kcmp, vcmpt, ovt):
    b, _, s = qt.shape
    n_slc = s // SLC_BLOCK
    m = kcmp.shape[1]
    return pl.pallas_call(
        _nsa_kernel,
        grid=(b, s // TQ),
        in_specs=[
            pl.BlockSpec((1, 256, TQ), lambda bi, qi: (bi, 0, qi)),
            pl.BlockSpec((1, 256, TQ), lambda bi, qi: (bi, 0, qi)),
            pl.BlockSpec((1, 16, TQ), lambda bi, qi: (bi, 0, qi)),
            pl.BlockSpec((1, s, LANES), lambda bi, qi: (bi, 0, 0)),
            pl.BlockSpec((1, s, LANES), lambda bi, qi: (bi, 0, 0)),
            pl.BlockSpec((1, s // LANES, LANES, LANES), lambda bi, qi: (bi, 0, 0, 0)),
            pl.BlockSpec((1, m, LANES), lambda bi, qi: (bi, 0, 0)),
            pl.BlockSpec((1, HEAD_DIM, m), lambda bi, qi: (bi, 0, 0)),
            pl.BlockSpec((n_slc, m), lambda bi, qi: (0, 0)),
        ],
        out_specs=pl.BlockSpec((1, TQ, 256), lambda bi, qi: (bi, qi, 0)),
        out_shape=jax.ShapeDtypeStruct((b, s, 256), BF16),
        scratch_shapes=[pltpu.VMEM((n_slc, TQ), F32),
                        pltpu.VMEM((KT, NSA_HEADS * TQ), F32), pltpu.VMEM((KT, NSA_HEADS * TQ), F32),
                        pltpu.VMEM((KT, NSA_HEADS * TQ), BF16), pltpu.VMEM((KT, NSA_HEADS * TQ), BF16)],
        compiler_params=pltpu.CompilerParams(dimension_semantics=("arbitrary", "arbitrary"),
                                             vmem_limit_bytes=VMEM_LIMIT),
        name="sparse_attention",
    )(qt, qrt, gt, kk, ksx, vvt, kcmp, vcmpt, ovt)


def _outproj_kernel(x_ref, yabc_ref, yd_ref, wo_ref, g_ref, b_ref, rw_ref, rb_ref, x1_ref, gates_ref):
    mix = _dot(yabc_ref[...], wo_ref[0:768, :]) + _dot(yd_ref[...], wo_ref[768:1024, :])
    x1 = _layer_norm(DN_ALPHA * x_ref[...] + mix, g_ref[...], b_ref[...])
    x1_ref[...] = x1

    ts = x1.shape[0]
    aff = _sigmoid(_dot(x1.astype(BF16), rw_ref[...])).T[0:N_EXPERTS, :]
    biased = aff + rb_ref[...]
    row = lambda a, i: a[i:i + 1, :]

    gscores = []
    for gi in range(N_GROUPS):
        a, b, c, d = (row(biased, gi * EXPERTS_PER_GROUP + i) for i in range(EXPERTS_PER_GROUP))
        hi1, lo1, hi2, lo2 = jnp.maximum(a, b), jnp.minimum(a, b), jnp.maximum(c, d), jnp.minimum(c, d)
        gscores.append(jnp.maximum(hi1, hi2) + jnp.maximum(jnp.minimum(hi1, hi2), jnp.maximum(lo1, lo2)))
    best, gsel = gscores[0], jnp.zeros((1, ts), jnp.int32)
    for gi in range(1, N_GROUPS):
        better = gscores[gi] > best
        gsel = jnp.where(better, gi, gsel)
        best = jnp.where(better, gscores[gi], best)

    eid_i = lax.broadcasted_iota(jnp.int32, (N_EXPERTS, ts), 0)
    eid = eid_i.astype(F32)
    masked = jnp.where(jnp.right_shift(eid_i, 2) == gsel, biased, NEG)
    picks = []
    for _ in range(2):
        top = jnp.max(masked, axis=0, keepdims=True)
        first = jnp.min(jnp.where(masked == top, eid, float(N_EXPERTS)), axis=0, keepdims=True)
        pick = eid == first
        picks.append(pick)
        masked = jnp.where(pick, -jnp.inf, masked)
    chosen = picks[0] | picks[1]
    w_sel = jnp.where(chosen, aff, 0.0)
    gates_t = w_sel / jnp.sum(w_sel, axis=0, keepdims=True)
    grp8 = jnp.where(lax.broadcasted_iota(jnp.int32, (8, ts), 0) == 0, gsel.astype(F32), 0.0)
    gates_ref[...] = jnp.concatenate([gates_t, grp8, jnp.zeros((LANES - N_EXPERTS - 8, ts), F32)], axis=0).T


def _outproj(x2d, yabc, yd, wo, g, b, rw, rb):
    n, d = x2d.shape
    ts = TS_OUT
    const = lambda shape: pl.BlockSpec(shape, lambda i: (0,) * len(shape))
    return pl.pallas_call(
        _outproj_kernel,
        grid=(n // ts,),
        in_specs=[
            pl.BlockSpec((ts, d), lambda i: (i, 0)),
            pl.BlockSpec((ts, 768), lambda i: (i, 0)),
            pl.BlockSpec((ts, 256), lambda i: (i, 0)),
            const((d, d)), const((1, d)), const((1, d)), const((d, LANES)), const((N_EXPERTS, 1)),
        ],
        out_specs=(pl.BlockSpec((ts, d), lambda i: (i, 0)), pl.BlockSpec((ts, LANES), lambda i: (i, 0))),
        out_shape=(jax.ShapeDtypeStruct((n, d), F32), jax.ShapeDtypeStruct((n, LANES), F32)),
        compiler_params=pltpu.CompilerParams(dimension_semantics=("arbitrary",), vmem_limit_bytes=VMEM_LIMIT),
        name="outproj_ln_route",
    )(x2d, yabc, yd, wo, g, b, rw, rb)


def _split3(x):
    hi = x.astype(BF16)
    r1 = x - hi.astype(F32)
    mid = r1.astype(BF16)
    lo = (r1 - mid.astype(F32)).astype(BF16)
    return hi, mid, lo


def _dispatch_kernel(x1_ref, gates_ref, tri_ref, xs_ref, gs_ref, post_ref, cnt_ref):
    tm = x1_ref.shape[0]
    gates = gates_ref[...]
    gsel = gates.T[GROUP_LANE:GROUP_LANE + 1, :]
    onehot = jnp.where(lax.broadcasted_iota(jnp.int32, (8, tm), 0).astype(F32) == gsel, 1.0, 0.0)
    n = jnp.sum(onehot, axis=1, keepdims=True)
    n_al = jnp.floor((n + (ROW_ALIGN - 1)) * (1.0 / ROW_ALIGN)) * ROW_ALIGN
    starts = [jnp.zeros((1, 1), F32)]
    for gi in range(1, N_GROUPS):
        starts.append(starts[-1] + n_al[gi - 1:gi, :])
    start = jnp.concatenate(starts + [jnp.zeros((8 - N_GROUPS, 1), F32)], axis=0)
    before = _dot(onehot.astype(BF16), tri_ref[...])
    pos = jnp.sum(onehot * (start + before), axis=0, keepdims=True)
    perm = jnp.where(lax.broadcasted_iota(jnp.int32, (SORT_ROWS, tm), 0).astype(F32) == pos, 1.0, 0.0).astype(BF16)

    xs_ref[0, 0:SORT_ROWS, :] = _dot(perm, x1_ref[...].astype(BF16)).astype(BF16)
    xs_ref[0, SORT_ROWS:SORT_BUF, :] = jnp.zeros((SORT_BUF - SORT_ROWS, xs_ref.shape[2]), BF16)
    g_hi, g_mid, g_lo = _split3(gates)
    gs_ref[0, 0:SORT_ROWS, :] = _dot(perm, g_hi) + _dot(perm, g_mid) + _dot(perm, g_lo)
    gs_ref[0, SORT_ROWS:SORT_BUF, :] = jnp.zeros((SORT_BUF - SORT_ROWS, LANES), F32)
    pos8 = jnp.where(lax.broadcasted_iota(jnp.int32, (8, tm), 0) == 0, pos, 0.0)
    post_ref[...] = jnp.concatenate([pos8, jnp.zeros((LANES - 8, tm), F32)], axis=0).T
    cnt_ref[0] = jnp.broadcast_to(n, (8, LANES))


def _dispatch(x1, gates, tri):
    n, d = x1.shape
    tm = TM_MOE
    nt = n // tm
    return pl.pallas_call(
        _dispatch_kernel,
        grid=(nt,),
        in_specs=[
            pl.BlockSpec((tm, d), lambda i: (i, 0)),
            pl.BlockSpec((tm, LANES), lambda i: (i, 0)),
            pl.BlockSpec((tm, tm), lambda i: (0, 0)),
        ],
        out_specs=(
            pl.BlockSpec((1, SORT_BUF, d), lambda i: (i, 0, 0)),
            pl.BlockSpec((1, SORT_BUF, LANES), lambda i: (i, 0, 0)),
            pl.BlockSpec((tm, LANES), lambda i: (i, 0)),
            pl.BlockSpec((1, 8, LANES), lambda i: (i, 0, 0)),
        ),
        out_shape=(
            jax.ShapeDtypeStruct((nt, SORT_BUF, d), BF16),
            jax.ShapeDtypeStruct((nt, SORT_BUF, LANES), F32),
            jax.ShapeDtypeStruct((n, LANES), F32),
            jax.ShapeDtypeStruct((nt, 8, LANES), F32),
        ),
        compiler_params=pltpu.CompilerParams(dimension_semantics=("arbitrary",), vmem_limit_bytes=VMEM_LIMIT),
        name="moe_dispatch",
    )(x1, gates, tri)


def _moe_kernel(start_ref, trips_ref, xs_ref, gs_ref, wg_ref, wu_ref, wd_ref, acc_ref):
    i, e = pl.program_id(0), pl.program_id(1)

    @pl.when(e == 0)
    def _():
        acc_ref[...] = jnp.zeros(acc_ref.shape, F32)

    grp = lax.shift_right_logical(e, 2)
    for tile in range(TILES_PER_STEP):
        slot = (i * TILES_PER_STEP + tile) * N_GROUPS + grp
        row0 = start_ref[slot]

        def window(w, carry, tile=tile, row0=row0):
            rows = pl.ds(pl.multiple_of(row0 + w * WIN_ROWS, ROW_ALIGN), WIN_ROWS)
            xw = xs_ref[tile, rows, :]
            hg = _dot(xw, wg_ref[0])
            h = hg * _sigmoid(hg) * _dot(xw, wu_ref[0])
            y = _dot(h.astype(BF16), wd_ref[0])
            gw = gs_ref[tile, rows, :]
            lane = lax.broadcasted_iota(jnp.int32, gw.shape, 1)
            gcol = jnp.sum(jnp.where(lane == e, gw, 0.0), axis=1, keepdims=True)
            acc_ref[tile, rows, :] += y * gcol
            return carry

        lax.fori_loop(0, trips_ref[slot], window, 0)


def _moe(start, trips, xs, gs, wg, wu, wd):
    nt, _, d = xs.shape
    ne, _, de = wg.shape
    tps = TILES_PER_STEP
    grid_spec = pltpu.PrefetchScalarGridSpec(
        num_scalar_prefetch=2,
        grid=(nt // tps, ne),
        in_specs=[
            pl.BlockSpec((tps, SORT_BUF, d), lambda i, e, *_: (i, 0, 0)),
            pl.BlockSpec((tps, SORT_BUF, LANES), lambda i, e, *_: (i, 0, 0)),
            pl.BlockSpec((1, d, de), lambda i, e, *_: (e, 0, 0)),
            pl.BlockSpec((1, d, de), lambda i, e, *_: (e, 0, 0)),
            pl.BlockSpec((1, de, d), lambda i, e, *_: (e, 0, 0)),
        ],
        out_specs=pl.BlockSpec((tps, SORT_BUF, d), lambda i, e, *_: (i, 0, 0)),
    )
    return pl.pallas_call(
        _moe_kernel,
        grid_spec=grid_spec,
        out_shape=jax.ShapeDtypeStruct((nt, SORT_BUF, d), F32),
        compiler_params=pltpu.CompilerParams(dimension_semantics=("arbitrary", "arbitrary"),
                                             vmem_limit_bytes=VMEM_LIMIT),
        name="moe_experts",
    )(start, trips, xs, gs, wg, wu, wd)


def _combine_kernel(x1_ref, acc_ref, post_ref, g_ref, b_ref, out_ref):
    tm = x1_ref.shape[0]
    acc = acc_ref[0]
    hi = acc.astype(BF16)
    lo = (acc - hi.astype(F32)).astype(BF16)
    pos = post_ref[...][:, 0:1]
    unperm = jnp.where(lax.broadcasted_iota(jnp.int32, (tm, SORT_ROWS), 1).astype(F32) == pos, 1.0, 0.0).astype(BF16)
    moe = _dot(unperm, hi) + _dot(unperm, lo)
    out_ref[...] = _layer_norm(DN_ALPHA * x1_ref[...] + moe, g_ref[...], b_ref[...])


def _combine(x1, acc, post, g, b):
    n, d = x1.shape
    tm = TM_MOE
    const = lambda shape: pl.BlockSpec(shape, lambda i: (0,) * len(shape))
    return pl.pallas_call(
        _combine_kernel,
        grid=(n // tm,),
        in_specs=[
            pl.BlockSpec((tm, d), lambda i: (i, 0)),
            pl.BlockSpec((1, SORT_ROWS, d), lambda i: (i, 0, 0)),
            pl.BlockSpec((tm, LANES), lambda i: (i, 0)),
            const((1, d)), const((1, d)),
        ],
        out_specs=pl.BlockSpec((tm, d), lambda i: (i, 0)),
        out_shape=jax.ShapeDtypeStruct((n, d), F32),
        compiler_params=pltpu.CompilerParams(dimension_semantics=("arbitrary",), vmem_limit_bytes=VMEM_LIMIT),
        name="moe_combine_ln",
    )(x1, acc, post, g, b)


def _group_windows(cnt):
    n = cnt[:, 0:N_GROUPS, 0].astype(jnp.int32)
    n_al = (n + (ROW_ALIGN - 1)) // ROW_ALIGN * ROW_ALIGN
    start = jnp.cumsum(n_al, axis=1) - n_al
    trips = (n + (WIN_ROWS - 1)) // WIN_ROWS
    return start.reshape(-1), trips.reshape(-1)


def _permute_w_in(w):
    sizes = (256, 256, 256, 256, 256, 256, 256, 64, 64, 64, 64, 64, 64, 12)
    offs = np.concatenate([[0], np.cumsum(sizes)])
    part = lambda i: w[:, int(offs[i]):int(offs[i + 1])]
    a_b, a_c, a_h, b_val, b_gate, c_p, q, kc, vc, ks, vs, kw, vw, g = (part(i) for i in range(14))
    pad = jnp.zeros((w.shape[0], LANES - g.shape[1]), w.dtype)
    return jnp.concatenate([a_b, a_c, a_h, b_val, b_gate, c_p, q, kc, vc, ks, kw, vs, vw, g, pad], axis=1).astype(BF16)


def _rope_tables(s):
    inv = 1.0 / (ROPE_THETA ** (jnp.arange(0, HEAD_DIM, 2, dtype=F32) / HEAD_DIM))
    ang = jnp.arange(s, dtype=F32)[:, None] * inv[None, :]
    ang = jnp.concatenate([ang, ang], -1)
    sign = jnp.concatenate([-jnp.ones((HEAD_DIM // 2,), F32), jnp.ones((HEAD_DIM // 2,), F32)])
    return jnp.tile(jnp.cos(ang), (1, NSA_HEADS)), jnp.tile(jnp.sin(ang) * sign[None, :], (1, NSA_HEADS))


def _overlap_t(s, m):
    n_cmp = (s - CMP_BLOCK) // CMP_STRIDE + 1
    cmp_start = np.arange(m) * CMP_STRIDE
    slc_start = np.arange(s // SLC_BLOCK) * SLC_BLOCK
    ov = (cmp_start[None, :] < slc_start[:, None] + SLC_BLOCK) & (cmp_start[None, :] + CMP_BLOCK > slc_start[:, None])
    ov = ov & (np.arange(m)[None, :] < n_cmp)
    return jnp.asarray(ov, BF16)


def _block_diag(w):
    g, c, _ = w.shape
    out = jnp.zeros((g * c, g * c), w.dtype)
    for i in range(g):
        out = out.at[i * c:(i + 1) * c, i * c:(i + 1) * c].set(w[i])
    return out


def kernel(x, w_in, conv_a_w, conv_b_w, conv_b_b, cf_ln_g, cf_ln_b, pool_w, pool_scale, cmp_pe_k, cmp_pe_v, cmp_k_w1, cmp_k_w2, cmp_v_w1, cmp_v_w2, w_out, ln1_g, ln1_b, ln2_g, ln2_b, router_w, router_bias, moe_w_gate, moe_w_up, moe_w_down):
    b, s, d = x.shape
    n = b * s
    m = s // CMP_STRIDE
    tok_per_row = CMP_STRIDE
    cos4, sin4 = _rope_tables(s)
    ovt = _overlap_t(s, m)
    rw = jnp.concatenate([router_w, jnp.zeros((d, LANES - N_EXPERTS), router_w.dtype)], axis=1).astype(BF16)
    rb = router_bias.reshape(N_EXPERTS, 1).astype(F32)
    pad_w2 = lambda w: jnp.concatenate([w, jnp.zeros((CMP_HIDDEN, LANES - HEAD_DIM), w.dtype)], axis=1).astype(BF16)
    row = lambda v: v.reshape(1, -1)
    tri = jnp.asarray(np.triu(np.ones((TM_MOE, TM_MOE), np.float32), k=1), BF16)

    for l in range(DEPTH):
        yabc, qt, qrt, kcvc, kk, ksx, vvt, gt = _inproj(
            x, _permute_w_in(w_in[l]), cos4, sin4, conv_a_w[l], conv_b_w[l], row(conv_b_b[l]),
            row(cf_ln_g[l]), row(cf_ln_b[l]), _block_diag(pool_w[l]).astype(BF16), row(pool_scale[l]))
        kc16 = kcvc[:, :, 0:HEAD_DIM].reshape(b, m, tok_per_row * HEAD_DIM)
        vc16 = kcvc[:, :, HEAD_DIM:2 * HEAD_DIM].reshape(b, m, tok_per_row * HEAD_DIM)
        kcmp, vcmpt = _compress(
            kc16, vc16, cmp_pe_k[l].reshape(2, tok_per_row * HEAD_DIM), cmp_pe_v[l].reshape(2, tok_per_row * HEAD_DIM),
            cmp_k_w1[l].astype(BF16), pad_w2(cmp_k_w2[l]), cmp_v_w1[l].astype(BF16), pad_w2(cmp_v_w2[l]))
        yd = _nsa(qt, qrt, gt, kk, ksx, vvt, kcmp, vcmpt, ovt)
        x1, gates = _outproj(x.reshape(n, d), yabc.reshape(n, 768), yd.reshape(n, 256), w_out[l].astype(BF16),
                             row(ln1_g[l]), row(ln1_b[l]), rw, rb)
        xs, gs, post, cnt = _dispatch(x1, gates, tri)
        start, trips = _group_windows(cnt)
        acc = _moe(start, trips, xs, gs, moe_w_gate[l].astype(BF16), moe_w_up[l].astype(BF16),
                   moe_w_down[l].astype(BF16))
        x = _combine(x1, acc, post, row(ln2_g[l]), row(ln2_b[l])).reshape(b, s, d)
    return x
```

```python
import functools

import jax
import jax.numpy as jnp
import numpy as np
from jax import lax
from jax.experimental import pallas as pl
from jax.experimental.pallas import tpu as pltpu

F32 = jnp.float32
BF16 = jnp.bfloat16

D_MODEL = 1024
DEPTH = 2
GROUP_W = 256
HEAD_DIM = 64
SC_KERNEL = 3
CF_KERNEL = 31
POOL_WINDOWS = (2, 4, 8, 16)
POOL_GW = 64
NSA_HEADS = 4
CMP_BLOCK = 32
CMP_STRIDE = 16
CMP_HIDDEN = 256
SLC_BLOCK = 64
SLC_TOPN = 16
WIN = 512
N_EXPERTS = 16
N_GROUPS = 4
EXPERTS_PER_GROUP = 4
D_EXPERT = 512
DN_ALPHA = (2 * DEPTH) ** 0.25
LN_EPS = 1e-5
NEG = -1e30
FORCE = 1e4
ROPE_THETA = 10000.0
QK_SCALE = HEAD_DIM ** -0.5
LOG2E = 1.4426950408889634

LANES = 128
OFF_A, OFF_B, OFF_C, OFF_Q, OFF_KCVC, OFF_KK, OFF_VV, OFF_G, IN_PAD = 0, 768, 1280, 1536, 1792, 1920, 2048, 2176, 2304

HALO_A = 8
HALO_B = 32
HALO_C = 16

TS_IN = 512
TQ = 256
KT = 512
TS_OUT = 512
TM_MOE = 1024
ROW_ALIGN = 16
WIN_ROWS = 288
SORT_ROWS = TM_MOE + LANES
SORT_BUF = SORT_ROWS + 2 * LANES
GROUP_LANE = 16
TILES_PER_STEP = 2
assert SORT_ROWS >= TM_MOE + N_GROUPS * (ROW_ALIGN - 1)
assert SORT_BUF >= TM_MOE + N_GROUPS * (ROW_ALIGN - 1) + WIN_ROWS
VMEM_LIMIT = 56 * 1024 * 1024


def _sigmoid(x):
    return jax.nn.sigmoid(x)


def _layer_norm(h, g, b):
    mu = jnp.mean(h, axis=-1, keepdims=True)
    d = h - mu
    var = jnp.mean(d * d, axis=-1, keepdims=True)
    return d * lax.rsqrt(var + LN_EPS) * g + b


def _dot(a, b):
    return jnp.dot(a, b, preferred_element_type=F32)


def _inproj_kernel(x_ref, w_ref, cos_ref, sin_ref, cva_ref, cvb_ref, cvbb_ref, lng_ref, lnb_ref,
                   poolw_ref, pools_ref,
                   yabc_ref, qt_ref, qrt_ref, kcvc_ref, kk_ref, ksx_ref, vvt_ref, gt_ref,
                   ch_ext, u_ext, p_ext, u_sh):
    j = pl.program_id(1)
    ts = x_ref.shape[1]
    xb = x_ref[0].astype(BF16)

    def proj(lo, hi):
        return _dot(xb, w_ref[:, lo:hi])

    def carry_halo(ext, halo):
        @pl.when(j == 0)
        def _():
            ext[0:halo, :] = jnp.zeros((halo, ext.shape[1]), F32)

        @pl.when(j > 0)
        def _():
            ext[0:halo, :] = ext[ts:ts + halo, :]

    pa = proj(OFF_A, OFF_B)
    a_b, ch = pa[:, 0:256], pa[:, 256:512] * pa[:, 512:768]
    carry_halo(ch_ext, HALO_A)
    ch_ext[HALO_A:HALO_A + ts, :] = ch
    conv = cva_ref[2:3, :] * ch
    for k in range(SC_KERNEL - 1):
        off = HALO_A - (SC_KERNEL - 1) + k
        conv = conv + cva_ref[k:k + 1, :] * ch_ext[off:off + ts, :]
    yabc_ref[0, :, 0:256] = (a_b * conv).astype(BF16)

    pb = proj(OFF_B, OFF_C)
    u = pb[:, 0:256] * _sigmoid(pb[:, 256:512])
    carry_halo(u_ext, HALO_B)
    u_ext[HALO_B:HALO_B + ts, :] = u
    span = u_sh.shape[1]
    for r in range(1, 8):
        u_sh[r - 1, :, :] = u_ext[r:r + span, :]
    def mixer_c():
        pc = proj(OFF_C, OFF_Q)
        carry_halo(p_ext, HALO_C)
        p_ext[HALO_C:HALO_C + ts, :] = pc
        pe = p_ext[...]
        s2 = pe + pltpu.roll(pe, 1, 0)
        s4 = s2 + pltpu.roll(s2, 2, 0)
        s8 = s4 + pltpu.roll(s4, 4, 0)
        s16 = s8 + pltpu.roll(s8, 8, 0)
        lane = lax.broadcasted_iota(jnp.int32, (ts, 256), 1)
        grp = jnp.right_shift(lane, 6)
        wsum = jnp.where(grp == 0, s2[HALO_C:], jnp.where(grp == 1, s4[HALO_C:], jnp.where(grp == 2, s8[HALO_C:], s16[HALO_C:])))
        win = jnp.where(grp == 0, 2, jnp.where(grp == 1, 4, jnp.where(grp == 2, 8, 16)))
        t1 = j * ts + lax.broadcasted_iota(jnp.int32, (ts, 256), 0) + 1
        cnt = jnp.minimum(win, t1).astype(F32)
        dlt = wsum / cnt - pc
        yabc_ref[0, :, 512:768] = (_dot(dlt.astype(BF16), poolw_ref[...]) * pools_ref[...]).astype(BF16)

    def queries():
        cos4, sin4 = cos_ref[...], sin_ref[...]
        q = proj(OFF_Q, OFF_KCVC)
        lane_q = jnp.bitwise_and(lax.broadcasted_iota(jnp.int32, (ts, 256), 1), HEAD_DIM - 1)
        q_sw = jnp.where(lane_q < HEAD_DIM // 2, pltpu.roll(q, 256 - HEAD_DIM // 2, 1), pltpu.roll(q, HEAD_DIM // 2, 1))
        qr = q * cos4 + q_sw * sin4
        qt_ref[0] = (q * QK_SCALE).T.astype(BF16)
        qrt_ref[0] = (qr * (QK_SCALE * LOG2E)).T.astype(BF16)

    def compress_inputs():
        kcvc_ref[0] = proj(OFF_KCVC, OFF_KK)

    def keys():
        k2 = proj(OFF_KK, OFF_VV)
        lane_k = jnp.bitwise_and(lax.broadcasted_iota(jnp.int32, (ts, LANES), 1), HEAD_DIM - 1)
        k_sw = jnp.where(lane_k < HEAD_DIM // 2, pltpu.roll(k2, LANES - HEAD_DIM // 2, 1), pltpu.roll(k2, HEAD_DIM // 2, 1))
        k_rot = k2 * cos_ref[:, 0:LANES] + k_sw * sin_ref[:, 0:LANES]
        kk_ref[0] = k_rot.astype(BF16)
        lane_i = lax.broadcasted_iota(jnp.int32, (ts, LANES), 1)
        blk_in_tile = jnp.bitwise_and(jnp.right_shift(j * ts + lax.broadcasted_iota(jnp.int32, (ts, LANES), 0), 6),
                                      KT // SLC_BLOCK - 1)
        ksx_ref[0] = jnp.where(lane_i < HEAD_DIM, k_rot, jnp.where(lane_i - HEAD_DIM == blk_in_tile, 1.0, 0.0)).astype(BF16)

    def values():
        v2t = proj(OFF_VV, OFF_G).T
        for c in range(ts // LANES):
            vvt_ref[0, c] = v2t[:, c * LANES:(c + 1) * LANES].astype(BF16)

    def branch_gates():
        gt_ref[0] = _sigmoid(proj(OFF_G, IN_PAD)).T[0:16, :]

    interleaved = [mixer_c, queries, compress_inputs, keys, values, branch_gates]
    rows = 64
    assert ts // rows >= len(interleaved)
    for c in range(ts // rows):
        acc = jnp.zeros((rows, 256), F32) + cvbb_ref[...]
        for k in range(CF_KERNEL):
            off = HALO_B - (CF_KERNEL - 1) + k
            base = off // 8 * 8 + c * rows
            src = u_ext if off % 8 == 0 else u_sh.at[off % 8 - 1]
            acc = acc + cvb_ref[k:k + 1, :] * src[base:base + rows, :]
        v = _layer_norm(acc, lng_ref[...], lnb_ref[...])
        yabc_ref[0, c * rows:(c + 1) * rows, 256:512] = (v * _sigmoid(v)).astype(BF16)
        if c < len(interleaved):
            interleaved[c]()


def _inproj(x, w_in_p, cos4, sin4, cva, cvb, cvbb, lng, lnb, poolw, pools):
    b, s, d = x.shape
    ts = TS_IN
    grid = (b, s // ts)
    const = lambda shape: pl.BlockSpec(shape, lambda bi, ji: (0,) * len(shape))
    out_shape = (
        jax.ShapeDtypeStruct((b, s, 768), BF16),
        jax.ShapeDtypeStruct((b, 256, s), BF16),
        jax.ShapeDtypeStruct((b, 256, s), BF16),
        jax.ShapeDtypeStruct((b, s, LANES), F32),
        jax.ShapeDtypeStruct((b, s, LANES), BF16),
        jax.ShapeDtypeStruct((b, s, LANES), BF16),
        jax.ShapeDtypeStruct((b, s // LANES, LANES, LANES), BF16),
        jax.ShapeDtypeStruct((b, 16, s), F32),
    )
    return pl.pallas_call(
        _inproj_kernel,
        grid=grid,
        in_specs=[
            pl.BlockSpec((1, ts, d), lambda bi, ji: (bi, ji, 0)),
            const((d, IN_PAD)),
            pl.BlockSpec((ts, 256), lambda bi, ji: (ji, 0)),
            pl.BlockSpec((ts, 256), lambda bi, ji: (ji, 0)),
            const((SC_KERNEL, 256)), const((CF_KERNEL, 256)), const((1, 256)), const((1, 256)), const((1, 256)),
            const((256, 256)), const((1, 256)),
        ],
        out_specs=(
            pl.BlockSpec((1, ts, 768), lambda bi, ji: (bi, ji, 0)),
            pl.BlockSpec((1, 256, ts), lambda bi, ji: (bi, 0, ji)),
            pl.BlockSpec((1, 256, ts), lambda bi, ji: (bi, 0, ji)),
            pl.BlockSpec((1, ts, LANES), lambda bi, ji: (bi, ji, 0)),
            pl.BlockSpec((1, ts, LANES), lambda bi, ji: (bi, ji, 0)),
            pl.BlockSpec((1, ts, LANES), lambda bi, ji: (bi, ji, 0)),
            pl.BlockSpec((1, ts // LANES, LANES, LANES), lambda bi, ji: (bi, ji, 0, 0)),
            pl.BlockSpec((1, 16, ts), lambda bi, ji: (bi, 0, ji)),
        ),
        out_shape=out_shape,
        scratch_shapes=[
            pltpu.VMEM((ts + HALO_A, 256), F32),
            pltpu.VMEM((ts + HALO_B, 256), F32),
            pltpu.VMEM((ts + HALO_C, 256), F32),
            pltpu.VMEM((7, ts + HALO_B - 8, 256), F32),
        ],
        compiler_params=pltpu.CompilerParams(dimension_semantics=("arbitrary", "arbitrary"),
                                             vmem_limit_bytes=VMEM_LIMIT),
        name="inproj_mixers",
    )(x, w_in_p, cos4, sin4, cva, cvb, cvbb, lng, lnb, poolw, pools)


def _compress_kernel(kc_ref, vc_ref, pek_ref, pev_ref, wk1_ref, wk2_ref, wv1_ref, wv2_ref, kcmp_ref, vcmpt_ref):
    def comp(x_ref, pe_ref, w1_ref, w2_ref):
        x = x_ref[0]
        m = x.shape[0]
        half = x.shape[1]
        first = _dot((x + pe_ref[0:1, :]).astype(BF16), w1_ref[0:half, :])
        second = _dot((x + pe_ref[1:2, :]).astype(BF16), w1_ref[half:2 * half, :])
        hid = first + pltpu.roll(second, m - 1, 0)
        return _dot((hid * _sigmoid(hid)).astype(BF16), w2_ref[...])

    kcmp_ref[0] = comp(kc_ref, pek_ref, wk1_ref, wk2_ref).astype(BF16)
    vcmpt_ref[0] = comp(vc_ref, pev_ref, wv1_ref, wv2_ref).T[0:HEAD_DIM, :].astype(BF16)


def _compress(kc16, vc16, pek, pev, wk1, wk2, wv1, wv2):
    b, m, w = kc16.shape
    const = lambda shape: pl.BlockSpec(shape, lambda bi: (0,) * len(shape))
    return pl.pallas_call(
        _compress_kernel,
        grid=(b,),
        in_specs=[
            pl.BlockSpec((1, m, w), lambda bi: (bi, 0, 0)),
            pl.BlockSpec((1, m, w), lambda bi: (bi, 0, 0)),
            const((2, w)), const((2, w)),
            const((2 * w, CMP_HIDDEN)), const((CMP_HIDDEN, LANES)),
            const((2 * w, CMP_HIDDEN)), const((CMP_HIDDEN, LANES)),
        ],
        out_specs=(
            pl.BlockSpec((1, m, LANES), lambda bi: (bi, 0, 0)),
            pl.BlockSpec((1, HEAD_DIM, m), lambda bi: (bi, 0, 0)),
        ),
        out_shape=(jax.ShapeDtypeStruct((b, m, LANES), BF16), jax.ShapeDtypeStruct((b, HEAD_DIM, m), BF16)),
        compiler_params=pltpu.CompilerParams(dimension_semantics=("arbitrary",), vmem_limit_bytes=VMEM_LIMIT),
        name="compress_kv",
    )(kc16, vc16, pek, pev, wk1, wk2, wv1, wv2)


def _nsa_kernel(qt_ref, qrt_ref, gt_ref, kk_ref, ksx_ref, vvt_ref, kcmp_ref, vcmpt_ref, ovt_ref, yd_ref, sel_scr, sa_scr, sb_scr, pa_scr, pb_scr, diag_scr, acc_scr):
    qb = pl.program_id(1)
    tq = qt_ref.shape[2]
    nh = NSA_HEADS
    wq = nh * tq
    s0 = qb * tq
    n_slc = sel_scr.shape[0]
    n_cmp_pad = kcmp_ref.shape[1]

    def stack_heads(ref):
        x = ref[0]
        return jnp.concatenate([x[h * HEAD_DIM:(h + 1) * HEAD_DIM, :] for h in range(nh)], axis=1)

    zeros = jnp.zeros((HEAD_DIM, wq), BF16)
    q_top, qr_top = stack_heads(qt_ref), stack_heads(qrt_ref)
    q_lo = jnp.concatenate([q_top, zeros], axis=0)
    qr_lo = jnp.concatenate([qr_top, zeros], axis=0)
    qr_hi = jnp.concatenate([zeros, qr_top], axis=0)
    t_row = s0 + lax.broadcasted_iota(jnp.int32, (1, tq), 1)
    lanes4 = lambda x: jnp.concatenate([x] * nh, axis=1)
    t_all = lanes4(t_row)

    def compress_and_select(nc):
        nb = nc // (SLC_BLOCK // CMP_STRIDE)
        sc = _dot(kcmp_ref[0, 0:nc, :], q_lo)
        cmp_end = lax.broadcasted_iota(jnp.int32, (nc, wq), 0) * CMP_STRIDE + (CMP_BLOCK - 1)
        sc = jnp.where(cmp_end <= t_all, sc, NEG)
        mx = jnp.max(sc, axis=0, keepdims=True)
        ex = jnp.exp(sc - mx)
        den = jnp.sum(ex, axis=0, keepdims=True)
        p_cmp = jnp.where(t_all >= CMP_BLOCK - 1, ex * (1.0 / den), 0.0)
        o_cmp = _dot(vcmpt_ref[0, :, 0:nc], p_cmp.astype(BF16))
        p_heads = p_cmp[:, 0:tq]
        for h in range(1, nh):
            p_heads = p_heads + p_cmp[:, h * tq:(h + 1) * tq]
        imp = _dot(ovt_ref[0:nb, 0:nc], p_heads.astype(BF16))

        blk_i = lax.broadcasted_iota(jnp.int32, (nb, tq), 0)
        blk = blk_i.astype(F32)
        cur = jnp.right_shift(t_row, 6)
        forced = (blk_i == 0) | (blk_i == cur) | (blk_i == cur - 1)
        valid = blk_i * SLC_BLOCK <= t_row
        imp = jnp.where(valid, jnp.where(forced, FORCE, imp), NEG)
        sel = jnp.zeros((nb, tq), F32)
        for _ in range(min(SLC_TOPN, nb)):
            top = jnp.max(imp, axis=0, keepdims=True)
            first = jnp.min(jnp.where(imp == top, blk, float(nb)), axis=0, keepdims=True)
            pick = blk == first
            sel = jnp.where(pick, 1.0, sel)
            imp = jnp.where(pick, -jnp.inf, imp)
        bias = jnp.where(valid & (sel > 0.0), 0.0, NEG)
        own0 = qb * (tq // SLC_BLOCK)
        own = (blk_i >= own0) & (blk_i < own0 + tq // SLC_BLOCK)
        sel_scr[0:nb, :] = jnp.where(own, NEG, bias)
        if nb < n_slc:
            sel_scr[nb:n_slc, :] = jnp.full((n_slc - nb, tq), NEG, F32)
        key_blk = own0 + jnp.right_shift(lax.broadcasted_iota(jnp.int32, (tq, nb), 0), 6)
        expand = jnp.where(lax.broadcasted_iota(jnp.int32, (tq, nb), 1) == key_blk, 1.0, 0.0).astype(BF16)
        diag_scr[...] = _dot(expand, bias.astype(BF16))
        return o_cmp

    size_classes = 4
    class_rows = n_cmp_pad // size_classes
    visible = (s0 + tq) // CMP_STRIDE
    o_cmp = lax.switch((visible - 1) // class_rows,
                       [functools.partial(compress_and_select, (i + 1) * class_rows) for i in range(size_classes)])

    blocks_per_tile = KT // SLC_BLOCK
    sub = KT // LANES

    def online_update(state, s, v_rows):
        m_i, l_i, acc = state
        m_new = jnp.maximum(m_i, jnp.max(s, axis=0, keepdims=True))
        alpha = jnp.exp2(m_i - m_new)
        p = jnp.exp2(s - m_new)
        l_new = alpha * l_i + jnp.sum(p, axis=0, keepdims=True)
        pb = p.astype(BF16)
        pv = _dot(v_rows[0], pb[0:LANES, :])
        for i in range(1, len(v_rows)):
            pv = pv + _dot(v_rows[i], pb[i * LANES:(i + 1) * LANES, :])
        return m_new, l_new, alpha * acc + pv

    last_tile = kk_ref.shape[1] // KT - 1

    def score_chunks(kt):
        kd = jnp.minimum(kt, last_tile)
        bias8 = sel_scr[pl.ds(pl.multiple_of(kd * blocks_per_tile, blocks_per_tile), blocks_per_tile), :]
        bias8 = jnp.where(kt <= last_tile, bias8, NEG)
        bias_rows = jnp.concatenate([lanes4(bias8), jnp.zeros((16 - blocks_per_tile, wq), F32)], axis=0).astype(BF16)
        q_bias = jnp.concatenate([qr_top, bias_rows, jnp.zeros((HEAD_DIM - 16, wq), BF16)], axis=0)
        for c in range(sub):
            kc0 = pl.multiple_of(kd * KT + c * LANES, LANES)
            yield c, _dot(ksx_ref[0, pl.ds(kc0, LANES), :], q_bias)

    def fold8(x, op):
        out = x[0:8, :]
        for r in range(1, x.shape[0] // 8):
            out = op(out, x[8 * r:8 * r + 8, :])
        return out

    def v_rows(first, count, lo):
        vt = vvt_ref[0, pl.ds(first, count)]
        return jnp.concatenate([vt[c, lo:lo + HEAD_DIM, :] for c in range(count)], axis=1)

    def pv_tile(kt, p_ref):
        return _dot(v_rows(pl.multiple_of(jnp.minimum(kt, last_tile) * sub, sub), sub, 0), p_ref[...])

    def phase(kt, s_cur, s_next, p_cur, p_prev, state):
        m_i, l_i, mt_cur, alpha_prev = state
        m_new = jnp.maximum(m_i, jnp.max(mt_cur, axis=0, keepdims=True))
        alpha = jnp.exp2(m_i - m_new)
        mt_next = jnp.full((8, wq), NEG, F32)
        l_new = alpha * l_i
        for c, s_n in score_chunks(kt + 1):
            rows = slice(c * LANES, (c + 1) * LANES)
            mt_next = jnp.maximum(mt_next, fold8(s_n, jnp.maximum))
            s_next[rows, :] = s_n
            p = jnp.exp2(s_cur[rows, :] - m_new)
            l_new = l_new + fold8(p, jnp.add)
            p_cur[rows, :] = p.astype(BF16)
            if p_prev is not None and c == 1:
                acc_scr[...] = alpha_prev * acc_scr[...] + pv_tile(kt - 1, p_prev)
        return m_new, l_new, mt_next, alpha

    def slc_pair(j, state):
        state = phase(2 * j + 1, sb_scr, sa_scr, pb_scr, pa_scr, state)
        return phase(2 * j + 2, sa_scr, sb_scr, pa_scr, pb_scr, state)

    init = (jnp.full((1, wq), NEG, F32), jnp.zeros((1, wq), F32), jnp.zeros((HEAD_DIM, wq), F32))
    mt0 = jnp.full((8, wq), NEG, F32)
    for c, s_n in score_chunks(0):
        mt0 = jnp.maximum(mt0, fold8(s_n, jnp.maximum))
        sa_scr[c * LANES:(c + 1) * LANES, :] = s_n
    key_d = s0 + lax.broadcasted_iota(jnp.int32, (tq, tq), 0)
    s_d = (_dot(kk_ref[0, pl.ds(pl.multiple_of(s0, tq), tq), :], qr_lo)
           + lanes4(jnp.where(key_d <= t_row, diag_scr[...], NEG)))
    m_d = jnp.max(s_d, axis=0, keepdims=True)
    p_d = jnp.exp2(s_d - m_d)
    acc_scr[...] = _dot(v_rows(qb * (tq // LANES), tq // LANES, 0), p_d.astype(BF16))
    state = (m_d, fold8(p_d, jnp.add), mt0, jnp.ones((1, wq), F32))
    state = phase(0, sa_scr, sb_scr, pa_scr, None, state)
    n_tiles = (s0 + tq + KT - 1) // KT
    pairs = n_tiles // 2
    _, l8_s, _, alpha_last = lax.fori_loop(0, pairs, slc_pair, state)
    acc_s = alpha_last * acc_scr[...] + pv_tile(2 * pairs, pa_scr)
    l_s = jnp.sum(l8_s, axis=0, keepdims=True)

    wkeys = WIN + tq
    k0 = pl.multiple_of(jnp.maximum(s0 - WIN, 0), LANES)
    key = k0 + lax.broadcasted_iota(jnp.int32, (wkeys, tq), 0)
    wbias = jnp.where((key <= t_row) & (key > t_row - WIN), 0.0, NEG)
    s_w = _dot(kk_ref[0, pl.ds(k0, wkeys), :], qr_hi) + lanes4(wbias)
    vt_w = vvt_ref[0, pl.ds(jnp.maximum(qb * (tq // LANES) - WIN // LANES, 0), wkeys // LANES)]
    _, l_w, acc_w = online_update(init, s_w, [vt_w[i, HEAD_DIM:2 * HEAD_DIM, :] for i in range(wkeys // LANES)])

    g = gt_ref[0]
    gate = lambda br: jnp.concatenate([g[h * 3 + br:h * 3 + br + 1, :] for h in range(nh)], axis=1)
    o = gate(0) * o_cmp + gate(1) * (acc_s * (1.0 / l_s)) + gate(2) * (acc_w * (1.0 / l_w))
    o_rows = jnp.concatenate([o[:, h * tq:(h + 1) * tq] for h in range(nh)], axis=0)
    yd_ref[0] = o_rows.T.astype(BF16)


def _nsa(qt, qrt, gt, kk, ksx, vvt, kcmp, vcmpt, ovt):
    b, _, s = qt.shape
    n_slc = s // SLC_BLOCK
    m = kcmp.shape[1]
    return pl.pallas_call(
        _nsa_kernel,
        grid=(b, s // TQ),
        in_specs=[
            pl.BlockSpec((1, 256, TQ), lambda bi, qi: (bi, 0, qi)),
            pl.BlockSpec((1, 256, TQ), lambda bi, qi: (bi, 0, qi)),
            pl.BlockSpec((1, 16, TQ), lambda bi, qi: (bi, 0, qi)),
            pl.BlockSpec((1, s, LANES), lambda bi, qi: (bi, 0, 0)),
            pl.BlockSpec((1, s, LANES), lambda bi, qi: (bi, 0, 0)),
            pl.BlockSpec((1, s // LANES, LANES, LANES), lambda bi, qi: (bi, 0, 0, 0)),
            pl.BlockSpec((1, m, LANES), lambda bi, qi: (bi, 0, 0)),
            pl.BlockSpec((1, HEAD_DIM, m), lambda bi, qi: (bi, 0, 0)),
            pl.BlockSpec((n_slc, m), lambda bi, qi: (0, 0)),
        ],
        out_specs=pl.BlockSpec((1, TQ, 256), lambda bi, qi: (bi, qi, 0)),
        out_shape=jax.ShapeDtypeStruct((b, s, 256), BF16),
        scratch_shapes=[pltpu.VMEM((n_slc, TQ), F32),
                        pltpu.VMEM((KT, NSA_HEADS * TQ), F32), pltpu.VMEM((KT, NSA_HEADS * TQ), F32),
                        pltpu.VMEM((KT, NSA_HEADS * TQ), BF16), pltpu.VMEM((KT, NSA_HEADS * TQ), BF16),
                        pltpu.VMEM((TQ, TQ), F32), pltpu.VMEM((HEAD_DIM, NSA_HEADS * TQ), F32)],
        compiler_params=pltpu.CompilerParams(dimension_semantics=("arbitrary", "arbitrary"),
                                             vmem_limit_bytes=VMEM_LIMIT),
        name="sparse_attention",
    )(qt, qrt, gt, kk, ksx, vvt, kcmp, vcmpt, ovt)


def _outproj_kernel(x_ref, yabc_ref, yd_ref, wo_ref, g_ref, b_ref, rw_ref, rb_ref, x1_ref, gates_ref):
    mix = _dot(yabc_ref[...], wo_ref[0:768, :]) + _dot(yd_ref[...], wo_ref[768:1024, :])
    x1 = _layer_norm(DN_ALPHA * x_ref[...] + mix, g_ref[...], b_ref[...])
    x1_ref[...] = x1

    ts = x1.shape[0]
    aff = _sigmoid(_dot(x1.astype(BF16), rw_ref[...])).T[0:N_EXPERTS, :]
    biased = aff + rb_ref[...]
    row = lambda a, i: a[i:i + 1, :]

    gscores = []
    for gi in range(N_GROUPS):
        a, b, c, d = (row(biased, gi * EXPERTS_PER_GROUP + i) for i in range(EXPERTS_PER_GROUP))
        hi1, lo1, hi2, lo2 = jnp.maximum(a, b), jnp.minimum(a, b), jnp.maximum(c, d), jnp.minimum(c, d)
        gscores.append(jnp.maximum(hi1, hi2) + jnp.maximum(jnp.minimum(hi1, hi2), jnp.maximum(lo1, lo2)))
    best, gsel = gscores[0], jnp.zeros((1, ts), jnp.int32)
    for gi in range(1, N_GROUPS):
        better = gscores[gi] > best
        gsel = jnp.where(better, gi, gsel)
        best = jnp.where(better, gscores[gi], best)

    eid_i = lax.broadcasted_iota(jnp.int32, (N_EXPERTS, ts), 0)
    eid = eid_i.astype(F32)
    masked = jnp.where(jnp.right_shift(eid_i, 2) == gsel, biased, NEG)
    picks = []
    for _ in range(2):
        top = jnp.max(masked, axis=0, keepdims=True)
        first = jnp.min(jnp.where(masked == top, eid, float(N_EXPERTS)), axis=0, keepdims=True)
        pick = eid == first
        picks.append(pick)
        masked = jnp.where(pick, -jnp.inf, masked)
    chosen = picks[0] | picks[1]
    w_sel = jnp.where(chosen, aff, 0.0)
    gates_t = w_sel / jnp.sum(w_sel, axis=0, keepdims=True)
    grp8 = jnp.where(lax.broadcasted_iota(jnp.int32, (8, ts), 0) == 0, gsel.astype(F32), 0.0)
    gates_ref[...] = jnp.concatenate([gates_t, grp8, jnp.zeros((LANES - N_EXPERTS - 8, ts), F32)], axis=0).T


def _outproj(x2d, yabc, yd, wo, g, b, rw, rb):
    n, d = x2d.shape
    ts = TS_OUT
    const = lambda shape: pl.BlockSpec(shape, lambda i: (0,) * len(shape))
    return pl.pallas_call(
        _outproj_kernel,
        grid=(n // ts,),
        in_specs=[
            pl.BlockSpec((ts, d), lambda i: (i, 0)),
            pl.BlockSpec((ts, 768), lambda i: (i, 0)),
            pl.BlockSpec((ts, 256), lambda i: (i, 0)),
            const((d, d)), const((1, d)), const((1, d)), const((d, LANES)), const((N_EXPERTS, 1)),
        ],
        out_specs=(pl.BlockSpec((ts, d), lambda i: (i, 0)), pl.BlockSpec((ts, LANES), lambda i: (i, 0))),
        out_shape=(jax.ShapeDtypeStruct((n, d), F32), jax.ShapeDtypeStruct((n, LANES), F32)),
        compiler_params=pltpu.CompilerParams(dimension_semantics=("arbitrary",), vmem_limit_bytes=VMEM_LIMIT),
        name="outproj_ln_route",
    )(x2d, yabc, yd, wo, g, b, rw, rb)


def _split3(x):
    hi = x.astype(BF16)
    r1 = x - hi.astype(F32)
    mid = r1.astype(BF16)
    lo = (r1 - mid.astype(F32)).astype(BF16)
    return hi, mid, lo


def _dispatch_kernel(x1_ref, gates_ref, tri_ref, xs_ref, gs_ref, post_ref, cnt_ref):
    tm = x1_ref.shape[0]
    gates = gates_ref[...]
    gsel = gates.T[GROUP_LANE:GROUP_LANE + 1, :]
    onehot = jnp.where(lax.broadcasted_iota(jnp.int32, (8, tm), 0).astype(F32) == gsel, 1.0, 0.0)
    n = jnp.sum(onehot, axis=1, keepdims=True)
    n_al = jnp.floor((n + (ROW_ALIGN - 1)) * (1.0 / ROW_ALIGN)) * ROW_ALIGN
    starts = [jnp.zeros((1, 1), F32)]
    for gi in range(1, N_GROUPS):
        starts.append(starts[-1] + n_al[gi - 1:gi, :])
    start = jnp.concatenate(starts + [jnp.zeros((8 - N_GROUPS, 1), F32)], axis=0)
    before = _dot(onehot.astype(BF16), tri_ref[...])
    pos = jnp.sum(onehot * (start + before), axis=0, keepdims=True)
    perm = jnp.where(lax.broadcasted_iota(jnp.int32, (SORT_ROWS, tm), 0).astype(F32) == pos, 1.0, 0.0).astype(BF16)

    xs_ref[0, 0:SORT_ROWS, :] = _dot(perm, x1_ref[...].astype(BF16)).astype(BF16)
    xs_ref[0, SORT_ROWS:SORT_BUF, :] = jnp.zeros((SORT_BUF - SORT_ROWS, xs_ref.shape[2]), BF16)
    g_hi, g_mid, g_lo = _split3(gates)
    gs_ref[0, 0:SORT_ROWS, :] = _dot(perm, g_hi) + _dot(perm, g_mid) + _dot(perm, g_lo)
    gs_ref[0, SORT_ROWS:SORT_BUF, :] = jnp.zeros((SORT_BUF - SORT_ROWS, LANES), F32)
    pos8 = jnp.where(lax.broadcasted_iota(jnp.int32, (8, tm), 0) == 0, pos, 0.0)
    post_ref[...] = jnp.concatenate([pos8, jnp.zeros((LANES - 8, tm), F32)], axis=0).T
    cnt_ref[0] = jnp.broadcast_to(n, (8, LANES))


def _dispatch(x1, gates, tri):
    n, d = x1.shape
    tm = TM_MOE
    nt = n // tm
    return pl.pallas_call(
        _dispatch_kernel,
        grid=(nt,),
        in_specs=[
            pl.BlockSpec((tm, d), lambda i: (i, 0)),
            pl.BlockSpec((tm, LANES), lambda i: (i, 0)),
            pl.BlockSpec((tm, tm), lambda i: (0, 0)),
        ],
        out_specs=(
            pl.BlockSpec((1, SORT_BUF, d), lambda i: (i, 0, 0)),
            pl.BlockSpec((1, SORT_BUF, LANES), lambda i: (i, 0, 0)),
            pl.BlockSpec((tm, LANES), lambda i: (i, 0)),
            pl.BlockSpec((1, 8, LANES), lambda i: (i, 0, 0)),
        ),
        out_shape=(
            jax.ShapeDtypeStruct((nt, SORT_BUF, d), BF16),
            jax.ShapeDtypeStruct((nt, SORT_BUF, LANES), F32),
            jax.ShapeDtypeStruct((n, LANES), F32),
            jax.ShapeDtypeStruct((nt, 8, LANES), F32),
        ),
        compiler_params=pltpu.CompilerParams(dimension_semantics=("arbitrary",), vmem_limit_bytes=VMEM_LIMIT),
        name="moe_dispatch",
    )(x1, gates, tri)


def _moe_kernel(start_ref, trips_ref, xs_ref, gs_ref, wg_ref, wu_ref, wd_ref, acc_ref):
    i, e = pl.program_id(0), pl.program_id(1)

    @pl.when(e == 0)
    def _():
        acc_ref[...] = jnp.zeros(acc_ref.shape, F32)

    grp = lax.shift_right_logical(e, 2)
    for tile in range(TILES_PER_STEP):
        slot = (i * TILES_PER_STEP + tile) * N_GROUPS + grp
        row0 = start_ref[slot]

        def window(w, carry, tile=tile, row0=row0):
            rows = pl.ds(pl.multiple_of(row0 + w * WIN_ROWS, ROW_ALIGN), WIN_ROWS)
            xw = xs_ref[tile, rows, :]
            hg = _dot(xw, wg_ref[0])
            h = hg * _sigmoid(hg) * _dot(xw, wu_ref[0])
            y = _dot(h.astype(BF16), wd_ref[0])
            gw = gs_ref[tile, rows, :]
            lane = lax.broadcasted_iota(jnp.int32, gw.shape, 1)
            gcol = jnp.sum(jnp.where(lane == e, gw, 0.0), axis=1, keepdims=True)
            acc_ref[tile, rows, :] += y * gcol
            return carry

        lax.fori_loop(0, trips_ref[slot], window, 0)


def _moe(start, trips, xs, gs, wg, wu, wd):
    nt, _, d = xs.shape
    ne, _, de = wg.shape
    tps = TILES_PER_STEP
    grid_spec = pltpu.PrefetchScalarGridSpec(
        num_scalar_prefetch=2,
        grid=(nt // tps, ne),
        in_specs=[
            pl.BlockSpec((tps, SORT_BUF, d), lambda i, e, *_: (i, 0, 0)),
            pl.BlockSpec((tps, SORT_BUF, LANES), lambda i, e, *_: (i, 0, 0)),
            pl.BlockSpec((1, d, de), lambda i, e, *_: (e, 0, 0)),
            pl.BlockSpec((1, d, de), lambda i, e, *_: (e, 0, 0)),
            pl.BlockSpec((1, de, d), lambda i, e, *_: (e, 0, 0)),
        ],
        out_specs=pl.BlockSpec((tps, SORT_BUF, d), lambda i, e, *_: (i, 0, 0)),
    )
    return pl.pallas_call(
        _moe_kernel,
        grid_spec=grid_spec,
        out_shape=jax.ShapeDtypeStruct((nt, SORT_BUF, d), F32),
        compiler_params=pltpu.CompilerParams(dimension_semantics=("arbitrary", "arbitrary"),
                                             vmem_limit_bytes=VMEM_LIMIT),
        name="moe_experts",
    )(start, trips, xs, gs, wg, wu, wd)


def _combine_kernel(x1_ref, acc_ref, post_ref, g_ref, b_ref, out_ref):
    tm = x1_ref.shape[0]
    acc = acc_ref[0]
    hi = acc.astype(BF16)
    lo = (acc - hi.astype(F32)).astype(BF16)
    pos = post_ref[...][:, 0:1]
    unperm = jnp.where(lax.broadcasted_iota(jnp.int32, (tm, SORT_ROWS), 1).astype(F32) == pos, 1.0, 0.0).astype(BF16)
    moe = _dot(unperm, hi) + _dot(unperm, lo)
    out_ref[...] = _layer_norm(DN_ALPHA * x1_ref[...] + moe, g_ref[...], b_ref[...])


def _combine(x1, acc, post, g, b):
    n, d = x1.shape
    tm = TM_MOE
    const = lambda shape: pl.BlockSpec(shape, lambda i: (0,) * len(shape))
    return pl.pallas_call(
        _combine_kernel,
        grid=(n // tm,),
        in_specs=[
            pl.BlockSpec((tm, d), lambda i: (i, 0)),
            pl.BlockSpec((1, SORT_ROWS, d), lambda i: (i, 0, 0)),
            pl.BlockSpec((tm, LANES), lambda i: (i, 0)),
            const((1, d)), const((1, d)),
        ],
        out_specs=pl.BlockSpec((tm, d), lambda i: (i, 0)),
        out_shape=jax.ShapeDtypeStruct((n, d), F32),
        compiler_params=pltpu.CompilerParams(dimension_semantics=("arbitrary",), vmem_limit_bytes=VMEM_LIMIT),
        name="moe_combine_ln",
    )(x1, acc, post, g, b)


def _group_windows(cnt):
    n = cnt[:, 0:N_GROUPS, 0].astype(jnp.int32)
    n_al = (n + (ROW_ALIGN - 1)) // ROW_ALIGN * ROW_ALIGN
    start = jnp.cumsum(n_al, axis=1) - n_al
    trips = (n + (WIN_ROWS - 1)) // WIN_ROWS
    return start.reshape(-1), trips.reshape(-1)


def _permute_w_in(w):
    sizes = (256, 256, 256, 256, 256, 256, 256, 64, 64, 64, 64, 64, 64, 12)
    offs = np.concatenate([[0], np.cumsum(sizes)])
    part = lambda i: w[:, int(offs[i]):int(offs[i + 1])]
    a_b, a_c, a_h, b_val, b_gate, c_p, q, kc, vc, ks, vs, kw, vw, g = (part(i) for i in range(14))
    pad = jnp.zeros((w.shape[0], LANES - g.shape[1]), w.dtype)
    return jnp.concatenate([a_b, a_c, a_h, b_val, b_gate, c_p, q, kc, vc, ks, kw, vs, vw, g, pad], axis=1).astype(BF16)


def _rope_tables(s):
    inv = 1.0 / (ROPE_THETA ** (jnp.arange(0, HEAD_DIM, 2, dtype=F32) / HEAD_DIM))
    ang = jnp.arange(s, dtype=F32)[:, None] * inv[None, :]
    ang = jnp.concatenate([ang, ang], -1)
    sign = jnp.concatenate([-jnp.ones((HEAD_DIM // 2,), F32), jnp.ones((HEAD_DIM // 2,), F32)])
    return jnp.tile(jnp.cos(ang), (1, NSA_HEADS)), jnp.tile(jnp.sin(ang) * sign[None, :], (1, NSA_HEADS))


def _overlap_t(s, m):
    n_cmp = (s - CMP_BLOCK) // CMP_STRIDE + 1
    cmp_start = np.arange(m) * CMP_STRIDE
    slc_start = np.arange(s // SLC_BLOCK) * SLC_BLOCK
    ov = (cmp_start[None, :] < slc_start[:, None] + SLC_BLOCK) & (cmp_start[None, :] + CMP_BLOCK > slc_start[:, None])
    ov = ov & (np.arange(m)[None, :] < n_cmp)
    return jnp.asarray(ov, BF16)


def _block_diag(w):
    g, c, _ = w.shape
    out = jnp.zeros((g * c, g * c), w.dtype)
    for i in range(g):
        out = out.at[i * c:(i + 1) * c, i * c:(i + 1) * c].set(w[i])
    return out


def kernel(x, w_in, conv_a_w, conv_b_w, conv_b_b, cf_ln_g, cf_ln_b, pool_w, pool_scale, cmp_pe_k, cmp_pe_v, cmp_k_w1, cmp_k_w2, cmp_v_w1, cmp_v_w2, w_out, ln1_g, ln1_b, ln2_g, ln2_b, router_w, router_bias, moe_w_gate, moe_w_up, moe_w_down):
    b, s, d = x.shape
    n = b * s
    m = s // CMP_STRIDE
    tok_per_row = CMP_STRIDE
    cos4, sin4 = _rope_tables(s)
    ovt = _overlap_t(s, m)
    rw = jnp.concatenate([router_w, jnp.zeros((d, LANES - N_EXPERTS), router_w.dtype)], axis=1).astype(BF16)
    rb = router_bias.reshape(N_EXPERTS, 1).astype(F32)
    pad_w2 = lambda w: jnp.concatenate([w, jnp.zeros((CMP_HIDDEN, LANES - HEAD_DIM), w.dtype)], axis=1).astype(BF16)
    row = lambda v: v.reshape(1, -1)
    tri = jnp.asarray(np.triu(np.ones((TM_MOE, TM_MOE), np.float32), k=1), BF16)

    for l in range(DEPTH):
        yabc, qt, qrt, kcvc, kk, ksx, vvt, gt = _inproj(
            x, _permute_w_in(w_in[l]), cos4, sin4, conv_a_w[l], conv_b_w[l], row(conv_b_b[l]),
            row(cf_ln_g[l]), row(cf_ln_b[l]), _block_diag(pool_w[l]).astype(BF16), row(pool_scale[l]))
        kc16 = kcvc[:, :, 0:HEAD_DIM].reshape(b, m, tok_per_row * HEAD_DIM)
        vc16 = kcvc[:, :, HEAD_DIM:2 * HEAD_DIM].reshape(b, m, tok_per_row * HEAD_DIM)
        kcmp, vcmpt = _compress(
            kc16, vc16, cmp_pe_k[l].reshape(2, tok_per_row * HEAD_DIM), cmp_pe_v[l].reshape(2, tok_per_row * HEAD_DIM),
            cmp_k_w1[l].astype(BF16), pad_w2(cmp_k_w2[l]), cmp_v_w1[l].astype(BF16), pad_w2(cmp_v_w2[l]))
        yd = _nsa(qt, qrt, gt, kk, ksx, vvt, kcmp, vcmpt, ovt)
        x1, gates = _outproj(x.reshape(n, d), yabc.reshape(n, 768), yd.reshape(n, 256), w_out[l].astype(BF16),
                             row(ln1_g[l]), row(ln1_b[l]), rw, rb)
        xs, gs, post, cnt = _dispatch(x1, gates, tri)
        start, trips = _group_windows(cnt)
        acc = _moe(start, trips, xs, gs, moe_w_gate[l].astype(BF16), moe_w_up[l].astype(BF16),
                   moe_w_down[l].astype(BF16))
        x = _combine(x1, acc, post, row(ln2_g[l]), row(ln2_b[l])).reshape(b, s, d)
    return x
```

```python
import functools

import jax
import jax.numpy as jnp
import numpy as np
from jax import lax
from jax.experimental import pallas as pl
from jax.experimental.pallas import tpu as pltpu

F32 = jnp.float32
BF16 = jnp.bfloat16

D_MODEL = 1024
DEPTH = 2
GROUP_W = 256
HEAD_DIM = 64
SC_KERNEL = 3
CF_KERNEL = 31
POOL_WINDOWS = (2, 4, 8, 16)
POOL_GW = 64
NSA_HEADS = 4
CMP_BLOCK = 32
CMP_STRIDE = 16
CMP_HIDDEN = 256
SLC_BLOCK = 64
SLC_TOPN = 16
WIN = 512
N_EXPERTS = 16
N_GROUPS = 4
EXPERTS_PER_GROUP = 4
D_EXPERT = 512
DN_ALPHA = (2 * DEPTH) ** 0.25
LN_EPS = 1e-5
NEG = -1e30
FORCE = 1e4
ROPE_THETA = 10000.0
QK_SCALE = HEAD_DIM ** -0.5
LOG2E = 1.4426950408889634

LANES = 128
OFF_A, OFF_B, OFF_C, OFF_Q, OFF_KCVC, OFF_KK, OFF_VV, OFF_G, IN_PAD = 0, 768, 1280, 1536, 1792, 1920, 2048, 2176, 2304

HALO_A = 8
HALO_B = 32
HALO_C = 16

TS_IN = 512
TQ = 256
KT = 512
TS_OUT = 512
TM_MOE = 1024
ROW_ALIGN = 16
WIN_ROWS = 288
SORT_ROWS = TM_MOE + LANES
SORT_BUF = SORT_ROWS + 2 * LANES
GROUP_LANE = 16
TILES_PER_STEP = 2
assert SORT_ROWS >= TM_MOE + N_GROUPS * (ROW_ALIGN - 1)
assert SORT_BUF >= TM_MOE + N_GROUPS * (ROW_ALIGN - 1) + WIN_ROWS
VMEM_LIMIT = 56 * 1024 * 1024


def _sigmoid(x):
    return jax.nn.sigmoid(x)


def _layer_norm(h, g, b):
    mu = jnp.mean(h, axis=-1, keepdims=True)
    d = h - mu
    var = jnp.mean(d * d, axis=-1, keepdims=True)
    return d * lax.rsqrt(var + LN_EPS) * g + b


def _dot(a, b):
    return jnp.dot(a, b, preferred_element_type=F32)


def _inproj_kernel(x_ref, w_ref, cos_ref, sin_ref, cva_ref, cvb_ref, cvbb_ref, lng_ref, lnb_ref,
                   poolw_ref, pools_ref,
                   yabc_ref, qt_ref, qrt_ref, kcvc_ref, kk_ref, ksx_ref, vvt_ref, gt_ref,
                   ch_ext, u_ext, p_ext, u_sh):
    j = pl.program_id(1)
    ts = x_ref.shape[1]
    xb = x_ref[0].astype(BF16)

    def proj(lo, hi):
        return _dot(xb, w_ref[:, lo:hi])

    def carry_halo(ext, halo):
        @pl.when(j == 0)
        def _():
            ext[0:halo, :] = jnp.zeros((halo, ext.shape[1]), F32)

        @pl.when(j > 0)
        def _():
            ext[0:halo, :] = ext[ts:ts + halo, :]

    pa = proj(OFF_A, OFF_B)
    a_b, ch = pa[:, 0:256], pa[:, 256:512] * pa[:, 512:768]
    carry_halo(ch_ext, HALO_A)
    ch_ext[HALO_A:HALO_A + ts, :] = ch
    conv = cva_ref[2:3, :] * ch
    for k in range(SC_KERNEL - 1):
        off = HALO_A - (SC_KERNEL - 1) + k
        conv = conv + cva_ref[k:k + 1, :] * ch_ext[off:off + ts, :]
    yabc_ref[0, :, 0:256] = (a_b * conv).astype(BF16)

    pb = proj(OFF_B, OFF_C)
    u = pb[:, 0:256] * _sigmoid(pb[:, 256:512])
    carry_halo(u_ext, HALO_B)
    u_ext[HALO_B:HALO_B + ts, :] = u
    span = u_sh.shape[1]
    for r in range(1, 8):
        u_sh[r - 1, :, :] = u_ext[r:r + span, :]
    def mixer_c():
        pc = proj(OFF_C, OFF_Q)
        carry_halo(p_ext, HALO_C)
        p_ext[HALO_C:HALO_C + ts, :] = pc
        pe = p_ext[...]
        s2 = pe + pltpu.roll(pe, 1, 0)
        s4 = s2 + pltpu.roll(s2, 2, 0)
        s8 = s4 + pltpu.roll(s4, 4, 0)
        s16 = s8 + pltpu.roll(s8, 8, 0)
        lane = lax.broadcasted_iota(jnp.int32, (ts, 256), 1)
        grp = jnp.right_shift(lane, 6)
        wsum = jnp.where(grp == 0, s2[HALO_C:], jnp.where(grp == 1, s4[HALO_C:], jnp.where(grp == 2, s8[HALO_C:], s16[HALO_C:])))
        win = jnp.where(grp == 0, 2, jnp.where(grp == 1, 4, jnp.where(grp == 2, 8, 16)))
        t1 = j * ts + lax.broadcasted_iota(jnp.int32, (ts, 256), 0) + 1
        cnt = jnp.minimum(win, t1).astype(F32)
        dlt = wsum / cnt - pc
        yabc_ref[0, :, 512:768] = (_dot(dlt.astype(BF16), poolw_ref[...]) * pools_ref[...]).astype(BF16)

    def queries():
        cos4, sin4 = cos_ref[...], sin_ref[...]
        q = proj(OFF_Q, OFF_KCVC)
        lane_q = jnp.bitwise_and(lax.broadcasted_iota(jnp.int32, (ts, 256), 1), HEAD_DIM - 1)
        q_sw = jnp.where(lane_q < HEAD_DIM // 2, pltpu.roll(q, 256 - HEAD_DIM // 2, 1), pltpu.roll(q, HEAD_DIM // 2, 1))
        qr = q * cos4 + q_sw * sin4
        qt_ref[0] = (q * (QK_SCALE * LOG2E)).T.astype(BF16)
        qrt_ref[0] = (qr * (QK_SCALE * LOG2E)).T.astype(BF16)

    def compress_inputs():
        kcvc_ref[0] = proj(OFF_KCVC, OFF_KK)

    def keys():
        k2 = proj(OFF_KK, OFF_VV)
        lane_k = jnp.bitwise_and(lax.broadcasted_iota(jnp.int32, (ts, LANES), 1), HEAD_DIM - 1)
        k_sw = jnp.where(lane_k < HEAD_DIM // 2, pltpu.roll(k2, LANES - HEAD_DIM // 2, 1), pltpu.roll(k2, HEAD_DIM // 2, 1))
        k_rot = k2 * cos_ref[:, 0:LANES] + k_sw * sin_ref[:, 0:LANES]
        kk_ref[0] = k_rot.astype(BF16)
        lane_i = lax.broadcasted_iota(jnp.int32, (ts, LANES), 1)
        blk_in_tile = jnp.bitwise_and(jnp.right_shift(j * ts + lax.broadcasted_iota(jnp.int32, (ts, LANES), 0), 6),
                                      KT // SLC_BLOCK - 1)
        ksx_ref[0] = jnp.where(lane_i < HEAD_DIM, k_rot, jnp.where(lane_i - HEAD_DIM == blk_in_tile, 1.0, 0.0)).astype(BF16)

    def values():
        v2t = proj(OFF_VV, OFF_G).T
        for c in range(ts // LANES):
            vvt_ref[0, c] = v2t[:, c * LANES:(c + 1) * LANES].astype(BF16)

    def branch_gates():
        gt_ref[0] = _sigmoid(proj(OFF_G, IN_PAD)).T[0:16, :]

    interleaved = [mixer_c, queries, compress_inputs, keys, values, branch_gates]
    rows = 64
    assert ts // rows >= len(interleaved)
    for c in range(ts // rows):
        acc = jnp.zeros((rows, 256), F32) + cvbb_ref[...]
        for k in range(CF_KERNEL):
            off = HALO_B - (CF_KERNEL - 1) + k
            base = off // 8 * 8 + c * rows
            src = u_ext if off % 8 == 0 else u_sh.at[off % 8 - 1]
            acc = acc + cvb_ref[k:k + 1, :] * src[base:base + rows, :]
        v = _layer_norm(acc, lng_ref[...], lnb_ref[...])
        yabc_ref[0, c * rows:(c + 1) * rows, 256:512] = (v * _sigmoid(v)).astype(BF16)
        if c < len(interleaved):
            interleaved[c]()


def _inproj(x, w_in_p, cos4, sin4, cva, cvb, cvbb, lng, lnb, poolw, pools):
    b, s, d = x.shape
    ts = TS_IN
    grid = (b, s // ts)
    const = lambda shape: pl.BlockSpec(shape, lambda bi, ji: (0,) * len(shape))
    out_shape = (
        jax.ShapeDtypeStruct((b, s, 768), BF16),
        jax.ShapeDtypeStruct((b, 256, s), BF16),
        jax.ShapeDtypeStruct((b, 256, s), BF16),
        jax.ShapeDtypeStruct((b, s, LANES), F32),
        jax.ShapeDtypeStruct((b, s, LANES), BF16),
        jax.ShapeDtypeStruct((b, s, LANES), BF16),
        jax.ShapeDtypeStruct((b, s // LANES, LANES, LANES), BF16),
        jax.ShapeDtypeStruct((b, 16, s), F32),
    )
    return pl.pallas_call(
        _inproj_kernel,
        grid=grid,
        in_specs=[
            pl.BlockSpec((1, ts, d), lambda bi, ji: (bi, ji, 0)),
            const((d, IN_PAD)),
            pl.BlockSpec((ts, 256), lambda bi, ji: (ji, 0)),
            pl.BlockSpec((ts, 256), lambda bi, ji: (ji, 0)),
            const((SC_KERNEL, 256)), const((CF_KERNEL, 256)), const((1, 256)), const((1, 256)), const((1, 256)),
            const((256, 256)), const((1, 256)),
        ],
        out_specs=(
            pl.BlockSpec((1, ts, 768), lambda bi, ji: (bi, ji, 0)),
            pl.BlockSpec((1, 256, ts), lambda bi, ji: (bi, 0, ji)),
            pl.BlockSpec((1, 256, ts), lambda bi, ji: (bi, 0, ji)),
            pl.BlockSpec((1, ts, LANES), lambda bi, ji: (bi, ji, 0)),
            pl.BlockSpec((1, ts, LANES), lambda bi, ji: (bi, ji, 0)),
            pl.BlockSpec((1, ts, LANES), lambda bi, ji: (bi, ji, 0)),
            pl.BlockSpec((1, ts // LANES, LANES, LANES), lambda bi, ji: (bi, ji, 0, 0)),
            pl.BlockSpec((1, 16, ts), lambda bi, ji: (bi, 0, ji)),
        ),
        out_shape=out_shape,
        scratch_shapes=[
            pltpu.VMEM((ts + HALO_A, 256), F32),
            pltpu.VMEM((ts + HALO_B, 256), F32),
            pltpu.VMEM((ts + HALO_C, 256), F32),
            pltpu.VMEM((7, ts + HALO_B - 8, 256), F32),
        ],
        compiler_params=pltpu.CompilerParams(dimension_semantics=("arbitrary", "arbitrary"),
                                             vmem_limit_bytes=VMEM_LIMIT),
        name="inproj_mixers",
    )(x, w_in_p, cos4, sin4, cva, cvb, cvbb, lng, lnb, poolw, pools)


def _compress_kernel(kc_ref, vc_ref, pek_ref, pev_ref, wk1_ref, wk2_ref, wv1_ref, wv2_ref, kcmp_ref, vcmpt_ref):
    def comp(x_ref, pe_ref, w1_ref, w2_ref):
        x = x_ref[0]
        m = x.shape[0]
        half = x.shape[1]
        first = _dot((x + pe_ref[0:1, :]).astype(BF16), w1_ref[0:half, :])
        second = _dot((x + pe_ref[1:2, :]).astype(BF16), w1_ref[half:2 * half, :])
        hid = first + pltpu.roll(second, m - 1, 0)
        return _dot((hid * _sigmoid(hid)).astype(BF16), w2_ref[...])

    kcmp_ref[0] = comp(kc_ref, pek_ref, wk1_ref, wk2_ref).astype(BF16)
    vcmpt_ref[0] = comp(vc_ref, pev_ref, wv1_ref, wv2_ref).T[0:HEAD_DIM, :].astype(BF16)


def _compress(kc16, vc16, pek, pev, wk1, wk2, wv1, wv2):
    b, m, w = kc16.shape
    const = lambda shape: pl.BlockSpec(shape, lambda bi: (0,) * len(shape))
    return pl.pallas_call(
        _compress_kernel,
        grid=(b,),
        in_specs=[
            pl.BlockSpec((1, m, w), lambda bi: (bi, 0, 0)),
            pl.BlockSpec((1, m, w), lambda bi: (bi, 0, 0)),
            const((2, w)), const((2, w)),
            const((2 * w, CMP_HIDDEN)), const((CMP_HIDDEN, LANES)),
            const((2 * w, CMP_HIDDEN)), const((CMP_HIDDEN, LANES)),
        ],
        out_specs=(
            pl.BlockSpec((1, m, LANES), lambda bi: (bi, 0, 0)),
            pl.BlockSpec((1, HEAD_DIM, m), lambda bi: (bi, 0, 0)),
        ),
        out_shape=(jax.ShapeDtypeStruct((b, m, LANES), BF16), jax.ShapeDtypeStruct((b, HEAD_DIM, m), BF16)),
        compiler_params=pltpu.CompilerParams(dimension_semantics=("arbitrary",), vmem_limit_bytes=VMEM_LIMIT),
        name="compress_kv",
    )(kc16, vc16, pek, pev, wk1, wk2, wv1, wv2)


def _nsa_kernel(qt_ref, qrt_ref, gt_ref, kk_ref, ksx_ref, vvt_ref, kcmp_ref, vcmpt_ref, ovt_ref, yd_ref, sel_scr, sa_scr, sb_scr, pa_scr, pb_scr, diag_scr, acc_scr):
    qb = pl.program_id(1)
    tq = qt_ref.shape[2]
    nh = NSA_HEADS
    wq = nh * tq
    s0 = qb * tq
    n_slc = sel_scr.shape[0]
    n_cmp_pad = kcmp_ref.shape[1]

    def stack_heads(ref):
        x = ref[0]
        return jnp.concatenate([x[h * HEAD_DIM:(h + 1) * HEAD_DIM, :] for h in range(nh)], axis=1)

    zeros = jnp.zeros((HEAD_DIM, wq), BF16)
    q_top, qr_top = stack_heads(qt_ref), stack_heads(qrt_ref)
    q_lo = jnp.concatenate([q_top, zeros], axis=0)
    qr_lo = jnp.concatenate([qr_top, zeros], axis=0)
    qr_hi = jnp.concatenate([zeros, qr_top], axis=0)
    t_row = s0 + lax.broadcasted_iota(jnp.int32, (1, tq), 1)
    lanes4 = lambda x: jnp.concatenate([x] * nh, axis=1)
    t_all = lanes4(t_row)

    size_classes = 4
    class_rows = n_cmp_pad // size_classes
    assert CMP_STRIDE * class_rows >= tq >= 2 * SLC_BLOCK

    def compress_and_select(nc):
        nb = nc // (SLC_BLOCK // CMP_STRIDE)
        cmp_end = lax.broadcasted_iota(jnp.int32, (nc, tq), 0) * CMP_STRIDE + (CMP_BLOCK - 1)
        sc = _dot(kcmp_ref[0, 0:nc, :], q_lo) + lanes4(jnp.where(cmp_end <= t_row, 0.0, NEG))
        mx = jnp.max(sc, axis=0, keepdims=True)
        ex = jnp.exp2(sc - mx)
        den = jnp.sum(ex, axis=0, keepdims=True)
        p_cmp = ex * jnp.where(t_all >= CMP_BLOCK - 1, 1.0 / den, 0.0)
        o_cmp = _dot(vcmpt_ref[0, :, 0:nc], p_cmp.astype(BF16))
        p_heads = p_cmp[:, 0:tq]
        for h in range(1, nh):
            p_heads = p_heads + p_cmp[:, h * tq:(h + 1) * tq]
        imp = _dot(ovt_ref[0:nb, 0:nc], p_heads.astype(BF16))

        blk_i = lax.broadcasted_iota(jnp.int32, (nb, tq), 0)
        blk = blk_i.astype(F32)
        cur = jnp.right_shift(t_row, 6)
        forced = (blk_i == 0) | (blk_i == cur) | (blk_i == cur - 1)
        valid = blk_i * SLC_BLOCK <= t_row
        sel = jnp.where(forced, 1.0, 0.0)
        n_forced = jnp.sum(sel, axis=0, keepdims=True)
        imp = jnp.where(valid, jnp.where(forced, -jnp.inf, imp), NEG)
        min_forced = 3 if nc > class_rows else 1
        for r in range(min(SLC_TOPN - min_forced, nb)):
            top = jnp.max(imp, axis=0, keepdims=True)
            first = jnp.min(jnp.where(imp == top, blk, float(nb)), axis=0, keepdims=True)
            pick = (blk == first) & (n_forced + r < SLC_TOPN)
            sel = jnp.where(pick, 1.0, sel)
            imp = jnp.where(pick, -jnp.inf, imp)
        bias = jnp.where(valid & (sel > 0.0), 0.0, NEG)
        own0 = qb * (tq // SLC_BLOCK)
        own = (blk_i >= own0) & (blk_i < own0 + tq // SLC_BLOCK)
        sel_scr[0:nb, :] = jnp.where(own, NEG, bias)
        if nb < n_slc:
            sel_scr[nb:n_slc, :] = jnp.full((n_slc - nb, tq), NEG, F32)
        key_blk = own0 + jnp.right_shift(lax.broadcasted_iota(jnp.int32, (tq, nb), 0), 6)
        expand = jnp.where(lax.broadcasted_iota(jnp.int32, (tq, nb), 1) == key_blk, 1.0, 0.0).astype(BF16)
        diag_scr[...] = _dot(expand, bias.astype(BF16))
        return o_cmp

    visible = (s0 + tq) // CMP_STRIDE
    o_cmp = lax.switch((visible - 1) // class_rows,
                       [functools.partial(compress_and_select, (i + 1) * class_rows) for i in range(size_classes)])

    blocks_per_tile = KT // SLC_BLOCK
    sub = KT // LANES

    def online_update(state, s, v_rows):
        m_i, l_i, acc = state
        m_new = jnp.maximum(m_i, jnp.max(s, axis=0, keepdims=True))
        alpha = jnp.exp2(m_i - m_new)
        p = jnp.exp2(s - m_new)
        l_new = alpha * l_i + jnp.sum(p, axis=0, keepdims=True)
        pb = p.astype(BF16)
        pv = _dot(v_rows[0], pb[0:LANES, :])
        for i in range(1, len(v_rows)):
            pv = pv + _dot(v_rows[i], pb[i * LANES:(i + 1) * LANES, :])
        return m_new, l_new, alpha * acc + pv

    last_tile = kk_ref.shape[1] // KT - 1

    def score_chunks(kt):
        kd = jnp.minimum(kt, last_tile)
        bias8 = sel_scr[pl.ds(pl.multiple_of(kd * blocks_per_tile, blocks_per_tile), blocks_per_tile), :]
        bias8 = jnp.where(kt <= last_tile, bias8, NEG)
        bias_rows = jnp.concatenate([lanes4(bias8), jnp.zeros((16 - blocks_per_tile, wq), F32)], axis=0).astype(BF16)
        q_bias = jnp.concatenate([qr_top, bias_rows, jnp.zeros((HEAD_DIM - 16, wq), BF16)], axis=0)
        for c in range(sub):
            kc0 = pl.multiple_of(kd * KT + c * LANES, LANES)
            yield c, _dot(ksx_ref[0, pl.ds(kc0, LANES), :], q_bias)

    def fold8(x, op):
        out = x[0:8, :]
        for r in range(1, x.shape[0] // 8):
            out = op(out, x[8 * r:8 * r + 8, :])
        return out

    def v_rows(first, count, lo):
        vt = vvt_ref[0, pl.ds(first, count)]
        return jnp.concatenate([vt[c, lo:lo + HEAD_DIM, :] for c in range(count)], axis=1)

    def pv_tile(kt, p_ref):
        return _dot(v_rows(pl.multiple_of(jnp.minimum(kt, last_tile) * sub, sub), sub, 0), p_ref[...])

    def phase(kt, s_cur, s_next, p_cur, p_prev, state):
        m_i, l_i, mt_cur, alpha_prev = state
        m_new = jnp.maximum(m_i, jnp.max(mt_cur, axis=0, keepdims=True))
        alpha = jnp.exp2(m_i - m_new)
        mt_next = jnp.full((8, wq), NEG, F32)
        l_new = alpha * l_i
        for c, s_n in score_chunks(kt + 1):
            rows = slice(c * LANES, (c + 1) * LANES)
            mt_next = jnp.maximum(mt_next, fold8(s_n, jnp.maximum))
            s_next[rows, :] = s_n
            p = jnp.exp2(s_cur[rows, :] - m_new)
            l_new = l_new + fold8(p, jnp.add)
            p_cur[rows, :] = p.astype(BF16)
            if p_prev is not None and c == 1:
                acc_scr[...] = alpha_prev * acc_scr[...] + pv_tile(kt - 1, p_prev)
        return m_new, l_new, mt_next, alpha

    def slc_pair(j, state):
        state = phase(2 * j + 1, sb_scr, sa_scr, pb_scr, pa_scr, state)
        return phase(2 * j + 2, sa_scr, sb_scr, pa_scr, pb_scr, state)

    init = (jnp.full((1, wq), NEG, F32), jnp.zeros((1, wq), F32), jnp.zeros((HEAD_DIM, wq), F32))
    mt0 = jnp.full((8, wq), NEG, F32)
    for c, s_n in score_chunks(0):
        mt0 = jnp.maximum(mt0, fold8(s_n, jnp.maximum))
        sa_scr[c * LANES:(c + 1) * LANES, :] = s_n
    key_d = s0 + lax.broadcasted_iota(jnp.int32, (tq, tq), 0)
    s_d = (_dot(kk_ref[0, pl.ds(pl.multiple_of(s0, tq), tq), :], qr_lo)
           + lanes4(jnp.where(key_d <= t_row, diag_scr[...], NEG)))
    m_d = jnp.max(s_d, axis=0, keepdims=True)
    p_d = jnp.exp2(s_d - m_d)
    acc_scr[...] = _dot(v_rows(qb * (tq // LANES), tq // LANES, 0), p_d.astype(BF16))
    state = (m_d, fold8(p_d, jnp.add), mt0, jnp.ones((1, wq), F32))
    state = phase(0, sa_scr, sb_scr, pa_scr, None, state)
    n_tiles = (s0 + tq + KT - 1) // KT
    pairs = n_tiles // 2
    _, l8_s, _, alpha_last = lax.fori_loop(0, pairs, slc_pair, state)
    acc_s = alpha_last * acc_scr[...] + pv_tile(2 * pairs, pa_scr)
    l_s = jnp.sum(l8_s, axis=0, keepdims=True)

    wkeys = WIN + tq
    k0 = pl.multiple_of(jnp.maximum(s0 - WIN, 0), LANES)
    key = k0 + lax.broadcasted_iota(jnp.int32, (wkeys, tq), 0)
    wbias = jnp.where((key <= t_row) & (key > t_row - WIN), 0.0, NEG)
    s_w = _dot(kk_ref[0, pl.ds(k0, wkeys), :], qr_hi) + lanes4(wbias)
    vt_w = vvt_ref[0, pl.ds(jnp.maximum(qb * (tq // LANES) - WIN // LANES, 0), wkeys // LANES)]
    _, l_w, acc_w = online_update(init, s_w, [vt_w[i, HEAD_DIM:2 * HEAD_DIM, :] for i in range(wkeys // LANES)])

    g = gt_ref[0]
    gate = lambda br: jnp.concatenate([g[h * 3 + br:h * 3 + br + 1, :] for h in range(nh)], axis=1)
    o = gate(0) * o_cmp + gate(1) * (acc_s * (1.0 / l_s)) + gate(2) * (acc_w * (1.0 / l_w))
    o_rows = jnp.concatenate([o[:, h * tq:(h + 1) * tq] for h in range(nh)], axis=0)
    yd_ref[0] = o_rows.T.astype(BF16)


def _nsa(qt, qrt, gt, kk, ksx, vvt, kcmp, vcmpt, ovt):
    b, _, s = qt.shape
    n_slc = s // SLC_BLOCK
    m = kcmp.shape[1]
    return pl.pallas_call(
        _nsa_kernel,
        grid=(b, s // TQ),
        in_specs=[
            pl.BlockSpec((1, 256, TQ), lambda bi, qi: (bi, 0, qi)),
            pl.BlockSpec((1, 256, TQ), lambda bi, qi: (bi, 0, qi)),
            pl.BlockSpec((1, 16, TQ), lambda bi, qi: (bi, 0, qi)),
            pl.BlockSpec((1, s, LANES), lambda bi, qi: (bi, 0, 0)),
            pl.BlockSpec((1, s, LANES), lambda bi, qi: (bi, 0, 0)),
            pl.BlockSpec((1, s // LANES, LANES, LANES), lambda bi, qi: (bi, 0, 0, 0)),
            pl.BlockSpec((1, m, LANES), lambda bi, qi: (bi, 0, 0)),
            pl.BlockSpec((1, HEAD_DIM, m), lambda bi, qi: (bi, 0, 0)),
            pl.BlockSpec((n_slc, m), lambda bi, qi: (0, 0)),
        ],
        out_specs=pl.BlockSpec((1, TQ, 256), lambda bi, qi: (bi, qi, 0)),
        out_shape=jax.ShapeDtypeStruct((b, s, 256), BF16),
        scratch_shapes=[pltpu.VMEM((n_slc, TQ), F32),
                        pltpu.VMEM((KT, NSA_HEADS * TQ), F32), pltpu.VMEM((KT, NSA_HEADS * TQ), F32),
                        pltpu.VMEM((KT, NSA_HEADS * TQ), BF16), pltpu.VMEM((KT, NSA_HEADS * TQ), BF16),
                        pltpu.VMEM((TQ, TQ), F32), pltpu.VMEM((HEAD_DIM, NSA_HEADS * TQ), F32)],
        compiler_params=pltpu.CompilerParams(dimension_semantics=("arbitrary", "arbitrary"),
                                             vmem_limit_bytes=VMEM_LIMIT),
        name="sparse_attention",
    )(qt, qrt, gt, kk, ksx, vvt, kcmp, vcmpt, ovt)


def _outproj_kernel(x_ref, yabc_ref, yd_ref, wo_ref, g_ref, b_ref, rw_ref, rb_ref, x1_ref, gates_ref):
    mix = _dot(yabc_ref[...], wo_ref[0:768, :]) + _dot(yd_ref[...], wo_ref[768:1024, :])
    x1 = _layer_norm(DN_ALPHA * x_ref[...] + mix, g_ref[...], b_ref[...])
    x1_ref[...] = x1

    ts = x1.shape[0]
    aff = _sigmoid(_dot(x1.astype(BF16), rw_ref[...])).T[0:N_EXPERTS, :]
    biased = aff + rb_ref[...]
    row = lambda a, i: a[i:i + 1, :]

    gscores = []
    for gi in range(N_GROUPS):
        a, b, c, d = (row(biased, gi * EXPERTS_PER_GROUP + i) for i in range(EXPERTS_PER_GROUP))
        hi1, lo1, hi2, lo2 = jnp.maximum(a, b), jnp.minimum(a, b), jnp.maximum(c, d), jnp.minimum(c, d)
        gscores.append(jnp.maximum(hi1, hi2) + jnp.maximum(jnp.minimum(hi1, hi2), jnp.maximum(lo1, lo2)))
    best, gsel = gscores[0], jnp.zeros((1, ts), jnp.int32)
    for gi in range(1, N_GROUPS):
        better = gscores[gi] > best
        gsel = jnp.where(better, gi, gsel)
        best = jnp.where(better, gscores[gi], best)

    eid_i = lax.broadcasted_iota(jnp.int32, (N_EXPERTS, ts), 0)
    eid = eid_i.astype(F32)
    masked = jnp.where(jnp.right_shift(eid_i, 2) == gsel, biased, NEG)
    picks = []
    for _ in range(2):
        top = jnp.max(masked, axis=0, keepdims=True)
        first = jnp.min(jnp.where(masked == top, eid, float(N_EXPERTS)), axis=0, keepdims=True)
        pick = eid == first
        picks.append(pick)
        masked = jnp.where(pick, -jnp.inf, masked)
    chosen = picks[0] | picks[1]
    w_sel = jnp.where(chosen, aff, 0.0)
    gates_t = w_sel / jnp.sum(w_sel, axis=0, keepdims=True)
    grp8 = jnp.where(lax.broadcasted_iota(jnp.int32, (8, ts), 0) == 0, gsel.astype(F32), 0.0)
    gates_ref[...] = jnp.concatenate([gates_t, grp8, jnp.zeros((LANES - N_EXPERTS - 8, ts), F32)], axis=0).T


def _outproj(x2d, yabc, yd, wo, g, b, rw, rb):
    n, d = x2d.shape
    ts = TS_OUT
    const = lambda shape: pl.BlockSpec(shape, lambda i: (0,) * len(shape))
    return pl.pallas_call(
        _outproj_kernel,
        grid=(n // ts,),
        in_specs=[
            pl.BlockSpec((ts, d), lambda i: (i, 0)),
            pl.BlockSpec((ts, 768), lambda i: (i, 0)),
            pl.BlockSpec((ts, 256), lambda i: (i, 0)),
            const((d, d)), const((1, d)), const((1, d)), const((d, LANES)), const((N_EXPERTS, 1)),
        ],
        out_specs=(pl.BlockSpec((ts, d), lambda i: (i, 0)), pl.BlockSpec((ts, LANES), lambda i: (i, 0))),
        out_shape=(jax.ShapeDtypeStruct((n, d), F32), jax.ShapeDtypeStruct((n, LANES), F32)),
        compiler_params=pltpu.CompilerParams(dimension_semantics=("arbitrary",), vmem_limit_bytes=VMEM_LIMIT),
        name="outproj_ln_route",
    )(x2d, yabc, yd, wo, g, b, rw, rb)


def _split3(x):
    hi = x.astype(BF16)
    r1 = x - hi.astype(F32)
    mid = r1.astype(BF16)
    lo = (r1 - mid.astype(F32)).astype(BF16)
    return hi, mid, lo


def _dispatch_kernel(x1_ref, gates_ref, tri_ref, xs_ref, gs_ref, post_ref, cnt_ref):
    tm = x1_ref.shape[0]
    gates = gates_ref[...]
    gsel = gates.T[GROUP_LANE:GROUP_LANE + 1, :]
    onehot = jnp.where(lax.broadcasted_iota(jnp.int32, (8, tm), 0).astype(F32) == gsel, 1.0, 0.0)
    n = jnp.sum(onehot, axis=1, keepdims=True)
    n_al = jnp.floor((n + (ROW_ALIGN - 1)) * (1.0 / ROW_ALIGN)) * ROW_ALIGN
    starts = [jnp.zeros((1, 1), F32)]
    for gi in range(1, N_GROUPS):
        starts.append(starts[-1] + n_al[gi - 1:gi, :])
    start = jnp.concatenate(starts + [jnp.zeros((8 - N_GROUPS, 1), F32)], axis=0)
    before = _dot(onehot.astype(BF16), tri_ref[...])
    pos = jnp.sum(onehot * (start + before), axis=0, keepdims=True)
    perm = jnp.where(lax.broadcasted_iota(jnp.int32, (SORT_ROWS, tm), 0).astype(F32) == pos, 1.0, 0.0).astype(BF16)

    xs_ref[0, 0:SORT_ROWS, :] = _dot(perm, x1_ref[...].astype(BF16)).astype(BF16)
    xs_ref[0, SORT_ROWS:SORT_BUF, :] = jnp.zeros((SORT_BUF - SORT_ROWS, xs_ref.shape[2]), BF16)
    g_hi, g_mid, g_lo = _split3(gates)
    gs_ref[0, 0:SORT_ROWS, :] = _dot(perm, g_hi) + _dot(perm, g_mid) + _dot(perm, g_lo)
    gs_ref[0, SORT_ROWS:SORT_BUF, :] = jnp.zeros((SORT_BUF - SORT_ROWS, LANES), F32)
    pos8 = jnp.where(lax.broadcasted_iota(jnp.int32, (8, tm), 0) == 0, pos, 0.0)
    post_ref[...] = jnp.concatenate([pos8, jnp.zeros((LANES - 8, tm), F32)], axis=0).T
    cnt_ref[0] = jnp.broadcast_to(n, (8, LANES))


def _dispatch(x1, gates, tri):
    n, d = x1.shape
    tm = TM_MOE
    nt = n // tm
    return pl.pallas_call(
        _dispatch_kernel,
        grid=(nt,),
        in_specs=[
            pl.BlockSpec((tm, d), lambda i: (i, 0)),
            pl.BlockSpec((tm, LANES), lambda i: (i, 0)),
            pl.BlockSpec((tm, tm), lambda i: (0, 0)),
        ],
        out_specs=(
            pl.BlockSpec((1, SORT_BUF, d), lambda i: (i, 0, 0)),
            pl.BlockSpec((1, SORT_BUF, LANES), lambda i: (i, 0, 0)),
            pl.BlockSpec((tm, LANES), lambda i: (i, 0)),
            pl.BlockSpec((1, 8, LANES), lambda i: (i, 0, 0)),
        ),
        out_shape=(
            jax.ShapeDtypeStruct((nt, SORT_BUF, d), BF16),
            jax.ShapeDtypeStruct((nt, SORT_BUF, LANES), F32),
            jax.ShapeDtypeStruct((n, LANES), F32),
            jax.ShapeDtypeStruct((nt, 8, LANES), F32),
        ),
        compiler_params=pltpu.CompilerParams(dimension_semantics=("arbitrary",), vmem_limit_bytes=VMEM_LIMIT),
        name="moe_dispatch",
    )(x1, gates, tri)


def _moe_kernel(start_ref, trips_ref, xs_ref, gs_ref, wg_ref, wu_ref, wd_ref, acc_ref):
    i, e = pl.program_id(0), pl.program_id(1)

    @pl.when(e == 0)
    def _():
        acc_ref[...] = jnp.zeros(acc_ref.shape, F32)

    grp = lax.shift_right_logical(e, 2)
    for tile in range(TILES_PER_STEP):
        slot = (i * TILES_PER_STEP + tile) * N_GROUPS + grp
        row0 = start_ref[slot]

        def window(w, carry, tile=tile, row0=row0):
            rows = pl.ds(pl.multiple_of(row0 + w * WIN_ROWS, ROW_ALIGN), WIN_ROWS)
            xw = xs_ref[tile, rows, :]
            hg = _dot(xw, wg_ref[0])
            h = hg * _sigmoid(hg) * _dot(xw, wu_ref[0])
            y = _dot(h.astype(BF16), wd_ref[0])
            gw = gs_ref[tile, rows, :]
            lane = lax.broadcasted_iota(jnp.int32, gw.shape, 1)
            gcol = jnp.sum(jnp.where(lane == e, gw, 0.0), axis=1, keepdims=True)
            acc_ref[tile, rows, :] += y * gcol
            return carry

        lax.fori_loop(0, trips_ref[slot], window, 0)


def _moe(start, trips, xs, gs, wg, wu, wd):
    nt, _, d = xs.shape
    ne, _, de = wg.shape
    tps = TILES_PER_STEP
    grid_spec = pltpu.PrefetchScalarGridSpec(
        num_scalar_prefetch=2,
        grid=(nt // tps, ne),
        in_specs=[
            pl.BlockSpec((tps, SORT_BUF, d), lambda i, e, *_: (i, 0, 0)),
            pl.BlockSpec((tps, SORT_BUF, LANES), lambda i, e, *_: (i, 0, 0)),
            pl.BlockSpec((1, d, de), lambda i, e, *_: (e, 0, 0)),
            pl.BlockSpec((1, d, de), lambda i, e, *_: (e, 0, 0)),
            pl.BlockSpec((1, de, d), lambda i, e, *_: (e, 0, 0)),
        ],
        out_specs=pl.BlockSpec((tps, SORT_BUF, d), lambda i, e, *_: (i, 0, 0)),
    )
    return pl.pallas_call(
        _moe_kernel,
        grid_spec=grid_spec,
        out_shape=jax.ShapeDtypeStruct((nt, SORT_BUF, d), F32),
        compiler_params=pltpu.CompilerParams(dimension_semantics=("arbitrary", "arbitrary"),
                                             vmem_limit_bytes=VMEM_LIMIT),
        name="moe_experts",
    )(start, trips, xs, gs, wg, wu, wd)


def _combine_kernel(x1_ref, acc_ref, post_ref, g_ref, b_ref, out_ref):
    tm = x1_ref.shape[0]
    acc = acc_ref[0]
    hi = acc.astype(BF16)
    lo = (acc - hi.astype(F32)).astype(BF16)
    pos = post_ref[...][:, 0:1]
    unperm = jnp.where(lax.broadcasted_iota(jnp.int32, (tm, SORT_ROWS), 1).astype(F32) == pos, 1.0, 0.0).astype(BF16)
    moe = _dot(unperm, hi) + _dot(unperm, lo)
    out_ref[...] = _layer_norm(DN_ALPHA * x1_ref[...] + moe, g_ref[...], b_ref[...])


def _combine(x1, acc, post, g, b):
    n, d = x1.shape
    tm = TM_MOE
    const = lambda shape: pl.BlockSpec(shape, lambda i: (0,) * len(shape))
    return pl.pallas_call(
        _combine_kernel,
        grid=(n // tm,),
        in_specs=[
            pl.BlockSpec((tm, d), lambda i: (i, 0)),
            pl.BlockSpec((1, SORT_ROWS, d), lambda i: (i, 0, 0)),
            pl.BlockSpec((tm, LANES), lambda i: (i, 0)),
            const((1, d)), const((1, d)),
        ],
        out_specs=pl.BlockSpec((tm, d), lambda i: (i, 0)),
        out_shape=jax.ShapeDtypeStruct((n, d), F32),
        compiler_params=pltpu.CompilerParams(dimension_semantics=("arbitrary",), vmem_limit_bytes=VMEM_LIMIT),
        name="moe_combine_ln",
    )(x1, acc, post, g, b)


def _group_windows(cnt):
    n = cnt[:, 0:N_GROUPS, 0].astype(jnp.int32)
    n_al = (n + (ROW_ALIGN - 1)) // ROW_ALIGN * ROW_ALIGN
    start = jnp.cumsum(n_al, axis=1) - n_al
    trips = (n + (WIN_ROWS - 1)) // WIN_ROWS
    return start.reshape(-1), trips.reshape(-1)


def _permute_w_in(w):
    sizes = (256, 256, 256, 256, 256, 256, 256, 64, 64, 64, 64, 64, 64, 12)
    offs = np.concatenate([[0], np.cumsum(sizes)])
    part = lambda i: w[:, int(offs[i]):int(offs[i + 1])]
    a_b, a_c, a_h, b_val, b_gate, c_p, q, kc, vc, ks, vs, kw, vw, g = (part(i) for i in range(14))
    pad = jnp.zeros((w.shape[0], LANES - g.shape[1]), w.dtype)
    return jnp.concatenate([a_b, a_c, a_h, b_val, b_gate, c_p, q, kc, vc, ks, kw, vs, vw, g, pad], axis=1).astype(BF16)


def _rope_tables(s):
    inv = 1.0 / (ROPE_THETA ** (jnp.arange(0, HEAD_DIM, 2, dtype=F32) / HEAD_DIM))
    ang = jnp.arange(s, dtype=F32)[:, None] * inv[None, :]
    ang = jnp.concatenate([ang, ang], -1)
    sign = jnp.concatenate([-jnp.ones((HEAD_DIM // 2,), F32), jnp.ones((HEAD_DIM // 2,), F32)])
    return jnp.tile(jnp.cos(ang), (1, NSA_HEADS)), jnp.tile(jnp.sin(ang) * sign[None, :], (1, NSA_HEADS))


def _overlap_t(s, m):
    n_cmp = (s - CMP_BLOCK) // CMP_STRIDE + 1
    cmp_start = np.arange(m) * CMP_STRIDE
    slc_start = np.arange(s // SLC_BLOCK) * SLC_BLOCK
    ov = (cmp_start[None, :] < slc_start[:, None] + SLC_BLOCK) & (cmp_start[None, :] + CMP_BLOCK > slc_start[:, None])
    ov = ov & (np.arange(m)[None, :] < n_cmp)
    return jnp.asarray(ov, BF16)


def _block_diag(w):
    g, c, _ = w.shape
    out = jnp.zeros((g * c, g * c), w.dtype)
    for i in range(g):
        out = out.at[i * c:(i + 1) * c, i * c:(i + 1) * c].set(w[i])
    return out


def kernel(x, w_in, conv_a_w, conv_b_w, conv_b_b, cf_ln_g, cf_ln_b, pool_w, pool_scale, cmp_pe_k, cmp_pe_v, cmp_k_w1, cmp_k_w2, cmp_v_w1, cmp_v_w2, w_out, ln1_g, ln1_b, ln2_g, ln2_b, router_w, router_bias, moe_w_gate, moe_w_up, moe_w_down):
    b, s, d = x.shape
    n = b * s
    m = s // CMP_STRIDE
    tok_per_row = CMP_STRIDE
    cos4, sin4 = _rope_tables(s)
    ovt = _overlap_t(s, m)
    rw = jnp.concatenate([router_w, jnp.zeros((d, LANES - N_EXPERTS), router_w.dtype)], axis=1).astype(BF16)
    rb = router_bias.reshape(N_EXPERTS, 1).astype(F32)
    pad_w2 = lambda w: jnp.concatenate([w, jnp.zeros((CMP_HIDDEN, LANES - HEAD_DIM), w.dtype)], axis=1).astype(BF16)
    row = lambda v: v.reshape(1, -1)
    tri = jnp.asarray(np.triu(np.ones((TM_MOE, TM_MOE), np.float32), k=1), BF16)

    for l in range(DEPTH):
        yabc, qt, qrt, kcvc, kk, ksx, vvt, gt = _inproj(
            x, _permute_w_in(w_in[l]), cos4, sin4, conv_a_w[l], conv_b_w[l], row(conv_b_b[l]),
            row(cf_ln_g[l]), row(cf_ln_b[l]), _block_diag(pool_w[l]).astype(BF16), row(pool_scale[l]))
        kc16 = kcvc[:, :, 0:HEAD_DIM].reshape(b, m, tok_per_row * HEAD_DIM)
        vc16 = kcvc[:, :, HEAD_DIM:2 * HEAD_DIM].reshape(b, m, tok_per_row * HEAD_DIM)
        kcmp, vcmpt = _compress(
            kc16, vc16, cmp_pe_k[l].reshape(2, tok_per_row * HEAD_DIM), cmp_pe_v[l].reshape(2, tok_per_row * HEAD_DIM),
            cmp_k_w1[l].astype(BF16), pad_w2(cmp_k_w2[l]), cmp_v_w1[l].astype(BF16), pad_w2(cmp_v_w2[l]))
        yd = _nsa(qt, qrt, gt, kk, ksx, vvt, kcmp, vcmpt, ovt)
        x1, gates = _outproj(x.reshape(n, d), yabc.reshape(n, 768), yd.reshape(n, 256), w_out[l].astype(BF16),
                             row(ln1_g[l]), row(ln1_b[l]), rw, rb)
        xs, gs, post, cnt = _dispatch(x1, gates, tri)
        start, trips = _group_windows(cnt)
        acc = _moe(start, trips, xs, gs, moe_w_gate[l].astype(BF16), moe_w_up[l].astype(BF16),
                   moe_w_down[l].astype(BF16))
        x = _combine(x1, acc, post, row(ln2_g[l]), row(ln2_b[l])).reshape(b, s, d)
    return x
```

```python
import functools

import jax
import jax.numpy as jnp
import numpy as np
from jax import lax
from jax.experimental import pallas as pl
from jax.experimental.pallas import tpu as pltpu

F32 = jnp.float32
BF16 = jnp.bfloat16

D_MODEL = 1024
DEPTH = 2
GROUP_W = 256
HEAD_DIM = 64
SC_KERNEL = 3
CF_KERNEL = 31
POOL_WINDOWS = (2, 4, 8, 16)
POOL_GW = 64
NSA_HEADS = 4
CMP_BLOCK = 32
CMP_STRIDE = 16
CMP_HIDDEN = 256
SLC_BLOCK = 64
SLC_TOPN = 16
WIN = 512
N_EXPERTS = 16
N_GROUPS = 4
EXPERTS_PER_GROUP = 4
D_EXPERT = 512
DN_ALPHA = (2 * DEPTH) ** 0.25
LN_EPS = 1e-5
NEG = -1e30
FORCE = 1e4
ROPE_THETA = 10000.0
QK_SCALE = HEAD_DIM ** -0.5
LOG2E = 1.4426950408889634

LANES = 128
OFF_A, OFF_B, OFF_C, OFF_Q, OFF_KCVC, OFF_KK, OFF_VV, OFF_G, IN_PAD = 0, 768, 1280, 1536, 1792, 1920, 2048, 2176, 2304

HALO_A = 8
HALO_B = 32
HALO_C = 16

TS_IN = 512
TQ = 256
KT = 512
TS_OUT = 512
TM_MOE = 1024
ROW_ALIGN = 16
WIN_ROWS = 288
SORT_ROWS = TM_MOE + LANES
SORT_BUF = SORT_ROWS + 2 * LANES
GROUP_LANE = 16
TILES_PER_STEP = 2
assert SORT_ROWS >= TM_MOE + N_GROUPS * (ROW_ALIGN - 1)
assert SORT_BUF >= TM_MOE + N_GROUPS * (ROW_ALIGN - 1) + WIN_ROWS
VMEM_LIMIT = 56 * 1024 * 1024


def _sigmoid(x):
    return jax.nn.sigmoid(x)


def _layer_norm(h, g, b):
    mu = jnp.mean(h, axis=-1, keepdims=True)
    d = h - mu
    var = jnp.mean(d * d, axis=-1, keepdims=True)
    return d * lax.rsqrt(var + LN_EPS) * g + b


def _dot(a, b):
    return jnp.dot(a, b, preferred_element_type=F32)


def _inproj_kernel(x_ref, w_ref, cos_ref, sin_ref, cva_ref, cvb_ref, cvbb_ref, lng_ref, lnb_ref,
                   poolw_ref, pools_ref,
                   yabc_ref, qt_ref, qrt_ref, kcvc_ref, kk_ref, ksx_ref, vvt_ref, gt_ref,
                   ch_ext, u_ext, p_ext, u_sh):
    j = pl.program_id(1)
    ts = x_ref.shape[1]
    xb = x_ref[0].astype(BF16)

    def proj(lo, hi):
        return _dot(xb, w_ref[:, lo:hi])

    def carry_halo(ext, halo):
        @pl.when(j == 0)
        def _():
            ext[0:halo, :] = jnp.zeros((halo, ext.shape[1]), F32)

        @pl.when(j > 0)
        def _():
            ext[0:halo, :] = ext[ts:ts + halo, :]

    pa = proj(OFF_A, OFF_B)
    a_b, ch = pa[:, 0:256], pa[:, 256:512] * pa[:, 512:768]
    carry_halo(ch_ext, HALO_A)
    ch_ext[HALO_A:HALO_A + ts, :] = ch
    conv = cva_ref[2:3, :] * ch
    for k in range(SC_KERNEL - 1):
        off = HALO_A - (SC_KERNEL - 1) + k
        conv = conv + cva_ref[k:k + 1, :] * ch_ext[off:off + ts, :]
    yabc_ref[0, :, 0:256] = (a_b * conv).astype(BF16)

    pb = proj(OFF_B, OFF_C)
    u = pb[:, 0:256] * _sigmoid(pb[:, 256:512])
    carry_halo(u_ext, HALO_B)
    u_ext[HALO_B:HALO_B + ts, :] = u
    span = u_sh.shape[1]
    for r in range(1, 8):
        u_sh[r - 1, :, :] = u_ext[r:r + span, :]
    def mixer_c():
        pc = proj(OFF_C, OFF_Q)
        carry_halo(p_ext, HALO_C)
        p_ext[HALO_C:HALO_C + ts, :] = pc
        pe = p_ext[...]
        s2 = pe + pltpu.roll(pe, 1, 0)
        s4 = s2 + pltpu.roll(s2, 2, 0)
        s8 = s4 + pltpu.roll(s4, 4, 0)
        s16 = s8 + pltpu.roll(s8, 8, 0)
        lane = lax.broadcasted_iota(jnp.int32, (ts, 256), 1)
        grp = jnp.right_shift(lane, 6)
        wsum = jnp.where(grp == 0, s2[HALO_C:], jnp.where(grp == 1, s4[HALO_C:], jnp.where(grp == 2, s8[HALO_C:], s16[HALO_C:])))
        win = jnp.where(grp == 0, 2, jnp.where(grp == 1, 4, jnp.where(grp == 2, 8, 16)))
        t1 = j * ts + lax.broadcasted_iota(jnp.int32, (ts, 256), 0) + 1
        cnt = jnp.minimum(win, t1).astype(F32)
        dlt = wsum / cnt - pc
        yabc_ref[0, :, 512:768] = (_dot(dlt.astype(BF16), poolw_ref[...]) * pools_ref[...]).astype(BF16)

    def queries():
        cos4, sin4 = cos_ref[...], sin_ref[...]
        q = proj(OFF_Q, OFF_KCVC)
        lane_q = jnp.bitwise_and(lax.broadcasted_iota(jnp.int32, (ts, 256), 1), HEAD_DIM - 1)
        q_sw = jnp.where(lane_q < HEAD_DIM // 2, pltpu.roll(q, 256 - HEAD_DIM // 2, 1), pltpu.roll(q, HEAD_DIM // 2, 1))
        qr = q * cos4 + q_sw * sin4
        qt_ref[0] = (q * (QK_SCALE * LOG2E)).T.astype(BF16)
        qrt_ref[0] = (qr * (QK_SCALE * LOG2E)).T.astype(BF16)

    def compress_inputs():
        kcvc_ref[0] = proj(OFF_KCVC, OFF_KK)

    def keys():
        k2 = proj(OFF_KK, OFF_VV)
        lane_k = jnp.bitwise_and(lax.broadcasted_iota(jnp.int32, (ts, LANES), 1), HEAD_DIM - 1)
        k_sw = jnp.where(lane_k < HEAD_DIM // 2, pltpu.roll(k2, LANES - HEAD_DIM // 2, 1), pltpu.roll(k2, HEAD_DIM // 2, 1))
        k_rot = k2 * cos_ref[:, 0:LANES] + k_sw * sin_ref[:, 0:LANES]
        kk_ref[0] = k_rot.astype(BF16)
        lane_i = lax.broadcasted_iota(jnp.int32, (ts, LANES), 1)
        blk_in_tile = jnp.bitwise_and(jnp.right_shift(j * ts + lax.broadcasted_iota(jnp.int32, (ts, LANES), 0), 6),
                                      KT // SLC_BLOCK - 1)
        ksx_ref[0] = jnp.where(lane_i < HEAD_DIM, k_rot, jnp.where(lane_i - HEAD_DIM == blk_in_tile, 1.0, 0.0)).astype(BF16)

    def values():
        v2t = proj(OFF_VV, OFF_G).T
        for c in range(ts // LANES):
            vvt_ref[0, c] = v2t[:, c * LANES:(c + 1) * LANES].astype(BF16)

    def branch_gates():
        gt_ref[0] = _sigmoid(proj(OFF_G, IN_PAD)).T[0:16, :]

    interleaved = [mixer_c, queries, compress_inputs, keys, values, branch_gates]
    rows = 64
    assert ts // rows >= len(interleaved)
    for c in range(ts // rows):
        acc = jnp.zeros((rows, 256), F32) + cvbb_ref[...]
        for k in range(CF_KERNEL):
            off = HALO_B - (CF_KERNEL - 1) + k
            base = off // 8 * 8 + c * rows
            src = u_ext if off % 8 == 0 else u_sh.at[off % 8 - 1]
            acc = acc + cvb_ref[k:k + 1, :] * src[base:base + rows, :]
        v = _layer_norm(acc, lng_ref[...], lnb_ref[...])
        yabc_ref[0, c * rows:(c + 1) * rows, 256:512] = (v * _sigmoid(v)).astype(BF16)
        if c < len(interleaved):
            interleaved[c]()


def _inproj(x, w_in_p, cos4, sin4, cva, cvb, cvbb, lng, lnb, poolw, pools):
    b, s, d = x.shape
    ts = TS_IN
    grid = (b, s // ts)
    const = lambda shape: pl.BlockSpec(shape, lambda bi, ji: (0,) * len(shape))
    out_shape = (
        jax.ShapeDtypeStruct((b, s, 768), BF16),
        jax.ShapeDtypeStruct((b, 256, s), BF16),
        jax.ShapeDtypeStruct((b, 256, s), BF16),
        jax.ShapeDtypeStruct((b, s, LANES), F32),
        jax.ShapeDtypeStruct((b, s, LANES), BF16),
        jax.ShapeDtypeStruct((b, s, LANES), BF16),
        jax.ShapeDtypeStruct((b, s // LANES, LANES, LANES), BF16),
        jax.ShapeDtypeStruct((b, 16, s), F32),
    )
    return pl.pallas_call(
        _inproj_kernel,
        grid=grid,
        in_specs=[
            pl.BlockSpec((1, ts, d), lambda bi, ji: (bi, ji, 0)),
            const((d, IN_PAD)),
            pl.BlockSpec((ts, 256), lambda bi, ji: (ji, 0)),
            pl.BlockSpec((ts, 256), lambda bi, ji: (ji, 0)),
            const((SC_KERNEL, 256)), const((CF_KERNEL, 256)), const((1, 256)), const((1, 256)), const((1, 256)),
            const((256, 256)), const((1, 256)),
        ],
        out_specs=(
            pl.BlockSpec((1, ts, 768), lambda bi, ji: (bi, ji, 0)),
            pl.BlockSpec((1, 256, ts), lambda bi, ji: (bi, 0, ji)),
            pl.BlockSpec((1, 256, ts), lambda bi, ji: (bi, 0, ji)),
            pl.BlockSpec((1, ts, LANES), lambda bi, ji: (bi, ji, 0)),
            pl.BlockSpec((1, ts, LANES), lambda bi, ji: (bi, ji, 0)),
            pl.BlockSpec((1, ts, LANES), lambda bi, ji: (bi, ji, 0)),
            pl.BlockSpec((1, ts // LANES, LANES, LANES), lambda bi, ji: (bi, ji, 0, 0)),
            pl.BlockSpec((1, 16, ts), lambda bi, ji: (bi, 0, ji)),
        ),
        out_shape=out_shape,
        scratch_shapes=[
            pltpu.VMEM((ts + HALO_A, 256), F32),
            pltpu.VMEM((ts + HALO_B, 256), F32),
            pltpu.VMEM((ts + HALO_C, 256), F32),
            pltpu.VMEM((7, ts + HALO_B - 8, 256), F32),
        ],
        compiler_params=pltpu.CompilerParams(dimension_semantics=("arbitrary", "arbitrary"),
                                             vmem_limit_bytes=VMEM_LIMIT),
        name="inproj_mixers",
    )(x, w_in_p, cos4, sin4, cva, cvb, cvbb, lng, lnb, poolw, pools)


def _compress_kernel(kc_ref, vc_ref, pek_ref, pev_ref, wk1_ref, wk2_ref, wv1_ref, wv2_ref, kcmp_ref, vcmpt_ref):
    def comp(x_ref, pe_ref, w1_ref, w2_ref):
        x = x_ref[0]
        m = x.shape[0]
        half = x.shape[1]
        first = _dot((x + pe_ref[0:1, :]).astype(BF16), w1_ref[0:half, :])
        second = _dot((x + pe_ref[1:2, :]).astype(BF16), w1_ref[half:2 * half, :])
        hid = first + pltpu.roll(second, m - 1, 0)
        return _dot((hid * _sigmoid(hid)).astype(BF16), w2_ref[...])

    kcmp_ref[0] = comp(kc_ref, pek_ref, wk1_ref, wk2_ref).astype(BF16)
    vcmpt_ref[0] = comp(vc_ref, pev_ref, wv1_ref, wv2_ref).T[0:HEAD_DIM, :].astype(BF16)


def _compress(kc16, vc16, pek, pev, wk1, wk2, wv1, wv2):
    b, m, w = kc16.shape
    const = lambda shape: pl.BlockSpec(shape, lambda bi: (0,) * len(shape))
    return pl.pallas_call(
        _compress_kernel,
        grid=(b,),
        in_specs=[
            pl.BlockSpec((1, m, w), lambda bi: (bi, 0, 0)),
            pl.BlockSpec((1, m, w), lambda bi: (bi, 0, 0)),
            const((2, w)), const((2, w)),
            const((2 * w, CMP_HIDDEN)), const((CMP_HIDDEN, LANES)),
            const((2 * w, CMP_HIDDEN)), const((CMP_HIDDEN, LANES)),
        ],
        out_specs=(
            pl.BlockSpec((1, m, LANES), lambda bi: (bi, 0, 0)),
            pl.BlockSpec((1, HEAD_DIM, m), lambda bi: (bi, 0, 0)),
        ),
        out_shape=(jax.ShapeDtypeStruct((b, m, LANES), BF16), jax.ShapeDtypeStruct((b, HEAD_DIM, m), BF16)),
        compiler_params=pltpu.CompilerParams(dimension_semantics=("arbitrary",), vmem_limit_bytes=VMEM_LIMIT),
        name="compress_kv",
    )(kc16, vc16, pek, pev, wk1, wk2, wv1, wv2)


def _nsa_kernel(qt_ref, qrt_ref, gt_ref, kk_ref, ksx_ref, vvt_ref, kcmp_ref, vcmpt_ref, ovt_ref, yd_ref, sel_scr, sa_scr, sb_scr, pa_scr, pb_scr, diag_scr, acc_scr):
    qb = pl.program_id(1)
    tq = qt_ref.shape[2]
    nh = NSA_HEADS
    wq = nh * tq
    s0 = qb * tq
    n_slc = sel_scr.shape[0]
    n_cmp_pad = kcmp_ref.shape[1]

    def stack_heads(ref):
        x = ref[0]
        return jnp.concatenate([x[h * HEAD_DIM:(h + 1) * HEAD_DIM, :] for h in range(nh)], axis=1)

    zeros = jnp.zeros((HEAD_DIM, wq), BF16)
    q_top, qr_top = stack_heads(qt_ref), stack_heads(qrt_ref)
    q_lo = jnp.concatenate([q_top, zeros], axis=0)
    qr_lo = jnp.concatenate([qr_top, zeros], axis=0)
    qr_hi = jnp.concatenate([zeros, qr_top], axis=0)
    t_row = s0 + lax.broadcasted_iota(jnp.int32, (1, tq), 1)
    lanes4 = lambda x: jnp.concatenate([x] * nh, axis=1)
    t_all = lanes4(t_row)

    size_classes = 4
    class_rows = n_cmp_pad // size_classes
    assert CMP_STRIDE * class_rows >= tq >= 2 * SLC_BLOCK

    def compress_and_select(nc):
        nb = nc // (SLC_BLOCK // CMP_STRIDE)
        cmp_end = lax.broadcasted_iota(jnp.int32, (nc, tq), 0) * CMP_STRIDE + (CMP_BLOCK - 1)
        sc = _dot(kcmp_ref[0, 0:nc, :], q_lo) + lanes4(jnp.where(cmp_end <= t_row, 0.0, NEG))
        mx = jnp.max(sc, axis=0, keepdims=True)
        ex = jnp.exp2(sc - mx)
        den = jnp.sum(ex, axis=0, keepdims=True)
        p_cmp = ex * jnp.where(t_all >= CMP_BLOCK - 1, 1.0 / den, 0.0)
        o_cmp = _dot(vcmpt_ref[0, :, 0:nc], p_cmp.astype(BF16))
        p_heads = p_cmp[:, 0:tq]
        for h in range(1, nh):
            p_heads = p_heads + p_cmp[:, h * tq:(h + 1) * tq]
        imp = _dot(ovt_ref[0:nb, 0:nc], p_heads.astype(BF16))

        blk_i = lax.broadcasted_iota(jnp.int32, (nb, tq), 0)
        blk = blk_i.astype(F32)
        cur = jnp.right_shift(t_row, 6)
        forced = (blk_i == 0) | (blk_i == cur) | (blk_i == cur - 1)
        valid = blk_i * SLC_BLOCK <= t_row
        sel = jnp.where(forced, 1.0, 0.0)
        n_forced = jnp.sum(sel, axis=0, keepdims=True)
        imp = jnp.where(valid, jnp.where(forced, -jnp.inf, imp), NEG)
        min_forced = 3 if nc > class_rows else 1
        for r in range(min(SLC_TOPN - min_forced, nb)):
            top = jnp.max(imp, axis=0, keepdims=True)
            first = jnp.min(jnp.where(imp == top, blk, float(nb)), axis=0, keepdims=True)
            pick = (blk == first) & (n_forced + r < SLC_TOPN)
            sel = jnp.where(pick, 1.0, sel)
            imp = jnp.where(pick, -jnp.inf, imp)
        bias = jnp.where(valid & (sel > 0.0), 0.0, NEG)
        own0 = qb * (tq // SLC_BLOCK)
        own = (blk_i >= own0) & (blk_i < own0 + tq // SLC_BLOCK)
        sel_scr[0:nb, :] = jnp.where(own, NEG, bias)
        if nb < n_slc:
            sel_scr[nb:n_slc, :] = jnp.full((n_slc - nb, tq), NEG, F32)
        key_blk = own0 + jnp.right_shift(lax.broadcasted_iota(jnp.int32, (tq, nb), 0), 6)
        expand = jnp.where(lax.broadcasted_iota(jnp.int32, (tq, nb), 1) == key_blk, 1.0, 0.0).astype(BF16)
        diag_scr[...] = _dot(expand, bias.astype(BF16))
        return o_cmp

    visible = (s0 + tq) // CMP_STRIDE
    o_cmp = lax.switch((visible - 1) // class_rows,
                       [functools.partial(compress_and_select, (i + 1) * class_rows) for i in range(size_classes)])

    blocks_per_tile = KT // SLC_BLOCK
    sub = KT // LANES

    last_tile = kk_ref.shape[1] // KT - 1

    def score_chunks(kt):
        kd = jnp.minimum(kt, last_tile)
        bias8 = sel_scr[pl.ds(pl.multiple_of(kd * blocks_per_tile, blocks_per_tile), blocks_per_tile), :]
        bias8 = jnp.where(kt <= last_tile, bias8, NEG)
        pad = [jnp.zeros((16 - blocks_per_tile, wq), F32)] if blocks_per_tile < 16 else []
        bias_rows = jnp.concatenate([lanes4(bias8)] + pad, axis=0).astype(BF16)
        q_bias = jnp.concatenate([qr_top, bias_rows, jnp.zeros((HEAD_DIM - 16, wq), BF16)], axis=0)
        s_tile = _dot(ksx_ref[0, pl.ds(pl.multiple_of(kd * KT, KT), KT), :], q_bias)
        for c in range(sub):
            yield c, s_tile[c * LANES:(c + 1) * LANES, :]

    def fold8(x, op):
        out = x[0:8, :]
        for r in range(1, x.shape[0] // 8):
            out = op(out, x[8 * r:8 * r + 8, :])
        return out

    def v_rows(first, count, lo):
        vt = vvt_ref[0, pl.ds(first, count)]
        return jnp.concatenate([vt[c, lo:lo + HEAD_DIM, :] for c in range(count)], axis=1)

    def pv_tile(kt, p_ref):
        return _dot(v_rows(pl.multiple_of(jnp.minimum(kt, last_tile) * sub, sub), sub, 0), p_ref[...])

    def phase(kt, s_cur, s_next, p_cur, p_prev, state):
        m_i, l_i, mt_cur, alpha_prev = state
        m_new = jnp.maximum(m_i, jnp.max(mt_cur, axis=0, keepdims=True))
        alpha = jnp.exp2(m_i - m_new)
        mt_next = jnp.full((8, wq), NEG, F32)
        l_new = alpha * l_i
        for c, s_n in score_chunks(kt + 1):
            rows = slice(c * LANES, (c + 1) * LANES)
            mt_next = jnp.maximum(mt_next, fold8(s_n, jnp.maximum))
            s_next[rows, :] = s_n
            p = jnp.exp2(s_cur[rows, :] - m_new)
            l_new = l_new + fold8(p, jnp.add)
            p_cur[rows, :] = p.astype(BF16)
        if p_prev is not None:
            acc_scr[...] = alpha_prev * acc_scr[...] + pv_tile(kt - 1, p_prev)
        return m_new, l_new, mt_next, alpha

    def slc_pair(j, state):
        state = phase(2 * j + 1, sb_scr, sa_scr, pb_scr, pa_scr, state)
        return phase(2 * j + 2, sa_scr, sb_scr, pa_scr, pb_scr, state)

    mt0 = jnp.full((8, wq), NEG, F32)
    for c, s_n in score_chunks(0):
        mt0 = jnp.maximum(mt0, fold8(s_n, jnp.maximum))
        sa_scr[c * LANES:(c + 1) * LANES, :] = s_n
    key_d = s0 + lax.broadcasted_iota(jnp.int32, (tq, tq), 0)
    s_d = (_dot(kk_ref[0, pl.ds(pl.multiple_of(s0, tq), tq), :], qr_lo)
           + lanes4(jnp.where(key_d <= t_row, diag_scr[...], NEG)))
    m_d = jnp.max(s_d, axis=0, keepdims=True)
    p_d = jnp.exp2(s_d - m_d)
    acc_scr[...] = _dot(v_rows(qb * (tq // LANES), tq // LANES, 0), p_d.astype(BF16))
    state = (m_d, fold8(p_d, jnp.add), mt0, jnp.ones((1, wq), F32))
    state = phase(0, sa_scr, sb_scr, pa_scr, None, state)
    n_tiles = (s0 + tq + KT - 1) // KT
    pairs = n_tiles // 2
    _, l8_s, _, alpha_last = lax.fori_loop(0, pairs, slc_pair, state)
    acc_s = alpha_last * acc_scr[...] + pv_tile(2 * pairs, pa_scr)
    l_s = jnp.sum(l8_s, axis=0, keepdims=True)

    wkeys = WIN + tq
    k0 = pl.multiple_of(jnp.maximum(s0 - WIN, 0), LANES)
    key = k0 + lax.broadcasted_iota(jnp.int32, (wkeys, tq), 0)
    wbias = jnp.where((key <= t_row) & (key > t_row - WIN), 0.0, NEG)
    s_w = _dot(kk_ref[0, pl.ds(k0, wkeys), :], qr_hi) + lanes4(wbias)
    vt_w = vvt_ref[0, pl.ds(jnp.maximum(qb * (tq // LANES) - WIN // LANES, 0), wkeys // LANES)]
    p_w = jnp.exp2(s_w - jnp.max(s_w, axis=0, keepdims=True))
    l_w = jnp.sum(p_w, axis=0, keepdims=True)
    v_w = jnp.concatenate([vt_w[i, HEAD_DIM:2 * HEAD_DIM, :] for i in range(wkeys // LANES)], axis=1)
    acc_w = _dot(v_w, p_w.astype(BF16))

    g = gt_ref[0]
    gate = lambda br: jnp.concatenate([g[h * 3 + br:h * 3 + br + 1, :] for h in range(nh)], axis=1)
    o = gate(0) * o_cmp + gate(1) * (acc_s * (1.0 / l_s)) + gate(2) * (acc_w * (1.0 / l_w))
    o_rows = jnp.concatenate([o[:, h * tq:(h + 1) * tq] for h in range(nh)], axis=0)
    yd_ref[0] = o_rows.T.astype(BF16)


def _nsa(qt, qrt, gt, kk, ksx, vvt, kcmp, vcmpt, ovt):
    b, _, s = qt.shape
    n_slc = s // SLC_BLOCK
    m = kcmp.shape[1]
    return pl.pallas_call(
        _nsa_kernel,
        grid=(b, s // TQ),
        in_specs=[
            pl.BlockSpec((1, 256, TQ), lambda bi, qi: (bi, 0, qi)),
            pl.BlockSpec((1, 256, TQ), lambda bi, qi: (bi, 0, qi)),
            pl.BlockSpec((1, 16, TQ), lambda bi, qi: (bi, 0, qi)),
            pl.BlockSpec((1, s, LANES), lambda bi, qi: (bi, 0, 0)),
            pl.BlockSpec((1, s, LANES), lambda bi, qi: (bi, 0, 0)),
            pl.BlockSpec((1, s // LANES, LANES, LANES), lambda bi, qi: (bi, 0, 0, 0)),
            pl.BlockSpec((1, m, LANES), lambda bi, qi: (bi, 0, 0)),
            pl.BlockSpec((1, HEAD_DIM, m), lambda bi, qi: (bi, 0, 0)),
            pl.BlockSpec((n_slc, m), lambda bi, qi: (0, 0)),
        ],
        out_specs=pl.BlockSpec((1, TQ, 256), lambda bi, qi: (bi, qi, 0)),
        out_shape=jax.ShapeDtypeStruct((b, s, 256), BF16),
        scratch_shapes=[pltpu.VMEM((n_slc, TQ), F32),
                        pltpu.VMEM((KT, NSA_HEADS * TQ), F32), pltpu.VMEM((KT, NSA_HEADS * TQ), F32),
                        pltpu.VMEM((KT, NSA_HEADS * TQ), BF16), pltpu.VMEM((KT, NSA_HEADS * TQ), BF16),
                        pltpu.VMEM((TQ, TQ), F32), pltpu.VMEM((HEAD_DIM, NSA_HEADS * TQ), F32)],
        compiler_params=pltpu.CompilerParams(dimension_semantics=("arbitrary", "arbitrary"),
                                             vmem_limit_bytes=VMEM_LIMIT),
        name="sparse_attention",
    )(qt, qrt, gt, kk, ksx, vvt, kcmp, vcmpt, ovt)


def _outproj_kernel(x_ref, yabc_ref, yd_ref, wo_ref, g_ref, b_ref, rw_ref, rb_ref, x1_ref, gates_ref):
    mix = _dot(yabc_ref[...], wo_ref[0:768, :]) + _dot(yd_ref[...], wo_ref[768:1024, :])
    x1 = _layer_norm(DN_ALPHA * x_ref[...] + mix, g_ref[...], b_ref[...])
    x1_ref[...] = x1

    ts = x1.shape[0]
    aff = _sigmoid(_dot(x1.astype(BF16), rw_ref[...])).T[0:N_EXPERTS, :]
    biased = aff + rb_ref[...]
    row = lambda a, i: a[i:i + 1, :]

    gscores = []
    for gi in range(N_GROUPS):
        a, b, c, d = (row(biased, gi * EXPERTS_PER_GROUP + i) for i in range(EXPERTS_PER_GROUP))
        hi1, lo1, hi2, lo2 = jnp.maximum(a, b), jnp.minimum(a, b), jnp.maximum(c, d), jnp.minimum(c, d)
        gscores.append(jnp.maximum(hi1, hi2) + jnp.maximum(jnp.minimum(hi1, hi2), jnp.maximum(lo1, lo2)))
    best, gsel = gscores[0], jnp.zeros((1, ts), jnp.int32)
    for gi in range(1, N_GROUPS):
        better = gscores[gi] > best
        gsel = jnp.where(better, gi, gsel)
        best = jnp.where(better, gscores[gi], best)

    eid_i = lax.broadcasted_iota(jnp.int32, (N_EXPERTS, ts), 0)
    eid = eid_i.astype(F32)
    masked = jnp.where(jnp.right_shift(eid_i, 2) == gsel, biased, NEG)
    picks = []
    for _ in range(2):
        top = jnp.max(masked, axis=0, keepdims=True)
        first = jnp.min(jnp.where(masked == top, eid, float(N_EXPERTS)), axis=0, keepdims=True)
        pick = eid == first
        picks.append(pick)
        masked = jnp.where(pick, -jnp.inf, masked)
    chosen = picks[0] | picks[1]
    w_sel = jnp.where(chosen, aff, 0.0)
    gates_t = w_sel / jnp.sum(w_sel, axis=0, keepdims=True)
    grp8 = jnp.where(lax.broadcasted_iota(jnp.int32, (8, ts), 0) == 0, gsel.astype(F32), 0.0)
    gates_ref[...] = jnp.concatenate([gates_t, grp8, jnp.zeros((LANES - N_EXPERTS - 8, ts), F32)], axis=0).T


def _outproj(x2d, yabc, yd, wo, g, b, rw, rb):
    n, d = x2d.shape
    ts = TS_OUT
    const = lambda shape: pl.BlockSpec(shape, lambda i: (0,) * len(shape))
    return pl.pallas_call(
        _outproj_kernel,
        grid=(n // ts,),
        in_specs=[
            pl.BlockSpec((ts, d), lambda i: (i, 0)),
            pl.BlockSpec((ts, 768), lambda i: (i, 0)),
            pl.BlockSpec((ts, 256), lambda i: (i, 0)),
            const((d, d)), const((1, d)), const((1, d)), const((d, LANES)), const((N_EXPERTS, 1)),
        ],
        out_specs=(pl.BlockSpec((ts, d), lambda i: (i, 0)), pl.BlockSpec((ts, LANES), lambda i: (i, 0))),
        out_shape=(jax.ShapeDtypeStruct((n, d), F32), jax.ShapeDtypeStruct((n, LANES), F32)),
        compiler_params=pltpu.CompilerParams(dimension_semantics=("arbitrary",), vmem_limit_bytes=VMEM_LIMIT),
        name="outproj_ln_route",
    )(x2d, yabc, yd, wo, g, b, rw, rb)


def _split3(x):
    hi = x.astype(BF16)
    r1 = x - hi.astype(F32)
    mid = r1.astype(BF16)
    lo = (r1 - mid.astype(F32)).astype(BF16)
    return hi, mid, lo


def _dispatch_kernel(x1_ref, gates_ref, tri_ref, xs_ref, gs_ref, post_ref, cnt_ref):
    tm = x1_ref.shape[0]
    gates = gates_ref[...]
    gsel = gates.T[GROUP_LANE:GROUP_LANE + 1, :]
    onehot = jnp.where(lax.broadcasted_iota(jnp.int32, (8, tm), 0).astype(F32) == gsel, 1.0, 0.0)
    n = jnp.sum(onehot, axis=1, keepdims=True)
    n_al = jnp.floor((n + (ROW_ALIGN - 1)) * (1.0 / ROW_ALIGN)) * ROW_ALIGN
    starts = [jnp.zeros((1, 1), F32)]
    for gi in range(1, N_GROUPS):
        starts.append(starts[-1] + n_al[gi - 1:gi, :])
    start = jnp.concatenate(starts + [jnp.zeros((8 - N_GROUPS, 1), F32)], axis=0)
    before = _dot(onehot.astype(BF16), tri_ref[...])
    pos = jnp.sum(onehot * (start + before), axis=0, keepdims=True)
    perm = jnp.where(lax.broadcasted_iota(jnp.int32, (SORT_ROWS, tm), 0).astype(F32) == pos, 1.0, 0.0).astype(BF16)

    xs_ref[0, 0:SORT_ROWS, :] = _dot(perm, x1_ref[...].astype(BF16)).astype(BF16)
    xs_ref[0, SORT_ROWS:SORT_BUF, :] = jnp.zeros((SORT_BUF - SORT_ROWS, xs_ref.shape[2]), BF16)
    g_hi, g_mid, g_lo = _split3(gates)
    gs_ref[0, 0:SORT_ROWS, :] = _dot(perm, g_hi) + _dot(perm, g_mid) + _dot(perm, g_lo)
    gs_ref[0, SORT_ROWS:SORT_BUF, :] = jnp.zeros((SORT_BUF - SORT_ROWS, LANES), F32)
    pos8 = jnp.where(lax.broadcasted_iota(jnp.int32, (8, tm), 0) == 0, pos, 0.0)
    post_ref[...] = jnp.concatenate([pos8, jnp.zeros((LANES - 8, tm), F32)], axis=0).T
    cnt_ref[0] = jnp.broadcast_to(n, (8, LANES))


def _dispatch(x1, gates, tri):
    n, d = x1.shape
    tm = TM_MOE
    nt = n // tm
    return pl.pallas_call(
        _dispatch_kernel,
        grid=(nt,),
        in_specs=[
            pl.BlockSpec((tm, d), lambda i: (i, 0)),
            pl.BlockSpec((tm, LANES), lambda i: (i, 0)),
            pl.BlockSpec((tm, tm), lambda i: (0, 0)),
        ],
        out_specs=(
            pl.BlockSpec((1, SORT_BUF, d), lambda i: (i, 0, 0)),
            pl.BlockSpec((1, SORT_BUF, LANES), lambda i: (i, 0, 0)),
            pl.BlockSpec((tm, LANES), lambda i: (i, 0)),
            pl.BlockSpec((1, 8, LANES), lambda i: (i, 0, 0)),
        ),
        out_shape=(
            jax.ShapeDtypeStruct((nt, SORT_BUF, d), BF16),
            jax.ShapeDtypeStruct((nt, SORT_BUF, LANES), F32),
            jax.ShapeDtypeStruct((n, LANES), F32),
            jax.ShapeDtypeStruct((nt, 8, LANES), F32),
        ),
        compiler_params=pltpu.CompilerParams(dimension_semantics=("arbitrary",), vmem_limit_bytes=VMEM_LIMIT),
        name="moe_dispatch",
    )(x1, gates, tri)


def _moe_kernel(start_ref, trips_ref, xs_ref, gs_ref, wg_ref, wu_ref, wd_ref, acc_ref):
    i, e = pl.program_id(0), pl.program_id(1)

    @pl.when(e == 0)
    def _():
        acc_ref[...] = jnp.zeros(acc_ref.shape, F32)

    grp = lax.shift_right_logical(e, 2)
    for tile in range(TILES_PER_STEP):
        slot = (i * TILES_PER_STEP + tile) * N_GROUPS + grp
        row0 = start_ref[slot]

        def window(w, carry, tile=tile, row0=row0):
            rows = pl.ds(pl.multiple_of(row0 + w * WIN_ROWS, ROW_ALIGN), WIN_ROWS)
            xw = xs_ref[tile, rows, :]
            hg = _dot(xw, wg_ref[0])
            h = hg * _sigmoid(hg) * _dot(xw, wu_ref[0])
            y = _dot(h.astype(BF16), wd_ref[0])
            gw = gs_ref[tile, rows, :]
            lane = lax.broadcasted_iota(jnp.int32, gw.shape, 1)
            gcol = jnp.sum(jnp.where(lane == e, gw, 0.0), axis=1, keepdims=True)
            acc_ref[tile, rows, :] += y * gcol
            return carry

        lax.fori_loop(0, trips_ref[slot], window, 0)


def _moe(start, trips, xs, gs, wg, wu, wd):
    nt, _, d = xs.shape
    ne, _, de = wg.shape
    tps = TILES_PER_STEP
    grid_spec = pltpu.PrefetchScalarGridSpec(
        num_scalar_prefetch=2,
        grid=(nt // tps, ne),
        in_specs=[
            pl.BlockSpec((tps, SORT_BUF, d), lambda i, e, *_: (i, 0, 0)),
            pl.BlockSpec((tps, SORT_BUF, LANES), lambda i, e, *_: (i, 0, 0)),
            pl.BlockSpec((1, d, de), lambda i, e, *_: (e, 0, 0)),
            pl.BlockSpec((1, d, de), lambda i, e, *_: (e, 0, 0)),
            pl.BlockSpec((1, de, d), lambda i, e, *_: (e, 0, 0)),
        ],
        out_specs=pl.BlockSpec((tps, SORT_BUF, d), lambda i, e, *_: (i, 0, 0)),
    )
    return pl.pallas_call(
        _moe_kernel,
        grid_spec=grid_spec,
        out_shape=jax.ShapeDtypeStruct((nt, SORT_BUF, d), F32),
        compiler_params=pltpu.CompilerParams(dimension_semantics=("arbitrary", "arbitrary"),
                                             vmem_limit_bytes=VMEM_LIMIT),
        name="moe_experts",
    )(start, trips, xs, gs, wg, wu, wd)


def _combine_kernel(x1_ref, acc_ref, post_ref, g_ref, b_ref, out_ref):
    tm = x1_ref.shape[0]
    acc = acc_ref[0]
    hi = acc.astype(BF16)
    lo = (acc - hi.astype(F32)).astype(BF16)
    pos = post_ref[...][:, 0:1]
    unperm = jnp.where(lax.broadcasted_iota(jnp.int32, (tm, SORT_ROWS), 1).astype(F32) == pos, 1.0, 0.0).astype(BF16)
    moe = _dot(unperm, hi) + _dot(unperm, lo)
    out_ref[...] = _layer_norm(DN_ALPHA * x1_ref[...] + moe, g_ref[...], b_ref[...])


def _combine(x1, acc, post, g, b):
    n, d = x1.shape
    tm = TM_MOE
    const = lambda shape: pl.BlockSpec(shape, lambda i: (0,) * len(shape))
    return pl.pallas_call(
        _combine_kernel,
        grid=(n // tm,),
        in_specs=[
            pl.BlockSpec((tm, d), lambda i: (i, 0)),
            pl.BlockSpec((1, SORT_ROWS, d), lambda i: (i, 0, 0)),
            pl.BlockSpec((tm, LANES), lambda i: (i, 0)),
            const((1, d)), const((1, d)),
        ],
        out_specs=pl.BlockSpec((tm, d), lambda i: (i, 0)),
        out_shape=jax.ShapeDtypeStruct((n, d), F32),
        compiler_params=pltpu.CompilerParams(dimension_semantics=("arbitrary",), vmem_limit_bytes=VMEM_LIMIT),
        name="moe_combine_ln",
    )(x1, acc, post, g, b)


def _group_windows(cnt):
    n = cnt[:, 0:N_GROUPS, 0].astype(jnp.int32)
    n_al = (n + (ROW_ALIGN - 1)) // ROW_ALIGN * ROW_ALIGN
    start = jnp.cumsum(n_al, axis=1) - n_al
    trips = (n + (WIN_ROWS - 1)) // WIN_ROWS
    return start.reshape(-1), trips.reshape(-1)


def _permute_w_in(w):
    sizes = (256, 256, 256, 256, 256, 256, 256, 64, 64, 64, 64, 64, 64, 12)
    offs = np.concatenate([[0], np.cumsum(sizes)])
    part = lambda i: w[:, int(offs[i]):int(offs[i + 1])]
    a_b, a_c, a_h, b_val, b_gate, c_p, q, kc, vc, ks, vs, kw, vw, g = (part(i) for i in range(14))
    pad = jnp.zeros((w.shape[0], LANES - g.shape[1]), w.dtype)
    return jnp.concatenate([a_b, a_c, a_h, b_val, b_gate, c_p, q, kc, vc, ks, kw, vs, vw, g, pad], axis=1).astype(BF16)


def _rope_tables(s):
    inv = 1.0 / (ROPE_THETA ** (jnp.arange(0, HEAD_DIM, 2, dtype=F32) / HEAD_DIM))
    ang = jnp.arange(s, dtype=F32)[:, None] * inv[None, :]
    ang = jnp.concatenate([ang, ang], -1)
    sign = jnp.concatenate([-jnp.ones((HEAD_DIM // 2,), F32), jnp.ones((HEAD_DIM // 2,), F32)])
    return jnp.tile(jnp.cos(ang), (1, NSA_HEADS)), jnp.tile(jnp.sin(ang) * sign[None, :], (1, NSA_HEADS))


def _overlap_t(s, m):
    n_cmp = (s - CMP_BLOCK) // CMP_STRIDE + 1
    cmp_start = np.arange(m) * CMP_STRIDE
    slc_start = np.arange(s // SLC_BLOCK) * SLC_BLOCK
    ov = (cmp_start[None, :] < slc_start[:, None] + SLC_BLOCK) & (cmp_start[None, :] + CMP_BLOCK > slc_start[:, None])
    ov = ov & (np.arange(m)[None, :] < n_cmp)
    return jnp.asarray(ov, BF16)


def _block_diag(w):
    g, c, _ = w.shape
    out = jnp.zeros((g * c, g * c), w.dtype)
    for i in range(g):
        out = out.at[i * c:(i + 1) * c, i * c:(i + 1) * c].set(w[i])
    return out


def kernel(x, w_in, conv_a_w, conv_b_w, conv_b_b, cf_ln_g, cf_ln_b, pool_w, pool_scale, cmp_pe_k, cmp_pe_v, cmp_k_w1, cmp_k_w2, cmp_v_w1, cmp_v_w2, w_out, ln1_g, ln1_b, ln2_g, ln2_b, router_w, router_bias, moe_w_gate, moe_w_up, moe_w_down):
    b, s, d = x.shape
    n = b * s
    m = s // CMP_STRIDE
    tok_per_row = CMP_STRIDE
    cos4, sin4 = _rope_tables(s)
    ovt = _overlap_t(s, m)
    rw = jnp.concatenate([router_w, jnp.zeros((d, LANES - N_EXPERTS), router_w.dtype)], axis=1).astype(BF16)
    rb = router_bias.reshape(N_EXPERTS, 1).astype(F32)
    pad_w2 = lambda w: jnp.concatenate([w, jnp.zeros((CMP_HIDDEN, LANES - HEAD_DIM), w.dtype)], axis=1).astype(BF16)
    row = lambda v: v.reshape(1, -1)
    tri = jnp.asarray(np.triu(np.ones((TM_MOE, TM_MOE), np.float32), k=1), BF16)

    for l in range(DEPTH):
        yabc, qt, qrt, kcvc, kk, ksx, vvt, gt = _inproj(
            x, _permute_w_in(w_in[l]), cos4, sin4, conv_a_w[l], conv_b_w[l], row(conv_b_b[l]),
            row(cf_ln_g[l]), row(cf_ln_b[l]), _block_diag(pool_w[l]).astype(BF16), row(pool_scale[l]))
        kc16 = kcvc[:, :, 0:HEAD_DIM].reshape(b, m, tok_per_row * HEAD_DIM)
        vc16 = kcvc[:, :, HEAD_DIM:2 * HEAD_DIM].reshape(b, m, tok_per_row * HEAD_DIM)
        kcmp, vcmpt = _compress(
            kc16, vc16, cmp_pe_k[l].reshape(2, tok_per_row * HEAD_DIM), cmp_pe_v[l].reshape(2, tok_per_row * HEAD_DIM),
            cmp_k_w1[l].astype(BF16), pad_w2(cmp_k_w2[l]), cmp_v_w1[l].astype(BF16), pad_w2(cmp_v_w2[l]))
        yd = _nsa(qt, qrt, gt, kk, ksx, vvt, kcmp, vcmpt, ovt)
        x1, gates = _outproj(x.reshape(n, d), yabc.reshape(n, 768), yd.reshape(n, 256), w_out[l].astype(BF16),
                             row(ln1_g[l]), row(ln1_b[l]), rw, rb)
        xs, gs, post, cnt = _dispatch(x1, gates, tri)
        start, trips = _group_windows(cnt)
        acc = _moe(start, trips, xs, gs, moe_w_gate[l].astype(BF16), moe_w_up[l].astype(BF16),
                   moe_w_down[l].astype(BF16))
        x = _combine(x1, acc, post, row(ln2_g[l]), row(ln2_b[l])).reshape(b, s, d)
    return x
```

```python
import functools

import jax
import jax.numpy as jnp
import numpy as np
from jax import lax
from jax.experimental import pallas as pl
from jax.experimental.pallas import tpu as pltpu

F32 = jnp.float32
BF16 = jnp.bfloat16

D_MODEL = 1024
DEPTH = 2
GROUP_W = 256
HEAD_DIM = 64
SC_KERNEL = 3
CF_KERNEL = 31
POOL_WINDOWS = (2, 4, 8, 16)
POOL_GW = 64
NSA_HEADS = 4
CMP_BLOCK = 32
CMP_STRIDE = 16
CMP_HIDDEN = 256
SLC_BLOCK = 64
SLC_TOPN = 16
WIN = 512
N_EXPERTS = 16
N_GROUPS = 4
EXPERTS_PER_GROUP = 4
D_EXPERT = 512
DN_ALPHA = (2 * DEPTH) ** 0.25
LN_EPS = 1e-5
NEG = -1e30
FORCE = 1e4
ROPE_THETA = 10000.0
QK_SCALE = HEAD_DIM ** -0.5
LOG2E = 1.4426950408889634

LANES = 128
OFF_A, OFF_B, OFF_C, OFF_Q, OFF_KCVC, OFF_KK, OFF_VV, OFF_G, IN_PAD = 0, 768, 1280, 1536, 1792, 1920, 2048, 2176, 2304

HALO_A = 8
HALO_B = 32
HALO_C = 16

TS_IN = 512
TQ = 256
KT = 512
TS_OUT = 512
TM_MOE = 1024
ROW_ALIGN = 16
WIN_ROWS = 288
SORT_ROWS = TM_MOE + LANES
SORT_BUF = SORT_ROWS + 2 * LANES
GROUP_LANE = 16
TILES_PER_STEP = 2
assert SORT_ROWS >= TM_MOE + N_GROUPS * (ROW_ALIGN - 1)
assert SORT_BUF >= TM_MOE + N_GROUPS * (ROW_ALIGN - 1) + WIN_ROWS
assert WIN_ROWS % ROW_ALIGN == 0 and (SORT_BUF - WIN_ROWS) % ROW_ALIGN == 0
VMEM_LIMIT = 56 * 1024 * 1024


def _sigmoid(x):
    return jax.nn.sigmoid(x)


def _layer_norm(h, g, b):
    mu = jnp.mean(h, axis=-1, keepdims=True)
    d = h - mu
    var = jnp.mean(d * d, axis=-1, keepdims=True)
    return d * lax.rsqrt(var + LN_EPS) * g + b


def _dot(a, b):
    return jnp.dot(a, b, preferred_element_type=F32)


def _inproj_kernel(x_ref, w_ref, cos_ref, sin_ref, cva_ref, cvb_ref, cvbb_ref, lng_ref, lnb_ref,
                   poolw_ref, pools_ref,
                   yabc_ref, qt_ref, qrt_ref, kcvc_ref, kk_ref, ksx_ref, vvt_ref, gt_ref,
                   ch_ext, u_ext, p_ext, u_sh):
    j = pl.program_id(1)
    ts = x_ref.shape[1]
    xb = x_ref[0].astype(BF16)

    def proj(lo, hi):
        return _dot(xb, w_ref[:, lo:hi])

    def carry_halo(ext, halo):
        @pl.when(j == 0)
        def _():
            ext[0:halo, :] = jnp.zeros((halo, ext.shape[1]), F32)

        @pl.when(j > 0)
        def _():
            ext[0:halo, :] = ext[ts:ts + halo, :]

    pa = proj(OFF_A, OFF_B)
    a_b, ch = pa[:, 0:256], pa[:, 256:512] * pa[:, 512:768]
    carry_halo(ch_ext, HALO_A)
    ch_ext[HALO_A:HALO_A + ts, :] = ch
    conv = cva_ref[2:3, :] * ch
    for k in range(SC_KERNEL - 1):
        off = HALO_A - (SC_KERNEL - 1) + k
        conv = conv + cva_ref[k:k + 1, :] * ch_ext[off:off + ts, :]
    yabc_ref[0, :, 0:256] = (a_b * conv).astype(BF16)

    pb = proj(OFF_B, OFF_C)
    u = pb[:, 0:256] * _sigmoid(pb[:, 256:512])
    carry_halo(u_ext, HALO_B)
    u_ext[HALO_B:HALO_B + ts, :] = u
    span = u_sh.shape[1]
    for r in range(1, 8):
        u_sh[r - 1, :, :] = u_ext[r:r + span, :]
    def mixer_c():
        pc = proj(OFF_C, OFF_Q)
        carry_halo(p_ext, HALO_C)
        p_ext[HALO_C:HALO_C + ts, :] = pc
        pe = p_ext[...]
        s2 = pe + pltpu.roll(pe, 1, 0)
        s4 = s2 + pltpu.roll(s2, 2, 0)
        s8 = s4 + pltpu.roll(s4, 4, 0)
        s16 = s8 + pltpu.roll(s8, 8, 0)
        lane = lax.broadcasted_iota(jnp.int32, (ts, 256), 1)
        grp = jnp.right_shift(lane, 6)
        wsum = jnp.where(grp == 0, s2[HALO_C:], jnp.where(grp == 1, s4[HALO_C:], jnp.where(grp == 2, s8[HALO_C:], s16[HALO_C:])))
        win = jnp.where(grp == 0, 2, jnp.where(grp == 1, 4, jnp.where(grp == 2, 8, 16)))
        t1 = j * ts + lax.broadcasted_iota(jnp.int32, (ts, 256), 0) + 1
        cnt = jnp.minimum(win, t1).astype(F32)
        dlt = wsum / cnt - pc
        yabc_ref[0, :, 512:768] = (_dot(dlt.astype(BF16), poolw_ref[...]) * pools_ref[...]).astype(BF16)

    def queries():
        cos4, sin4 = cos_ref[...], sin_ref[...]
        q = proj(OFF_Q, OFF_KCVC)
        lane_q = jnp.bitwise_and(lax.broadcasted_iota(jnp.int32, (ts, 256), 1), HEAD_DIM - 1)
        q_sw = jnp.where(lane_q < HEAD_DIM // 2, pltpu.roll(q, 256 - HEAD_DIM // 2, 1), pltpu.roll(q, HEAD_DIM // 2, 1))
        qr = q * cos4 + q_sw * sin4
        qt_ref[0] = (q * (QK_SCALE * LOG2E)).T.astype(BF16)
        qrt_ref[0] = (qr * (QK_SCALE * LOG2E)).T.astype(BF16)

    def compress_inputs():
        kcvc_ref[0] = proj(OFF_KCVC, OFF_KK)

    def keys():
        k2 = proj(OFF_KK, OFF_VV)
        lane_k = jnp.bitwise_and(lax.broadcasted_iota(jnp.int32, (ts, LANES), 1), HEAD_DIM - 1)
        k_sw = jnp.where(lane_k < HEAD_DIM // 2, pltpu.roll(k2, LANES - HEAD_DIM // 2, 1), pltpu.roll(k2, HEAD_DIM // 2, 1))
        k_rot = k2 * cos_ref[:, 0:LANES] + k_sw * sin_ref[:, 0:LANES]
        kk_ref[0] = k_rot.astype(BF16)
        lane_i = lax.broadcasted_iota(jnp.int32, (ts, LANES), 1)
        blk_in_tile = jnp.bitwise_and(jnp.right_shift(j * ts + lax.broadcasted_iota(jnp.int32, (ts, LANES), 0), 6),
                                      KT // SLC_BLOCK - 1)
        ksx_ref[0] = jnp.where(lane_i < HEAD_DIM, k_rot, jnp.where(lane_i - HEAD_DIM == blk_in_tile, 1.0, 0.0)).astype(BF16)

    def values():
        v2t = proj(OFF_VV, OFF_G).T
        for c in range(ts // LANES):
            vvt_ref[0, c] = v2t[:, c * LANES:(c + 1) * LANES].astype(BF16)

    def branch_gates():
        gt_ref[0] = _sigmoid(proj(OFF_G, IN_PAD)).T[0:16, :]

    interleaved = [mixer_c, queries, compress_inputs, keys, values, branch_gates]
    rows = 64
    assert ts // rows >= len(interleaved)
    for c in range(ts // rows):
        acc = jnp.zeros((rows, 256), F32) + cvbb_ref[...]
        for k in range(CF_KERNEL):
            off = HALO_B - (CF_KERNEL - 1) + k
            base = off // 8 * 8 + c * rows
            src = u_ext if off % 8 == 0 else u_sh.at[off % 8 - 1]
            acc = acc + cvb_ref[k:k + 1, :] * src[base:base + rows, :]
        v = _layer_norm(acc, lng_ref[...], lnb_ref[...])
        yabc_ref[0, c * rows:(c + 1) * rows, 256:512] = (v * _sigmoid(v)).astype(BF16)
        if c < len(interleaved):
            interleaved[c]()


def _inproj(x, w_in_p, cos4, sin4, cva, cvb, cvbb, lng, lnb, poolw, pools):
    b, s, d = x.shape
    ts = TS_IN
    grid = (b, s // ts)
    const = lambda shape: pl.BlockSpec(shape, lambda bi, ji: (0,) * len(shape))
    out_shape = (
        jax.ShapeDtypeStruct((b, s, 768), BF16),
        jax.ShapeDtypeStruct((b, 256, s), BF16),
        jax.ShapeDtypeStruct((b, 256, s), BF16),
        jax.ShapeDtypeStruct((b, s, LANES), F32),
        jax.ShapeDtypeStruct((b, s, LANES), BF16),
        jax.ShapeDtypeStruct((b, s, LANES), BF16),
        jax.ShapeDtypeStruct((b, s // LANES, LANES, LANES), BF16),
        jax.ShapeDtypeStruct((b, 16, s), F32),
    )
    return pl.pallas_call(
        _inproj_kernel,
        grid=grid,
        in_specs=[
            pl.BlockSpec((1, ts, d), lambda bi, ji: (bi, ji, 0)),
            const((d, IN_PAD)),
            pl.BlockSpec((ts, 256), lambda bi, ji: (ji, 0)),
            pl.BlockSpec((ts, 256), lambda bi, ji: (ji, 0)),
            const((SC_KERNEL, 256)), const((CF_KERNEL, 256)), const((1, 256)), const((1, 256)), const((1, 256)),
            const((256, 256)), const((1, 256)),
        ],
        out_specs=(
            pl.BlockSpec((1, ts, 768), lambda bi, ji: (bi, ji, 0)),
            pl.BlockSpec((1, 256, ts), lambda bi, ji: (bi, 0, ji)),
            pl.BlockSpec((1, 256, ts), lambda bi, ji: (bi, 0, ji)),
            pl.BlockSpec((1, ts, LANES), lambda bi, ji: (bi, ji, 0)),
            pl.BlockSpec((1, ts, LANES), lambda bi, ji: (bi, ji, 0)),
            pl.BlockSpec((1, ts, LANES), lambda bi, ji: (bi, ji, 0)),
            pl.BlockSpec((1, ts // LANES, LANES, LANES), lambda bi, ji: (bi, ji, 0, 0)),
            pl.BlockSpec((1, 16, ts), lambda bi, ji: (bi, 0, ji)),
        ),
        out_shape=out_shape,
        scratch_shapes=[
            pltpu.VMEM((ts + HALO_A, 256), F32),
            pltpu.VMEM((ts + HALO_B, 256), F32),
            pltpu.VMEM((ts + HALO_C, 256), F32),
            pltpu.VMEM((7, ts + HALO_B - 8, 256), F32),
        ],
        compiler_params=pltpu.CompilerParams(dimension_semantics=("arbitrary", "arbitrary"),
                                             vmem_limit_bytes=VMEM_LIMIT),
        name="inproj_mixers",
    )(x, w_in_p, cos4, sin4, cva, cvb, cvbb, lng, lnb, poolw, pools)


def _compress_kernel(kc_ref, vc_ref, pek_ref, pev_ref, wk1_ref, wk2_ref, wv1_ref, wv2_ref, kcmp_ref, vcmpt_ref):
    def comp(x_ref, pe_ref, w1_ref, w2_ref):
        x = x_ref[0]
        m = x.shape[0]
        half = x.shape[1]
        first = _dot((x + pe_ref[0:1, :]).astype(BF16), w1_ref[0:half, :])
        second = _dot((x + pe_ref[1:2, :]).astype(BF16), w1_ref[half:2 * half, :])
        hid = first + pltpu.roll(second, m - 1, 0)
        return _dot((hid * _sigmoid(hid)).astype(BF16), w2_ref[...])

    kcmp_ref[0] = comp(kc_ref, pek_ref, wk1_ref, wk2_ref).astype(BF16)
    vcmpt_ref[0] = comp(vc_ref, pev_ref, wv1_ref, wv2_ref).T[0:HEAD_DIM, :].astype(BF16)


def _compress(kc16, vc16, pek, pev, wk1, wk2, wv1, wv2):
    b, m, w = kc16.shape
    const = lambda shape: pl.BlockSpec(shape, lambda bi: (0,) * len(shape))
    return pl.pallas_call(
        _compress_kernel,
        grid=(b,),
        in_specs=[
            pl.BlockSpec((1, m, w), lambda bi: (bi, 0, 0)),
            pl.BlockSpec((1, m, w), lambda bi: (bi, 0, 0)),
            const((2, w)), const((2, w)),
            const((2 * w, CMP_HIDDEN)), const((CMP_HIDDEN, LANES)),
            const((2 * w, CMP_HIDDEN)), const((CMP_HIDDEN, LANES)),
        ],
        out_specs=(
            pl.BlockSpec((1, m, LANES), lambda bi: (bi, 0, 0)),
            pl.BlockSpec((1, HEAD_DIM, m), lambda bi: (bi, 0, 0)),
        ),
        out_shape=(jax.ShapeDtypeStruct((b, m, LANES), BF16), jax.ShapeDtypeStruct((b, HEAD_DIM, m), BF16)),
        compiler_params=pltpu.CompilerParams(dimension_semantics=("arbitrary",), vmem_limit_bytes=VMEM_LIMIT),
        name="compress_kv",
    )(kc16, vc16, pek, pev, wk1, wk2, wv1, wv2)


def _nsa_kernel(qt_ref, qrt_ref, gt_ref, kk_ref, ksx_ref, vvt_ref, kcmp_ref, vcmpt_ref, ovt_ref, yd_ref, sel_scr, sa_scr, sb_scr, pa_scr, pb_scr, diag_scr, acc_scr):
    qb = pl.program_id(1)
    tq = qt_ref.shape[2]
    nh = NSA_HEADS
    wq = nh * tq
    s0 = qb * tq
    n_slc = sel_scr.shape[0]
    n_cmp_pad = kcmp_ref.shape[1]

    def stack_heads(ref):
        x = ref[0]
        return jnp.concatenate([x[h * HEAD_DIM:(h + 1) * HEAD_DIM, :] for h in range(nh)], axis=1)

    zeros = jnp.zeros((HEAD_DIM, wq), BF16)
    q_top, qr_top = stack_heads(qt_ref), stack_heads(qrt_ref)
    q_lo = jnp.concatenate([q_top, zeros], axis=0)
    qr_lo = jnp.concatenate([qr_top, zeros], axis=0)
    qr_hi = jnp.concatenate([zeros, qr_top], axis=0)
    t_row = s0 + lax.broadcasted_iota(jnp.int32, (1, tq), 1)
    lanes4 = lambda x: jnp.concatenate([x] * nh, axis=1)
    t_all = lanes4(t_row)

    size_classes = 4
    class_rows = n_cmp_pad // size_classes
    assert CMP_STRIDE * class_rows >= tq >= 2 * SLC_BLOCK

    def compress_and_select(nc):
        nb = nc // (SLC_BLOCK // CMP_STRIDE)
        cmp_end = lax.broadcasted_iota(jnp.int32, (nc, tq), 0) * CMP_STRIDE + (CMP_BLOCK - 1)
        sc = _dot(kcmp_ref[0, 0:nc, :], q_lo) + lanes4(jnp.where(cmp_end <= t_row, 0.0, NEG))
        mx = jnp.max(sc, axis=0, keepdims=True)
        ex = jnp.exp2(sc - mx)
        den = jnp.sum(ex, axis=0, keepdims=True)
        p_cmp = ex * jnp.where(t_all >= CMP_BLOCK - 1, 1.0 / den, 0.0)
        o_cmp = _dot(vcmpt_ref[0, :, 0:nc], p_cmp.astype(BF16))
        p_heads = p_cmp[:, 0:tq]
        for h in range(1, nh):
            p_heads = p_heads + p_cmp[:, h * tq:(h + 1) * tq]
        imp = _dot(ovt_ref[0:nb, 0:nc], p_heads.astype(BF16))

        blk_i = lax.broadcasted_iota(jnp.int32, (nb, tq), 0)
        blk = blk_i.astype(F32)
        cur = jnp.right_shift(t_row, 6)
        forced = (blk_i == 0) | (blk_i == cur) | (blk_i == cur - 1)
        valid = blk_i * SLC_BLOCK <= t_row
        sel = jnp.where(forced, 1.0, 0.0)
        n_forced = jnp.sum(sel, axis=0, keepdims=True)
        imp = jnp.where(valid, jnp.where(forced, -jnp.inf, imp), NEG)
        min_forced = 3 if nc > class_rows else 1
        for r in range(min(SLC_TOPN - min_forced, nb)):
            top = jnp.max(imp, axis=0, keepdims=True)
            first = jnp.min(jnp.where(imp == top, blk, float(nb)), axis=0, keepdims=True)
            pick = (blk == first) & (n_forced + r < SLC_TOPN)
            sel = jnp.where(pick, 1.0, sel)
            imp = jnp.where(pick, -jnp.inf, imp)
        bias = jnp.where(valid & (sel > 0.0), 0.0, NEG)
        own0 = qb * (tq // SLC_BLOCK)
        own = (blk_i >= own0) & (blk_i < own0 + tq // SLC_BLOCK)
        sel_scr[0:nb, :] = jnp.where(own, NEG, bias)
        if nb < n_slc:
            sel_scr[nb:n_slc, :] = jnp.full((n_slc - nb, tq), NEG, F32)
        key_blk = own0 + jnp.right_shift(lax.broadcasted_iota(jnp.int32, (tq, nb), 0), 6)
        expand = jnp.where(lax.broadcasted_iota(jnp.int32, (tq, nb), 1) == key_blk, 1.0, 0.0).astype(BF16)
        diag_scr[...] = _dot(expand, bias.astype(BF16))
        return o_cmp

    visible = (s0 + tq) // CMP_STRIDE
    o_cmp = lax.switch((visible - 1) // class_rows,
                       [functools.partial(compress_and_select, (i + 1) * class_rows) for i in range(size_classes)])

    blocks_per_tile = KT // SLC_BLOCK
    sub = KT // LANES

    last_tile = kk_ref.shape[1] // KT - 1

    def score_chunks(kt):
        kd = jnp.minimum(kt, last_tile)
        bias8 = sel_scr[pl.ds(pl.multiple_of(kd * blocks_per_tile, blocks_per_tile), blocks_per_tile), :]
        bias8 = jnp.where(kt <= last_tile, bias8, NEG)
        pad = [jnp.zeros((16 - blocks_per_tile, wq), F32)] if blocks_per_tile < 16 else []
        bias_rows = jnp.concatenate([lanes4(bias8)] + pad, axis=0).astype(BF16)
        q_bias = jnp.concatenate([qr_top, bias_rows, jnp.zeros((HEAD_DIM - 16, wq), BF16)], axis=0)
        s_tile = _dot(ksx_ref[0, pl.ds(pl.multiple_of(kd * KT, KT), KT), :], q_bias)
        for c in range(sub):
            yield c, s_tile[c * LANES:(c + 1) * LANES, :]

    def fold8(x, op):
        out = x[0:8, :]
        for r in range(1, x.shape[0] // 8):
            out = op(out, x[8 * r:8 * r + 8, :])
        return out

    def v_rows(first, count, lo):
        vt = vvt_ref[0, pl.ds(first, count)]
        return jnp.concatenate([vt[c, lo:lo + HEAD_DIM, :] for c in range(count)], axis=1)

    def pv_tile(kt, p_ref):
        return _dot(v_rows(pl.multiple_of(jnp.minimum(kt, last_tile) * sub, sub), sub, 0), p_ref[...])

    def phase(kt, s_cur, s_next, p_cur, p_prev, state):
        m_i, l_i, mt_cur, alpha_prev = state
        m_new = jnp.maximum(m_i, jnp.max(mt_cur, axis=0, keepdims=True))
        alpha = jnp.exp2(m_i - m_new)
        mt_next = jnp.full((8, wq), NEG, F32)
        l_new = alpha * l_i
        for c, s_n in score_chunks(kt + 1):
            rows = slice(c * LANES, (c + 1) * LANES)
            mt_next = jnp.maximum(mt_next, fold8(s_n, jnp.maximum))
            s_next[rows, :] = s_n
            p = jnp.exp2(s_cur[rows, :] - m_new)
            l_new = l_new + fold8(p, jnp.add)
            p_cur[rows, :] = p.astype(BF16)
        if p_prev is not None:
            acc_scr[...] = alpha_prev * acc_scr[...] + pv_tile(kt - 1, p_prev)
        return m_new, l_new, mt_next, alpha

    def slc_pair(j, state):
        state = phase(2 * j + 1, sb_scr, sa_scr, pb_scr, pa_scr, state)
        return phase(2 * j + 2, sa_scr, sb_scr, pa_scr, pb_scr, state)

    mt0 = jnp.full((8, wq), NEG, F32)
    for c, s_n in score_chunks(0):
        mt0 = jnp.maximum(mt0, fold8(s_n, jnp.maximum))
        sa_scr[c * LANES:(c + 1) * LANES, :] = s_n
    key_d = s0 + lax.broadcasted_iota(jnp.int32, (tq, tq), 0)
    s_d = (_dot(kk_ref[0, pl.ds(pl.multiple_of(s0, tq), tq), :], qr_lo)
           + lanes4(jnp.where(key_d <= t_row, diag_scr[...], NEG)))
    m_d = jnp.max(s_d, axis=0, keepdims=True)
    p_d = jnp.exp2(s_d - m_d)
    acc_scr[...] = _dot(v_rows(qb * (tq // LANES), tq // LANES, 0), p_d.astype(BF16))
    state = (m_d, fold8(p_d, jnp.add), mt0, jnp.ones((1, wq), F32))
    state = phase(0, sa_scr, sb_scr, pa_scr, None, state)
    n_tiles = (s0 + tq + KT - 1) // KT
    pairs = n_tiles // 2
    _, l8_s, _, alpha_last = lax.fori_loop(0, pairs, slc_pair, state)
    acc_s = alpha_last * acc_scr[...] + pv_tile(2 * pairs, pa_scr)
    l_s = jnp.sum(l8_s, axis=0, keepdims=True)

    wkeys = WIN + tq
    k0 = pl.multiple_of(jnp.maximum(s0 - WIN, 0), LANES)
    key = k0 + lax.broadcasted_iota(jnp.int32, (wkeys, tq), 0)
    wbias = jnp.where((key <= t_row) & (key > t_row - WIN), 0.0, NEG)
    s_w = _dot(kk_ref[0, pl.ds(k0, wkeys), :], qr_hi) + lanes4(wbias)
    vt_w = vvt_ref[0, pl.ds(jnp.maximum(qb * (tq // LANES) - WIN // LANES, 0), wkeys // LANES)]
    p_w = jnp.exp2(s_w - jnp.max(s_w, axis=0, keepdims=True))
    l_w = jnp.sum(p_w, axis=0, keepdims=True)
    v_w = jnp.concatenate([vt_w[i, HEAD_DIM:2 * HEAD_DIM, :] for i in range(wkeys // LANES)], axis=1)
    acc_w = _dot(v_w, p_w.astype(BF16))

    g = gt_ref[0]
    gate = lambda br: jnp.concatenate([g[h * 3 + br:h * 3 + br + 1, :] for h in range(nh)], axis=1)
    o = gate(0) * o_cmp + gate(1) * (acc_s * (1.0 / l_s)) + gate(2) * (acc_w * (1.0 / l_w))
    o_rows = jnp.concatenate([o[:, h * tq:(h + 1) * tq] for h in range(nh)], axis=0)
    yd_ref[0] = o_rows.T.astype(BF16)


def _nsa(qt, qrt, gt, kk, ksx, vvt, kcmp, vcmpt, ovt):
    b, _, s = qt.shape
    n_slc = s // SLC_BLOCK
    m = kcmp.shape[1]
    return pl.pallas_call(
        _nsa_kernel,
        grid=(b, s // TQ),
        in_specs=[
            pl.BlockSpec((1, 256, TQ), lambda bi, qi: (bi, 0, qi)),
            pl.BlockSpec((1, 256, TQ), lambda bi, qi: (bi, 0, qi)),
            pl.BlockSpec((1, 16, TQ), lambda bi, qi: (bi, 0, qi)),
            pl.BlockSpec((1, s, LANES), lambda bi, qi: (bi, 0, 0)),
            pl.BlockSpec((1, s, LANES), lambda bi, qi: (bi, 0, 0)),
            pl.BlockSpec((1, s // LANES, LANES, LANES), lambda bi, qi: (bi, 0, 0, 0)),
            pl.BlockSpec((1, m, LANES), lambda bi, qi: (bi, 0, 0)),
            pl.BlockSpec((1, HEAD_DIM, m), lambda bi, qi: (bi, 0, 0)),
            pl.BlockSpec((n_slc, m), lambda bi, qi: (0, 0)),
        ],
        out_specs=pl.BlockSpec((1, TQ, 256), lambda bi, qi: (bi, qi, 0)),
        out_shape=jax.ShapeDtypeStruct((b, s, 256), BF16),
        scratch_shapes=[pltpu.VMEM((n_slc, TQ), F32),
                        pltpu.VMEM((KT, NSA_HEADS * TQ), F32), pltpu.VMEM((KT, NSA_HEADS * TQ), F32),
                        pltpu.VMEM((KT, NSA_HEADS * TQ), BF16), pltpu.VMEM((KT, NSA_HEADS * TQ), BF16),
                        pltpu.VMEM((TQ, TQ), F32), pltpu.VMEM((HEAD_DIM, NSA_HEADS * TQ), F32)],
        compiler_params=pltpu.CompilerParams(dimension_semantics=("arbitrary", "arbitrary"),
                                             vmem_limit_bytes=VMEM_LIMIT),
        name="sparse_attention",
    )(qt, qrt, gt, kk, ksx, vvt, kcmp, vcmpt, ovt)


def _outproj_kernel(x_ref, yabc_ref, yd_ref, wo_ref, g_ref, b_ref, rw_ref, rb_ref, x1_ref, gates_ref):
    mix = _dot(yabc_ref[...], wo_ref[0:768, :]) + _dot(yd_ref[...], wo_ref[768:1024, :])
    x1 = _layer_norm(DN_ALPHA * x_ref[...] + mix, g_ref[...], b_ref[...])
    x1_ref[...] = x1

    ts = x1.shape[0]
    aff = _sigmoid(_dot(x1.astype(BF16), rw_ref[...])).T[0:N_EXPERTS, :]
    biased = aff + rb_ref[...]
    row = lambda a, i: a[i:i + 1, :]

    gscores = []
    for gi in range(N_GROUPS):
        a, b, c, d = (row(biased, gi * EXPERTS_PER_GROUP + i) for i in range(EXPERTS_PER_GROUP))
        hi1, lo1, hi2, lo2 = jnp.maximum(a, b), jnp.minimum(a, b), jnp.maximum(c, d), jnp.minimum(c, d)
        gscores.append(jnp.maximum(hi1, hi2) + jnp.maximum(jnp.minimum(hi1, hi2), jnp.maximum(lo1, lo2)))
    best, gsel = gscores[0], jnp.zeros((1, ts), jnp.int32)
    for gi in range(1, N_GROUPS):
        better = gscores[gi] > best
        gsel = jnp.where(better, gi, gsel)
        best = jnp.where(better, gscores[gi], best)

    eid_i = lax.broadcasted_iota(jnp.int32, (N_EXPERTS, ts), 0)
    eid = eid_i.astype(F32)
    masked = jnp.where(jnp.right_shift(eid_i, 2) == gsel, biased, NEG)
    picks = []
    for _ in range(2):
        top = jnp.max(masked, axis=0, keepdims=True)
        first = jnp.min(jnp.where(masked == top, eid, float(N_EXPERTS)), axis=0, keepdims=True)
        pick = eid == first
        picks.append(pick)
        masked = jnp.where(pick, -jnp.inf, masked)
    chosen = picks[0] | picks[1]
    w_sel = jnp.where(chosen, aff, 0.0)
    gates_t = w_sel / jnp.sum(w_sel, axis=0, keepdims=True)
    grp8 = jnp.where(lax.broadcasted_iota(jnp.int32, (8, ts), 0) == 0, gsel.astype(F32), 0.0)
    gates_ref[...] = jnp.concatenate([gates_t, grp8, jnp.zeros((LANES - N_EXPERTS - 8, ts), F32)], axis=0).T


def _outproj(x2d, yabc, yd, wo, g, b, rw, rb):
    n, d = x2d.shape
    ts = TS_OUT
    const = lambda shape: pl.BlockSpec(shape, lambda i: (0,) * len(shape))
    return pl.pallas_call(
        _outproj_kernel,
        grid=(n // ts,),
        in_specs=[
            pl.BlockSpec((ts, d), lambda i: (i, 0)),
            pl.BlockSpec((ts, 768), lambda i: (i, 0)),
            pl.BlockSpec((ts, 256), lambda i: (i, 0)),
            const((d, d)), const((1, d)), const((1, d)), const((d, LANES)), const((N_EXPERTS, 1)),
        ],
        out_specs=(pl.BlockSpec((ts, d), lambda i: (i, 0)), pl.BlockSpec((ts, LANES), lambda i: (i, 0))),
        out_shape=(jax.ShapeDtypeStruct((n, d), F32), jax.ShapeDtypeStruct((n, LANES), F32)),
        compiler_params=pltpu.CompilerParams(dimension_semantics=("arbitrary",), vmem_limit_bytes=VMEM_LIMIT),
        name="outproj_ln_route",
    )(x2d, yabc, yd, wo, g, b, rw, rb)


def _split3(x):
    hi = x.astype(BF16)
    r1 = x - hi.astype(F32)
    mid = r1.astype(BF16)
    lo = (r1 - mid.astype(F32)).astype(BF16)
    return hi, mid, lo


def _dispatch_kernel(x1_ref, gates_ref, tri_ref, xs_ref, gs_ref, post_ref, cnt_ref):
    tm = x1_ref.shape[0]
    gates = gates_ref[...]
    gsel = gates.T[GROUP_LANE:GROUP_LANE + 1, :]
    onehot = jnp.where(lax.broadcasted_iota(jnp.int32, (8, tm), 0).astype(F32) == gsel, 1.0, 0.0)
    n = jnp.sum(onehot, axis=1, keepdims=True)
    n_al = jnp.floor((n + (ROW_ALIGN - 1)) * (1.0 / ROW_ALIGN)) * ROW_ALIGN
    starts = [jnp.zeros((1, 1), F32)]
    for gi in range(1, N_GROUPS):
        starts.append(starts[-1] + n_al[gi - 1:gi, :])
    start = jnp.concatenate(starts + [jnp.zeros((8 - N_GROUPS, 1), F32)], axis=0)
    before = _dot(onehot.astype(BF16), tri_ref[...])
    pos = jnp.sum(onehot * (start + before), axis=0, keepdims=True)
    perm = jnp.where(lax.broadcasted_iota(jnp.int32, (SORT_ROWS, tm), 0).astype(F32) == pos, 1.0, 0.0).astype(BF16)

    xs_ref[0, 0:SORT_ROWS, :] = _dot(perm, x1_ref[...].astype(BF16)).astype(BF16)
    xs_ref[0, SORT_ROWS:SORT_BUF, :] = jnp.zeros((SORT_BUF - SORT_ROWS, xs_ref.shape[2]), BF16)
    g_hi, g_mid, g_lo = _split3(gates)
    gs_ref[0, 0:SORT_ROWS, :] = _dot(perm, g_hi) + _dot(perm, g_mid) + _dot(perm, g_lo)
    gs_ref[0, SORT_ROWS:SORT_BUF, :] = jnp.zeros((SORT_BUF - SORT_ROWS, LANES), F32)
    pos8 = jnp.where(lax.broadcasted_iota(jnp.int32, (8, tm), 0) == 0, pos, 0.0)
    post_ref[...] = jnp.concatenate([pos8, jnp.zeros((LANES - 8, tm), F32)], axis=0).T
    cnt_ref[0] = jnp.broadcast_to(n, (8, LANES))


def _dispatch(x1, gates, tri):
    n, d = x1.shape
    tm = TM_MOE
    nt = n // tm
    return pl.pallas_call(
        _dispatch_kernel,
        grid=(nt,),
        in_specs=[
            pl.BlockSpec((tm, d), lambda i: (i, 0)),
            pl.BlockSpec((tm, LANES), lambda i: (i, 0)),
            pl.BlockSpec((tm, tm), lambda i: (0, 0)),
        ],
        out_specs=(
            pl.BlockSpec((1, SORT_BUF, d), lambda i: (i, 0, 0)),
            pl.BlockSpec((1, SORT_BUF, LANES), lambda i: (i, 0, 0)),
            pl.BlockSpec((tm, LANES), lambda i: (i, 0)),
            pl.BlockSpec((1, 8, LANES), lambda i: (i, 0, 0)),
        ),
        out_shape=(
            jax.ShapeDtypeStruct((nt, SORT_BUF, d), BF16),
            jax.ShapeDtypeStruct((nt, SORT_BUF, LANES), F32),
            jax.ShapeDtypeStruct((n, LANES), F32),
            jax.ShapeDtypeStruct((nt, 8, LANES), F32),
        ),
        compiler_params=pltpu.CompilerParams(dimension_semantics=("arbitrary",), vmem_limit_bytes=VMEM_LIMIT),
        name="moe_dispatch",
    )(x1, gates, tri)


def _moe_kernel(start_ref, trips_ref, xs_ref, gs_ref, wg_ref, wu_ref, wd_ref, acc_ref):
    i, e = pl.program_id(0), pl.program_id(1)

    @pl.when(e == 0)
    def _():
        acc_ref[...] = jnp.zeros(acc_ref.shape, F32)

    grp = lax.shift_right_logical(e, 2)
    slots = [(i * TILES_PER_STEP + tile) * N_GROUPS + grp for tile in range(TILES_PER_STEP)]
    row0 = [start_ref[s] for s in slots]
    need = [trips_ref[s] for s in slots]

    def window(w, carry):
        rows = [pl.ds(pl.multiple_of(jnp.minimum(r + w * WIN_ROWS, SORT_BUF - WIN_ROWS), ROW_ALIGN), WIN_ROWS)
                for r in row0]
        xw = jnp.concatenate([xs_ref[tile, rows[tile], :] for tile in range(TILES_PER_STEP)], axis=0)
        hg = _dot(xw, wg_ref[0])
        h = hg * _sigmoid(hg) * _dot(xw, wu_ref[0])
        y = _dot(h.astype(BF16), wd_ref[0])
        for tile in range(TILES_PER_STEP):
            gw = gs_ref[tile, rows[tile], :]
            lane = lax.broadcasted_iota(jnp.int32, gw.shape, 1)
            gcol = jnp.sum(jnp.where((lane == e) & (w < need[tile]), gw, 0.0), axis=1, keepdims=True)
            acc_ref[tile, rows[tile], :] += y[tile * WIN_ROWS:(tile + 1) * WIN_ROWS, :] * gcol
        return carry

    lax.fori_loop(0, functools.reduce(jnp.maximum, need), window, 0)


def _moe(start, trips, xs, gs, wg, wu, wd):
    nt, _, d = xs.shape
    ne, _, de = wg.shape
    tps = TILES_PER_STEP
    grid_spec = pltpu.PrefetchScalarGridSpec(
        num_scalar_prefetch=2,
        grid=(nt // tps, ne),
        in_specs=[
            pl.BlockSpec((tps, SORT_BUF, d), lambda i, e, *_: (i, 0, 0)),
            pl.BlockSpec((tps, SORT_BUF, LANES), lambda i, e, *_: (i, 0, 0)),
            pl.BlockSpec((1, d, de), lambda i, e, *_: (e, 0, 0)),
            pl.BlockSpec((1, d, de), lambda i, e, *_: (e, 0, 0)),
            pl.BlockSpec((1, de, d), lambda i, e, *_: (e, 0, 0)),
        ],
        out_specs=pl.BlockSpec((tps, SORT_BUF, d), lambda i, e, *_: (i, 0, 0)),
    )
    return pl.pallas_call(
        _moe_kernel,
        grid_spec=grid_spec,
        out_shape=jax.ShapeDtypeStruct((nt, SORT_BUF, d), F32),
        compiler_params=pltpu.CompilerParams(dimension_semantics=("arbitrary", "arbitrary"),
                                             vmem_limit_bytes=VMEM_LIMIT),
        name="moe_experts",
    )(start, trips, xs, gs, wg, wu, wd)


def _combine_kernel(x1_ref, acc_ref, post_ref, g_ref, b_ref, out_ref):
    tm = x1_ref.shape[0]
    acc = acc_ref[0]
    hi = acc.astype(BF16)
    lo = (acc - hi.astype(F32)).astype(BF16)
    pos = post_ref[...][:, 0:1]
    unperm = jnp.where(lax.broadcasted_iota(jnp.int32, (tm, SORT_ROWS), 1).astype(F32) == pos, 1.0, 0.0).astype(BF16)
    moe = _dot(unperm, hi) + _dot(unperm, lo)
    out_ref[...] = _layer_norm(DN_ALPHA * x1_ref[...] + moe, g_ref[...], b_ref[...])


def _combine(x1, acc, post, g, b):
    n, d = x1.shape
    tm = TM_MOE
    const = lambda shape: pl.BlockSpec(shape, lambda i: (0,) * len(shape))
    return pl.pallas_call(
        _combine_kernel,
        grid=(n // tm,),
        in_specs=[
            pl.BlockSpec((tm, d), lambda i: (i, 0)),
            pl.BlockSpec((1, SORT_ROWS, d), lambda i: (i, 0, 0)),
            pl.BlockSpec((tm, LANES), lambda i: (i, 0)),
            const((1, d)), const((1, d)),
        ],
        out_specs=pl.BlockSpec((tm, d), lambda i: (i, 0)),
        out_shape=jax.ShapeDtypeStruct((n, d), F32),
        compiler_params=pltpu.CompilerParams(dimension_semantics=("arbitrary",), vmem_limit_bytes=VMEM_LIMIT),
        name="moe_combine_ln",
    )(x1, acc, post, g, b)


def _group_windows(cnt):
    n = cnt[:, 0:N_GROUPS, 0].astype(jnp.int32)
    n_al = (n + (ROW_ALIGN - 1)) // ROW_ALIGN * ROW_ALIGN
    start = jnp.cumsum(n_al, axis=1) - n_al
    trips = (n + (WIN_ROWS - 1)) // WIN_ROWS
    return start.reshape(-1), trips.reshape(-1)


def _permute_w_in(w):
    sizes = (256, 256, 256, 256, 256, 256, 256, 64, 64, 64, 64, 64, 64, 12)
    offs = np.concatenate([[0], np.cumsum(sizes)])
    part = lambda i: w[:, int(offs[i]):int(offs[i + 1])]
    a_b, a_c, a_h, b_val, b_gate, c_p, q, kc, vc, ks, vs, kw, vw, g = (part(i) for i in range(14))
    pad = jnp.zeros((w.shape[0], LANES - g.shape[1]), w.dtype)
    return jnp.concatenate([a_b, a_c, a_h, b_val, b_gate, c_p, q, kc, vc, ks, kw, vs, vw, g, pad], axis=1).astype(BF16)


def _rope_tables(s):
    inv = 1.0 / (ROPE_THETA ** (jnp.arange(0, HEAD_DIM, 2, dtype=F32) / HEAD_DIM))
    ang = jnp.arange(s, dtype=F32)[:, None] * inv[None, :]
    ang = jnp.concatenate([ang, ang], -1)
    sign = jnp.concatenate([-jnp.ones((HEAD_DIM // 2,), F32), jnp.ones((HEAD_DIM // 2,), F32)])
    return jnp.tile(jnp.cos(ang), (1, NSA_HEADS)), jnp.tile(jnp.sin(ang) * sign[None, :], (1, NSA_HEADS))


def _overlap_t(s, m):
    n_cmp = (s - CMP_BLOCK) // CMP_STRIDE + 1
    cmp_start = np.arange(m) * CMP_STRIDE
    slc_start = np.arange(s // SLC_BLOCK) * SLC_BLOCK
    ov = (cmp_start[None, :] < slc_start[:, None] + SLC_BLOCK) & (cmp_start[None, :] + CMP_BLOCK > slc_start[:, None])
    ov = ov & (np.arange(m)[None, :] < n_cmp)
    return jnp.asarray(ov, BF16)


def _block_diag(w):
    g, c, _ = w.shape
    out = jnp.zeros((g * c, g * c), w.dtype)
    for i in range(g):
        out = out.at[i * c:(i + 1) * c, i * c:(i + 1) * c].set(w[i])
    return out


def kernel(x, w_in, conv_a_w, conv_b_w, conv_b_b, cf_ln_g, cf_ln_b, pool_w, pool_scale, cmp_pe_k, cmp_pe_v, cmp_k_w1, cmp_k_w2, cmp_v_w1, cmp_v_w2, w_out, ln1_g, ln1_b, ln2_g, ln2_b, router_w, router_bias, moe_w_gate, moe_w_up, moe_w_down):
    b, s, d = x.shape
    n = b * s
    m = s // CMP_STRIDE
    tok_per_row = CMP_STRIDE
    cos4, sin4 = _rope_tables(s)
    ovt = _overlap_t(s, m)
    rw = jnp.concatenate([router_w, jnp.zeros((d, LANES - N_EXPERTS), router_w.dtype)], axis=1).astype(BF16)
    rb = router_bias.reshape(N_EXPERTS, 1).astype(F32)
    pad_w2 = lambda w: jnp.concatenate([w, jnp.zeros((CMP_HIDDEN, LANES - HEAD_DIM), w.dtype)], axis=1).astype(BF16)
    row = lambda v: v.reshape(1, -1)
    tri = jnp.asarray(np.triu(np.ones((TM_MOE, TM_MOE), np.float32), k=1), BF16)

    for l in range(DEPTH):
        yabc, qt, qrt, kcvc, kk, ksx, vvt, gt = _inproj(
            x, _permute_w_in(w_in[l]), cos4, sin4, conv_a_w[l], conv_b_w[l], row(conv_b_b[l]),
            row(cf_ln_g[l]), row(cf_ln_b[l]), _block_diag(pool_w[l]).astype(BF16), row(pool_scale[l]))
        kc16 = kcvc[:, :, 0:HEAD_DIM].reshape(b, m, tok_per_row * HEAD_DIM)
        vc16 = kcvc[:, :, HEAD_DIM:2 * HEAD_DIM].reshape(b, m, tok_per_row * HEAD_DIM)
        kcmp, vcmpt = _compress(
            kc16, vc16, cmp_pe_k[l].reshape(2, tok_per_row * HEAD_DIM), cmp_pe_v[l].reshape(2, tok_per_row * HEAD_DIM),
            cmp_k_w1[l].astype(BF16), pad_w2(cmp_k_w2[l]), cmp_v_w1[l].astype(BF16), pad_w2(cmp_v_w2[l]))
        yd = _nsa(qt, qrt, gt, kk, ksx, vvt, kcmp, vcmpt, ovt)
        x1, gates = _outproj(x.reshape(n, d), yabc.reshape(n, 768), yd.reshape(n, 256), w_out[l].astype(BF16),
                             row(ln1_g[l]), row(ln1_b[l]), rw, rb)
        xs, gs, post, cnt = _dispatch(x1, gates, tri)
        start, trips = _group_windows(cnt)
        acc = _moe(start, trips, xs, gs, moe_w_gate[l].astype(BF16), moe_w_up[l].astype(BF16),
                   moe_w_down[l].astype(BF16))
        x = _combine(x1, acc, post, row(ln2_g[l]), row(ln2_b[l])).reshape(b, s, d)
    return x
```

```python
import functools

import jax
import jax.numpy as jnp
import numpy as np
from jax import lax
from jax.experimental import pallas as pl
from jax.experimental.pallas import tpu as pltpu

F32 = jnp.float32
BF16 = jnp.bfloat16

D_MODEL = 1024
DEPTH = 2
GROUP_W = 256
HEAD_DIM = 64
SC_KERNEL = 3
CF_KERNEL = 31
POOL_WINDOWS = (2, 4, 8, 16)
POOL_GW = 64
NSA_HEADS = 4
CMP_BLOCK = 32
CMP_STRIDE = 16
CMP_HIDDEN = 256
SLC_BLOCK = 64
SLC_TOPN = 16
WIN = 512
N_EXPERTS = 16
N_GROUPS = 4
EXPERTS_PER_GROUP = 4
D_EXPERT = 512
DN_ALPHA = (2 * DEPTH) ** 0.25
LN_EPS = 1e-5
NEG = -1e30
FORCE = 1e4
ROPE_THETA = 10000.0
QK_SCALE = HEAD_DIM ** -0.5
LOG2E = 1.4426950408889634

LANES = 128
OFF_A, OFF_B, OFF_C, OFF_Q, OFF_KCVC, OFF_KK, OFF_VV, OFF_G, IN_PAD = 0, 768, 1280, 1536, 1792, 1920, 2048, 2176, 2304

HALO_A = 8
HALO_B = 32
HALO_C = 16

TS_IN = 512
TQ = 256
KT = 512
TS_OUT = 512
TM_MOE = 1024
ROW_ALIGN = 16
WIN_ROWS = 288
SORT_ROWS = TM_MOE + LANES
SORT_BUF = SORT_ROWS + 2 * LANES
GROUP_LANE = 16
TILES_PER_STEP = 2
assert SORT_ROWS >= TM_MOE + N_GROUPS * (ROW_ALIGN - 1)
assert SORT_BUF >= TM_MOE + N_GROUPS * (ROW_ALIGN - 1) + WIN_ROWS
assert WIN_ROWS % ROW_ALIGN == 0 and (SORT_BUF - WIN_ROWS) % ROW_ALIGN == 0
VMEM_LIMIT = 56 * 1024 * 1024


def _sigmoid(x):
    return jax.nn.sigmoid(x)


def _layer_norm(h, g, b):
    mu = jnp.mean(h, axis=-1, keepdims=True)
    d = h - mu
    var = jnp.mean(d * d, axis=-1, keepdims=True)
    return d * lax.rsqrt(var + LN_EPS) * g + b


def _dot(a, b):
    return jnp.dot(a, b, preferred_element_type=F32)


def _layer_spec(l, shape):
    return pl.BlockSpec((None,) + tuple(shape), lambda *_: (l,) + (0,) * len(shape))


def _inproj_kernel(x_ref, w_ref, cos_ref, sin_ref, cva_ref, cvb_ref, cvbb_ref, lng_ref, lnb_ref,
                   poolw_ref, pools_ref,
                   yabc_ref, qt_ref, qrt_ref, kcvc_ref, kk_ref, ksx_ref, vvt_ref, gt_ref,
                   ch_ext, u_ext, p_ext, u_sh):
    j = pl.program_id(1)
    ts = x_ref.shape[1]
    xb = x_ref[0].astype(BF16)

    def proj(lo, hi):
        return _dot(xb, w_ref[:, lo:hi])

    def carry_halo(ext, halo):
        @pl.when(j == 0)
        def _():
            ext[0:halo, :] = jnp.zeros((halo, ext.shape[1]), F32)

        @pl.when(j > 0)
        def _():
            ext[0:halo, :] = ext[ts:ts + halo, :]

    pa = proj(OFF_A, OFF_B)
    a_b, ch = pa[:, 0:256], pa[:, 256:512] * pa[:, 512:768]
    carry_halo(ch_ext, HALO_A)
    ch_ext[HALO_A:HALO_A + ts, :] = ch
    conv = cva_ref[2:3, :] * ch
    for k in range(SC_KERNEL - 1):
        off = HALO_A - (SC_KERNEL - 1) + k
        conv = conv + cva_ref[k:k + 1, :] * ch_ext[off:off + ts, :]
    yabc_ref[0, :, 0:256] = (a_b * conv).astype(BF16)

    pb = proj(OFF_B, OFF_C)
    u = pb[:, 0:256] * _sigmoid(pb[:, 256:512])
    carry_halo(u_ext, HALO_B)
    u_ext[HALO_B:HALO_B + ts, :] = u
    span = u_sh.shape[1]
    for r in range(1, 8):
        u_sh[r - 1, :, :] = u_ext[r:r + span, :]
    def mixer_c():
        pc = proj(OFF_C, OFF_Q)
        carry_halo(p_ext, HALO_C)
        p_ext[HALO_C:HALO_C + ts, :] = pc
        pe = p_ext[...]
        s2 = pe + pltpu.roll(pe, 1, 0)
        s4 = s2 + pltpu.roll(s2, 2, 0)
        s8 = s4 + pltpu.roll(s4, 4, 0)
        s16 = s8 + pltpu.roll(s8, 8, 0)
        lane = lax.broadcasted_iota(jnp.int32, (ts, 256), 1)
        grp = jnp.right_shift(lane, 6)
        wsum = jnp.where(grp == 0, s2[HALO_C:], jnp.where(grp == 1, s4[HALO_C:], jnp.where(grp == 2, s8[HALO_C:], s16[HALO_C:])))
        win = jnp.where(grp == 0, 2, jnp.where(grp == 1, 4, jnp.where(grp == 2, 8, 16)))
        t1 = j * ts + lax.broadcasted_iota(jnp.int32, (ts, 256), 0) + 1
        cnt = jnp.minimum(win, t1).astype(F32)
        dlt = wsum / cnt - pc
        yabc_ref[0, :, 512:768] = (_dot(dlt.astype(BF16), poolw_ref[...]) * pools_ref[...]).astype(BF16)

    def queries():
        cos4, sin4 = cos_ref[...], sin_ref[...]
        q = proj(OFF_Q, OFF_KCVC)
        lane_q = jnp.bitwise_and(lax.broadcasted_iota(jnp.int32, (ts, 256), 1), HEAD_DIM - 1)
        q_sw = jnp.where(lane_q < HEAD_DIM // 2, pltpu.roll(q, 256 - HEAD_DIM // 2, 1), pltpu.roll(q, HEAD_DIM // 2, 1))
        qr = q * cos4 + q_sw * sin4
        qt_ref[0] = (q * (QK_SCALE * LOG2E)).T.astype(BF16)
        qrt_ref[0] = (qr * (QK_SCALE * LOG2E)).T.astype(BF16)

    def compress_inputs():
        kcvc_ref[0] = proj(OFF_KCVC, OFF_KK)

    def keys():
        k2 = proj(OFF_KK, OFF_VV)
        lane_k = jnp.bitwise_and(lax.broadcasted_iota(jnp.int32, (ts, LANES), 1), HEAD_DIM - 1)
        k_sw = jnp.where(lane_k < HEAD_DIM // 2, pltpu.roll(k2, LANES - HEAD_DIM // 2, 1), pltpu.roll(k2, HEAD_DIM // 2, 1))
        k_rot = k2 * cos_ref[:, 0:LANES] + k_sw * sin_ref[:, 0:LANES]
        kk_ref[0] = k_rot.astype(BF16)
        lane_i = lax.broadcasted_iota(jnp.int32, (ts, LANES), 1)
        blk_in_tile = jnp.bitwise_and(jnp.right_shift(j * ts + lax.broadcasted_iota(jnp.int32, (ts, LANES), 0), 6),
                                      KT // SLC_BLOCK - 1)
        ksx_ref[0] = jnp.where(lane_i < HEAD_DIM, k_rot, jnp.where(lane_i - HEAD_DIM == blk_in_tile, 1.0, 0.0)).astype(BF16)

    def values():
        v2t = proj(OFF_VV, OFF_G).T
        for c in range(ts // LANES):
            vvt_ref[0, c] = v2t[:, c * LANES:(c + 1) * LANES].astype(BF16)

    def branch_gates():
        gt_ref[0] = _sigmoid(proj(OFF_G, IN_PAD)).T[0:16, :]

    interleaved = [mixer_c, queries, compress_inputs, keys, values, branch_gates]
    rows = 64
    assert ts // rows >= len(interleaved)
    for c in range(ts // rows):
        acc = jnp.zeros((rows, 256), F32) + cvbb_ref[...]
        for k in range(CF_KERNEL):
            off = HALO_B - (CF_KERNEL - 1) + k
            base = off // 8 * 8 + c * rows
            src = u_ext if off % 8 == 0 else u_sh.at[off % 8 - 1]
            acc = acc + cvb_ref[k:k + 1, :] * src[base:base + rows, :]
        v = _layer_norm(acc, lng_ref[...], lnb_ref[...])
        yabc_ref[0, c * rows:(c + 1) * rows, 256:512] = (v * _sigmoid(v)).astype(BF16)
        if c < len(interleaved):
            interleaved[c]()


def _inproj(l, x, w_in_p, cos4, sin4, cva, cvb, cvbb, lng, lnb, poolw, pools):
    b, s, d = x.shape
    ts = TS_IN
    grid = (b, s // ts)
    out_shape = (
        jax.ShapeDtypeStruct((b, s, 768), BF16),
        jax.ShapeDtypeStruct((b, 256, s), BF16),
        jax.ShapeDtypeStruct((b, 256, s), BF16),
        jax.ShapeDtypeStruct((b, s, LANES), F32),
        jax.ShapeDtypeStruct((b, s, LANES), BF16),
        jax.ShapeDtypeStruct((b, s, LANES), BF16),
        jax.ShapeDtypeStruct((b, s // LANES, LANES, LANES), BF16),
        jax.ShapeDtypeStruct((b, 16, s), F32),
    )
    return pl.pallas_call(
        _inproj_kernel,
        grid=grid,
        in_specs=[
            pl.BlockSpec((1, ts, d), lambda bi, ji: (bi, ji, 0)),
            _layer_spec(l, (d, IN_PAD)),
            pl.BlockSpec((ts, 256), lambda bi, ji: (ji, 0)),
            pl.BlockSpec((ts, 256), lambda bi, ji: (ji, 0)),
            _layer_spec(l, (SC_KERNEL, 256)), _layer_spec(l, (CF_KERNEL, 256)), _layer_spec(l, (1, 256)),
            _layer_spec(l, (1, 256)), _layer_spec(l, (1, 256)), _layer_spec(l, (256, 256)), _layer_spec(l, (1, 256)),
        ],
        out_specs=(
            pl.BlockSpec((1, ts, 768), lambda bi, ji: (bi, ji, 0)),
            pl.BlockSpec((1, 256, ts), lambda bi, ji: (bi, 0, ji)),
            pl.BlockSpec((1, 256, ts), lambda bi, ji: (bi, 0, ji)),
            pl.BlockSpec((1, ts, LANES), lambda bi, ji: (bi, ji, 0)),
            pl.BlockSpec((1, ts, LANES), lambda bi, ji: (bi, ji, 0)),
            pl.BlockSpec((1, ts, LANES), lambda bi, ji: (bi, ji, 0)),
            pl.BlockSpec((1, ts // LANES, LANES, LANES), lambda bi, ji: (bi, ji, 0, 0)),
            pl.BlockSpec((1, 16, ts), lambda bi, ji: (bi, 0, ji)),
        ),
        out_shape=out_shape,
        scratch_shapes=[
            pltpu.VMEM((ts + HALO_A, 256), F32),
            pltpu.VMEM((ts + HALO_B, 256), F32),
            pltpu.VMEM((ts + HALO_C, 256), F32),
            pltpu.VMEM((7, ts + HALO_B - 8, 256), F32),
        ],
        compiler_params=pltpu.CompilerParams(dimension_semantics=("arbitrary", "arbitrary"),
                                             vmem_limit_bytes=VMEM_LIMIT),
        name="inproj_mixers",
    )(x, w_in_p, cos4, sin4, cva, cvb, cvbb, lng, lnb, poolw, pools)


def _compress_kernel(kc_ref, vc_ref, pek_ref, pev_ref, wk1_ref, wk2_ref, wv1_ref, wv2_ref, kcmp_ref, vcmpt_ref):
    def comp(x_ref, pe_ref, w1_ref, w2_ref):
        x = x_ref[0]
        m = x.shape[0]
        half = x.shape[1]
        first = _dot((x + pe_ref[0:1, :]).astype(BF16), w1_ref[0:half, :])
        second = _dot((x + pe_ref[1:2, :]).astype(BF16), w1_ref[half:2 * half, :])
        hid = first + pltpu.roll(second, m - 1, 0)
        return _dot((hid * _sigmoid(hid)).astype(BF16), w2_ref[...])

    kcmp_ref[0] = comp(kc_ref, pek_ref, wk1_ref, wk2_ref).astype(BF16)
    vcmpt_ref[0] = comp(vc_ref, pev_ref, wv1_ref, wv2_ref).T[0:HEAD_DIM, :].astype(BF16)


def _compress(l, kc16, vc16, pek, pev, wk1, wk2, wv1, wv2):
    b, m, w = kc16.shape
    return pl.pallas_call(
        _compress_kernel,
        grid=(b,),
        in_specs=[
            pl.BlockSpec((1, m, w), lambda bi: (bi, 0, 0)),
            pl.BlockSpec((1, m, w), lambda bi: (bi, 0, 0)),
            _layer_spec(l, (2, w)), _layer_spec(l, (2, w)),
            _layer_spec(l, (2 * w, CMP_HIDDEN)), _layer_spec(l, (CMP_HIDDEN, LANES)),
            _layer_spec(l, (2 * w, CMP_HIDDEN)), _layer_spec(l, (CMP_HIDDEN, LANES)),
        ],
        out_specs=(
            pl.BlockSpec((1, m, LANES), lambda bi: (bi, 0, 0)),
            pl.BlockSpec((1, HEAD_DIM, m), lambda bi: (bi, 0, 0)),
        ),
        out_shape=(jax.ShapeDtypeStruct((b, m, LANES), BF16), jax.ShapeDtypeStruct((b, HEAD_DIM, m), BF16)),
        compiler_params=pltpu.CompilerParams(dimension_semantics=("arbitrary",), vmem_limit_bytes=VMEM_LIMIT),
        name="compress_kv",
    )(kc16, vc16, pek, pev, wk1, wk2, wv1, wv2)


def _nsa_kernel(qt_ref, qrt_ref, gt_ref, kk_ref, ksx_ref, vvt_ref, kcmp_ref, vcmpt_ref, ovt_ref, yd_ref, sel_scr, sa_scr, sb_scr, pa_scr, pb_scr, diag_scr, acc_scr):
    qb = pl.program_id(1)
    tq = qt_ref.shape[2]
    nh = NSA_HEADS
    wq = nh * tq
    s0 = qb * tq
    n_slc = sel_scr.shape[0]
    n_cmp_pad = kcmp_ref.shape[1]

    def stack_heads(ref):
        x = ref[0]
        return jnp.concatenate([x[h * HEAD_DIM:(h + 1) * HEAD_DIM, :] for h in range(nh)], axis=1)

    zeros = jnp.zeros((HEAD_DIM, wq), BF16)
    q_top, qr_top = stack_heads(qt_ref), stack_heads(qrt_ref)
    q_lo = jnp.concatenate([q_top, zeros], axis=0)
    qr_lo = jnp.concatenate([qr_top, zeros], axis=0)
    qr_hi = jnp.concatenate([zeros, qr_top], axis=0)
    t_row = s0 + lax.broadcasted_iota(jnp.int32, (1, tq), 1)
    lanes4 = lambda x: jnp.concatenate([x] * nh, axis=1)
    t_all = lanes4(t_row)

    size_classes = 4
    class_rows = n_cmp_pad // size_classes
    assert CMP_STRIDE * class_rows >= tq >= 2 * SLC_BLOCK

    def compress_and_select(nc):
        nb = nc // (SLC_BLOCK // CMP_STRIDE)
        cmp_end = lax.broadcasted_iota(jnp.int32, (nc, tq), 0) * CMP_STRIDE + (CMP_BLOCK - 1)
        sc = _dot(kcmp_ref[0, 0:nc, :], q_lo) + lanes4(jnp.where(cmp_end <= t_row, 0.0, NEG))
        mx = jnp.max(sc, axis=0, keepdims=True)
        ex = jnp.exp2(sc - mx)
        den = jnp.sum(ex, axis=0, keepdims=True)
        p_cmp = ex * jnp.where(t_all >= CMP_BLOCK - 1, 1.0 / den, 0.0)
        o_cmp = _dot(vcmpt_ref[0, :, 0:nc], p_cmp.astype(BF16))
        p_heads = p_cmp[:, 0:tq]
        for h in range(1, nh):
            p_heads = p_heads + p_cmp[:, h * tq:(h + 1) * tq]
        imp = _dot(ovt_ref[0:nb, 0:nc], p_heads.astype(BF16))

        blk_i = lax.broadcasted_iota(jnp.int32, (nb, tq), 0)
        blk = blk_i.astype(F32)
        cur = jnp.right_shift(t_row, 6)
        forced = (blk_i == 0) | (blk_i == cur) | (blk_i == cur - 1)
        valid = blk_i * SLC_BLOCK <= t_row
        sel = jnp.where(forced, 1.0, 0.0)
        n_forced = jnp.sum(sel, axis=0, keepdims=True)
        imp = jnp.where(valid, jnp.where(forced, -jnp.inf, imp), NEG)
        min_forced = 3 if nc > class_rows else 1
        for r in range(min(SLC_TOPN - min_forced, nb)):
            top = jnp.max(imp, axis=0, keepdims=True)
            first = jnp.min(jnp.where(imp == top, blk, float(nb)), axis=0, keepdims=True)
            pick = (blk == first) & (n_forced + r < SLC_TOPN)
            sel = jnp.where(pick, 1.0, sel)
            imp = jnp.where(pick, -jnp.inf, imp)
        bias = jnp.where(valid & (sel > 0.0), 0.0, NEG)
        own0 = qb * (tq // SLC_BLOCK)
        own = (blk_i >= own0) & (blk_i < own0 + tq // SLC_BLOCK)
        sel_scr[0:nb, :] = jnp.where(own, NEG, bias)
        if nb < n_slc:
            sel_scr[nb:n_slc, :] = jnp.full((n_slc - nb, tq), NEG, F32)
        key_blk = own0 + jnp.right_shift(lax.broadcasted_iota(jnp.int32, (tq, nb), 0), 6)
        expand = jnp.where(lax.broadcasted_iota(jnp.int32, (tq, nb), 1) == key_blk, 1.0, 0.0).astype(BF16)
        diag_scr[...] = _dot(expand, bias.astype(BF16))
        return o_cmp

    visible = (s0 + tq) // CMP_STRIDE
    o_cmp = lax.switch((visible - 1) // class_rows,
                       [functools.partial(compress_and_select, (i + 1) * class_rows) for i in range(size_classes)])

    blocks_per_tile = KT // SLC_BLOCK
    sub = KT // LANES

    last_tile = kk_ref.shape[1] // KT - 1

    def score_chunks(kt):
        kd = jnp.minimum(kt, last_tile)
        bias8 = sel_scr[pl.ds(pl.multiple_of(kd * blocks_per_tile, blocks_per_tile), blocks_per_tile), :]
        bias8 = jnp.where(kt <= last_tile, bias8, NEG)
        pad = [jnp.zeros((16 - blocks_per_tile, wq), F32)] if blocks_per_tile < 16 else []
        bias_rows = jnp.concatenate([lanes4(bias8)] + pad, axis=0).astype(BF16)
        q_bias = jnp.concatenate([qr_top, bias_rows, jnp.zeros((HEAD_DIM - 16, wq), BF16)], axis=0)
        s_tile = _dot(ksx_ref[0, pl.ds(pl.multiple_of(kd * KT, KT), KT), :], q_bias)
        for c in range(sub):
            yield c, s_tile[c * LANES:(c + 1) * LANES, :]

    def fold8(x, op):
        out = x[0:8, :]
        for r in range(1, x.shape[0] // 8):
            out = op(out, x[8 * r:8 * r + 8, :])
        return out

    def v_rows(first, count, lo):
        vt = vvt_ref[0, pl.ds(first, count)]
        return jnp.concatenate([vt[c, lo:lo + HEAD_DIM, :] for c in range(count)], axis=1)

    def pv_tile(kt, p_ref):
        return _dot(v_rows(pl.multiple_of(jnp.minimum(kt, last_tile) * sub, sub), sub, 0), p_ref[...])

    def phase(kt, s_cur, s_next, p_cur, p_prev, state):
        m_i, l_i, mt_cur, alpha_prev = state
        m_new = jnp.maximum(m_i, jnp.max(mt_cur, axis=0, keepdims=True))
        alpha = jnp.exp2(m_i - m_new)
        mt_next = jnp.full((8, wq), NEG, F32)
        l_new = alpha * l_i
        for c, s_n in score_chunks(kt + 1):
            rows = slice(c * LANES, (c + 1) * LANES)
            mt_next = jnp.maximum(mt_next, fold8(s_n, jnp.maximum))
            s_next[rows, :] = s_n
            p = jnp.exp2(s_cur[rows, :] - m_new)
            l_new = l_new + fold8(p, jnp.add)
            p_cur[rows, :] = p.astype(BF16)
        if p_prev is not None:
            acc_scr[...] = alpha_prev * acc_scr[...] + pv_tile(kt - 1, p_prev)
        return m_new, l_new, mt_next, alpha

    def slc_pair(j, state):
        state = phase(2 * j + 1, sb_scr, sa_scr, pb_scr, pa_scr, state)
        return phase(2 * j + 2, sa_scr, sb_scr, pa_scr, pb_scr, state)

    mt0 = jnp.full((8, wq), NEG, F32)
    for c, s_n in score_chunks(0):
        mt0 = jnp.maximum(mt0, fold8(s_n, jnp.maximum))
        sa_scr[c * LANES:(c + 1) * LANES, :] = s_n
    key_d = s0 + lax.broadcasted_iota(jnp.int32, (tq, tq), 0)
    s_d = (_dot(kk_ref[0, pl.ds(pl.multiple_of(s0, tq), tq), :], qr_lo)
           + lanes4(jnp.where(key_d <= t_row, diag_scr[...], NEG)))
    m_d = jnp.max(s_d, axis=0, keepdims=True)
    p_d = jnp.exp2(s_d - m_d)
    acc_scr[...] = _dot(v_rows(qb * (tq // LANES), tq // LANES, 0), p_d.astype(BF16))
    state = (m_d, fold8(p_d, jnp.add), mt0, jnp.ones((1, wq), F32))
    state = phase(0, sa_scr, sb_scr, pa_scr, None, state)
    n_tiles = (s0 + tq + KT - 1) // KT
    pairs = n_tiles // 2
    _, l8_s, _, alpha_last = lax.fori_loop(0, pairs, slc_pair, state)
    acc_s = alpha_last * acc_scr[...] + pv_tile(2 * pairs, pa_scr)
    l_s = jnp.sum(l8_s, axis=0, keepdims=True)

    wkeys = WIN + tq
    k0 = pl.multiple_of(jnp.maximum(s0 - WIN, 0), LANES)
    key = k0 + lax.broadcasted_iota(jnp.int32, (wkeys, tq), 0)
    wbias = jnp.where((key <= t_row) & (key > t_row - WIN), 0.0, NEG)
    s_w = _dot(kk_ref[0, pl.ds(k0, wkeys), :], qr_hi) + lanes4(wbias)
    vt_w = vvt_ref[0, pl.ds(jnp.maximum(qb * (tq // LANES) - WIN // LANES, 0), wkeys // LANES)]
    p_w = jnp.exp2(s_w - jnp.max(s_w, axis=0, keepdims=True))
    l_w = jnp.sum(p_w, axis=0, keepdims=True)
    v_w = jnp.concatenate([vt_w[i, HEAD_DIM:2 * HEAD_DIM, :] for i in range(wkeys // LANES)], axis=1)
    acc_w = _dot(v_w, p_w.astype(BF16))

    g = gt_ref[0]
    gate = lambda br: jnp.concatenate([g[h * 3 + br:h * 3 + br + 1, :] for h in range(nh)], axis=1)
    o = gate(0) * o_cmp + gate(1) * (acc_s * (1.0 / l_s)) + gate(2) * (acc_w * (1.0 / l_w))
    o_rows = jnp.concatenate([o[:, h * tq:(h + 1) * tq] for h in range(nh)], axis=0)
    yd_ref[0] = o_rows.T.astype(BF16)


def _nsa(qt, qrt, gt, kk, ksx, vvt, kcmp, vcmpt, ovt):
    b, _, s = qt.shape
    n_slc = s // SLC_BLOCK
    m = kcmp.shape[1]
    return pl.pallas_call(
        _nsa_kernel,
        grid=(b, s // TQ),
        in_specs=[
            pl.BlockSpec((1, 256, TQ), lambda bi, qi: (bi, 0, qi)),
            pl.BlockSpec((1, 256, TQ), lambda bi, qi: (bi, 0, qi)),
            pl.BlockSpec((1, 16, TQ), lambda bi, qi: (bi, 0, qi)),
            pl.BlockSpec((1, s, LANES), lambda bi, qi: (bi, 0, 0)),
            pl.BlockSpec((1, s, LANES), lambda bi, qi: (bi, 0, 0)),
            pl.BlockSpec((1, s // LANES, LANES, LANES), lambda bi, qi: (bi, 0, 0, 0)),
            pl.BlockSpec((1, m, LANES), lambda bi, qi: (bi, 0, 0)),
            pl.BlockSpec((1, HEAD_DIM, m), lambda bi, qi: (bi, 0, 0)),
            pl.BlockSpec((n_slc, m), lambda bi, qi: (0, 0)),
        ],
        out_specs=pl.BlockSpec((1, TQ, 256), lambda bi, qi: (bi, qi, 0)),
        out_shape=jax.ShapeDtypeStruct((b, s, 256), BF16),
        scratch_shapes=[pltpu.VMEM((n_slc, TQ), F32),
                        pltpu.VMEM((KT, NSA_HEADS * TQ), F32), pltpu.VMEM((KT, NSA_HEADS * TQ), F32),
                        pltpu.VMEM((KT, NSA_HEADS * TQ), BF16), pltpu.VMEM((KT, NSA_HEADS * TQ), BF16),
                        pltpu.VMEM((TQ, TQ), F32), pltpu.VMEM((HEAD_DIM, NSA_HEADS * TQ), F32)],
        compiler_params=pltpu.CompilerParams(dimension_semantics=("arbitrary", "arbitrary"),
                                             vmem_limit_bytes=VMEM_LIMIT),
        name="sparse_attention",
    )(qt, qrt, gt, kk, ksx, vvt, kcmp, vcmpt, ovt)


def _outproj_kernel(x_ref, yabc_ref, yd_ref, wo_ref, g_ref, b_ref, rw_ref, rb_ref, x1_ref, gates_ref):
    mix = _dot(yabc_ref[...], wo_ref[0:768, :]) + _dot(yd_ref[...], wo_ref[768:1024, :])
    x1 = _layer_norm(DN_ALPHA * x_ref[...] + mix, g_ref[...], b_ref[...])
    x1_ref[...] = x1

    ts = x1.shape[0]
    aff = _sigmoid(_dot(x1.astype(BF16), rw_ref[...])).T[0:N_EXPERTS, :]
    biased = aff + rb_ref[...]
    row = lambda a, i: a[i:i + 1, :]

    gscores = []
    for gi in range(N_GROUPS):
        a, b, c, d = (row(biased, gi * EXPERTS_PER_GROUP + i) for i in range(EXPERTS_PER_GROUP))
        hi1, lo1, hi2, lo2 = jnp.maximum(a, b), jnp.minimum(a, b), jnp.maximum(c, d), jnp.minimum(c, d)
        gscores.append(jnp.maximum(hi1, hi2) + jnp.maximum(jnp.minimum(hi1, hi2), jnp.maximum(lo1, lo2)))
    best, gsel = gscores[0], jnp.zeros((1, ts), jnp.int32)
    for gi in range(1, N_GROUPS):
        better = gscores[gi] > best
        gsel = jnp.where(better, gi, gsel)
        best = jnp.where(better, gscores[gi], best)

    eid_i = lax.broadcasted_iota(jnp.int32, (N_EXPERTS, ts), 0)
    eid = eid_i.astype(F32)
    masked = jnp.where(jnp.right_shift(eid_i, 2) == gsel, biased, NEG)
    picks = []
    for _ in range(2):
        top = jnp.max(masked, axis=0, keepdims=True)
        first = jnp.min(jnp.where(masked == top, eid, float(N_EXPERTS)), axis=0, keepdims=True)
        pick = eid == first
        picks.append(pick)
        masked = jnp.where(pick, -jnp.inf, masked)
    chosen = picks[0] | picks[1]
    w_sel = jnp.where(chosen, aff, 0.0)
    gates_t = w_sel / jnp.sum(w_sel, axis=0, keepdims=True)
    grp8 = jnp.where(lax.broadcasted_iota(jnp.int32, (8, ts), 0) == 0, gsel.astype(F32), 0.0)
    gates_ref[...] = jnp.concatenate([gates_t, grp8, jnp.zeros((LANES - N_EXPERTS - 8, ts), F32)], axis=0).T


def _outproj(l, x2d, yabc, yd, wo, g, b, rw, rb):
    n, d = x2d.shape
    ts = TS_OUT
    const = lambda shape: pl.BlockSpec(shape, lambda i: (0,) * len(shape))
    return pl.pallas_call(
        _outproj_kernel,
        grid=(n // ts,),
        in_specs=[
            pl.BlockSpec((ts, d), lambda i: (i, 0)),
            pl.BlockSpec((ts, 768), lambda i: (i, 0)),
            pl.BlockSpec((ts, 256), lambda i: (i, 0)),
            _layer_spec(l, (d, d)), _layer_spec(l, (1, d)), _layer_spec(l, (1, d)), const((d, LANES)), const((N_EXPERTS, 1)),
        ],
        out_specs=(pl.BlockSpec((ts, d), lambda i: (i, 0)), pl.BlockSpec((ts, LANES), lambda i: (i, 0))),
        out_shape=(jax.ShapeDtypeStruct((n, d), F32), jax.ShapeDtypeStruct((n, LANES), F32)),
        compiler_params=pltpu.CompilerParams(dimension_semantics=("arbitrary",), vmem_limit_bytes=VMEM_LIMIT),
        name="outproj_ln_route",
    )(x2d, yabc, yd, wo, g, b, rw, rb)


def _split3(x):
    hi = x.astype(BF16)
    r1 = x - hi.astype(F32)
    mid = r1.astype(BF16)
    lo = (r1 - mid.astype(F32)).astype(BF16)
    return hi, mid, lo


def _dispatch_kernel(x1_ref, gates_ref, tri_ref, xs_ref, gs_ref, post_ref, cnt_ref):
    tm = x1_ref.shape[0]
    gates = gates_ref[...]
    gsel = gates.T[GROUP_LANE:GROUP_LANE + 1, :]
    onehot = jnp.where(lax.broadcasted_iota(jnp.int32, (8, tm), 0).astype(F32) == gsel, 1.0, 0.0)
    n = jnp.sum(onehot, axis=1, keepdims=True)
    n_al = jnp.floor((n + (ROW_ALIGN - 1)) * (1.0 / ROW_ALIGN)) * ROW_ALIGN
    starts = [jnp.zeros((1, 1), F32)]
    for gi in range(1, N_GROUPS):
        starts.append(starts[-1] + n_al[gi - 1:gi, :])
    start = jnp.concatenate(starts + [jnp.zeros((8 - N_GROUPS, 1), F32)], axis=0)
    before = _dot(onehot.astype(BF16), tri_ref[...])
    pos = jnp.sum(onehot * (start + before), axis=0, keepdims=True)
    perm = jnp.where(lax.broadcasted_iota(jnp.int32, (SORT_ROWS, tm), 0).astype(F32) == pos, 1.0, 0.0).astype(BF16)

    xs_ref[0, 0:SORT_ROWS, :] = _dot(perm, x1_ref[...].astype(BF16)).astype(BF16)
    xs_ref[0, SORT_ROWS:SORT_BUF, :] = jnp.zeros((SORT_BUF - SORT_ROWS, xs_ref.shape[2]), BF16)
    g_hi, g_mid, g_lo = _split3(gates)
    gs_ref[0, 0:SORT_ROWS, :] = _dot(perm, g_hi) + _dot(perm, g_mid) + _dot(perm, g_lo)
    gs_ref[0, SORT_ROWS:SORT_BUF, :] = jnp.zeros((SORT_BUF - SORT_ROWS, LANES), F32)
    pos8 = jnp.where(lax.broadcasted_iota(jnp.int32, (8, tm), 0) == 0, pos, 0.0)
    post_ref[...] = jnp.concatenate([pos8, jnp.zeros((LANES - 8, tm), F32)], axis=0).T
    cnt_ref[0] = jnp.broadcast_to(n, (8, LANES))


def _dispatch(x1, gates, tri):
    n, d = x1.shape
    tm = TM_MOE
    nt = n // tm
    return pl.pallas_call(
        _dispatch_kernel,
        grid=(nt,),
        in_specs=[
            pl.BlockSpec((tm, d), lambda i: (i, 0)),
            pl.BlockSpec((tm, LANES), lambda i: (i, 0)),
            pl.BlockSpec((tm, tm), lambda i: (0, 0)),
        ],
        out_specs=(
            pl.BlockSpec((1, SORT_BUF, d), lambda i: (i, 0, 0)),
            pl.BlockSpec((1, SORT_BUF, LANES), lambda i: (i, 0, 0)),
            pl.BlockSpec((tm, LANES), lambda i: (i, 0)),
            pl.BlockSpec((1, 8, LANES), lambda i: (i, 0, 0)),
        ),
        out_shape=(
            jax.ShapeDtypeStruct((nt, SORT_BUF, d), BF16),
            jax.ShapeDtypeStruct((nt, SORT_BUF, LANES), F32),
            jax.ShapeDtypeStruct((n, LANES), F32),
            jax.ShapeDtypeStruct((nt, 8, LANES), F32),
        ),
        compiler_params=pltpu.CompilerParams(dimension_semantics=("arbitrary",), vmem_limit_bytes=VMEM_LIMIT),
        name="moe_dispatch",
    )(x1, gates, tri)


def _moe_kernel(start_ref, trips_ref, xs_ref, gs_ref, wg_ref, wu_ref, wd_ref, acc_ref):
    i, e = pl.program_id(0), pl.program_id(1)

    @pl.when(e == 0)
    def _():
        acc_ref[...] = jnp.zeros(acc_ref.shape, F32)

    grp = lax.shift_right_logical(e, 2)
    slots = [(i * TILES_PER_STEP + tile) * N_GROUPS + grp for tile in range(TILES_PER_STEP)]
    row0 = [start_ref[s] for s in slots]
    need = [trips_ref[s] for s in slots]

    def window(w, carry):
        rows = [pl.ds(pl.multiple_of(jnp.minimum(r + w * WIN_ROWS, SORT_BUF - WIN_ROWS), ROW_ALIGN), WIN_ROWS)
                for r in row0]
        xw = jnp.concatenate([xs_ref[tile, rows[tile], :] for tile in range(TILES_PER_STEP)], axis=0)
        hg = _dot(xw, wg_ref[0])
        h = hg * _sigmoid(hg) * _dot(xw, wu_ref[0])
        y = _dot(h.astype(BF16), wd_ref[0])
        for tile in range(TILES_PER_STEP):
            gw = gs_ref[tile, rows[tile], :]
            lane = lax.broadcasted_iota(jnp.int32, gw.shape, 1)
            gcol = jnp.sum(jnp.where((lane == e) & (w < need[tile]), gw, 0.0), axis=1, keepdims=True)
            acc_ref[tile, rows[tile], :] += y[tile * WIN_ROWS:(tile + 1) * WIN_ROWS, :] * gcol
        return carry

    lax.fori_loop(0, functools.reduce(jnp.maximum, need), window, 0)


def _moe(l, start, trips, xs, gs, wg, wu, wd):
    nt, _, d = xs.shape
    _, ne, _, de = wg.shape
    tps = TILES_PER_STEP
    grid_spec = pltpu.PrefetchScalarGridSpec(
        num_scalar_prefetch=2,
        grid=(nt // tps, ne),
        in_specs=[
            pl.BlockSpec((tps, SORT_BUF, d), lambda i, e, *_: (i, 0, 0)),
            pl.BlockSpec((tps, SORT_BUF, LANES), lambda i, e, *_: (i, 0, 0)),
            pl.BlockSpec((None, 1, d, de), lambda i, e, *_: (l, e, 0, 0)),
            pl.BlockSpec((None, 1, d, de), lambda i, e, *_: (l, e, 0, 0)),
            pl.BlockSpec((None, 1, de, d), lambda i, e, *_: (l, e, 0, 0)),
        ],
        out_specs=pl.BlockSpec((tps, SORT_BUF, d), lambda i, e, *_: (i, 0, 0)),
    )
    return pl.pallas_call(
        _moe_kernel,
        grid_spec=grid_spec,
        out_shape=jax.ShapeDtypeStruct((nt, SORT_BUF, d), F32),
        compiler_params=pltpu.CompilerParams(dimension_semantics=("arbitrary", "arbitrary"),
                                             vmem_limit_bytes=VMEM_LIMIT),
        name="moe_experts",
    )(start, trips, xs, gs, wg, wu, wd)


def _combine_kernel(x1_ref, acc_ref, post_ref, g_ref, b_ref, out_ref):
    tm = x1_ref.shape[0]
    acc = acc_ref[0]
    hi = acc.astype(BF16)
    lo = (acc - hi.astype(F32)).astype(BF16)
    pos = post_ref[...][:, 0:1]
    unperm = jnp.where(lax.broadcasted_iota(jnp.int32, (tm, SORT_ROWS), 1).astype(F32) == pos, 1.0, 0.0).astype(BF16)
    moe = _dot(unperm, hi) + _dot(unperm, lo)
    out_ref[...] = _layer_norm(DN_ALPHA * x1_ref[...] + moe, g_ref[...], b_ref[...])


def _combine(l, x1, acc, post, g, b):
    n, d = x1.shape
    tm = TM_MOE
    return pl.pallas_call(
        _combine_kernel,
        grid=(n // tm,),
        in_specs=[
            pl.BlockSpec((tm, d), lambda i: (i, 0)),
            pl.BlockSpec((1, SORT_ROWS, d), lambda i: (i, 0, 0)),
            pl.BlockSpec((tm, LANES), lambda i: (i, 0)),
            _layer_spec(l, (1, d)), _layer_spec(l, (1, d)),
        ],
        out_specs=pl.BlockSpec((tm, d), lambda i: (i, 0)),
        out_shape=jax.ShapeDtypeStruct((n, d), F32),
        compiler_params=pltpu.CompilerParams(dimension_semantics=("arbitrary",), vmem_limit_bytes=VMEM_LIMIT),
        name="moe_combine_ln",
    )(x1, acc, post, g, b)


def _group_windows(cnt):
    n = cnt[:, 0:N_GROUPS, 0].astype(jnp.int32)
    n_al = (n + (ROW_ALIGN - 1)) // ROW_ALIGN * ROW_ALIGN
    start = jnp.cumsum(n_al, axis=1) - n_al
    trips = (n + (WIN_ROWS - 1)) // WIN_ROWS
    return start.reshape(-1), trips.reshape(-1)


def _permute_w_in(w):
    sizes = (256, 256, 256, 256, 256, 256, 256, 64, 64, 64, 64, 64, 64, 12)
    offs = np.concatenate([[0], np.cumsum(sizes)])
    part = lambda i: w[..., int(offs[i]):int(offs[i + 1])]
    a_b, a_c, a_h, b_val, b_gate, c_p, q, kc, vc, ks, vs, kw, vw, g = (part(i) for i in range(14))
    pad = jnp.zeros(w.shape[:-1] + (LANES - g.shape[-1],), w.dtype)
    return jnp.concatenate([a_b, a_c, a_h, b_val, b_gate, c_p, q, kc, vc, ks, kw, vs, vw, g, pad], axis=-1).astype(BF16)


def _rope_tables(s):
    inv = 1.0 / (ROPE_THETA ** (jnp.arange(0, HEAD_DIM, 2, dtype=F32) / HEAD_DIM))
    ang = jnp.arange(s, dtype=F32)[:, None] * inv[None, :]
    ang = jnp.concatenate([ang, ang], -1)
    sign = jnp.concatenate([-jnp.ones((HEAD_DIM // 2,), F32), jnp.ones((HEAD_DIM // 2,), F32)])
    return jnp.tile(jnp.cos(ang), (1, NSA_HEADS)), jnp.tile(jnp.sin(ang) * sign[None, :], (1, NSA_HEADS))


def _overlap_t(s, m):
    n_cmp = (s - CMP_BLOCK) // CMP_STRIDE + 1
    cmp_start = np.arange(m) * CMP_STRIDE
    slc_start = np.arange(s // SLC_BLOCK) * SLC_BLOCK
    ov = (cmp_start[None, :] < slc_start[:, None] + SLC_BLOCK) & (cmp_start[None, :] + CMP_BLOCK > slc_start[:, None])
    ov = ov & (np.arange(m)[None, :] < n_cmp)
    return jnp.asarray(ov, BF16)


def _block_diag(w):
    nl, g, c, _ = w.shape
    return jnp.einsum('lgcd,gh->lgchd', w, jnp.eye(g, dtype=w.dtype)).reshape(nl, g * c, g * c)


def kernel(x, w_in, conv_a_w, conv_b_w, conv_b_b, cf_ln_g, cf_ln_b, pool_w, pool_scale, cmp_pe_k, cmp_pe_v, cmp_k_w1, cmp_k_w2, cmp_v_w1, cmp_v_w2, w_out, ln1_g, ln1_b, ln2_g, ln2_b, router_w, router_bias, moe_w_gate, moe_w_up, moe_w_down):
    b, s, d = x.shape
    n = b * s
    m = s // CMP_STRIDE
    tok_per_row = CMP_STRIDE
    cos4, sin4 = _rope_tables(s)
    ovt = _overlap_t(s, m)
    rw = jnp.concatenate([router_w, jnp.zeros((d, LANES - N_EXPERTS), router_w.dtype)], axis=1).astype(BF16)
    rb = router_bias.reshape(N_EXPERTS, 1).astype(F32)
    tri = jnp.asarray(np.triu(np.ones((TM_MOE, TM_MOE), np.float32), k=1), BF16)

    nl = w_in.shape[0]
    row = lambda v: v.reshape(nl, 1, -1)
    pad_w2 = lambda w: jnp.concatenate([w, jnp.zeros((nl, CMP_HIDDEN, LANES - HEAD_DIM), w.dtype)], axis=2).astype(BF16)
    w_in_p = _permute_w_in(w_in)
    pool_bd = _block_diag(pool_w).astype(BF16)
    pek, pev = (v.reshape(nl, 2, tok_per_row * HEAD_DIM) for v in (cmp_pe_k, cmp_pe_v))
    wk1, wv1, wk2, wv2 = cmp_k_w1.astype(BF16), cmp_v_w1.astype(BF16), pad_w2(cmp_k_w2), pad_w2(cmp_v_w2)
    wo = w_out.astype(BF16)
    wg, wu, wd = moe_w_gate.astype(BF16), moe_w_up.astype(BF16), moe_w_down.astype(BF16)
    cvbb, lng, lnb, pools = row(conv_b_b), row(cf_ln_g), row(cf_ln_b), row(pool_scale)
    g1, b1, g2, b2 = row(ln1_g), row(ln1_b), row(ln2_g), row(ln2_b)

    for l in range(DEPTH):
        yabc, qt, qrt, kcvc, kk, ksx, vvt, gt = _inproj(
            l, x, w_in_p, cos4, sin4, conv_a_w, conv_b_w, cvbb, lng, lnb, pool_bd, pools)
        kc16 = kcvc[:, :, 0:HEAD_DIM].reshape(b, m, tok_per_row * HEAD_DIM)
        vc16 = kcvc[:, :, HEAD_DIM:2 * HEAD_DIM].reshape(b, m, tok_per_row * HEAD_DIM)
        kcmp, vcmpt = _compress(l, kc16, vc16, pek, pev, wk1, wk2, wv1, wv2)
        yd = _nsa(qt, qrt, gt, kk, ksx, vvt, kcmp, vcmpt, ovt)
        x1, gates = _outproj(l, x.reshape(n, d), yabc.reshape(n, 768), yd.reshape(n, 256), wo, g1, b1, rw, rb)
        xs, gs, post, cnt = _dispatch(x1, gates, tri)
        start, trips = _group_windows(cnt)
        acc = _moe(l, start, trips, xs, gs, wg, wu, wd)
        x = _combine(l, x1, acc, post, g2, b2).reshape(b, s, d)
    return x
```

```python
import functools

import jax
import jax.numpy as jnp
import numpy as np
from jax import lax
from jax.experimental import pallas as pl
from jax.experimental.pallas import tpu as pltpu

F32 = jnp.float32
BF16 = jnp.bfloat16

D_MODEL = 1024
DEPTH = 2
GROUP_W = 256
HEAD_DIM = 64
SC_KERNEL = 3
CF_KERNEL = 31
POOL_WINDOWS = (2, 4, 8, 16)
POOL_GW = 64
NSA_HEADS = 4
CMP_BLOCK = 32
CMP_STRIDE = 16
CMP_HIDDEN = 256
SLC_BLOCK = 64
SLC_TOPN = 16
WIN = 512
N_EXPERTS = 16
N_GROUPS = 4
EXPERTS_PER_GROUP = 4
D_EXPERT = 512
DN_ALPHA = (2 * DEPTH) ** 0.25
LN_EPS = 1e-5
NEG = -1e30
FORCE = 1e4
ROPE_THETA = 10000.0
QK_SCALE = HEAD_DIM ** -0.5
LOG2E = 1.4426950408889634

LANES = 128
OFF_A, OFF_B, OFF_C, OFF_Q, OFF_KCVC, OFF_KK, OFF_VV, OFF_G, IN_PAD = 0, 768, 1280, 1536, 1792, 1920, 2048, 2176, 2304

HALO_A = 8
HALO_B = 32
HALO_C = 16

TS_IN = 512
TQ = 256
KT = 512
TS_OUT = 512
TM_MOE = 1024
ROW_ALIGN = 16
WIN_ROWS = 288
SORT_ROWS = TM_MOE + LANES
SORT_BUF = SORT_ROWS + 2 * LANES
GROUP_LANE = 16
TILES_PER_STEP = 2
assert SORT_ROWS >= TM_MOE + N_GROUPS * (ROW_ALIGN - 1)
assert SORT_BUF >= TM_MOE + N_GROUPS * (ROW_ALIGN - 1) + WIN_ROWS
assert WIN_ROWS % ROW_ALIGN == 0 and (SORT_BUF - WIN_ROWS) % ROW_ALIGN == 0
VMEM_LIMIT = 56 * 1024 * 1024


def _sigmoid(x):
    return jax.nn.sigmoid(x)


def _layer_norm(h, g, b):
    mu = jnp.mean(h, axis=-1, keepdims=True)
    d = h - mu
    var = jnp.mean(d * d, axis=-1, keepdims=True)
    return d * lax.rsqrt(var + LN_EPS) * g + b


def _dot(a, b):
    return jnp.dot(a, b, preferred_element_type=F32)


def _layer_spec(l, shape):
    return pl.BlockSpec((None,) + tuple(shape), lambda *_: (l,) + (0,) * len(shape))


def _inproj_kernel(x_ref, w_ref, cos_ref, sin_ref, cva_ref, cvb_ref, cvbb_ref, lng_ref, lnb_ref,
                   poolw_ref, pools_ref,
                   yabc_ref, qt_ref, qrt_ref, kcvc_ref, kk_ref, ksx_ref, vvt_ref, gt_ref,
                   ch_ext, u_ext, p_ext, u_sh):
    j = pl.program_id(1)
    ts = x_ref.shape[1]
    xb = x_ref[0].astype(BF16)

    def proj(lo, hi):
        return _dot(xb, w_ref[:, lo:hi])

    def carry_halo(ext, halo):
        @pl.when(j == 0)
        def _():
            ext[0:halo, :] = jnp.zeros((halo, ext.shape[1]), F32)

        @pl.when(j > 0)
        def _():
            ext[0:halo, :] = ext[ts:ts + halo, :]

    pa = proj(OFF_A, OFF_B)
    a_b, ch = pa[:, 0:256], pa[:, 256:512] * pa[:, 512:768]
    carry_halo(ch_ext, HALO_A)
    ch_ext[HALO_A:HALO_A + ts, :] = ch
    conv = cva_ref[2:3, :] * ch
    for k in range(SC_KERNEL - 1):
        off = HALO_A - (SC_KERNEL - 1) + k
        conv = conv + cva_ref[k:k + 1, :] * ch_ext[off:off + ts, :]
    yabc_ref[:, 0:256] = (a_b * conv).astype(BF16)

    pb = proj(OFF_B, OFF_C)
    u = pb[:, 0:256] * _sigmoid(pb[:, 256:512])
    carry_halo(u_ext, HALO_B)
    u_ext[HALO_B:HALO_B + ts, :] = u
    span = u_sh.shape[1]
    for r in range(1, 8):
        u_sh[r - 1, :, :] = u_ext[r:r + span, :]
    def mixer_c():
        pc = proj(OFF_C, OFF_Q)
        carry_halo(p_ext, HALO_C)
        p_ext[HALO_C:HALO_C + ts, :] = pc
        pe = p_ext[...]
        s2 = pe + pltpu.roll(pe, 1, 0)
        s4 = s2 + pltpu.roll(s2, 2, 0)
        s8 = s4 + pltpu.roll(s4, 4, 0)
        s16 = s8 + pltpu.roll(s8, 8, 0)
        lane = lax.broadcasted_iota(jnp.int32, (ts, 256), 1)
        grp = jnp.right_shift(lane, 6)
        wsum = jnp.where(grp == 0, s2[HALO_C:], jnp.where(grp == 1, s4[HALO_C:], jnp.where(grp == 2, s8[HALO_C:], s16[HALO_C:])))
        win = jnp.where(grp == 0, 2, jnp.where(grp == 1, 4, jnp.where(grp == 2, 8, 16)))
        t1 = j * ts + lax.broadcasted_iota(jnp.int32, (ts, 256), 0) + 1
        cnt = jnp.minimum(win, t1).astype(F32)
        dlt = wsum / cnt - pc
        yabc_ref[:, 512:768] = (_dot(dlt.astype(BF16), poolw_ref[...]) * pools_ref[...]).astype(BF16)

    def queries():
        cos4, sin4 = cos_ref[...], sin_ref[...]
        q = proj(OFF_Q, OFF_KCVC)
        lane_q = jnp.bitwise_and(lax.broadcasted_iota(jnp.int32, (ts, 256), 1), HEAD_DIM - 1)
        q_sw = jnp.where(lane_q < HEAD_DIM // 2, pltpu.roll(q, 256 - HEAD_DIM // 2, 1), pltpu.roll(q, HEAD_DIM // 2, 1))
        qr = q * cos4 + q_sw * sin4
        qt_ref[0] = (q * (QK_SCALE * LOG2E)).T.astype(BF16)
        qrt_ref[0] = (qr * (QK_SCALE * LOG2E)).T.astype(BF16)

    def compress_inputs():
        kcvc_ref[0] = proj(OFF_KCVC, OFF_KK)

    def keys():
        k2 = proj(OFF_KK, OFF_VV)
        lane_k = jnp.bitwise_and(lax.broadcasted_iota(jnp.int32, (ts, LANES), 1), HEAD_DIM - 1)
        k_sw = jnp.where(lane_k < HEAD_DIM // 2, pltpu.roll(k2, LANES - HEAD_DIM // 2, 1), pltpu.roll(k2, HEAD_DIM // 2, 1))
        k_rot = k2 * cos_ref[:, 0:LANES] + k_sw * sin_ref[:, 0:LANES]
        kk_ref[0] = k_rot.astype(BF16)
        lane_i = lax.broadcasted_iota(jnp.int32, (ts, LANES), 1)
        blk_in_tile = jnp.bitwise_and(jnp.right_shift(j * ts + lax.broadcasted_iota(jnp.int32, (ts, LANES), 0), 6),
                                      KT // SLC_BLOCK - 1)
        ksx_ref[0] = jnp.where(lane_i < HEAD_DIM, k_rot, jnp.where(lane_i - HEAD_DIM == blk_in_tile, 1.0, 0.0)).astype(BF16)

    def values():
        v2t = proj(OFF_VV, OFF_G).T
        for c in range(ts // LANES):
            vvt_ref[0, c] = v2t[:, c * LANES:(c + 1) * LANES].astype(BF16)

    def branch_gates():
        gt_ref[0] = _sigmoid(proj(OFF_G, IN_PAD)).T[0:16, :]

    interleaved = [mixer_c, queries, compress_inputs, keys, values, branch_gates]
    rows = 64
    assert ts // rows >= len(interleaved)
    for c in range(ts // rows):
        acc = jnp.zeros((rows, 256), F32) + cvbb_ref[...]
        for k in range(CF_KERNEL):
            off = HALO_B - (CF_KERNEL - 1) + k
            base = off // 8 * 8 + c * rows
            src = u_ext if off % 8 == 0 else u_sh.at[off % 8 - 1]
            acc = acc + cvb_ref[k:k + 1, :] * src[base:base + rows, :]
        v = _layer_norm(acc, lng_ref[...], lnb_ref[...])
        yabc_ref[c * rows:(c + 1) * rows, 256:512] = (v * _sigmoid(v)).astype(BF16)
        if c < len(interleaved):
            interleaved[c]()


def _inproj(l, x, w_in_p, cos4, sin4, cva, cvb, cvbb, lng, lnb, poolw, pools):
    b, s, d = x.shape
    ts = TS_IN
    grid = (b, s // ts)
    out_shape = (
        jax.ShapeDtypeStruct((b * s, 768), BF16),
        jax.ShapeDtypeStruct((b, 256, s), BF16),
        jax.ShapeDtypeStruct((b, 256, s), BF16),
        jax.ShapeDtypeStruct((b, s, LANES), F32),
        jax.ShapeDtypeStruct((b, s, LANES), BF16),
        jax.ShapeDtypeStruct((b, s, LANES), BF16),
        jax.ShapeDtypeStruct((b, s // LANES, LANES, LANES), BF16),
        jax.ShapeDtypeStruct((b, 16, s), F32),
    )
    return pl.pallas_call(
        _inproj_kernel,
        grid=grid,
        in_specs=[
            pl.BlockSpec((1, ts, d), lambda bi, ji: (bi, ji, 0)),
            _layer_spec(l, (d, IN_PAD)),
            pl.BlockSpec((ts, 256), lambda bi, ji: (ji, 0)),
            pl.BlockSpec((ts, 256), lambda bi, ji: (ji, 0)),
            _layer_spec(l, (SC_KERNEL, 256)), _layer_spec(l, (CF_KERNEL, 256)), _layer_spec(l, (1, 256)),
            _layer_spec(l, (1, 256)), _layer_spec(l, (1, 256)), _layer_spec(l, (256, 256)), _layer_spec(l, (1, 256)),
        ],
        out_specs=(
            pl.BlockSpec((ts, 768), lambda bi, ji: (bi * (s // ts) + ji, 0)),
            pl.BlockSpec((1, 256, ts), lambda bi, ji: (bi, 0, ji)),
            pl.BlockSpec((1, 256, ts), lambda bi, ji: (bi, 0, ji)),
            pl.BlockSpec((1, ts, LANES), lambda bi, ji: (bi, ji, 0)),
            pl.BlockSpec((1, ts, LANES), lambda bi, ji: (bi, ji, 0)),
            pl.BlockSpec((1, ts, LANES), lambda bi, ji: (bi, ji, 0)),
            pl.BlockSpec((1, ts // LANES, LANES, LANES), lambda bi, ji: (bi, ji, 0, 0)),
            pl.BlockSpec((1, 16, ts), lambda bi, ji: (bi, 0, ji)),
        ),
        out_shape=out_shape,
        scratch_shapes=[
            pltpu.VMEM((ts + HALO_A, 256), F32),
            pltpu.VMEM((ts + HALO_B, 256), F32),
            pltpu.VMEM((ts + HALO_C, 256), F32),
            pltpu.VMEM((7, ts + HALO_B - 8, 256), F32),
        ],
        compiler_params=pltpu.CompilerParams(dimension_semantics=("arbitrary", "arbitrary"),
                                             vmem_limit_bytes=VMEM_LIMIT),
        name="inproj_mixers",
    )(x, w_in_p, cos4, sin4, cva, cvb, cvbb, lng, lnb, poolw, pools)


def _compress_kernel(kc_ref, vc_ref, pek_ref, pev_ref, wk1_ref, wk2_ref, wv1_ref, wv2_ref, kcmp_ref, vcmpt_ref):
    def comp(x_ref, pe_ref, w1_ref, w2_ref):
        x = x_ref[0]
        m = x.shape[0]
        half = x.shape[1]
        first = _dot((x + pe_ref[0:1, :]).astype(BF16), w1_ref[0:half, :])
        second = _dot((x + pe_ref[1:2, :]).astype(BF16), w1_ref[half:2 * half, :])
        hid = first + pltpu.roll(second, m - 1, 0)
        return _dot((hid * _sigmoid(hid)).astype(BF16), w2_ref[...])

    kcmp_ref[0] = comp(kc_ref, pek_ref, wk1_ref, wk2_ref).astype(BF16)
    vcmpt_ref[0] = comp(vc_ref, pev_ref, wv1_ref, wv2_ref).T[0:HEAD_DIM, :].astype(BF16)


def _compress(l, kc16, vc16, pek, pev, wk1, wk2, wv1, wv2):
    b, m, w = kc16.shape
    return pl.pallas_call(
        _compress_kernel,
        grid=(b,),
        in_specs=[
            pl.BlockSpec((1, m, w), lambda bi: (bi, 0, 0)),
            pl.BlockSpec((1, m, w), lambda bi: (bi, 0, 0)),
            _layer_spec(l, (2, w)), _layer_spec(l, (2, w)),
            _layer_spec(l, (2 * w, CMP_HIDDEN)), _layer_spec(l, (CMP_HIDDEN, LANES)),
            _layer_spec(l, (2 * w, CMP_HIDDEN)), _layer_spec(l, (CMP_HIDDEN, LANES)),
        ],
        out_specs=(
            pl.BlockSpec((1, m, LANES), lambda bi: (bi, 0, 0)),
            pl.BlockSpec((1, HEAD_DIM, m), lambda bi: (bi, 0, 0)),
        ),
        out_shape=(jax.ShapeDtypeStruct((b, m, LANES), BF16), jax.ShapeDtypeStruct((b, HEAD_DIM, m), BF16)),
        compiler_params=pltpu.CompilerParams(dimension_semantics=("arbitrary",), vmem_limit_bytes=VMEM_LIMIT),
        name="compress_kv",
    )(kc16, vc16, pek, pev, wk1, wk2, wv1, wv2)


def _nsa_kernel(qt_ref, qrt_ref, gt_ref, kk_ref, ksx_ref, vvt_ref, kcmp_ref, vcmpt_ref, ovt_ref, yd_ref, sel_scr, sa_scr, sb_scr, pa_scr, pb_scr, diag_scr, acc_scr):
    qb = pl.program_id(1)
    tq = qt_ref.shape[2]
    nh = NSA_HEADS
    wq = nh * tq
    s0 = qb * tq
    n_slc = sel_scr.shape[0]
    n_cmp_pad = kcmp_ref.shape[1]

    def stack_heads(ref):
        x = ref[0]
        return jnp.concatenate([x[h * HEAD_DIM:(h + 1) * HEAD_DIM, :] for h in range(nh)], axis=1)

    zeros = jnp.zeros((HEAD_DIM, wq), BF16)
    q_top, qr_top = stack_heads(qt_ref), stack_heads(qrt_ref)
    q_lo = jnp.concatenate([q_top, zeros], axis=0)
    qr_lo = jnp.concatenate([qr_top, zeros], axis=0)
    qr_hi = jnp.concatenate([zeros, qr_top], axis=0)
    t_row = s0 + lax.broadcasted_iota(jnp.int32, (1, tq), 1)
    lanes4 = lambda x: jnp.concatenate([x] * nh, axis=1)
    t_all = lanes4(t_row)

    size_classes = 4
    class_rows = n_cmp_pad // size_classes
    assert CMP_STRIDE * class_rows >= tq >= 2 * SLC_BLOCK

    def compress_and_select(nc):
        nb = nc // (SLC_BLOCK // CMP_STRIDE)
        cmp_end = lax.broadcasted_iota(jnp.int32, (nc, tq), 0) * CMP_STRIDE + (CMP_BLOCK - 1)
        sc = _dot(kcmp_ref[0, 0:nc, :], q_lo) + lanes4(jnp.where(cmp_end <= t_row, 0.0, NEG))
        mx = jnp.max(sc, axis=0, keepdims=True)
        ex = jnp.exp2(sc - mx)
        den = jnp.sum(ex, axis=0, keepdims=True)
        p_cmp = ex * jnp.where(t_all >= CMP_BLOCK - 1, 1.0 / den, 0.0)
        o_cmp = _dot(vcmpt_ref[0, :, 0:nc], p_cmp.astype(BF16))
        p_heads = p_cmp[:, 0:tq]
        for h in range(1, nh):
            p_heads = p_heads + p_cmp[:, h * tq:(h + 1) * tq]
        imp = _dot(ovt_ref[0:nb, 0:nc], p_heads.astype(BF16))

        blk_i = lax.broadcasted_iota(jnp.int32, (nb, tq), 0)
        blk = blk_i.astype(F32)
        cur = jnp.right_shift(t_row, 6)
        forced = (blk_i == 0) | (blk_i == cur) | (blk_i == cur - 1)
        valid = blk_i * SLC_BLOCK <= t_row
        sel = jnp.where(forced, 1.0, 0.0)
        n_forced = jnp.sum(sel, axis=0, keepdims=True)
        imp = jnp.where(valid, jnp.where(forced, -jnp.inf, imp), NEG)
        min_forced = 3 if nc > class_rows else 1
        for r in range(min(SLC_TOPN - min_forced, nb)):
            top = jnp.max(imp, axis=0, keepdims=True)
            first = jnp.min(jnp.where(imp == top, blk, float(nb)), axis=0, keepdims=True)
            pick = (blk == first) & (n_forced + r < SLC_TOPN)
            sel = jnp.where(pick, 1.0, sel)
            imp = jnp.where(pick, -jnp.inf, imp)
        bias = jnp.where(valid & (sel > 0.0), 0.0, NEG)
        own0 = qb * (tq // SLC_BLOCK)
        own = (blk_i >= own0) & (blk_i < own0 + tq // SLC_BLOCK)
        sel_scr[0:nb, :] = jnp.where(own, NEG, bias)
        if nb < n_slc:
            sel_scr[nb:n_slc, :] = jnp.full((n_slc - nb, tq), NEG, F32)
        key_blk = own0 + jnp.right_shift(lax.broadcasted_iota(jnp.int32, (tq, nb), 0), 6)
        expand = jnp.where(lax.broadcasted_iota(jnp.int32, (tq, nb), 1) == key_blk, 1.0, 0.0).astype(BF16)
        diag_scr[...] = _dot(expand, bias.astype(BF16))
        return o_cmp

    visible = (s0 + tq) // CMP_STRIDE
    o_cmp = lax.switch((visible - 1) // class_rows,
                       [functools.partial(compress_and_select, (i + 1) * class_rows) for i in range(size_classes)])

    blocks_per_tile = KT // SLC_BLOCK
    sub = KT // LANES

    last_tile = kk_ref.shape[1] // KT - 1

    def score_chunks(kt):
        kd = jnp.minimum(kt, last_tile)
        bias8 = sel_scr[pl.ds(pl.multiple_of(kd * blocks_per_tile, blocks_per_tile), blocks_per_tile), :]
        bias8 = jnp.where(kt <= last_tile, bias8, NEG)
        pad = [jnp.zeros((16 - blocks_per_tile, wq), F32)] if blocks_per_tile < 16 else []
        bias_rows = jnp.concatenate([lanes4(bias8)] + pad, axis=0).astype(BF16)
        q_bias = jnp.concatenate([qr_top, bias_rows, jnp.zeros((HEAD_DIM - 16, wq), BF16)], axis=0)
        s_tile = _dot(ksx_ref[0, pl.ds(pl.multiple_of(kd * KT, KT), KT), :], q_bias)
        for c in range(sub):
            yield c, s_tile[c * LANES:(c + 1) * LANES, :]

    def fold8(x, op):
        out = x[0:8, :]
        for r in range(1, x.shape[0] // 8):
            out = op(out, x[8 * r:8 * r + 8, :])
        return out

    def v_rows(first, count, lo):
        vt = vvt_ref[0, pl.ds(first, count)]
        return jnp.concatenate([vt[c, lo:lo + HEAD_DIM, :] for c in range(count)], axis=1)

    def pv_tile(kt, p_ref):
        return _dot(v_rows(pl.multiple_of(jnp.minimum(kt, last_tile) * sub, sub), sub, 0), p_ref[...])

    def phase(kt, s_cur, s_next, p_cur, p_prev, state):
        m_i, l_i, mt_cur, alpha_prev = state
        m_new = jnp.maximum(m_i, jnp.max(mt_cur, axis=0, keepdims=True))
        alpha = jnp.exp2(m_i - m_new)
        mt_next = jnp.full((8, wq), NEG, F32)
        l_new = alpha * l_i
        for c, s_n in score_chunks(kt + 1):
            rows = slice(c * LANES, (c + 1) * LANES)
            mt_next = jnp.maximum(mt_next, fold8(s_n, jnp.maximum))
            s_next[rows, :] = s_n
            p = jnp.exp2(s_cur[rows, :] - m_new)
            l_new = l_new + fold8(p, jnp.add)
            p_cur[rows, :] = p.astype(BF16)
        if p_prev is not None:
            acc_scr[...] = alpha_prev * acc_scr[...] + pv_tile(kt - 1, p_prev)
        return m_new, l_new, mt_next, alpha

    def slc_pair(j, state):
        state = phase(2 * j + 1, sb_scr, sa_scr, pb_scr, pa_scr, state)
        return phase(2 * j + 2, sa_scr, sb_scr, pa_scr, pb_scr, state)

    mt0 = jnp.full((8, wq), NEG, F32)
    for c, s_n in score_chunks(0):
        mt0 = jnp.maximum(mt0, fold8(s_n, jnp.maximum))
        sa_scr[c * LANES:(c + 1) * LANES, :] = s_n
    key_d = s0 + lax.broadcasted_iota(jnp.int32, (tq, tq), 0)
    s_d = (_dot(kk_ref[0, pl.ds(pl.multiple_of(s0, tq), tq), :], qr_lo)
           + lanes4(jnp.where(key_d <= t_row, diag_scr[...], NEG)))
    m_d = jnp.max(s_d, axis=0, keepdims=True)
    p_d = jnp.exp2(s_d - m_d)
    acc_scr[...] = _dot(v_rows(qb * (tq // LANES), tq // LANES, 0), p_d.astype(BF16))
    state = (m_d, fold8(p_d, jnp.add), mt0, jnp.ones((1, wq), F32))
    state = phase(0, sa_scr, sb_scr, pa_scr, None, state)
    n_tiles = (s0 + tq + KT - 1) // KT
    pairs = n_tiles // 2
    _, l8_s, _, alpha_last = lax.fori_loop(0, pairs, slc_pair, state)
    acc_s = alpha_last * acc_scr[...] + pv_tile(2 * pairs, pa_scr)
    l_s = jnp.sum(l8_s, axis=0, keepdims=True)

    wkeys = WIN + tq
    k0 = pl.multiple_of(jnp.maximum(s0 - WIN, 0), LANES)
    key = k0 + lax.broadcasted_iota(jnp.int32, (wkeys, tq), 0)
    wbias = jnp.where((key <= t_row) & (key > t_row - WIN), 0.0, NEG)
    s_w = _dot(kk_ref[0, pl.ds(k0, wkeys), :], qr_hi) + lanes4(wbias)
    vt_w = vvt_ref[0, pl.ds(jnp.maximum(qb * (tq // LANES) - WIN // LANES, 0), wkeys // LANES)]
    p_w = jnp.exp2(s_w - jnp.max(s_w, axis=0, keepdims=True))
    l_w = jnp.sum(p_w, axis=0, keepdims=True)
    v_w = jnp.concatenate([vt_w[i, HEAD_DIM:2 * HEAD_DIM, :] for i in range(wkeys // LANES)], axis=1)
    acc_w = _dot(v_w, p_w.astype(BF16))

    g = gt_ref[0]
    gate = lambda br: jnp.concatenate([g[h * 3 + br:h * 3 + br + 1, :] for h in range(nh)], axis=1)
    o = gate(0) * o_cmp + gate(1) * (acc_s * (1.0 / l_s)) + gate(2) * (acc_w * (1.0 / l_w))
    o_rows = jnp.concatenate([o[:, h * tq:(h + 1) * tq] for h in range(nh)], axis=0)
    yd_ref[...] = o_rows.T.astype(BF16)


def _nsa(qt, qrt, gt, kk, ksx, vvt, kcmp, vcmpt, ovt):
    b, _, s = qt.shape
    n_slc = s // SLC_BLOCK
    m = kcmp.shape[1]
    return pl.pallas_call(
        _nsa_kernel,
        grid=(b, s // TQ),
        in_specs=[
            pl.BlockSpec((1, 256, TQ), lambda bi, qi: (bi, 0, qi)),
            pl.BlockSpec((1, 256, TQ), lambda bi, qi: (bi, 0, qi)),
            pl.BlockSpec((1, 16, TQ), lambda bi, qi: (bi, 0, qi)),
            pl.BlockSpec((1, s, LANES), lambda bi, qi: (bi, 0, 0)),
            pl.BlockSpec((1, s, LANES), lambda bi, qi: (bi, 0, 0)),
            pl.BlockSpec((1, s // LANES, LANES, LANES), lambda bi, qi: (bi, 0, 0, 0)),
            pl.BlockSpec((1, m, LANES), lambda bi, qi: (bi, 0, 0)),
            pl.BlockSpec((1, HEAD_DIM, m), lambda bi, qi: (bi, 0, 0)),
            pl.BlockSpec((n_slc, m), lambda bi, qi: (0, 0)),
        ],
        out_specs=pl.BlockSpec((TQ, 256), lambda bi, qi: (bi * (s // TQ) + qi, 0)),
        out_shape=jax.ShapeDtypeStruct((b * s, 256), BF16),
        scratch_shapes=[pltpu.VMEM((n_slc, TQ), F32),
                        pltpu.VMEM((KT, NSA_HEADS * TQ), F32), pltpu.VMEM((KT, NSA_HEADS * TQ), F32),
                        pltpu.VMEM((KT, NSA_HEADS * TQ), BF16), pltpu.VMEM((KT, NSA_HEADS * TQ), BF16),
                        pltpu.VMEM((TQ, TQ), F32), pltpu.VMEM((HEAD_DIM, NSA_HEADS * TQ), F32)],
        compiler_params=pltpu.CompilerParams(dimension_semantics=("arbitrary", "arbitrary"),
                                             vmem_limit_bytes=VMEM_LIMIT),
        name="sparse_attention",
    )(qt, qrt, gt, kk, ksx, vvt, kcmp, vcmpt, ovt)


def _outproj_kernel(x_ref, yabc_ref, yd_ref, wo_ref, g_ref, b_ref, rw_ref, rb_ref, x1_ref, gates_ref):
    mix = _dot(yabc_ref[...], wo_ref[0:768, :]) + _dot(yd_ref[...], wo_ref[768:1024, :])
    x1 = _layer_norm(DN_ALPHA * x_ref[...] + mix, g_ref[...], b_ref[...])
    x1_ref[...] = x1

    ts = x1.shape[0]
    aff = _sigmoid(_dot(x1.astype(BF16), rw_ref[...])).T[0:N_EXPERTS, :]
    biased = aff + rb_ref[...]
    row = lambda a, i: a[i:i + 1, :]

    gscores = []
    for gi in range(N_GROUPS):
        a, b, c, d = (row(biased, gi * EXPERTS_PER_GROUP + i) for i in range(EXPERTS_PER_GROUP))
        hi1, lo1, hi2, lo2 = jnp.maximum(a, b), jnp.minimum(a, b), jnp.maximum(c, d), jnp.minimum(c, d)
        gscores.append(jnp.maximum(hi1, hi2) + jnp.maximum(jnp.minimum(hi1, hi2), jnp.maximum(lo1, lo2)))
    best, gsel = gscores[0], jnp.zeros((1, ts), jnp.int32)
    for gi in range(1, N_GROUPS):
        better = gscores[gi] > best
        gsel = jnp.where(better, gi, gsel)
        best = jnp.where(better, gscores[gi], best)

    eid_i = lax.broadcasted_iota(jnp.int32, (N_EXPERTS, ts), 0)
    eid = eid_i.astype(F32)
    masked = jnp.where(jnp.right_shift(eid_i, 2) == gsel, biased, NEG)
    picks = []
    for _ in range(2):
        top = jnp.max(masked, axis=0, keepdims=True)
        first = jnp.min(jnp.where(masked == top, eid, float(N_EXPERTS)), axis=0, keepdims=True)
        pick = eid == first
        picks.append(pick)
        masked = jnp.where(pick, -jnp.inf, masked)
    chosen = picks[0] | picks[1]
    w_sel = jnp.where(chosen, aff, 0.0)
    gates_t = w_sel / jnp.sum(w_sel, axis=0, keepdims=True)
    grp8 = jnp.where(lax.broadcasted_iota(jnp.int32, (8, ts), 0) == 0, gsel.astype(F32), 0.0)
    gates_ref[...] = jnp.concatenate([gates_t, grp8, jnp.zeros((LANES - N_EXPERTS - 8, ts), F32)], axis=0).T


def _outproj(l, x, yabc, yd, wo, g, b, rw, rb):
    nb, s, d = x.shape
    n = nb * s
    ts = TS_OUT
    per_row = s // ts
    const = lambda shape: pl.BlockSpec(shape, lambda i: (0,) * len(shape))
    return pl.pallas_call(
        _outproj_kernel,
        grid=(n // ts,),
        in_specs=[
            pl.BlockSpec((None, ts, d), lambda i: (i // per_row, i % per_row, 0)),
            pl.BlockSpec((ts, 768), lambda i: (i, 0)),
            pl.BlockSpec((ts, 256), lambda i: (i, 0)),
            _layer_spec(l, (d, d)), _layer_spec(l, (1, d)), _layer_spec(l, (1, d)), const((d, LANES)), const((N_EXPERTS, 1)),
        ],
        out_specs=(pl.BlockSpec((ts, d), lambda i: (i, 0)), pl.BlockSpec((ts, LANES), lambda i: (i, 0))),
        out_shape=(jax.ShapeDtypeStruct((n, d), F32), jax.ShapeDtypeStruct((n, LANES), F32)),
        compiler_params=pltpu.CompilerParams(dimension_semantics=("arbitrary",), vmem_limit_bytes=VMEM_LIMIT),
        name="outproj_ln_route",
    )(x, yabc, yd, wo, g, b, rw, rb)


def _split3(x):
    hi = x.astype(BF16)
    r1 = x - hi.astype(F32)
    mid = r1.astype(BF16)
    lo = (r1 - mid.astype(F32)).astype(BF16)
    return hi, mid, lo


def _dispatch_kernel(x1_ref, gates_ref, tri_ref, xs_ref, gs_ref, post_ref, cnt_ref):
    tm = x1_ref.shape[0]
    gates = gates_ref[...]
    gsel = gates.T[GROUP_LANE:GROUP_LANE + 1, :]
    onehot = jnp.where(lax.broadcasted_iota(jnp.int32, (8, tm), 0).astype(F32) == gsel, 1.0, 0.0)
    n = jnp.sum(onehot, axis=1, keepdims=True)
    n_al = jnp.floor((n + (ROW_ALIGN - 1)) * (1.0 / ROW_ALIGN)) * ROW_ALIGN
    starts = [jnp.zeros((1, 1), F32)]
    for gi in range(1, N_GROUPS):
        starts.append(starts[-1] + n_al[gi - 1:gi, :])
    start = jnp.concatenate(starts + [jnp.zeros((8 - N_GROUPS, 1), F32)], axis=0)
    before = _dot(onehot.astype(BF16), tri_ref[...])
    pos = jnp.sum(onehot * (start + before), axis=0, keepdims=True)
    perm = jnp.where(lax.broadcasted_iota(jnp.int32, (SORT_ROWS, tm), 0).astype(F32) == pos, 1.0, 0.0).astype(BF16)

    xs_ref[0, 0:SORT_ROWS, :] = _dot(perm, x1_ref[...].astype(BF16)).astype(BF16)
    xs_ref[0, SORT_ROWS:SORT_BUF, :] = jnp.zeros((SORT_BUF - SORT_ROWS, xs_ref.shape[2]), BF16)
    g_hi, g_mid, g_lo = _split3(gates)
    gs_ref[0, 0:SORT_ROWS, :] = _dot(perm, g_hi) + _dot(perm, g_mid) + _dot(perm, g_lo)
    gs_ref[0, SORT_ROWS:SORT_BUF, :] = jnp.zeros((SORT_BUF - SORT_ROWS, LANES), F32)
    pos8 = jnp.where(lax.broadcasted_iota(jnp.int32, (8, tm), 0) == 0, pos, 0.0)
    post_ref[...] = jnp.concatenate([pos8, jnp.zeros((LANES - 8, tm), F32)], axis=0).T
    cnt_ref[0] = jnp.broadcast_to(n, (8, LANES))


def _dispatch(x1, gates, tri):
    n, d = x1.shape
    tm = TM_MOE
    nt = n // tm
    return pl.pallas_call(
        _dispatch_kernel,
        grid=(nt,),
        in_specs=[
            pl.BlockSpec((tm, d), lambda i: (i, 0)),
            pl.BlockSpec((tm, LANES), lambda i: (i, 0)),
            pl.BlockSpec((tm, tm), lambda i: (0, 0)),
        ],
        out_specs=(
            pl.BlockSpec((1, SORT_BUF, d), lambda i: (i, 0, 0)),
            pl.BlockSpec((1, SORT_BUF, LANES), lambda i: (i, 0, 0)),
            pl.BlockSpec((tm, LANES), lambda i: (i, 0)),
            pl.BlockSpec((1, 8, LANES), lambda i: (i, 0, 0)),
        ),
        out_shape=(
            jax.ShapeDtypeStruct((nt, SORT_BUF, d), BF16),
            jax.ShapeDtypeStruct((nt, SORT_BUF, LANES), F32),
            jax.ShapeDtypeStruct((n, LANES), F32),
            jax.ShapeDtypeStruct((nt, 8, LANES), F32),
        ),
        compiler_params=pltpu.CompilerParams(dimension_semantics=("arbitrary",), vmem_limit_bytes=VMEM_LIMIT),
        name="moe_dispatch",
    )(x1, gates, tri)


def _moe_kernel(start_ref, trips_ref, xs_ref, gs_ref, wg_ref, wu_ref, wd_ref, acc_ref):
    i, e = pl.program_id(0), pl.program_id(1)

    @pl.when(e == 0)
    def _():
        acc_ref[...] = jnp.zeros(acc_ref.shape, F32)

    grp = lax.shift_right_logical(e, 2)
    slots = [(i * TILES_PER_STEP + tile) * N_GROUPS + grp for tile in range(TILES_PER_STEP)]
    row0 = [start_ref[s] for s in slots]
    need = [trips_ref[s] for s in slots]

    def window(w, carry):
        rows = [pl.ds(pl.multiple_of(jnp.minimum(r + w * WIN_ROWS, SORT_BUF - WIN_ROWS), ROW_ALIGN), WIN_ROWS)
                for r in row0]
        xw = jnp.concatenate([xs_ref[tile, rows[tile], :] for tile in range(TILES_PER_STEP)], axis=0)
        hg = _dot(xw, wg_ref[0])
        h = hg * _sigmoid(hg) * _dot(xw, wu_ref[0])
        y = _dot(h.astype(BF16), wd_ref[0])
        for tile in range(TILES_PER_STEP):
            gw = gs_ref[tile, rows[tile], :]
            lane = lax.broadcasted_iota(jnp.int32, gw.shape, 1)
            gcol = jnp.sum(jnp.where((lane == e) & (w < need[tile]), gw, 0.0), axis=1, keepdims=True)
            acc_ref[tile, rows[tile], :] += y[tile * WIN_ROWS:(tile + 1) * WIN_ROWS, :] * gcol
        return carry

    lax.fori_loop(0, functools.reduce(jnp.maximum, need), window, 0)


def _moe(l, start, trips, xs, gs, wg, wu, wd):
    nt, _, d = xs.shape
    _, ne, _, de = wg.shape
    tps = TILES_PER_STEP
    grid_spec = pltpu.PrefetchScalarGridSpec(
        num_scalar_prefetch=2,
        grid=(nt // tps, ne),
        in_specs=[
            pl.BlockSpec((tps, SORT_BUF, d), lambda i, e, *_: (i, 0, 0)),
            pl.BlockSpec((tps, SORT_BUF, LANES), lambda i, e, *_: (i, 0, 0)),
            pl.BlockSpec((None, 1, d, de), lambda i, e, *_: (l, e, 0, 0)),
            pl.BlockSpec((None, 1, d, de), lambda i, e, *_: (l, e, 0, 0)),
            pl.BlockSpec((None, 1, de, d), lambda i, e, *_: (l, e, 0, 0)),
        ],
        out_specs=pl.BlockSpec((tps, SORT_BUF, d), lambda i, e, *_: (i, 0, 0)),
    )
    return pl.pallas_call(
        _moe_kernel,
        grid_spec=grid_spec,
        out_shape=jax.ShapeDtypeStruct((nt, SORT_BUF, d), F32),
        compiler_params=pltpu.CompilerParams(dimension_semantics=("arbitrary", "arbitrary"),
                                             vmem_limit_bytes=VMEM_LIMIT),
        name="moe_experts",
    )(start, trips, xs, gs, wg, wu, wd)


def _combine_kernel(x1_ref, acc_ref, post_ref, g_ref, b_ref, out_ref):
    tm = x1_ref.shape[0]
    acc = acc_ref[0]
    hi = acc.astype(BF16)
    lo = (acc - hi.astype(F32)).astype(BF16)
    pos = post_ref[...][:, 0:1]
    unperm = jnp.where(lax.broadcasted_iota(jnp.int32, (tm, SORT_ROWS), 1).astype(F32) == pos, 1.0, 0.0).astype(BF16)
    moe = _dot(unperm, hi) + _dot(unperm, lo)
    out_ref[...] = _layer_norm(DN_ALPHA * x1_ref[...] + moe, g_ref[...], b_ref[...])


def _combine(l, x1, acc, post, g, b, batch):
    n, d = x1.shape
    tm = TM_MOE
    per_row = n // batch // tm
    return pl.pallas_call(
        _combine_kernel,
        grid=(n // tm,),
        in_specs=[
            pl.BlockSpec((tm, d), lambda i: (i, 0)),
            pl.BlockSpec((1, SORT_ROWS, d), lambda i: (i, 0, 0)),
            pl.BlockSpec((tm, LANES), lambda i: (i, 0)),
            _layer_spec(l, (1, d)), _layer_spec(l, (1, d)),
        ],
        out_specs=pl.BlockSpec((None, tm, d), lambda i: (i // per_row, i % per_row, 0)),
        out_shape=jax.ShapeDtypeStruct((batch, n // batch, d), F32),
        compiler_params=pltpu.CompilerParams(dimension_semantics=("arbitrary",), vmem_limit_bytes=VMEM_LIMIT),
        name="moe_combine_ln",
    )(x1, acc, post, g, b)


def _group_windows(cnt):
    n = cnt[:, 0:N_GROUPS, 0].astype(jnp.int32)
    n_al = (n + (ROW_ALIGN - 1)) // ROW_ALIGN * ROW_ALIGN
    start = jnp.cumsum(n_al, axis=1) - n_al
    trips = (n + (WIN_ROWS - 1)) // WIN_ROWS
    return start.reshape(-1), trips.reshape(-1)


def _permute_w_in(w):
    sizes = (256, 256, 256, 256, 256, 256, 256, 64, 64, 64, 64, 64, 64, 12)
    offs = np.concatenate([[0], np.cumsum(sizes)])
    part = lambda i: w[..., int(offs[i]):int(offs[i + 1])]
    a_b, a_c, a_h, b_val, b_gate, c_p, q, kc, vc, ks, vs, kw, vw, g = (part(i) for i in range(14))
    pad = jnp.zeros(w.shape[:-1] + (LANES - g.shape[-1],), w.dtype)
    return jnp.concatenate([a_b, a_c, a_h, b_val, b_gate, c_p, q, kc, vc, ks, kw, vs, vw, g, pad], axis=-1).astype(BF16)


def _rope_tables(s):
    inv = 1.0 / (ROPE_THETA ** (jnp.arange(0, HEAD_DIM, 2, dtype=F32) / HEAD_DIM))
    ang = jnp.arange(s, dtype=F32)[:, None] * inv[None, :]
    ang = jnp.concatenate([ang, ang], -1)
    sign = jnp.concatenate([-jnp.ones((HEAD_DIM // 2,), F32), jnp.ones((HEAD_DIM // 2,), F32)])
    return jnp.tile(jnp.cos(ang), (1, NSA_HEADS)), jnp.tile(jnp.sin(ang) * sign[None, :], (1, NSA_HEADS))


def _overlap_t(s, m):
    n_cmp = (s - CMP_BLOCK) // CMP_STRIDE + 1
    cmp_start = np.arange(m) * CMP_STRIDE
    slc_start = np.arange(s // SLC_BLOCK) * SLC_BLOCK
    ov = (cmp_start[None, :] < slc_start[:, None] + SLC_BLOCK) & (cmp_start[None, :] + CMP_BLOCK > slc_start[:, None])
    ov = ov & (np.arange(m)[None, :] < n_cmp)
    return jnp.asarray(ov, BF16)


def _block_diag(w):
    nl, g, c, _ = w.shape
    return jnp.einsum('lgcd,gh->lgchd', w, jnp.eye(g, dtype=w.dtype)).reshape(nl, g * c, g * c)


def kernel(x, w_in, conv_a_w, conv_b_w, conv_b_b, cf_ln_g, cf_ln_b, pool_w, pool_scale, cmp_pe_k, cmp_pe_v, cmp_k_w1, cmp_k_w2, cmp_v_w1, cmp_v_w2, w_out, ln1_g, ln1_b, ln2_g, ln2_b, router_w, router_bias, moe_w_gate, moe_w_up, moe_w_down):
    b, s, d = x.shape
    m = s // CMP_STRIDE
    tok_per_row = CMP_STRIDE
    cos4, sin4 = _rope_tables(s)
    ovt = _overlap_t(s, m)
    rw = jnp.concatenate([router_w, jnp.zeros((d, LANES - N_EXPERTS), router_w.dtype)], axis=1).astype(BF16)
    rb = router_bias.reshape(N_EXPERTS, 1).astype(F32)
    tri = jnp.asarray(np.triu(np.ones((TM_MOE, TM_MOE), np.float32), k=1), BF16)

    nl = w_in.shape[0]
    row = lambda v: v.reshape(nl, 1, -1)
    pad_w2 = lambda w: jnp.concatenate([w, jnp.zeros((nl, CMP_HIDDEN, LANES - HEAD_DIM), w.dtype)], axis=2).astype(BF16)
    w_in_p = _permute_w_in(w_in)
    pool_bd = _block_diag(pool_w).astype(BF16)
    pek, pev = (v.reshape(nl, 2, tok_per_row * HEAD_DIM) for v in (cmp_pe_k, cmp_pe_v))
    wk1, wv1, wk2, wv2 = cmp_k_w1.astype(BF16), cmp_v_w1.astype(BF16), pad_w2(cmp_k_w2), pad_w2(cmp_v_w2)
    wo = w_out.astype(BF16)
    wg, wu, wd = moe_w_gate.astype(BF16), moe_w_up.astype(BF16), moe_w_down.astype(BF16)
    cvbb, lng, lnb, pools = row(conv_b_b), row(cf_ln_g), row(cf_ln_b), row(pool_scale)
    g1, b1, g2, b2 = row(ln1_g), row(ln1_b), row(ln2_g), row(ln2_b)

    for l in range(DEPTH):
        yabc, qt, qrt, kcvc, kk, ksx, vvt, gt = _inproj(
            l, x, w_in_p, cos4, sin4, conv_a_w, conv_b_w, cvbb, lng, lnb, pool_bd, pools)
        kc16 = kcvc[:, :, 0:HEAD_DIM].reshape(b, m, tok_per_row * HEAD_DIM)
        vc16 = kcvc[:, :, HEAD_DIM:2 * HEAD_DIM].reshape(b, m, tok_per_row * HEAD_DIM)
        kcmp, vcmpt = _compress(l, kc16, vc16, pek, pev, wk1, wk2, wv1, wv2)
        yd = _nsa(qt, qrt, gt, kk, ksx, vvt, kcmp, vcmpt, ovt)
        x1, gates = _outproj(l, x, yabc, yd, wo, g1, b1, rw, rb)
        xs, gs, post, cnt = _dispatch(x1, gates, tri)
        start, trips = _group_windows(cnt)
        acc = _moe(l, start, trips, xs, gs, wg, wu, wd)
        x = _combine(l, x1, acc, post, g2, b2, b)
    return x
```

```python
import functools

import jax
import jax.numpy as jnp
import numpy as np
from jax import lax
from jax.experimental import pallas as pl
from jax.experimental.pallas import tpu as pltpu

F32 = jnp.float32
BF16 = jnp.bfloat16

D_MODEL = 1024
DEPTH = 2
GROUP_W = 256
HEAD_DIM = 64
SC_KERNEL = 3
CF_KERNEL = 31
POOL_WINDOWS = (2, 4, 8, 16)
POOL_GW = 64
NSA_HEADS = 4
CMP_BLOCK = 32
CMP_STRIDE = 16
CMP_HIDDEN = 256
SLC_BLOCK = 64
SLC_TOPN = 16
WIN = 512
N_EXPERTS = 16
N_GROUPS = 4
EXPERTS_PER_GROUP = 4
D_EXPERT = 512
DN_ALPHA = (2 * DEPTH) ** 0.25
LN_EPS = 1e-5
NEG = -1e30
FORCE = 1e4
ROPE_THETA = 10000.0
QK_SCALE = HEAD_DIM ** -0.5
LOG2E = 1.4426950408889634

LANES = 128
OFF_A, OFF_B, OFF_C, OFF_Q, OFF_KCVC, OFF_KK, OFF_VV, OFF_G, IN_PAD = 0, 768, 1280, 1536, 1792, 1920, 2048, 2176, 2304

HALO_A = 8
HALO_B = 32
HALO_C = 16

TS_IN = 512
TQ = 256
KT = 512
TS_OUT = 512
TM_MOE = 1024
ROW_ALIGN = 16
WIN_ROWS = 288
SORT_ROWS = TM_MOE + LANES
SORT_BUF = SORT_ROWS + 2 * LANES
GROUP_LANE = 16
TILES_PER_STEP = 2
assert SORT_ROWS >= TM_MOE + N_GROUPS * (ROW_ALIGN - 1)
assert SORT_BUF >= TM_MOE + N_GROUPS * (ROW_ALIGN - 1) + WIN_ROWS
assert WIN_ROWS % ROW_ALIGN == 0 and (SORT_BUF - WIN_ROWS) % ROW_ALIGN == 0
VMEM_LIMIT = 56 * 1024 * 1024


def _sigmoid(x):
    return jax.nn.sigmoid(x)


def _layer_norm(h, g, b):
    mu = jnp.mean(h, axis=-1, keepdims=True)
    d = h - mu
    var = jnp.mean(d * d, axis=-1, keepdims=True)
    return d * lax.rsqrt(var + LN_EPS) * g + b


def _dot(a, b):
    return jnp.dot(a, b, preferred_element_type=F32)


def _layer_spec(l, shape):
    return pl.BlockSpec((None,) + tuple(shape), lambda *_: (l,) + (0,) * len(shape))


def _inproj_kernel(x_ref, w_ref, cos_ref, sin_ref, cva_ref, cvb_ref, cvbb_ref, lng_ref, lnb_ref,
                   poolw_ref, pools_ref,
                   yabc_ref, qt_ref, qrt_ref, kcvc_ref, kk_ref, ksx_ref, vvt_ref, gt_ref,
                   ch_ext, u_ext, p_ext, u_sh):
    j = pl.program_id(1)
    ts = x_ref.shape[1]
    xb = x_ref[0].astype(BF16)

    def proj(lo, hi):
        return _dot(xb, w_ref[:, lo:hi])

    def carry_halo(ext, halo):
        @pl.when(j == 0)
        def _():
            ext[0:halo, :] = jnp.zeros((halo, ext.shape[1]), F32)

        @pl.when(j > 0)
        def _():
            ext[0:halo, :] = ext[ts:ts + halo, :]

    pa = proj(OFF_A, OFF_B)
    a_b, ch = pa[:, 0:256], pa[:, 256:512] * pa[:, 512:768]
    carry_halo(ch_ext, HALO_A)
    ch_ext[HALO_A:HALO_A + ts, :] = ch
    conv = cva_ref[2:3, :] * ch
    for k in range(SC_KERNEL - 1):
        off = HALO_A - (SC_KERNEL - 1) + k
        conv = conv + cva_ref[k:k + 1, :] * ch_ext[off:off + ts, :]
    yabc_ref[:, 0:256] = (a_b * conv).astype(BF16)

    pb = proj(OFF_B, OFF_C)
    u = pb[:, 0:256] * _sigmoid(pb[:, 256:512])
    carry_halo(u_ext, HALO_B)
    u_ext[HALO_B:HALO_B + ts, :] = u
    span = u_sh.shape[1]
    for r in range(1, 8):
        u_sh[r - 1, :, :] = u_ext[r:r + span, :]
    def mixer_c():
        pc = proj(OFF_C, OFF_Q)
        carry_halo(p_ext, HALO_C)
        p_ext[HALO_C:HALO_C + ts, :] = pc
        pe = p_ext[...]
        s2 = pe + pltpu.roll(pe, 1, 0)
        s4 = s2 + pltpu.roll(s2, 2, 0)
        s8 = s4 + pltpu.roll(s4, 4, 0)
        s16 = s8 + pltpu.roll(s8, 8, 0)
        lane = lax.broadcasted_iota(jnp.int32, (ts, 256), 1)
        grp = jnp.right_shift(lane, 6)
        wsum = jnp.where(grp == 0, s2[HALO_C:], jnp.where(grp == 1, s4[HALO_C:], jnp.where(grp == 2, s8[HALO_C:], s16[HALO_C:])))
        win = jnp.where(grp == 0, 2, jnp.where(grp == 1, 4, jnp.where(grp == 2, 8, 16)))
        t1 = j * ts + lax.broadcasted_iota(jnp.int32, (ts, 256), 0) + 1
        cnt = jnp.minimum(win, t1).astype(F32)
        dlt = wsum / cnt - pc
        yabc_ref[:, 512:768] = (_dot(dlt.astype(BF16), poolw_ref[...]) * pools_ref[...]).astype(BF16)

    def queries():
        cos4, sin4 = cos_ref[...], sin_ref[...]
        q = proj(OFF_Q, OFF_KCVC)
        lane_q = jnp.bitwise_and(lax.broadcasted_iota(jnp.int32, (ts, 256), 1), HEAD_DIM - 1)
        q_sw = jnp.where(lane_q < HEAD_DIM // 2, pltpu.roll(q, 256 - HEAD_DIM // 2, 1), pltpu.roll(q, HEAD_DIM // 2, 1))
        qr = q * cos4 + q_sw * sin4
        qt_ref[0] = (q * (QK_SCALE * LOG2E)).T.astype(BF16)
        qrt_ref[0] = (qr * (QK_SCALE * LOG2E)).T.astype(BF16)

    def compress_inputs():
        kcvc_ref[0] = proj(OFF_KCVC, OFF_KK)

    def keys():
        k2 = proj(OFF_KK, OFF_VV)
        lane_k = jnp.bitwise_and(lax.broadcasted_iota(jnp.int32, (ts, LANES), 1), HEAD_DIM - 1)
        k_sw = jnp.where(lane_k < HEAD_DIM // 2, pltpu.roll(k2, LANES - HEAD_DIM // 2, 1), pltpu.roll(k2, HEAD_DIM // 2, 1))
        k_rot = k2 * cos_ref[:, 0:LANES] + k_sw * sin_ref[:, 0:LANES]
        kk_ref[0] = k_rot.astype(BF16)
        lane_i = lax.broadcasted_iota(jnp.int32, (ts, LANES), 1)
        blk_in_tile = jnp.bitwise_and(jnp.right_shift(j * ts + lax.broadcasted_iota(jnp.int32, (ts, LANES), 0), 6),
                                      KT // SLC_BLOCK - 1)
        ksx_ref[0] = jnp.where(lane_i < HEAD_DIM, k_rot, jnp.where(lane_i - HEAD_DIM == blk_in_tile, 1.0, 0.0)).astype(BF16)

    def values():
        v2t = proj(OFF_VV, OFF_G).T
        for c in range(ts // LANES):
            vvt_ref[0, c] = v2t[:, c * LANES:(c + 1) * LANES].astype(BF16)

    def branch_gates():
        gt_ref[0] = _sigmoid(proj(OFF_G, IN_PAD)).T[0:16, :]

    interleaved = [mixer_c, queries, compress_inputs, keys, values, branch_gates]
    rows = 64
    assert ts // rows >= len(interleaved)
    for c in range(ts // rows):
        acc = jnp.zeros((rows, 256), F32) + cvbb_ref[...]
        for k in range(CF_KERNEL):
            off = HALO_B - (CF_KERNEL - 1) + k
            base = off // 8 * 8 + c * rows
            src = u_ext if off % 8 == 0 else u_sh.at[off % 8 - 1]
            acc = acc + cvb_ref[k:k + 1, :] * src[base:base + rows, :]
        v = _layer_norm(acc, lng_ref[...], lnb_ref[...])
        yabc_ref[c * rows:(c + 1) * rows, 256:512] = (v * _sigmoid(v)).astype(BF16)
        if c < len(interleaved):
            interleaved[c]()


def _inproj(l, x, w_in_p, cos4, sin4, cva, cvb, cvbb, lng, lnb, poolw, pools):
    b, s, d = x.shape
    ts = TS_IN
    grid = (b, s // ts)
    out_shape = (
        jax.ShapeDtypeStruct((b * s, 768), BF16),
        jax.ShapeDtypeStruct((b, 256, s), BF16),
        jax.ShapeDtypeStruct((b, 256, s), BF16),
        jax.ShapeDtypeStruct((b, s, LANES), F32),
        jax.ShapeDtypeStruct((b, s, LANES), BF16),
        jax.ShapeDtypeStruct((b, s, LANES), BF16),
        jax.ShapeDtypeStruct((b, s // LANES, LANES, LANES), BF16),
        jax.ShapeDtypeStruct((b, 16, s), F32),
    )
    return pl.pallas_call(
        _inproj_kernel,
        grid=grid,
        in_specs=[
            pl.BlockSpec((1, ts, d), lambda bi, ji: (bi, ji, 0)),
            _layer_spec(l, (d, IN_PAD)),
            pl.BlockSpec((ts, 256), lambda bi, ji: (ji, 0)),
            pl.BlockSpec((ts, 256), lambda bi, ji: (ji, 0)),
            _layer_spec(l, (SC_KERNEL, 256)), _layer_spec(l, (CF_KERNEL, 256)), _layer_spec(l, (1, 256)),
            _layer_spec(l, (1, 256)), _layer_spec(l, (1, 256)), _layer_spec(l, (256, 256)), _layer_spec(l, (1, 256)),
        ],
        out_specs=(
            pl.BlockSpec((ts, 768), lambda bi, ji: (bi * (s // ts) + ji, 0)),
            pl.BlockSpec((1, 256, ts), lambda bi, ji: (bi, 0, ji)),
            pl.BlockSpec((1, 256, ts), lambda bi, ji: (bi, 0, ji)),
            pl.BlockSpec((1, ts, LANES), lambda bi, ji: (bi, ji, 0)),
            pl.BlockSpec((1, ts, LANES), lambda bi, ji: (bi, ji, 0)),
            pl.BlockSpec((1, ts, LANES), lambda bi, ji: (bi, ji, 0)),
            pl.BlockSpec((1, ts // LANES, LANES, LANES), lambda bi, ji: (bi, ji, 0, 0)),
            pl.BlockSpec((1, 16, ts), lambda bi, ji: (bi, 0, ji)),
        ),
        out_shape=out_shape,
        scratch_shapes=[
            pltpu.VMEM((ts + HALO_A, 256), F32),
            pltpu.VMEM((ts + HALO_B, 256), F32),
            pltpu.VMEM((ts + HALO_C, 256), F32),
            pltpu.VMEM((7, ts + HALO_B - 8, 256), F32),
        ],
        compiler_params=pltpu.CompilerParams(dimension_semantics=("arbitrary", "arbitrary"),
                                             vmem_limit_bytes=VMEM_LIMIT),
        name="inproj_mixers",
    )(x, w_in_p, cos4, sin4, cva, cvb, cvbb, lng, lnb, poolw, pools)


def _compress_kernel(kcvc_ref, pe_ref, w1_ref, wk2_ref, wv2_ref, kcmp_ref, vcmpt_ref):
    m = kcmp_ref.shape[1]

    def half(first_pos):
        acc = None
        for l in range(first_pos, first_pos + CMP_STRIDE):
            x_l = kcvc_ref[0, pl.ds(l - first_pos, m, stride=CMP_STRIDE), :] + pe_ref[l:l + 1, :]
            part = _dot(x_l.astype(BF16), w1_ref[l])
            acc = part if acc is None else acc + part
        return acc

    hid = half(0) + pltpu.roll(half(CMP_STRIDE), m - 1, 0)
    act = (hid * _sigmoid(hid)).astype(BF16)
    kcmp_ref[0] = _dot(act[:, 0:CMP_HIDDEN], wk2_ref[...]).astype(BF16)
    vcmpt_ref[0] = _dot(act[:, CMP_HIDDEN:2 * CMP_HIDDEN], wv2_ref[...]).T[0:HEAD_DIM, :].astype(BF16)


def _compress(l, kcvc, pe, w1, wk2, wv2):
    b, s, _ = kcvc.shape
    m = s // CMP_STRIDE
    return pl.pallas_call(
        _compress_kernel,
        grid=(b,),
        in_specs=[
            pl.BlockSpec((1, s, LANES), lambda bi: (bi, 0, 0)),
            _layer_spec(l, (CMP_BLOCK, LANES)),
            _layer_spec(l, (CMP_BLOCK, LANES, 2 * CMP_HIDDEN)),
            _layer_spec(l, (CMP_HIDDEN, LANES)), _layer_spec(l, (CMP_HIDDEN, LANES)),
        ],
        out_specs=(
            pl.BlockSpec((1, m, LANES), lambda bi: (bi, 0, 0)),
            pl.BlockSpec((1, HEAD_DIM, m), lambda bi: (bi, 0, 0)),
        ),
        out_shape=(jax.ShapeDtypeStruct((b, m, LANES), BF16), jax.ShapeDtypeStruct((b, HEAD_DIM, m), BF16)),
        compiler_params=pltpu.CompilerParams(dimension_semantics=("arbitrary",), vmem_limit_bytes=VMEM_LIMIT),
        name="compress_kv",
    )(kcvc, pe, w1, wk2, wv2)


def _nsa_kernel(qt_ref, qrt_ref, gt_ref, kk_ref, ksx_ref, vvt_ref, kcmp_ref, vcmpt_ref, ovt_ref, yd_ref, sel_scr, sa_scr, sb_scr, pa_scr, pb_scr, diag_scr, acc_scr):
    qb = pl.program_id(1)
    tq = qt_ref.shape[2]
    nh = NSA_HEADS
    wq = nh * tq
    s0 = qb * tq
    n_slc = sel_scr.shape[0]
    n_cmp_pad = kcmp_ref.shape[1]

    def stack_heads(ref):
        x = ref[0]
        return jnp.concatenate([x[h * HEAD_DIM:(h + 1) * HEAD_DIM, :] for h in range(nh)], axis=1)

    zeros = jnp.zeros((HEAD_DIM, wq), BF16)
    q_top, qr_top = stack_heads(qt_ref), stack_heads(qrt_ref)
    q_lo = jnp.concatenate([q_top, zeros], axis=0)
    qr_lo = jnp.concatenate([qr_top, zeros], axis=0)
    qr_hi = jnp.concatenate([zeros, qr_top], axis=0)
    t_row = s0 + lax.broadcasted_iota(jnp.int32, (1, tq), 1)
    lanes4 = lambda x: jnp.concatenate([x] * nh, axis=1)
    t_all = lanes4(t_row)

    size_classes = 4
    class_rows = n_cmp_pad // size_classes
    assert CMP_STRIDE * class_rows >= tq >= 2 * SLC_BLOCK

    def compress_and_select(nc):
        nb = nc // (SLC_BLOCK // CMP_STRIDE)
        cmp_end = lax.broadcasted_iota(jnp.int32, (nc, tq), 0) * CMP_STRIDE + (CMP_BLOCK - 1)
        sc = _dot(kcmp_ref[0, 0:nc, :], q_lo) + lanes4(jnp.where(cmp_end <= t_row, 0.0, NEG))
        mx = jnp.max(sc, axis=0, keepdims=True)
        ex = jnp.exp2(sc - mx)
        den = jnp.sum(ex, axis=0, keepdims=True)
        p_cmp = ex * jnp.where(t_all >= CMP_BLOCK - 1, 1.0 / den, 0.0)
        o_cmp = _dot(vcmpt_ref[0, :, 0:nc], p_cmp.astype(BF16))
        p_heads = p_cmp[:, 0:tq]
        for h in range(1, nh):
            p_heads = p_heads + p_cmp[:, h * tq:(h + 1) * tq]
        imp = _dot(ovt_ref[0:nb, 0:nc], p_heads.astype(BF16))

        blk_i = lax.broadcasted_iota(jnp.int32, (nb, tq), 0)
        blk = blk_i.astype(F32)
        cur = jnp.right_shift(t_row, 6)
        forced = (blk_i == 0) | (blk_i == cur) | (blk_i == cur - 1)
        valid = blk_i * SLC_BLOCK <= t_row
        sel = jnp.where(forced, 1.0, 0.0)
        n_forced = jnp.sum(sel, axis=0, keepdims=True)
        imp = jnp.where(valid, jnp.where(forced, -jnp.inf, imp), NEG)
        min_forced = 3 if nc > class_rows else 1
        for r in range(min(SLC_TOPN - min_forced, nb)):
            top = jnp.max(imp, axis=0, keepdims=True)
            first = jnp.min(jnp.where(imp == top, blk, float(nb)), axis=0, keepdims=True)
            pick = (blk == first) & (n_forced + r < SLC_TOPN)
            sel = jnp.where(pick, 1.0, sel)
            imp = jnp.where(pick, -jnp.inf, imp)
        bias = jnp.where(valid & (sel > 0.0), 0.0, NEG)
        own0 = qb * (tq // SLC_BLOCK)
        own = (blk_i >= own0) & (blk_i < own0 + tq // SLC_BLOCK)
        sel_scr[0:nb, :] = jnp.where(own, NEG, bias)
        if nb < n_slc:
            sel_scr[nb:n_slc, :] = jnp.full((n_slc - nb, tq), NEG, F32)
        key_blk = own0 + jnp.right_shift(lax.broadcasted_iota(jnp.int32, (tq, nb), 0), 6)
        expand = jnp.where(lax.broadcasted_iota(jnp.int32, (tq, nb), 1) == key_blk, 1.0, 0.0).astype(BF16)
        diag_scr[...] = _dot(expand, bias.astype(BF16))
        return o_cmp

    visible = (s0 + tq) // CMP_STRIDE
    o_cmp = lax.switch((visible - 1) // class_rows,
                       [functools.partial(compress_and_select, (i + 1) * class_rows) for i in range(size_classes)])

    blocks_per_tile = KT // SLC_BLOCK
    sub = KT // LANES

    last_tile = kk_ref.shape[1] // KT - 1

    def score_chunks(kt):
        kd = jnp.minimum(kt, last_tile)
        bias8 = sel_scr[pl.ds(pl.multiple_of(kd * blocks_per_tile, blocks_per_tile), blocks_per_tile), :]
        bias8 = jnp.where(kt <= last_tile, bias8, NEG)
        pad = [jnp.zeros((16 - blocks_per_tile, wq), F32)] if blocks_per_tile < 16 else []
        bias_rows = jnp.concatenate([lanes4(bias8)] + pad, axis=0).astype(BF16)
        q_bias = jnp.concatenate([qr_top, bias_rows, jnp.zeros((HEAD_DIM - 16, wq), BF16)], axis=0)
        s_tile = _dot(ksx_ref[0, pl.ds(pl.multiple_of(kd * KT, KT), KT), :], q_bias)
        for c in range(sub):
            yield c, s_tile[c * LANES:(c + 1) * LANES, :]

    def fold8(x, op):
        out = x[0:8, :]
        for r in range(1, x.shape[0] // 8):
            out = op(out, x[8 * r:8 * r + 8, :])
        return out

    def v_rows(first, count, lo):
        vt = vvt_ref[0, pl.ds(first, count)]
        return jnp.concatenate([vt[c, lo:lo + HEAD_DIM, :] for c in range(count)], axis=1)

    def pv_tile(kt, p_ref):
        return _dot(v_rows(pl.multiple_of(jnp.minimum(kt, last_tile) * sub, sub), sub, 0), p_ref[...])

    def phase(kt, s_cur, s_next, p_cur, p_prev, state):
        m_i, l_i, mt_cur, alpha_prev = state
        m_new = jnp.maximum(m_i, jnp.max(mt_cur, axis=0, keepdims=True))
        alpha = jnp.exp2(m_i - m_new)
        mt_next = jnp.full((8, wq), NEG, F32)
        l_new = alpha * l_i
        for c, s_n in score_chunks(kt + 1):
            rows = slice(c * LANES, (c + 1) * LANES)
            mt_next = jnp.maximum(mt_next, fold8(s_n, jnp.maximum))
            s_next[rows, :] = s_n
            p = jnp.exp2(s_cur[rows, :] - m_new)
            l_new = l_new + fold8(p, jnp.add)
            p_cur[rows, :] = p.astype(BF16)
        if p_prev is not None:
            acc_scr[...] = alpha_prev * acc_scr[...] + pv_tile(kt - 1, p_prev)
        return m_new, l_new, mt_next, alpha

    def slc_pair(j, state):
        state = phase(2 * j + 1, sb_scr, sa_scr, pb_scr, pa_scr, state)
        return phase(2 * j + 2, sa_scr, sb_scr, pa_scr, pb_scr, state)

    mt0 = jnp.full((8, wq), NEG, F32)
    for c, s_n in score_chunks(0):
        mt0 = jnp.maximum(mt0, fold8(s_n, jnp.maximum))
        sa_scr[c * LANES:(c + 1) * LANES, :] = s_n
    key_d = s0 + lax.broadcasted_iota(jnp.int32, (tq, tq), 0)
    s_d = (_dot(kk_ref[0, pl.ds(pl.multiple_of(s0, tq), tq), :], qr_lo)
           + lanes4(jnp.where(key_d <= t_row, diag_scr[...], NEG)))
    m_d = jnp.max(s_d, axis=0, keepdims=True)
    p_d = jnp.exp2(s_d - m_d)
    acc_scr[...] = _dot(v_rows(qb * (tq // LANES), tq // LANES, 0), p_d.astype(BF16))
    state = (m_d, fold8(p_d, jnp.add), mt0, jnp.ones((1, wq), F32))
    state = phase(0, sa_scr, sb_scr, pa_scr, None, state)
    n_tiles = (s0 + tq + KT - 1) // KT
    pairs = n_tiles // 2
    _, l8_s, _, alpha_last = lax.fori_loop(0, pairs, slc_pair, state)
    acc_s = alpha_last * acc_scr[...] + pv_tile(2 * pairs, pa_scr)
    l_s = jnp.sum(l8_s, axis=0, keepdims=True)

    wkeys = WIN + tq
    k0 = pl.multiple_of(jnp.maximum(s0 - WIN, 0), LANES)
    key = k0 + lax.broadcasted_iota(jnp.int32, (wkeys, tq), 0)
    wbias = jnp.where((key <= t_row) & (key > t_row - WIN), 0.0, NEG)
    s_w = _dot(kk_ref[0, pl.ds(k0, wkeys), :], qr_hi) + lanes4(wbias)
    vt_w = vvt_ref[0, pl.ds(jnp.maximum(qb * (tq // LANES) - WIN // LANES, 0), wkeys // LANES)]
    p_w = jnp.exp2(s_w - jnp.max(s_w, axis=0, keepdims=True))
    l_w = jnp.sum(p_w, axis=0, keepdims=True)
    v_w = jnp.concatenate([vt_w[i, HEAD_DIM:2 * HEAD_DIM, :] for i in range(wkeys // LANES)], axis=1)
    acc_w = _dot(v_w, p_w.astype(BF16))

    g = gt_ref[0]
    gate = lambda br: jnp.concatenate([g[h * 3 + br:h * 3 + br + 1, :] for h in range(nh)], axis=1)
    o = gate(0) * o_cmp + gate(1) * (acc_s * (1.0 / l_s)) + gate(2) * (acc_w * (1.0 / l_w))
    o_rows = jnp.concatenate([o[:, h * tq:(h + 1) * tq] for h in range(nh)], axis=0)
    yd_ref[...] = o_rows.T.astype(BF16)


def _nsa(qt, qrt, gt, kk, ksx, vvt, kcmp, vcmpt, ovt):
    b, _, s = qt.shape
    n_slc = s // SLC_BLOCK
    m = kcmp.shape[1]
    return pl.pallas_call(
        _nsa_kernel,
        grid=(b, s // TQ),
        in_specs=[
            pl.BlockSpec((1, 256, TQ), lambda bi, qi: (bi, 0, qi)),
            pl.BlockSpec((1, 256, TQ), lambda bi, qi: (bi, 0, qi)),
            pl.BlockSpec((1, 16, TQ), lambda bi, qi: (bi, 0, qi)),
            pl.BlockSpec((1, s, LANES), lambda bi, qi: (bi, 0, 0)),
            pl.BlockSpec((1, s, LANES), lambda bi, qi: (bi, 0, 0)),
            pl.BlockSpec((1, s // LANES, LANES, LANES), lambda bi, qi: (bi, 0, 0, 0)),
            pl.BlockSpec((1, m, LANES), lambda bi, qi: (bi, 0, 0)),
            pl.BlockSpec((1, HEAD_DIM, m), lambda bi, qi: (bi, 0, 0)),
            pl.BlockSpec((n_slc, m), lambda bi, qi: (0, 0)),
        ],
        out_specs=pl.BlockSpec((TQ, 256), lambda bi, qi: (bi * (s // TQ) + qi, 0)),
        out_shape=jax.ShapeDtypeStruct((b * s, 256), BF16),
        scratch_shapes=[pltpu.VMEM((n_slc, TQ), F32),
                        pltpu.VMEM((KT, NSA_HEADS * TQ), F32), pltpu.VMEM((KT, NSA_HEADS * TQ), F32),
                        pltpu.VMEM((KT, NSA_HEADS * TQ), BF16), pltpu.VMEM((KT, NSA_HEADS * TQ), BF16),
                        pltpu.VMEM((TQ, TQ), F32), pltpu.VMEM((HEAD_DIM, NSA_HEADS * TQ), F32)],
        compiler_params=pltpu.CompilerParams(dimension_semantics=("arbitrary", "arbitrary"),
                                             vmem_limit_bytes=VMEM_LIMIT),
        name="sparse_attention",
    )(qt, qrt, gt, kk, ksx, vvt, kcmp, vcmpt, ovt)


def _outproj_kernel(x_ref, yabc_ref, yd_ref, wo_ref, g_ref, b_ref, rw_ref, rb_ref, x1_ref, gates_ref):
    mix = _dot(yabc_ref[...], wo_ref[0:768, :]) + _dot(yd_ref[...], wo_ref[768:1024, :])
    x1 = _layer_norm(DN_ALPHA * x_ref[...] + mix, g_ref[...], b_ref[...])
    x1_ref[...] = x1

    ts = x1.shape[0]
    aff = _sigmoid(_dot(x1.astype(BF16), rw_ref[...])).T[0:N_EXPERTS, :]
    biased = aff + rb_ref[...]
    row = lambda a, i: a[i:i + 1, :]

    gscores = []
    for gi in range(N_GROUPS):
        a, b, c, d = (row(biased, gi * EXPERTS_PER_GROUP + i) for i in range(EXPERTS_PER_GROUP))
        hi1, lo1, hi2, lo2 = jnp.maximum(a, b), jnp.minimum(a, b), jnp.maximum(c, d), jnp.minimum(c, d)
        gscores.append(jnp.maximum(hi1, hi2) + jnp.maximum(jnp.minimum(hi1, hi2), jnp.maximum(lo1, lo2)))
    best, gsel = gscores[0], jnp.zeros((1, ts), jnp.int32)
    for gi in range(1, N_GROUPS):
        better = gscores[gi] > best
        gsel = jnp.where(better, gi, gsel)
        best = jnp.where(better, gscores[gi], best)

    eid_i = lax.broadcasted_iota(jnp.int32, (N_EXPERTS, ts), 0)
    eid = eid_i.astype(F32)
    masked = jnp.where(jnp.right_shift(eid_i, 2) == gsel, biased, NEG)
    picks = []
    for _ in range(2):
        top = jnp.max(masked, axis=0, keepdims=True)
        first = jnp.min(jnp.where(masked == top, eid, float(N_EXPERTS)), axis=0, keepdims=True)
        pick = eid == first
        picks.append(pick)
        masked = jnp.where(pick, -jnp.inf, masked)
    chosen = picks[0] | picks[1]
    w_sel = jnp.where(chosen, aff, 0.0)
    gates_t = w_sel / jnp.sum(w_sel, axis=0, keepdims=True)
    grp8 = jnp.where(lax.broadcasted_iota(jnp.int32, (8, ts), 0) == 0, gsel.astype(F32), 0.0)
    gates_ref[...] = jnp.concatenate([gates_t, grp8, jnp.zeros((LANES - N_EXPERTS - 8, ts), F32)], axis=0).T


def _outproj(l, x, yabc, yd, wo, g, b, rw, rb):
    nb, s, d = x.shape
    n = nb * s
    ts = TS_OUT
    per_row = s // ts
    const = lambda shape: pl.BlockSpec(shape, lambda i: (0,) * len(shape))
    return pl.pallas_call(
        _outproj_kernel,
        grid=(n // ts,),
        in_specs=[
            pl.BlockSpec((None, ts, d), lambda i: (i // per_row, i % per_row, 0)),
            pl.BlockSpec((ts, 768), lambda i: (i, 0)),
            pl.BlockSpec((ts, 256), lambda i: (i, 0)),
            _layer_spec(l, (d, d)), _layer_spec(l, (1, d)), _layer_spec(l, (1, d)), const((d, LANES)), const((N_EXPERTS, 1)),
        ],
        out_specs=(pl.BlockSpec((ts, d), lambda i: (i, 0)), pl.BlockSpec((ts, LANES), lambda i: (i, 0))),
        out_shape=(jax.ShapeDtypeStruct((n, d), F32), jax.ShapeDtypeStruct((n, LANES), F32)),
        compiler_params=pltpu.CompilerParams(dimension_semantics=("arbitrary",), vmem_limit_bytes=VMEM_LIMIT),
        name="outproj_ln_route",
    )(x, yabc, yd, wo, g, b, rw, rb)


def _split3(x):
    hi = x.astype(BF16)
    r1 = x - hi.astype(F32)
    mid = r1.astype(BF16)
    lo = (r1 - mid.astype(F32)).astype(BF16)
    return hi, mid, lo


def _dispatch_kernel(x1_ref, gates_ref, tri_ref, xs_ref, gs_ref, post_ref, cnt_ref):
    tm = x1_ref.shape[0]
    gates = gates_ref[...]
    gsel = gates.T[GROUP_LANE:GROUP_LANE + 1, :]
    onehot = jnp.where(lax.broadcasted_iota(jnp.int32, (8, tm), 0).astype(F32) == gsel, 1.0, 0.0)
    n = jnp.sum(onehot, axis=1, keepdims=True)
    n_al = jnp.floor((n + (ROW_ALIGN - 1)) * (1.0 / ROW_ALIGN)) * ROW_ALIGN
    starts = [jnp.zeros((1, 1), F32)]
    for gi in range(1, N_GROUPS):
        starts.append(starts[-1] + n_al[gi - 1:gi, :])
    start = jnp.concatenate(starts + [jnp.zeros((8 - N_GROUPS, 1), F32)], axis=0)
    before = _dot(onehot.astype(BF16), tri_ref[...])
    pos = jnp.sum(onehot * (start + before), axis=0, keepdims=True)
    perm = jnp.where(lax.broadcasted_iota(jnp.int32, (SORT_ROWS, tm), 0).astype(F32) == pos, 1.0, 0.0).astype(BF16)

    xs_ref[0, 0:SORT_ROWS, :] = _dot(perm, x1_ref[...].astype(BF16)).astype(BF16)
    xs_ref[0, SORT_ROWS:SORT_BUF, :] = jnp.zeros((SORT_BUF - SORT_ROWS, xs_ref.shape[2]), BF16)
    g_hi, g_mid, g_lo = _split3(gates)
    gs_ref[0, 0:SORT_ROWS, :] = _dot(perm, g_hi) + _dot(perm, g_mid) + _dot(perm, g_lo)
    gs_ref[0, SORT_ROWS:SORT_BUF, :] = jnp.zeros((SORT_BUF - SORT_ROWS, LANES), F32)
    pos8 = jnp.where(lax.broadcasted_iota(jnp.int32, (8, tm), 0) == 0, pos, 0.0)
    post_ref[...] = jnp.concatenate([pos8, jnp.zeros((LANES - 8, tm), F32)], axis=0).T
    cnt_ref[0] = jnp.broadcast_to(n, (8, LANES))


def _dispatch(x1, gates, tri):
    n, d = x1.shape
    tm = TM_MOE
    nt = n // tm
    return pl.pallas_call(
        _dispatch_kernel,
        grid=(nt,),
        in_specs=[
            pl.BlockSpec((tm, d), lambda i: (i, 0)),
            pl.BlockSpec((tm, LANES), lambda i: (i, 0)),
            pl.BlockSpec((tm, tm), lambda i: (0, 0)),
        ],
        out_specs=(
            pl.BlockSpec((1, SORT_BUF, d), lambda i: (i, 0, 0)),
            pl.BlockSpec((1, SORT_BUF, LANES), lambda i: (i, 0, 0)),
            pl.BlockSpec((tm, LANES), lambda i: (i, 0)),
            pl.BlockSpec((1, 8, LANES), lambda i: (i, 0, 0)),
        ),
        out_shape=(
            jax.ShapeDtypeStruct((nt, SORT_BUF, d), BF16),
            jax.ShapeDtypeStruct((nt, SORT_BUF, LANES), F32),
            jax.ShapeDtypeStruct((n, LANES), F32),
            jax.ShapeDtypeStruct((nt, 8, LANES), F32),
        ),
        compiler_params=pltpu.CompilerParams(dimension_semantics=("arbitrary",), vmem_limit_bytes=VMEM_LIMIT),
        name="moe_dispatch",
    )(x1, gates, tri)


def _moe_kernel(start_ref, trips_ref, xs_ref, gs_ref, wg_ref, wu_ref, wd_ref, acc_ref):
    i, e = pl.program_id(0), pl.program_id(1)

    @pl.when(e == 0)
    def _():
        acc_ref[...] = jnp.zeros(acc_ref.shape, F32)

    grp = lax.shift_right_logical(e, 2)
    slots = [(i * TILES_PER_STEP + tile) * N_GROUPS + grp for tile in range(TILES_PER_STEP)]
    row0 = [start_ref[s] for s in slots]
    need = [trips_ref[s] for s in slots]

    def window(w, carry):
        rows = [pl.ds(pl.multiple_of(jnp.minimum(r + w * WIN_ROWS, SORT_BUF - WIN_ROWS), ROW_ALIGN), WIN_ROWS)
                for r in row0]
        xw = jnp.concatenate([xs_ref[tile, rows[tile], :] for tile in range(TILES_PER_STEP)], axis=0)
        hg = _dot(xw, wg_ref[0])
        h = hg * _sigmoid(hg) * _dot(xw, wu_ref[0])
        y = _dot(h.astype(BF16), wd_ref[0])
        for tile in range(TILES_PER_STEP):
            gw = gs_ref[tile, rows[tile], :]
            lane = lax.broadcasted_iota(jnp.int32, gw.shape, 1)
            gcol = jnp.sum(jnp.where((lane == e) & (w < need[tile]), gw, 0.0), axis=1, keepdims=True)
            acc_ref[tile, rows[tile], :] += y[tile * WIN_ROWS:(tile + 1) * WIN_ROWS, :] * gcol
        return carry

    lax.fori_loop(0, functools.reduce(jnp.maximum, need), window, 0)


def _moe(l, start, trips, xs, gs, wg, wu, wd):
    nt, _, d = xs.shape
    _, ne, _, de = wg.shape
    tps = TILES_PER_STEP
    grid_spec = pltpu.PrefetchScalarGridSpec(
        num_scalar_prefetch=2,
        grid=(nt // tps, ne),
        in_specs=[
            pl.BlockSpec((tps, SORT_BUF, d), lambda i, e, *_: (i, 0, 0)),
            pl.BlockSpec((tps, SORT_BUF, LANES), lambda i, e, *_: (i, 0, 0)),
            pl.BlockSpec((None, 1, d, de), lambda i, e, *_: (l, e, 0, 0)),
            pl.BlockSpec((None, 1, d, de), lambda i, e, *_: (l, e, 0, 0)),
            pl.BlockSpec((None, 1, de, d), lambda i, e, *_: (l, e, 0, 0)),
        ],
        out_specs=pl.BlockSpec((tps, SORT_BUF, d), lambda i, e, *_: (i, 0, 0)),
    )
    return pl.pallas_call(
        _moe_kernel,
        grid_spec=grid_spec,
        out_shape=jax.ShapeDtypeStruct((nt, SORT_BUF, d), F32),
        compiler_params=pltpu.CompilerParams(dimension_semantics=("arbitrary", "arbitrary"),
                                             vmem_limit_bytes=VMEM_LIMIT),
        name="moe_experts",
    )(start, trips, xs, gs, wg, wu, wd)


def _combine_kernel(x1_ref, acc_ref, post_ref, g_ref, b_ref, out_ref):
    tm = x1_ref.shape[0]
    acc = acc_ref[0]
    hi = acc.astype(BF16)
    lo = (acc - hi.astype(F32)).astype(BF16)
    pos = post_ref[...][:, 0:1]
    unperm = jnp.where(lax.broadcasted_iota(jnp.int32, (tm, SORT_ROWS), 1).astype(F32) == pos, 1.0, 0.0).astype(BF16)
    moe = _dot(unperm, hi) + _dot(unperm, lo)
    out_ref[...] = _layer_norm(DN_ALPHA * x1_ref[...] + moe, g_ref[...], b_ref[...])


def _combine(l, x1, acc, post, g, b, batch):
    n, d = x1.shape
    tm = TM_MOE
    per_row = n // batch // tm
    return pl.pallas_call(
        _combine_kernel,
        grid=(n // tm,),
        in_specs=[
            pl.BlockSpec((tm, d), lambda i: (i, 0)),
            pl.BlockSpec((1, SORT_ROWS, d), lambda i: (i, 0, 0)),
            pl.BlockSpec((tm, LANES), lambda i: (i, 0)),
            _layer_spec(l, (1, d)), _layer_spec(l, (1, d)),
        ],
        out_specs=pl.BlockSpec((None, tm, d), lambda i: (i // per_row, i % per_row, 0)),
        out_shape=jax.ShapeDtypeStruct((batch, n // batch, d), F32),
        compiler_params=pltpu.CompilerParams(dimension_semantics=("arbitrary",), vmem_limit_bytes=VMEM_LIMIT),
        name="moe_combine_ln",
    )(x1, acc, post, g, b)


def _group_windows(cnt):
    n = cnt[:, 0:N_GROUPS, 0].astype(jnp.int32)
    n_al = (n + (ROW_ALIGN - 1)) // ROW_ALIGN * ROW_ALIGN
    start = jnp.cumsum(n_al, axis=1) - n_al
    trips = (n + (WIN_ROWS - 1)) // WIN_ROWS
    return start.reshape(-1), trips.reshape(-1)


def _permute_w_in(w):
    sizes = (256, 256, 256, 256, 256, 256, 256, 64, 64, 64, 64, 64, 64, 12)
    offs = np.concatenate([[0], np.cumsum(sizes)])
    part = lambda i: w[..., int(offs[i]):int(offs[i + 1])]
    a_b, a_c, a_h, b_val, b_gate, c_p, q, kc, vc, ks, vs, kw, vw, g = (part(i) for i in range(14))
    pad = jnp.zeros(w.shape[:-1] + (LANES - g.shape[-1],), w.dtype)
    return jnp.concatenate([a_b, a_c, a_h, b_val, b_gate, c_p, q, kc, vc, ks, kw, vs, vw, g, pad], axis=-1).astype(BF16)


def _rope_tables(s):
    inv = 1.0 / (ROPE_THETA ** (jnp.arange(0, HEAD_DIM, 2, dtype=F32) / HEAD_DIM))
    ang = jnp.arange(s, dtype=F32)[:, None] * inv[None, :]
    ang = jnp.concatenate([ang, ang], -1)
    sign = jnp.concatenate([-jnp.ones((HEAD_DIM // 2,), F32), jnp.ones((HEAD_DIM // 2,), F32)])
    return jnp.tile(jnp.cos(ang), (1, NSA_HEADS)), jnp.tile(jnp.sin(ang) * sign[None, :], (1, NSA_HEADS))


def _overlap_t(s, m):
    n_cmp = (s - CMP_BLOCK) // CMP_STRIDE + 1
    cmp_start = np.arange(m) * CMP_STRIDE
    slc_start = np.arange(s // SLC_BLOCK) * SLC_BLOCK
    ov = (cmp_start[None, :] < slc_start[:, None] + SLC_BLOCK) & (cmp_start[None, :] + CMP_BLOCK > slc_start[:, None])
    ov = ov & (np.arange(m)[None, :] < n_cmp)
    return jnp.asarray(ov, BF16)


def _block_diag(w):
    nl, g, c, _ = w.shape
    return jnp.einsum('lgcd,gh->lgchd', w, jnp.eye(g, dtype=w.dtype)).reshape(nl, g * c, g * c)


def kernel(x, w_in, conv_a_w, conv_b_w, conv_b_b, cf_ln_g, cf_ln_b, pool_w, pool_scale, cmp_pe_k, cmp_pe_v, cmp_k_w1, cmp_k_w2, cmp_v_w1, cmp_v_w2, w_out, ln1_g, ln1_b, ln2_g, ln2_b, router_w, router_bias, moe_w_gate, moe_w_up, moe_w_down):
    b, s, d = x.shape
    m = s // CMP_STRIDE
    cos4, sin4 = _rope_tables(s)
    ovt = _overlap_t(s, m)
    rw = jnp.concatenate([router_w, jnp.zeros((d, LANES - N_EXPERTS), router_w.dtype)], axis=1).astype(BF16)
    rb = router_bias.reshape(N_EXPERTS, 1).astype(F32)
    tri = jnp.asarray(np.triu(np.ones((TM_MOE, TM_MOE), np.float32), k=1), BF16)

    nl = w_in.shape[0]
    row = lambda v: v.reshape(nl, 1, -1)
    pad_w2 = lambda w: jnp.concatenate([w, jnp.zeros((nl, CMP_HIDDEN, LANES - HEAD_DIM), w.dtype)], axis=2).astype(BF16)
    w_in_p = _permute_w_in(w_in)
    pool_bd = _block_diag(pool_w).astype(BF16)
    pe_kv = jnp.concatenate([cmp_pe_k, cmp_pe_v], axis=-1)
    zeros_w1 = jnp.zeros((nl, CMP_BLOCK, HEAD_DIM, CMP_HIDDEN), cmp_k_w1.dtype)
    w1_kv = jnp.concatenate([
        jnp.concatenate([cmp_k_w1.reshape(nl, CMP_BLOCK, HEAD_DIM, CMP_HIDDEN), zeros_w1], axis=-1),
        jnp.concatenate([zeros_w1, cmp_v_w1.reshape(nl, CMP_BLOCK, HEAD_DIM, CMP_HIDDEN)], axis=-1)], axis=2).astype(BF16)
    wk2, wv2 = pad_w2(cmp_k_w2), pad_w2(cmp_v_w2)
    wo = w_out.astype(BF16)
    wg, wu, wd = moe_w_gate.astype(BF16), moe_w_up.astype(BF16), moe_w_down.astype(BF16)
    cvbb, lng, lnb, pools = row(conv_b_b), row(cf_ln_g), row(cf_ln_b), row(pool_scale)
    g1, b1, g2, b2 = row(ln1_g), row(ln1_b), row(ln2_g), row(ln2_b)

    for l in range(DEPTH):
        yabc, qt, qrt, kcvc, kk, ksx, vvt, gt = _inproj(
            l, x, w_in_p, cos4, sin4, conv_a_w, conv_b_w, cvbb, lng, lnb, pool_bd, pools)
        kcmp, vcmpt = _compress(l, kcvc, pe_kv, w1_kv, wk2, wv2)
        yd = _nsa(qt, qrt, gt, kk, ksx, vvt, kcmp, vcmpt, ovt)
        x1, gates = _outproj(l, x, yabc, yd, wo, g1, b1, rw, rb)
        xs, gs, post, cnt = _dispatch(x1, gates, tri)
        start, trips = _group_windows(cnt)
        acc = _moe(l, start, trips, xs, gs, wg, wu, wd)
        x = _combine(l, x1, acc, post, g2, b2, b)
    return x
```

```python
import functools

import jax
import jax.numpy as jnp
import numpy as np
from jax import lax
from jax.experimental import pallas as pl
from jax.experimental.pallas import tpu as pltpu

F32 = jnp.float32
BF16 = jnp.bfloat16

D_MODEL = 1024
DEPTH = 2
GROUP_W = 256
HEAD_DIM = 64
SC_KERNEL = 3
CF_KERNEL = 31
POOL_WINDOWS = (2, 4, 8, 16)
POOL_GW = 64
NSA_HEADS = 4
CMP_BLOCK = 32
CMP_STRIDE = 16
CMP_HIDDEN = 256
SLC_BLOCK = 64
SLC_TOPN = 16
WIN = 512
N_EXPERTS = 16
N_GROUPS = 4
EXPERTS_PER_GROUP = 4
D_EXPERT = 512
DN_ALPHA = (2 * DEPTH) ** 0.25
LN_EPS = 1e-5
NEG = -1e30
FORCE = 1e4
ROPE_THETA = 10000.0
QK_SCALE = HEAD_DIM ** -0.5
LOG2E = 1.4426950408889634

LANES = 128
OFF_A, OFF_B, OFF_C, OFF_Q, OFF_KCVC, OFF_KK, OFF_VV, OFF_G, IN_PAD = 0, 768, 1280, 1536, 1792, 1920, 2048, 2176, 2304

HALO_A = 8
HALO_B = 32
HALO_C = 16

TS_IN = 512
TQ = 512
KT = 512
TS_OUT = 512
TM_MOE = 1024
ROW_ALIGN = 16
WIN_ROWS = 288
SORT_ROWS = TM_MOE + LANES
SORT_BUF = SORT_ROWS + 2 * LANES
GROUP_LANE = 16
TILES_PER_STEP = 2
assert SORT_ROWS >= TM_MOE + N_GROUPS * (ROW_ALIGN - 1)
assert SORT_BUF >= TM_MOE + N_GROUPS * (ROW_ALIGN - 1) + WIN_ROWS
assert WIN_ROWS % ROW_ALIGN == 0 and (SORT_BUF - WIN_ROWS) % ROW_ALIGN == 0
VMEM_LIMIT = 56 * 1024 * 1024


def _sigmoid(x):
    return jax.nn.sigmoid(x)


def _layer_norm(h, g, b):
    mu = jnp.mean(h, axis=-1, keepdims=True)
    d = h - mu
    var = jnp.mean(d * d, axis=-1, keepdims=True)
    return d * lax.rsqrt(var + LN_EPS) * g + b


def _dot(a, b):
    return jnp.dot(a, b, preferred_element_type=F32)


def _layer_spec(l, shape):
    return pl.BlockSpec((None,) + tuple(shape), lambda *_: (l,) + (0,) * len(shape))


def _inproj_kernel(x_ref, w_ref, cos_ref, sin_ref, cva_ref, cvb_ref, cvbb_ref, lng_ref, lnb_ref,
                   poolw_ref, pools_ref,
                   yabc_ref, qt_ref, qrt_ref, kcvc_ref, kk_ref, ksx_ref, vvt_ref, gt_ref,
                   ch_ext, u_ext, p_ext, u_sh):
    j = pl.program_id(1)
    ts = x_ref.shape[1]
    xb = x_ref[0].astype(BF16)

    def proj(lo, hi):
        return _dot(xb, w_ref[:, lo:hi])

    def carry_halo(ext, halo):
        @pl.when(j == 0)
        def _():
            ext[0:halo, :] = jnp.zeros((halo, ext.shape[1]), F32)

        @pl.when(j > 0)
        def _():
            ext[0:halo, :] = ext[ts:ts + halo, :]

    pa = proj(OFF_A, OFF_B)
    a_b, ch = pa[:, 0:256], pa[:, 256:512] * pa[:, 512:768]
    carry_halo(ch_ext, HALO_A)
    ch_ext[HALO_A:HALO_A + ts, :] = ch
    conv = cva_ref[2:3, :] * ch
    for k in range(SC_KERNEL - 1):
        off = HALO_A - (SC_KERNEL - 1) + k
        conv = conv + cva_ref[k:k + 1, :] * ch_ext[off:off + ts, :]
    yabc_ref[:, 0:256] = (a_b * conv).astype(BF16)

    pb = proj(OFF_B, OFF_C)
    u = pb[:, 0:256] * _sigmoid(pb[:, 256:512])
    carry_halo(u_ext, HALO_B)
    u_ext[HALO_B:HALO_B + ts, :] = u
    span = u_sh.shape[1]
    for r in range(1, 8):
        u_sh[r - 1, :, :] = u_ext[r:r + span, :]
    def mixer_c():
        pc = proj(OFF_C, OFF_Q)
        carry_halo(p_ext, HALO_C)
        p_ext[HALO_C:HALO_C + ts, :] = pc
        pe = p_ext[...]
        s2 = pe + pltpu.roll(pe, 1, 0)
        s4 = s2 + pltpu.roll(s2, 2, 0)
        s8 = s4 + pltpu.roll(s4, 4, 0)
        s16 = s8 + pltpu.roll(s8, 8, 0)
        lane = lax.broadcasted_iota(jnp.int32, (ts, 256), 1)
        grp = jnp.right_shift(lane, 6)
        wsum = jnp.where(grp == 0, s2[HALO_C:], jnp.where(grp == 1, s4[HALO_C:], jnp.where(grp == 2, s8[HALO_C:], s16[HALO_C:])))
        win = jnp.where(grp == 0, 2, jnp.where(grp == 1, 4, jnp.where(grp == 2, 8, 16)))
        t1 = j * ts + lax.broadcasted_iota(jnp.int32, (ts, 256), 0) + 1
        cnt = jnp.minimum(win, t1).astype(F32)
        dlt = wsum / cnt - pc
        yabc_ref[:, 512:768] = (_dot(dlt.astype(BF16), poolw_ref[...]) * pools_ref[...]).astype(BF16)

    def queries():
        cos4, sin4 = cos_ref[...], sin_ref[...]
        q = proj(OFF_Q, OFF_KCVC)
        lane_q = jnp.bitwise_and(lax.broadcasted_iota(jnp.int32, (ts, 256), 1), HEAD_DIM - 1)
        q_sw = jnp.where(lane_q < HEAD_DIM // 2, pltpu.roll(q, 256 - HEAD_DIM // 2, 1), pltpu.roll(q, HEAD_DIM // 2, 1))
        qr = q * cos4 + q_sw * sin4
        qt_ref[0] = (q * (QK_SCALE * LOG2E)).T.astype(BF16)
        qrt_ref[0] = (qr * (QK_SCALE * LOG2E)).T.astype(BF16)

    def compress_inputs():
        kcvc_ref[0] = proj(OFF_KCVC, OFF_KK)

    def keys():
        k2 = proj(OFF_KK, OFF_VV)
        lane_k = jnp.bitwise_and(lax.broadcasted_iota(jnp.int32, (ts, LANES), 1), HEAD_DIM - 1)
        k_sw = jnp.where(lane_k < HEAD_DIM // 2, pltpu.roll(k2, LANES - HEAD_DIM // 2, 1), pltpu.roll(k2, HEAD_DIM // 2, 1))
        k_rot = k2 * cos_ref[:, 0:LANES] + k_sw * sin_ref[:, 0:LANES]
        kk_ref[0] = k_rot.astype(BF16)
        lane_i = lax.broadcasted_iota(jnp.int32, (ts, LANES), 1)
        blk_in_tile = jnp.bitwise_and(jnp.right_shift(j * ts + lax.broadcasted_iota(jnp.int32, (ts, LANES), 0), 6),
                                      KT // SLC_BLOCK - 1)
        ksx_ref[0] = jnp.where(lane_i < HEAD_DIM, k_rot, jnp.where(lane_i - HEAD_DIM == blk_in_tile, 1.0, 0.0)).astype(BF16)

    def values():
        v2t = proj(OFF_VV, OFF_G).T
        for c in range(ts // LANES):
            vvt_ref[0, c] = v2t[:, c * LANES:(c + 1) * LANES].astype(BF16)

    def branch_gates():
        gt_ref[0] = _sigmoid(proj(OFF_G, IN_PAD)).T[0:16, :]

    interleaved = [mixer_c, queries, compress_inputs, keys, values, branch_gates]
    rows = 64
    assert ts // rows >= len(interleaved)
    for c in range(ts // rows):
        acc = jnp.zeros((rows, 256), F32) + cvbb_ref[...]
        for k in range(CF_KERNEL):
            off = HALO_B - (CF_KERNEL - 1) + k
            base = off // 8 * 8 + c * rows
            src = u_ext if off % 8 == 0 else u_sh.at[off % 8 - 1]
            acc = acc + cvb_ref[k:k + 1, :] * src[base:base + rows, :]
        v = _layer_norm(acc, lng_ref[...], lnb_ref[...])
        yabc_ref[c * rows:(c + 1) * rows, 256:512] = (v * _sigmoid(v)).astype(BF16)
        if c < len(interleaved):
            interleaved[c]()


def _inproj(l, x, w_in_p, cos4, sin4, cva, cvb, cvbb, lng, lnb, poolw, pools):
    b, s, d = x.shape
    ts = TS_IN
    grid = (b, s // ts)
    out_shape = (
        jax.ShapeDtypeStruct((b * s, 768), BF16),
        jax.ShapeDtypeStruct((b, 256, s), BF16),
        jax.ShapeDtypeStruct((b, 256, s), BF16),
        jax.ShapeDtypeStruct((b, s, LANES), F32),
        jax.ShapeDtypeStruct((b, s, LANES), BF16),
        jax.ShapeDtypeStruct((b, s, LANES), BF16),
        jax.ShapeDtypeStruct((b, s // LANES, LANES, LANES), BF16),
        jax.ShapeDtypeStruct((b, 16, s), F32),
    )
    return pl.pallas_call(
        _inproj_kernel,
        grid=grid,
        in_specs=[
            pl.BlockSpec((1, ts, d), lambda bi, ji: (bi, ji, 0)),
            _layer_spec(l, (d, IN_PAD)),
            pl.BlockSpec((ts, 256), lambda bi, ji: (ji, 0)),
            pl.BlockSpec((ts, 256), lambda bi, ji: (ji, 0)),
            _layer_spec(l, (SC_KERNEL, 256)), _layer_spec(l, (CF_KERNEL, 256)), _layer_spec(l, (1, 256)),
            _layer_spec(l, (1, 256)), _layer_spec(l, (1, 256)), _layer_spec(l, (256, 256)), _layer_spec(l, (1, 256)),
        ],
        out_specs=(
            pl.BlockSpec((ts, 768), lambda bi, ji: (bi * (s // ts) + ji, 0)),
            pl.BlockSpec((1, 256, ts), lambda bi, ji: (bi, 0, ji)),
            pl.BlockSpec((1, 256, ts), lambda bi, ji: (bi, 0, ji)),
            pl.BlockSpec((1, ts, LANES), lambda bi, ji: (bi, ji, 0)),
            pl.BlockSpec((1, ts, LANES), lambda bi, ji: (bi, ji, 0)),
            pl.BlockSpec((1, ts, LANES), lambda bi, ji: (bi, ji, 0)),
            pl.BlockSpec((1, ts // LANES, LANES, LANES), lambda bi, ji: (bi, ji, 0, 0)),
            pl.BlockSpec((1, 16, ts), lambda bi, ji: (bi, 0, ji)),
        ),
        out_shape=out_shape,
        scratch_shapes=[
            pltpu.VMEM((ts + HALO_A, 256), F32),
            pltpu.VMEM((ts + HALO_B, 256), F32),
            pltpu.VMEM((ts + HALO_C, 256), F32),
            pltpu.VMEM((7, ts + HALO_B - 8, 256), F32),
        ],
        compiler_params=pltpu.CompilerParams(dimension_semantics=("arbitrary", "arbitrary"),
                                             vmem_limit_bytes=VMEM_LIMIT),
        name="inproj_mixers",
    )(x, w_in_p, cos4, sin4, cva, cvb, cvbb, lng, lnb, poolw, pools)


def _compress_kernel(kcvc_ref, pe_ref, w1_ref, wk2_ref, wv2_ref, kcmp_ref, vcmpt_ref):
    m = kcmp_ref.shape[1]

    def half(first_pos):
        acc = None
        for l in range(first_pos, first_pos + CMP_STRIDE):
            x_l = kcvc_ref[0, pl.ds(l - first_pos, m, stride=CMP_STRIDE), :] + pe_ref[l:l + 1, :]
            part = _dot(x_l.astype(BF16), w1_ref[l])
            acc = part if acc is None else acc + part
        return acc

    hid = half(0) + pltpu.roll(half(CMP_STRIDE), m - 1, 0)
    act = (hid * _sigmoid(hid)).astype(BF16)
    kcmp_ref[0] = _dot(act[:, 0:CMP_HIDDEN], wk2_ref[...]).astype(BF16)
    vcmpt_ref[0] = _dot(act[:, CMP_HIDDEN:2 * CMP_HIDDEN], wv2_ref[...]).T[0:HEAD_DIM, :].astype(BF16)


def _compress(l, kcvc, pe, w1, wk2, wv2):
    b, s, _ = kcvc.shape
    m = s // CMP_STRIDE
    return pl.pallas_call(
        _compress_kernel,
        grid=(b,),
        in_specs=[
            pl.BlockSpec((1, s, LANES), lambda bi: (bi, 0, 0)),
            _layer_spec(l, (CMP_BLOCK, LANES)),
            _layer_spec(l, (CMP_BLOCK, LANES, 2 * CMP_HIDDEN)),
            _layer_spec(l, (CMP_HIDDEN, LANES)), _layer_spec(l, (CMP_HIDDEN, LANES)),
        ],
        out_specs=(
            pl.BlockSpec((1, m, LANES), lambda bi: (bi, 0, 0)),
            pl.BlockSpec((1, HEAD_DIM, m), lambda bi: (bi, 0, 0)),
        ),
        out_shape=(jax.ShapeDtypeStruct((b, m, LANES), BF16), jax.ShapeDtypeStruct((b, HEAD_DIM, m), BF16)),
        compiler_params=pltpu.CompilerParams(dimension_semantics=("arbitrary",), vmem_limit_bytes=VMEM_LIMIT),
        name="compress_kv",
    )(kcvc, pe, w1, wk2, wv2)


def _nsa_kernel(qt_ref, qrt_ref, gt_ref, kk_ref, ksx_ref, vvt_ref, kcmp_ref, vcmpt_ref, ovt_ref, yd_ref, sel_scr, sa_scr, sb_scr, pa_scr, pb_scr, diag_scr, acc_scr):
    qb = pl.program_id(1)
    tq = qt_ref.shape[2]
    nh = NSA_HEADS
    wq = nh * tq
    s0 = qb * tq
    n_slc = sel_scr.shape[0]
    n_cmp_pad = kcmp_ref.shape[1]

    def stack_heads(ref):
        x = ref[0]
        return jnp.concatenate([x[h * HEAD_DIM:(h + 1) * HEAD_DIM, :] for h in range(nh)], axis=1)

    zeros = jnp.zeros((HEAD_DIM, wq), BF16)
    q_top, qr_top = stack_heads(qt_ref), stack_heads(qrt_ref)
    q_lo = jnp.concatenate([q_top, zeros], axis=0)
    qr_lo = jnp.concatenate([qr_top, zeros], axis=0)
    qr_hi = jnp.concatenate([zeros, qr_top], axis=0)
    t_row = s0 + lax.broadcasted_iota(jnp.int32, (1, tq), 1)
    lanes4 = lambda x: jnp.concatenate([x] * nh, axis=1)
    t_all = lanes4(t_row)

    size_classes = 4
    class_rows = n_cmp_pad // size_classes
    assert CMP_STRIDE * class_rows >= tq >= 2 * SLC_BLOCK

    def compress_and_select(nc):
        nb = nc // (SLC_BLOCK // CMP_STRIDE)
        cmp_end = lax.broadcasted_iota(jnp.int32, (nc, tq), 0) * CMP_STRIDE + (CMP_BLOCK - 1)
        sc = _dot(kcmp_ref[0, 0:nc, :], q_lo) + lanes4(jnp.where(cmp_end <= t_row, 0.0, NEG))
        mx = jnp.max(sc, axis=0, keepdims=True)
        ex = jnp.exp2(sc - mx)
        den = jnp.sum(ex, axis=0, keepdims=True)
        p_cmp = ex * jnp.where(t_all >= CMP_BLOCK - 1, 1.0 / den, 0.0)
        o_cmp = _dot(vcmpt_ref[0, :, 0:nc], p_cmp.astype(BF16))
        p_heads = p_cmp[:, 0:tq]
        for h in range(1, nh):
            p_heads = p_heads + p_cmp[:, h * tq:(h + 1) * tq]
        imp = _dot(ovt_ref[0:nb, 0:nc], p_heads.astype(BF16))

        blk_i = lax.broadcasted_iota(jnp.int32, (nb, tq), 0)
        blk = blk_i.astype(F32)
        cur = jnp.right_shift(t_row, 6)
        forced = (blk_i == 0) | (blk_i == cur) | (blk_i == cur - 1)
        valid = blk_i * SLC_BLOCK <= t_row
        sel = jnp.where(forced, 1.0, 0.0)
        n_forced = jnp.sum(sel, axis=0, keepdims=True)
        imp = jnp.where(valid, jnp.where(forced, -jnp.inf, imp), NEG)
        min_forced = 3 if nc > class_rows else 1
        for r in range(min(SLC_TOPN - min_forced, nb)):
            top = jnp.max(imp, axis=0, keepdims=True)
            first = jnp.min(jnp.where(imp == top, blk, float(nb)), axis=0, keepdims=True)
            pick = (blk == first) & (n_forced + r < SLC_TOPN)
            sel = jnp.where(pick, 1.0, sel)
            imp = jnp.where(pick, -jnp.inf, imp)
        bias = jnp.where(valid & (sel > 0.0), 0.0, NEG)
        own0 = qb * (tq // SLC_BLOCK)
        own = (blk_i >= own0) & (blk_i < own0 + tq // SLC_BLOCK)
        sel_scr[0:nb, :] = jnp.where(own, NEG, bias)
        if nb < n_slc:
            sel_scr[nb:n_slc, :] = jnp.full((n_slc - nb, tq), NEG, F32)
        key_blk = own0 + jnp.right_shift(lax.broadcasted_iota(jnp.int32, (tq, nb), 0), 6)
        expand = jnp.where(lax.broadcasted_iota(jnp.int32, (tq, nb), 1) == key_blk, 1.0, 0.0).astype(BF16)
        diag_scr[...] = _dot(expand, bias.astype(BF16))
        return o_cmp

    visible = (s0 + tq) // CMP_STRIDE
    o_cmp = lax.switch((visible - 1) // class_rows,
                       [functools.partial(compress_and_select, (i + 1) * class_rows) for i in range(size_classes)])

    blocks_per_tile = KT // SLC_BLOCK
    sub = KT // LANES

    last_tile = kk_ref.shape[1] // KT - 1

    def score_chunks(kt):
        kd = jnp.minimum(kt, last_tile)
        bias8 = sel_scr[pl.ds(pl.multiple_of(kd * blocks_per_tile, blocks_per_tile), blocks_per_tile), :]
        bias8 = jnp.where(kt <= last_tile, bias8, NEG)
        pad = [jnp.zeros((16 - blocks_per_tile, wq), F32)] if blocks_per_tile < 16 else []
        bias_rows = jnp.concatenate([lanes4(bias8)] + pad, axis=0).astype(BF16)
        q_bias = jnp.concatenate([qr_top, bias_rows, jnp.zeros((HEAD_DIM - 16, wq), BF16)], axis=0)
        s_tile = _dot(ksx_ref[0, pl.ds(pl.multiple_of(kd * KT, KT), KT), :], q_bias)
        for c in range(sub):
            yield c, s_tile[c * LANES:(c + 1) * LANES, :]

    def fold8(x, op):
        out = x[0:8, :]
        for r in range(1, x.shape[0] // 8):
            out = op(out, x[8 * r:8 * r + 8, :])
        return out

    def v_rows(first, count, lo):
        vt = vvt_ref[0, pl.ds(first, count)]
        return jnp.concatenate([vt[c, lo:lo + HEAD_DIM, :] for c in range(count)], axis=1)

    def pv_tile(kt, p_ref):
        return _dot(v_rows(pl.multiple_of(jnp.minimum(kt, last_tile) * sub, sub), sub, 0), p_ref[...])

    def phase(kt, s_cur, s_next, p_cur, p_prev, state):
        m_i, l_i, mt_cur, alpha_prev = state
        m_new = jnp.maximum(m_i, jnp.max(mt_cur, axis=0, keepdims=True))
        alpha = jnp.exp2(m_i - m_new)
        mt_next = jnp.full((8, wq), NEG, F32)
        l_new = alpha * l_i
        for c, s_n in score_chunks(kt + 1):
            rows = slice(c * LANES, (c + 1) * LANES)
            mt_next = jnp.maximum(mt_next, fold8(s_n, jnp.maximum))
            s_next[rows, :] = s_n
            p = jnp.exp2(s_cur[rows, :] - m_new)
            l_new = l_new + fold8(p, jnp.add)
            p_cur[rows, :] = p.astype(BF16)
        if p_prev is not None:
            acc_scr[...] = alpha_prev * acc_scr[...] + pv_tile(kt - 1, p_prev)
        return m_new, l_new, mt_next, alpha

    def slc_pair(j, state):
        state = phase(2 * j + 1, sb_scr, sa_scr, pb_scr, pa_scr, state)
        return phase(2 * j + 2, sa_scr, sb_scr, pa_scr, pb_scr, state)

    mt0 = jnp.full((8, wq), NEG, F32)
    for c, s_n in score_chunks(0):
        mt0 = jnp.maximum(mt0, fold8(s_n, jnp.maximum))
        sa_scr[c * LANES:(c + 1) * LANES, :] = s_n
    key_d = s0 + lax.broadcasted_iota(jnp.int32, (tq, tq), 0)
    s_d = (_dot(kk_ref[0, pl.ds(pl.multiple_of(s0, tq), tq), :], qr_lo)
           + lanes4(jnp.where(key_d <= t_row, diag_scr[...], NEG)))
    m_d = jnp.max(s_d, axis=0, keepdims=True)
    p_d = jnp.exp2(s_d - m_d)
    acc_scr[...] = _dot(v_rows(qb * (tq // LANES), tq // LANES, 0), p_d.astype(BF16))
    state = (m_d, fold8(p_d, jnp.add), mt0, jnp.ones((1, wq), F32))
    state = phase(0, sa_scr, sb_scr, pa_scr, None, state)
    n_tiles = (s0 + tq + KT - 1) // KT
    pairs = n_tiles // 2
    _, l8_s, _, alpha_last = lax.fori_loop(0, pairs, slc_pair, state)
    acc_s = alpha_last * acc_scr[...] + pv_tile(2 * pairs, pa_scr)
    l_s = jnp.sum(l8_s, axis=0, keepdims=True)

    wkeys = WIN + tq
    k0 = pl.multiple_of(jnp.maximum(s0 - WIN, 0), LANES)
    key = k0 + lax.broadcasted_iota(jnp.int32, (wkeys, tq), 0)
    wbias = jnp.where((key <= t_row) & (key > t_row - WIN), 0.0, NEG)
    s_w = _dot(kk_ref[0, pl.ds(k0, wkeys), :], qr_hi) + lanes4(wbias)
    vt_w = vvt_ref[0, pl.ds(jnp.maximum(qb * (tq // LANES) - WIN // LANES, 0), wkeys // LANES)]
    p_w = jnp.exp2(s_w - jnp.max(s_w, axis=0, keepdims=True))
    l_w = jnp.sum(p_w, axis=0, keepdims=True)
    v_w = jnp.concatenate([vt_w[i, HEAD_DIM:2 * HEAD_DIM, :] for i in range(wkeys // LANES)], axis=1)
    acc_w = _dot(v_w, p_w.astype(BF16))

    g = gt_ref[0]
    gate = lambda br: jnp.concatenate([g[h * 3 + br:h * 3 + br + 1, :] for h in range(nh)], axis=1)
    o = gate(0) * o_cmp + gate(1) * (acc_s * (1.0 / l_s)) + gate(2) * (acc_w * (1.0 / l_w))
    o_rows = jnp.concatenate([o[:, h * tq:(h + 1) * tq] for h in range(nh)], axis=0)
    yd_ref[...] = o_rows.T.astype(BF16)


def _nsa(qt, qrt, gt, kk, ksx, vvt, kcmp, vcmpt, ovt):
    b, _, s = qt.shape
    n_slc = s // SLC_BLOCK
    m = kcmp.shape[1]
    return pl.pallas_call(
        _nsa_kernel,
        grid=(b, s // TQ),
        in_specs=[
            pl.BlockSpec((1, 256, TQ), lambda bi, qi: (bi, 0, qi)),
            pl.BlockSpec((1, 256, TQ), lambda bi, qi: (bi, 0, qi)),
            pl.BlockSpec((1, 16, TQ), lambda bi, qi: (bi, 0, qi)),
            pl.BlockSpec((1, s, LANES), lambda bi, qi: (bi, 0, 0), pipeline_mode=pl.Buffered(1)),
            pl.BlockSpec((1, s, LANES), lambda bi, qi: (bi, 0, 0), pipeline_mode=pl.Buffered(1)),
            pl.BlockSpec((1, s // LANES, LANES, LANES), lambda bi, qi: (bi, 0, 0, 0), pipeline_mode=pl.Buffered(1)),
            pl.BlockSpec((1, m, LANES), lambda bi, qi: (bi, 0, 0)),
            pl.BlockSpec((1, HEAD_DIM, m), lambda bi, qi: (bi, 0, 0)),
            pl.BlockSpec((n_slc, m), lambda bi, qi: (0, 0)),
        ],
        out_specs=pl.BlockSpec((TQ, 256), lambda bi, qi: (bi * (s // TQ) + qi, 0)),
        out_shape=jax.ShapeDtypeStruct((b * s, 256), BF16),
        scratch_shapes=[pltpu.VMEM((n_slc, TQ), F32),
                        pltpu.VMEM((KT, NSA_HEADS * TQ), F32), pltpu.VMEM((KT, NSA_HEADS * TQ), F32),
                        pltpu.VMEM((KT, NSA_HEADS * TQ), BF16), pltpu.VMEM((KT, NSA_HEADS * TQ), BF16),
                        pltpu.VMEM((TQ, TQ), F32), pltpu.VMEM((HEAD_DIM, NSA_HEADS * TQ), F32)],
        compiler_params=pltpu.CompilerParams(dimension_semantics=("arbitrary", "arbitrary"),
                                             vmem_limit_bytes=VMEM_LIMIT),
        name="sparse_attention",
    )(qt, qrt, gt, kk, ksx, vvt, kcmp, vcmpt, ovt)


def _outproj_kernel(x_ref, yabc_ref, yd_ref, wo_ref, g_ref, b_ref, rw_ref, rb_ref, x1_ref, gates_ref):
    mix = _dot(yabc_ref[...], wo_ref[0:768, :]) + _dot(yd_ref[...], wo_ref[768:1024, :])
    x1 = _layer_norm(DN_ALPHA * x_ref[...] + mix, g_ref[...], b_ref[...])
    x1_ref[...] = x1

    ts = x1.shape[0]
    aff = _sigmoid(_dot(x1.astype(BF16), rw_ref[...])).T[0:N_EXPERTS, :]
    biased = aff + rb_ref[...]
    row = lambda a, i: a[i:i + 1, :]

    gscores = []
    for gi in range(N_GROUPS):
        a, b, c, d = (row(biased, gi * EXPERTS_PER_GROUP + i) for i in range(EXPERTS_PER_GROUP))
        hi1, lo1, hi2, lo2 = jnp.maximum(a, b), jnp.minimum(a, b), jnp.maximum(c, d), jnp.minimum(c, d)
        gscores.append(jnp.maximum(hi1, hi2) + jnp.maximum(jnp.minimum(hi1, hi2), jnp.maximum(lo1, lo2)))
    best, gsel = gscores[0], jnp.zeros((1, ts), jnp.int32)
    for gi in range(1, N_GROUPS):
        better = gscores[gi] > best
        gsel = jnp.where(better, gi, gsel)
        best = jnp.where(better, gscores[gi], best)

    eid_i = lax.broadcasted_iota(jnp.int32, (N_EXPERTS, ts), 0)
    eid = eid_i.astype(F32)
    masked = jnp.where(jnp.right_shift(eid_i, 2) == gsel, biased, NEG)
    picks = []
    for _ in range(2):
        top = jnp.max(masked, axis=0, keepdims=True)
        first = jnp.min(jnp.where(masked == top, eid, float(N_EXPERTS)), axis=0, keepdims=True)
        pick = eid == first
        picks.append(pick)
        masked = jnp.where(pick, -jnp.inf, masked)
    chosen = picks[0] | picks[1]
    w_sel = jnp.where(chosen, aff, 0.0)
    gates_t = w_sel / jnp.sum(w_sel, axis=0, keepdims=True)
    grp8 = jnp.where(lax.broadcasted_iota(jnp.int32, (8, ts), 0) == 0, gsel.astype(F32), 0.0)
    gates_ref[...] = jnp.concatenate([gates_t, grp8, jnp.zeros((LANES - N_EXPERTS - 8, ts), F32)], axis=0).T


def _outproj(l, x, yabc, yd, wo, g, b, rw, rb):
    nb, s, d = x.shape
    n = nb * s
    ts = TS_OUT
    per_row = s // ts
    const = lambda shape: pl.BlockSpec(shape, lambda i: (0,) * len(shape))
    return pl.pallas_call(
        _outproj_kernel,
        grid=(n // ts,),
        in_specs=[
            pl.BlockSpec((None, ts, d), lambda i: (i // per_row, i % per_row, 0)),
            pl.BlockSpec((ts, 768), lambda i: (i, 0)),
            pl.BlockSpec((ts, 256), lambda i: (i, 0)),
            _layer_spec(l, (d, d)), _layer_spec(l, (1, d)), _layer_spec(l, (1, d)), const((d, LANES)), const((N_EXPERTS, 1)),
        ],
        out_specs=(pl.BlockSpec((ts, d), lambda i: (i, 0)), pl.BlockSpec((ts, LANES), lambda i: (i, 0))),
        out_shape=(jax.ShapeDtypeStruct((n, d), F32), jax.ShapeDtypeStruct((n, LANES), F32)),
        compiler_params=pltpu.CompilerParams(dimension_semantics=("arbitrary",), vmem_limit_bytes=VMEM_LIMIT),
        name="outproj_ln_route",
    )(x, yabc, yd, wo, g, b, rw, rb)


def _split3(x):
    hi = x.astype(BF16)
    r1 = x - hi.astype(F32)
    mid = r1.astype(BF16)
    lo = (r1 - mid.astype(F32)).astype(BF16)
    return hi, mid, lo


def _dispatch_kernel(x1_ref, gates_ref, tri_ref, xs_ref, gs_ref, post_ref, cnt_ref):
    tm = x1_ref.shape[0]
    gates = gates_ref[...]
    gsel = gates.T[GROUP_LANE:GROUP_LANE + 1, :]
    onehot = jnp.where(lax.broadcasted_iota(jnp.int32, (8, tm), 0).astype(F32) == gsel, 1.0, 0.0)
    n = jnp.sum(onehot, axis=1, keepdims=True)
    n_al = jnp.floor((n + (ROW_ALIGN - 1)) * (1.0 / ROW_ALIGN)) * ROW_ALIGN
    starts = [jnp.zeros((1, 1), F32)]
    for gi in range(1, N_GROUPS):
        starts.append(starts[-1] + n_al[gi - 1:gi, :])
    start = jnp.concatenate(starts + [jnp.zeros((8 - N_GROUPS, 1), F32)], axis=0)
    before = _dot(onehot.astype(BF16), tri_ref[...])
    pos = jnp.sum(onehot * (start + before), axis=0, keepdims=True)
    perm = jnp.where(lax.broadcasted_iota(jnp.int32, (SORT_ROWS, tm), 0).astype(F32) == pos, 1.0, 0.0).astype(BF16)

    xs_ref[0, 0:SORT_ROWS, :] = _dot(perm, x1_ref[...].astype(BF16)).astype(BF16)
    xs_ref[0, SORT_ROWS:SORT_BUF, :] = jnp.zeros((SORT_BUF - SORT_ROWS, xs_ref.shape[2]), BF16)
    g_hi, g_mid, g_lo = _split3(gates)
    gs_ref[0, 0:SORT_ROWS, :] = _dot(perm, g_hi) + _dot(perm, g_mid) + _dot(perm, g_lo)
    gs_ref[0, SORT_ROWS:SORT_BUF, :] = jnp.zeros((SORT_BUF - SORT_ROWS, LANES), F32)
    pos8 = jnp.where(lax.broadcasted_iota(jnp.int32, (8, tm), 0) == 0, pos, 0.0)
    post_ref[...] = jnp.concatenate([pos8, jnp.zeros((LANES - 8, tm), F32)], axis=0).T
    cnt_ref[0] = jnp.broadcast_to(n, (8, LANES))


def _dispatch(x1, gates, tri):
    n, d = x1.shape
    tm = TM_MOE
    nt = n // tm
    return pl.pallas_call(
        _dispatch_kernel,
        grid=(nt,),
        in_specs=[
            pl.BlockSpec((tm, d), lambda i: (i, 0)),
            pl.BlockSpec((tm, LANES), lambda i: (i, 0)),
            pl.BlockSpec((tm, tm), lambda i: (0, 0)),
        ],
        out_specs=(
            pl.BlockSpec((1, SORT_BUF, d), lambda i: (i, 0, 0)),
            pl.BlockSpec((1, SORT_BUF, LANES), lambda i: (i, 0, 0)),
            pl.BlockSpec((tm, LANES), lambda i: (i, 0)),
            pl.BlockSpec((1, 8, LANES), lambda i: (i, 0, 0)),
        ),
        out_shape=(
            jax.ShapeDtypeStruct((nt, SORT_BUF, d), BF16),
            jax.ShapeDtypeStruct((nt, SORT_BUF, LANES), F32),
            jax.ShapeDtypeStruct((n, LANES), F32),
            jax.ShapeDtypeStruct((nt, 8, LANES), F32),
        ),
        compiler_params=pltpu.CompilerParams(dimension_semantics=("arbitrary",), vmem_limit_bytes=VMEM_LIMIT),
        name="moe_dispatch",
    )(x1, gates, tri)


def _moe_kernel(start_ref, trips_ref, xs_ref, gs_ref, wg_ref, wu_ref, wd_ref, acc_ref):
    i, e = pl.program_id(0), pl.program_id(1)

    @pl.when(e == 0)
    def _():
        acc_ref[...] = jnp.zeros(acc_ref.shape, F32)

    grp = lax.shift_right_logical(e, 2)
    slots = [(i * TILES_PER_STEP + tile) * N_GROUPS + grp for tile in range(TILES_PER_STEP)]
    row0 = [start_ref[s] for s in slots]
    need = [trips_ref[s] for s in slots]

    def window(w, carry):
        rows = [pl.ds(pl.multiple_of(jnp.minimum(r + w * WIN_ROWS, SORT_BUF - WIN_ROWS), ROW_ALIGN), WIN_ROWS)
                for r in row0]
        xw = jnp.concatenate([xs_ref[tile, rows[tile], :] for tile in range(TILES_PER_STEP)], axis=0)
        hg = _dot(xw, wg_ref[0])
        h = hg * _sigmoid(hg) * _dot(xw, wu_ref[0])
        y = _dot(h.astype(BF16), wd_ref[0])
        for tile in range(TILES_PER_STEP):
            gw = gs_ref[tile, rows[tile], :]
            lane = lax.broadcasted_iota(jnp.int32, gw.shape, 1)
            gcol = jnp.sum(jnp.where((lane == e) & (w < need[tile]), gw, 0.0), axis=1, keepdims=True)
            acc_ref[tile, rows[tile], :] += y[tile * WIN_ROWS:(tile + 1) * WIN_ROWS, :] * gcol
        return carry

    lax.fori_loop(0, functools.reduce(jnp.maximum, need), window, 0)


def _moe(l, start, trips, xs, gs, wg, wu, wd):
    nt, _, d = xs.shape
    _, ne, _, de = wg.shape
    tps = TILES_PER_STEP
    grid_spec = pltpu.PrefetchScalarGridSpec(
        num_scalar_prefetch=2,
        grid=(nt // tps, ne),
        in_specs=[
            pl.BlockSpec((tps, SORT_BUF, d), lambda i, e, *_: (i, 0, 0)),
            pl.BlockSpec((tps, SORT_BUF, LANES), lambda i, e, *_: (i, 0, 0)),
            pl.BlockSpec((None, 1, d, de), lambda i, e, *_: (l, e, 0, 0)),
            pl.BlockSpec((None, 1, d, de), lambda i, e, *_: (l, e, 0, 0)),
            pl.BlockSpec((None, 1, de, d), lambda i, e, *_: (l, e, 0, 0)),
        ],
        out_specs=pl.BlockSpec((tps, SORT_BUF, d), lambda i, e, *_: (i, 0, 0)),
    )
    return pl.pallas_call(
        _moe_kernel,
        grid_spec=grid_spec,
        out_shape=jax.ShapeDtypeStruct((nt, SORT_BUF, d), F32),
        compiler_params=pltpu.CompilerParams(dimension_semantics=("arbitrary", "arbitrary"),
                                             vmem_limit_bytes=VMEM_LIMIT),
        name="moe_experts",
    )(start, trips, xs, gs, wg, wu, wd)


def _combine_kernel(x1_ref, acc_ref, post_ref, g_ref, b_ref, out_ref):
    tm = x1_ref.shape[0]
    acc = acc_ref[0]
    hi = acc.astype(BF16)
    lo = (acc - hi.astype(F32)).astype(BF16)
    pos = post_ref[...][:, 0:1]
    unperm = jnp.where(lax.broadcasted_iota(jnp.int32, (tm, SORT_ROWS), 1).astype(F32) == pos, 1.0, 0.0).astype(BF16)
    moe = _dot(unperm, hi) + _dot(unperm, lo)
    out_ref[...] = _layer_norm(DN_ALPHA * x1_ref[...] + moe, g_ref[...], b_ref[...])


def _combine(l, x1, acc, post, g, b, batch):
    n, d = x1.shape
    tm = TM_MOE
    per_row = n // batch // tm
    return pl.pallas_call(
        _combine_kernel,
        grid=(n // tm,),
        in_specs=[
            pl.BlockSpec((tm, d), lambda i: (i, 0)),
            pl.BlockSpec((1, SORT_ROWS, d), lambda i: (i, 0, 0)),
            pl.BlockSpec((tm, LANES), lambda i: (i, 0)),
            _layer_spec(l, (1, d)), _layer_spec(l, (1, d)),
        ],
        out_specs=pl.BlockSpec((None, tm, d), lambda i: (i // per_row, i % per_row, 0)),
        out_shape=jax.ShapeDtypeStruct((batch, n // batch, d), F32),
        compiler_params=pltpu.CompilerParams(dimension_semantics=("arbitrary",), vmem_limit_bytes=VMEM_LIMIT),
        name="moe_combine_ln",
    )(x1, acc, post, g, b)


def _group_windows(cnt):
    n = cnt[:, 0:N_GROUPS, 0].astype(jnp.int32)
    n_al = (n + (ROW_ALIGN - 1)) // ROW_ALIGN * ROW_ALIGN
    start = jnp.cumsum(n_al, axis=1) - n_al
    trips = (n + (WIN_ROWS - 1)) // WIN_ROWS
    return start.reshape(-1), trips.reshape(-1)


def _permute_w_in(w):
    sizes = (256, 256, 256, 256, 256, 256, 256, 64, 64, 64, 64, 64, 64, 12)
    offs = np.concatenate([[0], np.cumsum(sizes)])
    part = lambda i: w[..., int(offs[i]):int(offs[i + 1])]
    a_b, a_c, a_h, b_val, b_gate, c_p, q, kc, vc, ks, vs, kw, vw, g = (part(i) for i in range(14))
    pad = jnp.zeros(w.shape[:-1] + (LANES - g.shape[-1],), w.dtype)
    return jnp.concatenate([a_b, a_c, a_h, b_val, b_gate, c_p, q, kc, vc, ks, kw, vs, vw, g, pad], axis=-1).astype(BF16)


def _rope_tables(s):
    inv = 1.0 / (ROPE_THETA ** (jnp.arange(0, HEAD_DIM, 2, dtype=F32) / HEAD_DIM))
    ang = jnp.arange(s, dtype=F32)[:, None] * inv[None, :]
    ang = jnp.concatenate([ang, ang], -1)
    sign = jnp.concatenate([-jnp.ones((HEAD_DIM // 2,), F32), jnp.ones((HEAD_DIM // 2,), F32)])
    return jnp.tile(jnp.cos(ang), (1, NSA_HEADS)), jnp.tile(jnp.sin(ang) * sign[None, :], (1, NSA_HEADS))


def _overlap_t(s, m):
    n_cmp = (s - CMP_BLOCK) // CMP_STRIDE + 1
    cmp_start = np.arange(m) * CMP_STRIDE
    slc_start = np.arange(s // SLC_BLOCK) * SLC_BLOCK
    ov = (cmp_start[None, :] < slc_start[:, None] + SLC_BLOCK) & (cmp_start[None, :] + CMP_BLOCK > slc_start[:, None])
    ov = ov & (np.arange(m)[None, :] < n_cmp)
    return jnp.asarray(ov, BF16)


def _block_diag(w):
    nl, g, c, _ = w.shape
    return jnp.einsum('lgcd,gh->lgchd', w, jnp.eye(g, dtype=w.dtype)).reshape(nl, g * c, g * c)


def kernel(x, w_in, conv_a_w, conv_b_w, conv_b_b, cf_ln_g, cf_ln_b, pool_w, pool_scale, cmp_pe_k, cmp_pe_v, cmp_k_w1, cmp_k_w2, cmp_v_w1, cmp_v_w2, w_out, ln1_g, ln1_b, ln2_g, ln2_b, router_w, router_bias, moe_w_gate, moe_w_up, moe_w_down):
    b, s, d = x.shape
    m = s // CMP_STRIDE
    cos4, sin4 = _rope_tables(s)
    ovt = _overlap_t(s, m)
    rw = jnp.concatenate([router_w, jnp.zeros((d, LANES - N_EXPERTS), router_w.dtype)], axis=1).astype(BF16)
    rb = router_bias.reshape(N_EXPERTS, 1).astype(F32)
    tri = jnp.asarray(np.triu(np.ones((TM_MOE, TM_MOE), np.float32), k=1), BF16)

    nl = w_in.shape[0]
    row = lambda v: v.reshape(nl, 1, -1)
    pad_w2 = lambda w: jnp.concatenate([w, jnp.zeros((nl, CMP_HIDDEN, LANES - HEAD_DIM), w.dtype)], axis=2).astype(BF16)
    w_in_p = _permute_w_in(w_in)
    pool_bd = _block_diag(pool_w).astype(BF16)
    pe_kv = jnp.concatenate([cmp_pe_k, cmp_pe_v], axis=-1)
    zeros_w1 = jnp.zeros((nl, CMP_BLOCK, HEAD_DIM, CMP_HIDDEN), cmp_k_w1.dtype)
    w1_kv = jnp.concatenate([
        jnp.concatenate([cmp_k_w1.reshape(nl, CMP_BLOCK, HEAD_DIM, CMP_HIDDEN), zeros_w1], axis=-1),
        jnp.concatenate([zeros_w1, cmp_v_w1.reshape(nl, CMP_BLOCK, HEAD_DIM, CMP_HIDDEN)], axis=-1)], axis=2).astype(BF16)
    wk2, wv2 = pad_w2(cmp_k_w2), pad_w2(cmp_v_w2)
    wo = w_out.astype(BF16)
    wg, wu, wd = moe_w_gate.astype(BF16), moe_w_up.astype(BF16), moe_w_down.astype(BF16)
    cvbb, lng, lnb, pools = row(conv_b_b), row(cf_ln_g), row(cf_ln_b), row(pool_scale)
    g1, b1, g2, b2 = row(ln1_g), row(ln1_b), row(ln2_g), row(ln2_b)

    for l in range(DEPTH):
        yabc, qt, qrt, kcvc, kk, ksx, vvt, gt = _inproj(
            l, x, w_in_p, cos4, sin4, conv_a_w, conv_b_w, cvbb, lng, lnb, pool_bd, pools)
        kcmp, vcmpt = _compress(l, kcvc, pe_kv, w1_kv, wk2, wv2)
        yd = _nsa(qt, qrt, gt, kk, ksx, vvt, kcmp, vcmpt, ovt)
        x1, gates = _outproj(l, x, yabc, yd, wo, g1, b1, rw, rb)
        xs, gs, post, cnt = _dispatch(x1, gates, tri)
        start, trips = _group_windows(cnt)
        acc = _moe(l, start, trips, xs, gs, wg, wu, wd)
        x = _combine(l, x1, acc, post, g2, b2, b)
    return x
```

```python
import functools

import jax
import jax.numpy as jnp
import numpy as np
from jax import lax
from jax.experimental import pallas as pl
from jax.experimental.pallas import tpu as pltpu

F32 = jnp.float32
BF16 = jnp.bfloat16

D_MODEL = 1024
DEPTH = 2
GROUP_W = 256
HEAD_DIM = 64
SC_KERNEL = 3
CF_KERNEL = 31
POOL_WINDOWS = (2, 4, 8, 16)
POOL_GW = 64
NSA_HEADS = 4
CMP_BLOCK = 32
CMP_STRIDE = 16
CMP_HIDDEN = 256
SLC_BLOCK = 64
SLC_TOPN = 16
WIN = 512
N_EXPERTS = 16
N_GROUPS = 4
EXPERTS_PER_GROUP = 4
D_EXPERT = 512
DN_ALPHA = (2 * DEPTH) ** 0.25
LN_EPS = 1e-5
NEG = -1e30
FORCE = 1e4
ROPE_THETA = 10000.0
QK_SCALE = HEAD_DIM ** -0.5
LOG2E = 1.4426950408889634

LANES = 128
OFF_A, OFF_B, OFF_C, OFF_Q, OFF_KCVC, OFF_KK, OFF_VV, OFF_G, IN_PAD = 0, 768, 1280, 1536, 1792, 1920, 2048, 2176, 2304

HALO_A = 8
HALO_B = 32
HALO_C = 16

TS_IN = 512
TQ = 512
KT = 512
TS_OUT = 512
TM_MOE = 1024
ROW_ALIGN = 16
WIN_ROWS = 288
SORT_ROWS = TM_MOE + LANES
SORT_BUF = SORT_ROWS + 2 * LANES
GROUP_LANE = 16
TILES_PER_STEP = 2
assert SORT_ROWS >= TM_MOE + N_GROUPS * (ROW_ALIGN - 1)
assert SORT_BUF >= TM_MOE + N_GROUPS * (ROW_ALIGN - 1) + WIN_ROWS
assert WIN_ROWS % ROW_ALIGN == 0 and (SORT_BUF - WIN_ROWS) % ROW_ALIGN == 0
VMEM_LIMIT = 56 * 1024 * 1024


def _sigmoid(x):
    return jax.nn.sigmoid(x)


def _layer_norm(h, g, b):
    mu = jnp.mean(h, axis=-1, keepdims=True)
    d = h - mu
    var = jnp.mean(d * d, axis=-1, keepdims=True)
    return d * lax.rsqrt(var + LN_EPS) * g + b


def _dot(a, b):
    return jnp.dot(a, b, preferred_element_type=F32)


def _layer_spec(l, shape):
    return pl.BlockSpec((None,) + tuple(shape), lambda *_: (l,) + (0,) * len(shape))


def _inproj_kernel(x_ref, w_ref, cos_ref, sin_ref, cva_ref, cvb_ref, cvbb_ref, lng_ref, lnb_ref,
                   poolw_ref, pools_ref,
                   yabc_ref, qt_ref, qrt_ref, kcvc_ref, kk_ref, ksx_ref, vvt_ref, gt_ref,
                   ch_ext, u_ext, p_ext, u_sh):
    j = pl.program_id(1)
    ts = x_ref.shape[1]
    xb = x_ref[0].astype(BF16)

    def proj(lo, hi):
        return _dot(xb, w_ref[:, lo:hi])

    def carry_halo(ext, halo):
        @pl.when(j == 0)
        def _():
            ext[0:halo, :] = jnp.zeros((halo, ext.shape[1]), F32)

        @pl.when(j > 0)
        def _():
            ext[0:halo, :] = ext[ts:ts + halo, :]

    pa = proj(OFF_A, OFF_B)
    a_b, ch = pa[:, 0:256], pa[:, 256:512] * pa[:, 512:768]
    carry_halo(ch_ext, HALO_A)
    ch_ext[HALO_A:HALO_A + ts, :] = ch
    conv = cva_ref[2:3, :] * ch
    for k in range(SC_KERNEL - 1):
        off = HALO_A - (SC_KERNEL - 1) + k
        conv = conv + cva_ref[k:k + 1, :] * ch_ext[off:off + ts, :]
    yabc_ref[:, 0:256] = (a_b * conv).astype(BF16)

    pb = proj(OFF_B, OFF_C)
    u = pb[:, 0:256] * _sigmoid(pb[:, 256:512])
    carry_halo(u_ext, HALO_B)
    u_ext[HALO_B:HALO_B + ts, :] = u
    span = u_sh.shape[1]
    for r in range(1, 8):
        u_sh[r - 1, :, :] = u_ext[r:r + span, :]
    def mixer_c():
        pc = proj(OFF_C, OFF_Q)
        carry_halo(p_ext, HALO_C)
        p_ext[HALO_C:HALO_C + ts, :] = pc
        pe = p_ext[...]
        s2 = pe + pltpu.roll(pe, 1, 0)
        s4 = s2 + pltpu.roll(s2, 2, 0)
        s8 = s4 + pltpu.roll(s4, 4, 0)
        s16 = s8 + pltpu.roll(s8, 8, 0)
        lane = lax.broadcasted_iota(jnp.int32, (ts, 256), 1)
        grp = jnp.right_shift(lane, 6)
        wsum = jnp.where(grp == 0, s2[HALO_C:], jnp.where(grp == 1, s4[HALO_C:], jnp.where(grp == 2, s8[HALO_C:], s16[HALO_C:])))
        win = jnp.where(grp == 0, 2, jnp.where(grp == 1, 4, jnp.where(grp == 2, 8, 16)))
        t1 = j * ts + lax.broadcasted_iota(jnp.int32, (ts, 256), 0) + 1
        cnt = jnp.minimum(win, t1).astype(F32)
        dlt = wsum / cnt - pc
        yabc_ref[:, 512:768] = (_dot(dlt.astype(BF16), poolw_ref[...]) * pools_ref[...]).astype(BF16)

    def queries():
        cos4, sin4 = cos_ref[...], sin_ref[...]
        q = proj(OFF_Q, OFF_KCVC)
        lane_q = jnp.bitwise_and(lax.broadcasted_iota(jnp.int32, (ts, 256), 1), HEAD_DIM - 1)
        q_sw = jnp.where(lane_q < HEAD_DIM // 2, pltpu.roll(q, 256 - HEAD_DIM // 2, 1), pltpu.roll(q, HEAD_DIM // 2, 1))
        qr = q * cos4 + q_sw * sin4
        qt_ref[0] = (q * (QK_SCALE * LOG2E)).T.astype(BF16)
        qrt_ref[0] = (qr * (QK_SCALE * LOG2E)).T.astype(BF16)

    def compress_inputs():
        kcvc_ref[0] = proj(OFF_KCVC, OFF_KK)

    def keys():
        k2 = proj(OFF_KK, OFF_VV)
        lane_k = jnp.bitwise_and(lax.broadcasted_iota(jnp.int32, (ts, LANES), 1), HEAD_DIM - 1)
        k_sw = jnp.where(lane_k < HEAD_DIM // 2, pltpu.roll(k2, LANES - HEAD_DIM // 2, 1), pltpu.roll(k2, HEAD_DIM // 2, 1))
        k_rot = k2 * cos_ref[:, 0:LANES] + k_sw * sin_ref[:, 0:LANES]
        kk_ref[0] = k_rot.astype(BF16)
        lane_i = lax.broadcasted_iota(jnp.int32, (ts, LANES), 1)
        blk_in_tile = jnp.bitwise_and(jnp.right_shift(j * ts + lax.broadcasted_iota(jnp.int32, (ts, LANES), 0), 6),
                                      KT // SLC_BLOCK - 1)
        ksx_ref[0] = jnp.where(lane_i < HEAD_DIM, k_rot, jnp.where(lane_i - HEAD_DIM == blk_in_tile, 1.0, 0.0)).astype(BF16)

    def values():
        v2t = proj(OFF_VV, OFF_G).T
        for c in range(ts // LANES):
            vvt_ref[0, c] = v2t[:, c * LANES:(c + 1) * LANES].astype(BF16)

    def branch_gates():
        gt_ref[0] = _sigmoid(proj(OFF_G, IN_PAD)).T[0:16, :]

    interleaved = [mixer_c, queries, compress_inputs, keys, values, branch_gates]
    rows = 64
    assert ts // rows >= len(interleaved)
    for c in range(ts // rows):
        acc = jnp.zeros((rows, 256), F32) + cvbb_ref[...]
        for k in range(CF_KERNEL):
            off = HALO_B - (CF_KERNEL - 1) + k
            base = off // 8 * 8 + c * rows
            src = u_ext if off % 8 == 0 else u_sh.at[off % 8 - 1]
            acc = acc + cvb_ref[k:k + 1, :] * src[base:base + rows, :]
        v = _layer_norm(acc, lng_ref[...], lnb_ref[...])
        yabc_ref[c * rows:(c + 1) * rows, 256:512] = (v * _sigmoid(v)).astype(BF16)
        if c < len(interleaved):
            interleaved[c]()


def _inproj(l, x, w_in_p, cos4, sin4, cva, cvb, cvbb, lng, lnb, poolw, pools):
    b, s, d = x.shape
    ts = TS_IN
    grid = (b, s // ts)
    out_shape = (
        jax.ShapeDtypeStruct((b * s, 768), BF16),
        jax.ShapeDtypeStruct((b, 256, s), BF16),
        jax.ShapeDtypeStruct((b, 256, s), BF16),
        jax.ShapeDtypeStruct((b, s, LANES), F32),
        jax.ShapeDtypeStruct((b, s, LANES), BF16),
        jax.ShapeDtypeStruct((b, s, LANES), BF16),
        jax.ShapeDtypeStruct((b, s // LANES, LANES, LANES), BF16),
        jax.ShapeDtypeStruct((b, 16, s), F32),
    )
    return pl.pallas_call(
        _inproj_kernel,
        grid=grid,
        in_specs=[
            pl.BlockSpec((1, ts, d), lambda bi, ji: (bi, ji, 0)),
            _layer_spec(l, (d, IN_PAD)),
            pl.BlockSpec((ts, 256), lambda bi, ji: (ji, 0)),
            pl.BlockSpec((ts, 256), lambda bi, ji: (ji, 0)),
            _layer_spec(l, (SC_KERNEL, 256)), _layer_spec(l, (CF_KERNEL, 256)), _layer_spec(l, (1, 256)),
            _layer_spec(l, (1, 256)), _layer_spec(l, (1, 256)), _layer_spec(l, (256, 256)), _layer_spec(l, (1, 256)),
        ],
        out_specs=(
            pl.BlockSpec((ts, 768), lambda bi, ji: (bi * (s // ts) + ji, 0)),
            pl.BlockSpec((1, 256, ts), lambda bi, ji: (bi, 0, ji)),
            pl.BlockSpec((1, 256, ts), lambda bi, ji: (bi, 0, ji)),
            pl.BlockSpec((1, ts, LANES), lambda bi, ji: (bi, ji, 0)),
            pl.BlockSpec((1, ts, LANES), lambda bi, ji: (bi, ji, 0)),
            pl.BlockSpec((1, ts, LANES), lambda bi, ji: (bi, ji, 0)),
            pl.BlockSpec((1, ts // LANES, LANES, LANES), lambda bi, ji: (bi, ji, 0, 0)),
            pl.BlockSpec((1, 16, ts), lambda bi, ji: (bi, 0, ji)),
        ),
        out_shape=out_shape,
        scratch_shapes=[
            pltpu.VMEM((ts + HALO_A, 256), F32),
            pltpu.VMEM((ts + HALO_B, 256), F32),
            pltpu.VMEM((ts + HALO_C, 256), F32),
            pltpu.VMEM((7, ts + HALO_B - 8, 256), F32),
        ],
        compiler_params=pltpu.CompilerParams(dimension_semantics=("arbitrary", "arbitrary"),
                                             vmem_limit_bytes=VMEM_LIMIT),
        name="inproj_mixers",
    )(x, w_in_p, cos4, sin4, cva, cvb, cvbb, lng, lnb, poolw, pools)


def _compress_kernel(kcvc_ref, pe_ref, w1_ref, wk2_ref, wv2_ref, kcmp_ref, vcmpt_ref):
    m = kcmp_ref.shape[1]

    def half(first_pos):
        acc = None
        for l in range(first_pos, first_pos + CMP_STRIDE):
            x_l = kcvc_ref[0, pl.ds(l - first_pos, m, stride=CMP_STRIDE), :] + pe_ref[l:l + 1, :]
            part = _dot(x_l.astype(BF16), w1_ref[l])
            acc = part if acc is None else acc + part
        return acc

    hid = half(0) + pltpu.roll(half(CMP_STRIDE), m - 1, 0)
    act = (hid * _sigmoid(hid)).astype(BF16)
    kcmp_ref[0] = _dot(act[:, 0:CMP_HIDDEN], wk2_ref[...]).astype(BF16)
    vcmpt_ref[0] = _dot(act[:, CMP_HIDDEN:2 * CMP_HIDDEN], wv2_ref[...]).T[0:HEAD_DIM, :].astype(BF16)


def _compress(l, kcvc, pe, w1, wk2, wv2):
    b, s, _ = kcvc.shape
    m = s // CMP_STRIDE
    return pl.pallas_call(
        _compress_kernel,
        grid=(b,),
        in_specs=[
            pl.BlockSpec((1, s, LANES), lambda bi: (bi, 0, 0)),
            _layer_spec(l, (CMP_BLOCK, LANES)),
            _layer_spec(l, (CMP_BLOCK, LANES, 2 * CMP_HIDDEN)),
            _layer_spec(l, (CMP_HIDDEN, LANES)), _layer_spec(l, (CMP_HIDDEN, LANES)),
        ],
        out_specs=(
            pl.BlockSpec((1, m, LANES), lambda bi: (bi, 0, 0)),
            pl.BlockSpec((1, HEAD_DIM, m), lambda bi: (bi, 0, 0)),
        ),
        out_shape=(jax.ShapeDtypeStruct((b, m, LANES), BF16), jax.ShapeDtypeStruct((b, HEAD_DIM, m), BF16)),
        compiler_params=pltpu.CompilerParams(dimension_semantics=("arbitrary",), vmem_limit_bytes=VMEM_LIMIT),
        name="compress_kv",
    )(kcvc, pe, w1, wk2, wv2)


def _nsa_kernel(qt_ref, qrt_ref, gt_ref, kk_ref, ksx_ref, vvt_ref, kcmp_ref, vcmpt_ref, ovt_ref, yd_ref, sel_scr, sa_scr, sb_scr, pa_scr, pb_scr, diag_scr, acc_scr):
    qb = pl.program_id(1)
    tq = qt_ref.shape[2]
    nh = NSA_HEADS
    wq = nh * tq
    s0 = qb * tq
    n_slc = sel_scr.shape[0]
    n_cmp_pad = kcmp_ref.shape[1]

    def stack_heads(ref):
        x = ref[0]
        return jnp.concatenate([x[h * HEAD_DIM:(h + 1) * HEAD_DIM, :] for h in range(nh)], axis=1)

    zeros = jnp.zeros((HEAD_DIM, wq), BF16)
    q_top, qr_top = stack_heads(qt_ref), stack_heads(qrt_ref)
    q_lo = jnp.concatenate([q_top, zeros], axis=0)
    qr_lo = jnp.concatenate([qr_top, zeros], axis=0)
    qr_hi = jnp.concatenate([zeros, qr_top], axis=0)
    t_row = s0 + lax.broadcasted_iota(jnp.int32, (1, tq), 1)
    lanes4 = lambda x: jnp.concatenate([x] * nh, axis=1)
    t_all = lanes4(t_row)

    size_classes = 4
    class_rows = n_cmp_pad // size_classes
    assert CMP_STRIDE * class_rows >= tq >= 2 * SLC_BLOCK

    def compress_and_select(nc):
        nb = nc // (SLC_BLOCK // CMP_STRIDE)
        cmp_end = lax.broadcasted_iota(jnp.int32, (nc, tq), 0) * CMP_STRIDE + (CMP_BLOCK - 1)
        sc = _dot(kcmp_ref[0, 0:nc, :], q_lo) + lanes4(jnp.where(cmp_end <= t_row, 0.0, NEG))
        mx = jnp.max(sc, axis=0, keepdims=True)
        ex = jnp.exp2(sc - mx)
        den = jnp.sum(ex, axis=0, keepdims=True)
        p_cmp = ex * jnp.where(t_all >= CMP_BLOCK - 1, 1.0 / den, 0.0)
        o_cmp = _dot(vcmpt_ref[0, :, 0:nc], p_cmp.astype(BF16))
        p_heads = p_cmp[:, 0:tq]
        for h in range(1, nh):
            p_heads = p_heads + p_cmp[:, h * tq:(h + 1) * tq]
        imp = _dot(ovt_ref[0:nb, 0:nc], p_heads.astype(BF16))

        blk_i = lax.broadcasted_iota(jnp.int32, (nb, tq), 0)
        blk = blk_i.astype(F32)
        cur = jnp.right_shift(t_row, 6)
        forced = (blk_i == 0) | (blk_i == cur) | (blk_i == cur - 1)
        valid = blk_i * SLC_BLOCK <= t_row
        sel = jnp.where(forced, 1.0, 0.0)
        n_forced = jnp.sum(sel, axis=0, keepdims=True)
        imp = jnp.where(valid, jnp.where(forced, -jnp.inf, imp), NEG)
        min_forced = 3 if nc > class_rows else 1
        for r in range(min(SLC_TOPN - min_forced, nb)):
            top = jnp.max(imp, axis=0, keepdims=True)
            first = jnp.min(jnp.where(imp == top, blk, float(nb)), axis=0, keepdims=True)
            pick = (blk == first) & (n_forced + r < SLC_TOPN)
            sel = jnp.where(pick, 1.0, sel)
            imp = jnp.where(pick, -jnp.inf, imp)
        bias = jnp.where(valid & (sel > 0.0), 0.0, NEG)
        own0 = qb * (tq // SLC_BLOCK)
        own = (blk_i >= own0) & (blk_i < own0 + tq // SLC_BLOCK)
        sel_scr[0:nb, :] = jnp.where(own, NEG, bias)
        if nb < n_slc:
            sel_scr[nb:n_slc, :] = jnp.full((n_slc - nb, tq), NEG, F32)
        key_blk = own0 + jnp.right_shift(lax.broadcasted_iota(jnp.int32, (tq, nb), 0), 6)
        expand = jnp.where(lax.broadcasted_iota(jnp.int32, (tq, nb), 1) == key_blk, 1.0, 0.0).astype(BF16)
        diag_scr[...] = _dot(expand, bias.astype(BF16))
        return o_cmp

    visible = (s0 + tq) // CMP_STRIDE
    o_cmp = lax.switch((visible - 1) // class_rows,
                       [functools.partial(compress_and_select, (i + 1) * class_rows) for i in range(size_classes)])

    blocks_per_tile = KT // SLC_BLOCK
    sub = KT // LANES

    last_tile = kk_ref.shape[1] // KT - 1

    def score_chunks(kt):
        kd = jnp.minimum(kt, last_tile)
        bias8 = sel_scr[pl.ds(pl.multiple_of(kd * blocks_per_tile, blocks_per_tile), blocks_per_tile), :]
        bias8 = jnp.where(kt <= last_tile, bias8, NEG)
        pad = [jnp.zeros((16 - blocks_per_tile, wq), F32)] if blocks_per_tile < 16 else []
        bias_rows = jnp.concatenate([lanes4(bias8)] + pad, axis=0).astype(BF16)
        q_bias = jnp.concatenate([qr_top, bias_rows, jnp.zeros((HEAD_DIM - 16, wq), BF16)], axis=0)
        s_tile = _dot(ksx_ref[0, pl.ds(pl.multiple_of(kd * KT, KT), KT), :], q_bias)
        for c in range(sub):
            yield c, s_tile[c * LANES:(c + 1) * LANES, :]

    def fold8(x, op):
        out = x[0:8, :]
        for r in range(1, x.shape[0] // 8):
            out = op(out, x[8 * r:8 * r + 8, :])
        return out

    def v_rows(first, count, lo):
        vt = vvt_ref[0, pl.ds(first, count)]
        return jnp.concatenate([vt[c, lo:lo + HEAD_DIM, :] for c in range(count)], axis=1)

    def pv_tile(kt, p_ref):
        return _dot(v_rows(pl.multiple_of(jnp.minimum(kt, last_tile) * sub, sub), sub, 0), p_ref[...])

    def phase(kt, s_cur, s_next, p_cur, p_prev, state):
        m_i, l_i, mt_cur, alpha_prev = state
        m_new = jnp.maximum(m_i, jnp.max(mt_cur, axis=0, keepdims=True))
        alpha = jnp.exp2(m_i - m_new)
        mt_next = jnp.full((8, wq), NEG, F32)
        l_new = alpha * l_i
        for c, s_n in score_chunks(kt + 1):
            rows = slice(c * LANES, (c + 1) * LANES)
            mt_next = jnp.maximum(mt_next, fold8(s_n, jnp.maximum))
            s_next[rows, :] = s_n
            p = jnp.exp2(s_cur[rows, :] - m_new)
            l_new = l_new + fold8(p, jnp.add)
            p_cur[rows, :] = p.astype(BF16)
        if p_prev is not None:
            acc_scr[...] = alpha_prev * acc_scr[...] + pv_tile(kt - 1, p_prev)
        return m_new, l_new, mt_next, alpha

    def slc_pair(j, state):
        state = phase(2 * j + 1, sb_scr, sa_scr, pb_scr, pa_scr, state)
        return phase(2 * j + 2, sa_scr, sb_scr, pa_scr, pb_scr, state)

    mt0 = jnp.full((8, wq), NEG, F32)
    for c, s_n in score_chunks(0):
        mt0 = jnp.maximum(mt0, fold8(s_n, jnp.maximum))
        sa_scr[c * LANES:(c + 1) * LANES, :] = s_n
    key_d = s0 + lax.broadcasted_iota(jnp.int32, (tq, tq), 0)
    s_d = (_dot(kk_ref[0, pl.ds(pl.multiple_of(s0, tq), tq), :], qr_lo)
           + lanes4(jnp.where(key_d <= t_row, diag_scr[...], NEG)))
    m_d = jnp.max(s_d, axis=0, keepdims=True)
    p_d = jnp.exp2(s_d - m_d)
    acc_scr[...] = _dot(v_rows(qb * (tq // LANES), tq // LANES, 0), p_d.astype(BF16))
    state = (m_d, fold8(p_d, jnp.add), mt0, jnp.ones((1, wq), F32))
    state = phase(0, sa_scr, sb_scr, pa_scr, None, state)
    n_tiles = (s0 + tq + KT - 1) // KT
    pairs = n_tiles // 2
    _, l8_s, _, alpha_last = lax.fori_loop(0, pairs, slc_pair, state)
    acc_s = alpha_last * acc_scr[...] + pv_tile(2 * pairs, pa_scr)
    l_s = jnp.sum(l8_s, axis=0, keepdims=True)

    wkeys = WIN + tq
    k0 = pl.multiple_of(jnp.maximum(s0 - WIN, 0), LANES)
    key = k0 + lax.broadcasted_iota(jnp.int32, (wkeys, tq), 0)
    wbias = jnp.where((key <= t_row) & (key > t_row - WIN), 0.0, NEG)
    s_w = _dot(kk_ref[0, pl.ds(k0, wkeys), :], qr_hi) + lanes4(wbias)
    vt_w = vvt_ref[0, pl.ds(jnp.maximum(qb * (tq // LANES) - WIN // LANES, 0), wkeys // LANES)]
    p_w = jnp.exp2(s_w - jnp.max(s_w, axis=0, keepdims=True))
    l_w = jnp.sum(p_w, axis=0, keepdims=True)
    v_w = jnp.concatenate([vt_w[i, HEAD_DIM:2 * HEAD_DIM, :] for i in range(wkeys // LANES)], axis=1)
    acc_w = _dot(v_w, p_w.astype(BF16))

    g = gt_ref[0]
    gate = lambda br: jnp.concatenate([g[h * 3 + br:h * 3 + br + 1, :] for h in range(nh)], axis=1)
    o = gate(0) * o_cmp + gate(1) * (acc_s * (1.0 / l_s)) + gate(2) * (acc_w * (1.0 / l_w))
    o_rows = jnp.concatenate([o[:, h * tq:(h + 1) * tq] for h in range(nh)], axis=0)
    yd_ref[...] = o_rows.T.astype(BF16)


def _nsa(qt, qrt, gt, kk, ksx, vvt, kcmp, vcmpt, ovt):
    b, _, s = qt.shape
    n_slc = s // SLC_BLOCK
    m = kcmp.shape[1]
    return pl.pallas_call(
        _nsa_kernel,
        grid=(b, s // TQ),
        in_specs=[
            pl.BlockSpec((1, 256, TQ), lambda bi, qi: (bi, 0, qi)),
            pl.BlockSpec((1, 256, TQ), lambda bi, qi: (bi, 0, qi)),
            pl.BlockSpec((1, 16, TQ), lambda bi, qi: (bi, 0, qi)),
            pl.BlockSpec((1, s, LANES), lambda bi, qi: (bi, 0, 0), pipeline_mode=pl.Buffered(1)),
            pl.BlockSpec((1, s, LANES), lambda bi, qi: (bi, 0, 0), pipeline_mode=pl.Buffered(1)),
            pl.BlockSpec((1, s // LANES, LANES, LANES), lambda bi, qi: (bi, 0, 0, 0), pipeline_mode=pl.Buffered(1)),
            pl.BlockSpec((1, m, LANES), lambda bi, qi: (bi, 0, 0)),
            pl.BlockSpec((1, HEAD_DIM, m), lambda bi, qi: (bi, 0, 0)),
            pl.BlockSpec((n_slc, m), lambda bi, qi: (0, 0)),
        ],
        out_specs=pl.BlockSpec((TQ, 256), lambda bi, qi: (bi * (s // TQ) + qi, 0)),
        out_shape=jax.ShapeDtypeStruct((b * s, 256), BF16),
        scratch_shapes=[pltpu.VMEM((n_slc, TQ), F32),
                        pltpu.VMEM((KT, NSA_HEADS * TQ), F32), pltpu.VMEM((KT, NSA_HEADS * TQ), F32),
                        pltpu.VMEM((KT, NSA_HEADS * TQ), BF16), pltpu.VMEM((KT, NSA_HEADS * TQ), BF16),
                        pltpu.VMEM((TQ, TQ), F32), pltpu.VMEM((HEAD_DIM, NSA_HEADS * TQ), F32)],
        compiler_params=pltpu.CompilerParams(dimension_semantics=("arbitrary", "arbitrary"),
                                             vmem_limit_bytes=VMEM_LIMIT),
        name="sparse_attention",
    )(qt, qrt, gt, kk, ksx, vvt, kcmp, vcmpt, ovt)


def _outproj_kernel(x_ref, yabc_ref, yd_ref, wo_ref, g_ref, b_ref, rw_ref, rb_ref, tri_ref,
                    x1_ref, xs_ref, gs_ref, post_ref, cnt_ref):
    mix = _dot(yabc_ref[...], wo_ref[0:768, :]) + _dot(yd_ref[...], wo_ref[768:1024, :])
    x1 = _layer_norm(DN_ALPHA * x_ref[...] + mix, g_ref[...], b_ref[...])
    x1_ref[...] = x1

    ts = x1.shape[0]
    aff = _sigmoid(_dot(x1.astype(BF16), rw_ref[...])).T[0:N_EXPERTS, :]
    biased = aff + rb_ref[...]
    row = lambda a, i: a[i:i + 1, :]

    gscores = []
    for gi in range(N_GROUPS):
        a, b, c, d = (row(biased, gi * EXPERTS_PER_GROUP + i) for i in range(EXPERTS_PER_GROUP))
        hi1, lo1, hi2, lo2 = jnp.maximum(a, b), jnp.minimum(a, b), jnp.maximum(c, d), jnp.minimum(c, d)
        gscores.append(jnp.maximum(hi1, hi2) + jnp.maximum(jnp.minimum(hi1, hi2), jnp.maximum(lo1, lo2)))
    best, gsel = gscores[0], jnp.zeros((1, ts), jnp.int32)
    for gi in range(1, N_GROUPS):
        better = gscores[gi] > best
        gsel = jnp.where(better, gi, gsel)
        best = jnp.where(better, gscores[gi], best)

    eid_i = lax.broadcasted_iota(jnp.int32, (N_EXPERTS, ts), 0)
    eid = eid_i.astype(F32)
    masked = jnp.where(jnp.right_shift(eid_i, 2) == gsel, biased, NEG)
    picks = []
    for _ in range(2):
        top = jnp.max(masked, axis=0, keepdims=True)
        first = jnp.min(jnp.where(masked == top, eid, float(N_EXPERTS)), axis=0, keepdims=True)
        pick = eid == first
        picks.append(pick)
        masked = jnp.where(pick, -jnp.inf, masked)
    chosen = picks[0] | picks[1]
    w_sel = jnp.where(chosen, aff, 0.0)
    gates_t = w_sel / jnp.sum(w_sel, axis=0, keepdims=True)
    grp8 = jnp.where(lax.broadcasted_iota(jnp.int32, (8, ts), 0) == 0, gsel.astype(F32), 0.0)
    gates = jnp.concatenate([gates_t, grp8, jnp.zeros((LANES - N_EXPERTS - 8, ts), F32)], axis=0).T
    _dispatch_rows(x1, gates, gsel.astype(F32), tri_ref, xs_ref, gs_ref, post_ref, cnt_ref)


def _outproj(l, x, yabc, yd, wo, g, b, rw, rb, tri):
    nb, s, d = x.shape
    n = nb * s
    ts = TM_MOE
    per_row = s // ts
    nt = n // ts
    const = lambda shape: pl.BlockSpec(shape, lambda i: (0,) * len(shape))
    return pl.pallas_call(
        _outproj_kernel,
        grid=(n // ts,),
        in_specs=[
            pl.BlockSpec((None, ts, d), lambda i: (i // per_row, i % per_row, 0)),
            pl.BlockSpec((ts, 768), lambda i: (i, 0)),
            pl.BlockSpec((ts, 256), lambda i: (i, 0)),
            _layer_spec(l, (d, d)), _layer_spec(l, (1, d)), _layer_spec(l, (1, d)), const((d, LANES)), const((N_EXPERTS, 1)),
            pl.BlockSpec((ts, ts), lambda i: (0, 0), pipeline_mode=pl.Buffered(1)),
        ],
        out_specs=(
            pl.BlockSpec((ts, d), lambda i: (i, 0)),
            pl.BlockSpec((1, SORT_BUF, d), lambda i: (i, 0, 0)),
            pl.BlockSpec((1, SORT_BUF, LANES), lambda i: (i, 0, 0)),
            pl.BlockSpec((ts, LANES), lambda i: (i, 0)),
            pl.BlockSpec((1, 8, LANES), lambda i: (i, 0, 0)),
        ),
        out_shape=(
            jax.ShapeDtypeStruct((n, d), F32),
            jax.ShapeDtypeStruct((nt, SORT_BUF, d), BF16),
            jax.ShapeDtypeStruct((nt, SORT_BUF, LANES), F32),
            jax.ShapeDtypeStruct((n, LANES), F32),
            jax.ShapeDtypeStruct((nt, 8, LANES), F32),
        ),
        compiler_params=pltpu.CompilerParams(dimension_semantics=("arbitrary",), vmem_limit_bytes=VMEM_LIMIT),
        name="outproj_route_dispatch",
    )(x, yabc, yd, wo, g, b, rw, rb, tri)


def _split3(x):
    hi = x.astype(BF16)
    r1 = x - hi.astype(F32)
    mid = r1.astype(BF16)
    lo = (r1 - mid.astype(F32)).astype(BF16)
    return hi, mid, lo


def _dispatch_rows(x1, gates, gsel, tri_ref, xs_ref, gs_ref, post_ref, cnt_ref):
    tm = x1.shape[0]
    onehot = jnp.where(lax.broadcasted_iota(jnp.int32, (8, tm), 0).astype(F32) == gsel, 1.0, 0.0)
    n = jnp.sum(onehot, axis=1, keepdims=True)
    n_al = jnp.floor((n + (ROW_ALIGN - 1)) * (1.0 / ROW_ALIGN)) * ROW_ALIGN
    starts = [jnp.zeros((1, 1), F32)]
    for gi in range(1, N_GROUPS):
        starts.append(starts[-1] + n_al[gi - 1:gi, :])
    start = jnp.concatenate(starts + [jnp.zeros((8 - N_GROUPS, 1), F32)], axis=0)
    before = _dot(onehot.astype(BF16), tri_ref[...])
    pos = jnp.sum(onehot * (start + before), axis=0, keepdims=True)
    perm = jnp.where(lax.broadcasted_iota(jnp.int32, (SORT_ROWS, tm), 0).astype(F32) == pos, 1.0, 0.0).astype(BF16)

    xs_ref[0, 0:SORT_ROWS, :] = _dot(perm, x1.astype(BF16)).astype(BF16)
    xs_ref[0, SORT_ROWS:SORT_BUF, :] = jnp.zeros((SORT_BUF - SORT_ROWS, xs_ref.shape[2]), BF16)
    g_hi, g_mid, g_lo = _split3(gates)
    gs_ref[0, 0:SORT_ROWS, :] = _dot(perm, g_hi) + _dot(perm, g_mid) + _dot(perm, g_lo)
    gs_ref[0, SORT_ROWS:SORT_BUF, :] = jnp.zeros((SORT_BUF - SORT_ROWS, LANES), F32)
    pos8 = jnp.where(lax.broadcasted_iota(jnp.int32, (8, tm), 0) == 0, pos, 0.0)
    post_ref[...] = jnp.concatenate([pos8, jnp.zeros((LANES - 8, tm), F32)], axis=0).T
    cnt_ref[0] = jnp.broadcast_to(n, (8, LANES))


def _moe_kernel(start_ref, trips_ref, xs_ref, gs_ref, wg_ref, wu_ref, wd_ref, acc_ref):
    i, e = pl.program_id(0), pl.program_id(1)

    @pl.when(e == 0)
    def _():
        acc_ref[...] = jnp.zeros(acc_ref.shape, F32)

    grp = lax.shift_right_logical(e, 2)
    slots = [(i * TILES_PER_STEP + tile) * N_GROUPS + grp for tile in range(TILES_PER_STEP)]
    row0 = [start_ref[s] for s in slots]
    need = [trips_ref[s] for s in slots]

    def window(w, carry):
        rows = [pl.ds(pl.multiple_of(jnp.minimum(r + w * WIN_ROWS, SORT_BUF - WIN_ROWS), ROW_ALIGN), WIN_ROWS)
                for r in row0]
        xw = jnp.concatenate([xs_ref[tile, rows[tile], :] for tile in range(TILES_PER_STEP)], axis=0)
        hg = _dot(xw, wg_ref[0])
        h = hg * _sigmoid(hg) * _dot(xw, wu_ref[0])
        y = _dot(h.astype(BF16), wd_ref[0])
        for tile in range(TILES_PER_STEP):
            gw = gs_ref[tile, rows[tile], :]
            lane = lax.broadcasted_iota(jnp.int32, gw.shape, 1)
            gcol = jnp.sum(jnp.where((lane == e) & (w < need[tile]), gw, 0.0), axis=1, keepdims=True)
            acc_ref[tile, rows[tile], :] += y[tile * WIN_ROWS:(tile + 1) * WIN_ROWS, :] * gcol
        return carry

    lax.fori_loop(0, functools.reduce(jnp.maximum, need), window, 0)


def _moe(l, start, trips, xs, gs, wg, wu, wd):
    nt, _, d = xs.shape
    _, ne, _, de = wg.shape
    tps = TILES_PER_STEP
    grid_spec = pltpu.PrefetchScalarGridSpec(
        num_scalar_prefetch=2,
        grid=(nt // tps, ne),
        in_specs=[
            pl.BlockSpec((tps, SORT_BUF, d), lambda i, e, *_: (i, 0, 0)),
            pl.BlockSpec((tps, SORT_BUF, LANES), lambda i, e, *_: (i, 0, 0)),
            pl.BlockSpec((None, 1, d, de), lambda i, e, *_: (l, e, 0, 0)),
            pl.BlockSpec((None, 1, d, de), lambda i, e, *_: (l, e, 0, 0)),
            pl.BlockSpec((None, 1, de, d), lambda i, e, *_: (l, e, 0, 0)),
        ],
        out_specs=pl.BlockSpec((tps, SORT_BUF, d), lambda i, e, *_: (i, 0, 0)),
    )
    return pl.pallas_call(
        _moe_kernel,
        grid_spec=grid_spec,
        out_shape=jax.ShapeDtypeStruct((nt, SORT_BUF, d), F32),
        compiler_params=pltpu.CompilerParams(dimension_semantics=("arbitrary", "arbitrary"),
                                             vmem_limit_bytes=VMEM_LIMIT),
        name="moe_experts",
    )(start, trips, xs, gs, wg, wu, wd)


def _combine_kernel(x1_ref, acc_ref, post_ref, g_ref, b_ref, out_ref):
    tm = x1_ref.shape[0]
    acc = acc_ref[0]
    hi = acc.astype(BF16)
    lo = (acc - hi.astype(F32)).astype(BF16)
    pos = post_ref[...][:, 0:1]
    unperm = jnp.where(lax.broadcasted_iota(jnp.int32, (tm, SORT_ROWS), 1).astype(F32) == pos, 1.0, 0.0).astype(BF16)
    moe = _dot(unperm, hi) + _dot(unperm, lo)
    out_ref[...] = _layer_norm(DN_ALPHA * x1_ref[...] + moe, g_ref[...], b_ref[...])


def _combine(l, x1, acc, post, g, b, batch):
    n, d = x1.shape
    tm = TM_MOE
    per_row = n // batch // tm
    return pl.pallas_call(
        _combine_kernel,
        grid=(n // tm,),
        in_specs=[
            pl.BlockSpec((tm, d), lambda i: (i, 0)),
            pl.BlockSpec((1, SORT_ROWS, d), lambda i: (i, 0, 0)),
            pl.BlockSpec((tm, LANES), lambda i: (i, 0)),
            _layer_spec(l, (1, d)), _layer_spec(l, (1, d)),
        ],
        out_specs=pl.BlockSpec((None, tm, d), lambda i: (i // per_row, i % per_row, 0)),
        out_shape=jax.ShapeDtypeStruct((batch, n // batch, d), F32),
        compiler_params=pltpu.CompilerParams(dimension_semantics=("arbitrary",), vmem_limit_bytes=VMEM_LIMIT),
        name="moe_combine_ln",
    )(x1, acc, post, g, b)


def _group_windows(cnt):
    n = cnt[:, 0:N_GROUPS, 0].astype(jnp.int32)
    n_al = (n + (ROW_ALIGN - 1)) // ROW_ALIGN * ROW_ALIGN
    start = jnp.cumsum(n_al, axis=1) - n_al
    trips = (n + (WIN_ROWS - 1)) // WIN_ROWS
    return start.reshape(-1), trips.reshape(-1)


def _permute_w_in(w):
    sizes = (256, 256, 256, 256, 256, 256, 256, 64, 64, 64, 64, 64, 64, 12)
    offs = np.concatenate([[0], np.cumsum(sizes)])
    part = lambda i: w[..., int(offs[i]):int(offs[i + 1])]
    a_b, a_c, a_h, b_val, b_gate, c_p, q, kc, vc, ks, vs, kw, vw, g = (part(i) for i in range(14))
    pad = jnp.zeros(w.shape[:-1] + (LANES - g.shape[-1],), w.dtype)
    return jnp.concatenate([a_b, a_c, a_h, b_val, b_gate, c_p, q, kc, vc, ks, kw, vs, vw, g, pad], axis=-1).astype(BF16)


def _rope_tables(s):
    inv = 1.0 / (ROPE_THETA ** (jnp.arange(0, HEAD_DIM, 2, dtype=F32) / HEAD_DIM))
    ang = jnp.arange(s, dtype=F32)[:, None] * inv[None, :]
    ang = jnp.concatenate([ang, ang], -1)
    sign = jnp.concatenate([-jnp.ones((HEAD_DIM // 2,), F32), jnp.ones((HEAD_DIM // 2,), F32)])
    return jnp.tile(jnp.cos(ang), (1, NSA_HEADS)), jnp.tile(jnp.sin(ang) * sign[None, :], (1, NSA_HEADS))


def _overlap_t(s, m):
    n_cmp = (s - CMP_BLOCK) // CMP_STRIDE + 1
    cmp_start = np.arange(m) * CMP_STRIDE
    slc_start = np.arange(s // SLC_BLOCK) * SLC_BLOCK
    ov = (cmp_start[None, :] < slc_start[:, None] + SLC_BLOCK) & (cmp_start[None, :] + CMP_BLOCK > slc_start[:, None])
    ov = ov & (np.arange(m)[None, :] < n_cmp)
    return jnp.asarray(ov, BF16)


def _block_diag(w):
    nl, g, c, _ = w.shape
    return jnp.einsum('lgcd,gh->lgchd', w, jnp.eye(g, dtype=w.dtype)).reshape(nl, g * c, g * c)


def kernel(x, w_in, conv_a_w, conv_b_w, conv_b_b, cf_ln_g, cf_ln_b, pool_w, pool_scale, cmp_pe_k, cmp_pe_v, cmp_k_w1, cmp_k_w2, cmp_v_w1, cmp_v_w2, w_out, ln1_g, ln1_b, ln2_g, ln2_b, router_w, router_bias, moe_w_gate, moe_w_up, moe_w_down):
    b, s, d = x.shape
    m = s // CMP_STRIDE
    cos4, sin4 = _rope_tables(s)
    ovt = _overlap_t(s, m)
    rw = jnp.concatenate([router_w, jnp.zeros((d, LANES - N_EXPERTS), router_w.dtype)], axis=1).astype(BF16)
    rb = router_bias.reshape(N_EXPERTS, 1).astype(F32)
    tri = jnp.asarray(np.triu(np.ones((TM_MOE, TM_MOE), np.float32), k=1), BF16)

    nl = w_in.shape[0]
    row = lambda v: v.reshape(nl, 1, -1)
    pad_w2 = lambda w: jnp.concatenate([w, jnp.zeros((nl, CMP_HIDDEN, LANES - HEAD_DIM), w.dtype)], axis=2).astype(BF16)
    w_in_p = _permute_w_in(w_in)
    pool_bd = _block_diag(pool_w).astype(BF16)
    pe_kv = jnp.concatenate([cmp_pe_k, cmp_pe_v], axis=-1)
    zeros_w1 = jnp.zeros((nl, CMP_BLOCK, HEAD_DIM, CMP_HIDDEN), cmp_k_w1.dtype)
    w1_kv = jnp.concatenate([
        jnp.concatenate([cmp_k_w1.reshape(nl, CMP_BLOCK, HEAD_DIM, CMP_HIDDEN), zeros_w1], axis=-1),
        jnp.concatenate([zeros_w1, cmp_v_w1.reshape(nl, CMP_BLOCK, HEAD_DIM, CMP_HIDDEN)], axis=-1)], axis=2).astype(BF16)
    wk2, wv2 = pad_w2(cmp_k_w2), pad_w2(cmp_v_w2)
    wo = w_out.astype(BF16)
    wg, wu, wd = moe_w_gate.astype(BF16), moe_w_up.astype(BF16), moe_w_down.astype(BF16)
    cvbb, lng, lnb, pools = row(conv_b_b), row(cf_ln_g), row(cf_ln_b), row(pool_scale)
    g1, b1, g2, b2 = row(ln1_g), row(ln1_b), row(ln2_g), row(ln2_b)

    for l in range(DEPTH):
        yabc, qt, qrt, kcvc, kk, ksx, vvt, gt = _inproj(
            l, x, w_in_p, cos4, sin4, conv_a_w, conv_b_w, cvbb, lng, lnb, pool_bd, pools)
        kcmp, vcmpt = _compress(l, kcvc, pe_kv, w1_kv, wk2, wv2)
        yd = _nsa(qt, qrt, gt, kk, ksx, vvt, kcmp, vcmpt, ovt)
        x1, xs, gs, post, cnt = _outproj(l, x, yabc, yd, wo, g1, b1, rw, rb, tri)
        start, trips = _group_windows(cnt)
        acc = _moe(l, start, trips, xs, gs, wg, wu, wd)
        x = _combine(l, x1, acc, post, g2, b2, b)
    return x
```
